```python
import functools
import jax, jax.numpy as jnp
from jax import lax
import numpy as np

D_MODEL = 1024
BATCH = 2
SEQ = 16384
DEPTH = 1
DEC_BATCH = 8
DEC_SEQ = 64
PAST_LEN = 4096

CHUNK = 64
HEAD_DIM = 64
A_HEADS = 8
A_PREV_CHUNKS = 8
A_ROWS = A_PREV_CHUNKS * CHUNK
REL_CLIP = 128
B_HEADS = 8
B_KV_HEADS = 2
B_GROUP = B_HEADS // B_KV_HEADS
WINDOW = 128
B_PREV_CHUNKS = WINDOW // CHUNK
ROPE_THETA = 10000.0
N_EXPERTS = 256
TOP_K = 8
N_GROUPS = 8
TOPK_GROUPS = 4
EXPERT_DIM = 256
SHARED_DIM = 256
ROUTED_SCALE = 2.5
ROW_BLOCK = 128
EPS = 1e-6
IN_SPLITS = (A_HEADS * HEAD_DIM, A_HEADS * HEAD_DIM, A_HEADS * HEAD_DIM,
             B_HEADS * HEAD_DIM, B_KV_HEADS * HEAD_DIM, B_KV_HEADS * HEAD_DIM, 2 * D_MODEL)
N_IN = sum(IN_SPLITS)

kernel_name = 'hybrid_chunk_stream_encoder_step'


def _rmsnorm(x, g):
    xf = x.astype(jnp.float32)
    y = xf * lax.rsqrt(jnp.mean(xf * xf, axis=-1, keepdims=True) + EPS)
    return (y * g.astype(jnp.float32)).astype(x.dtype)


def _rope(x, pos):
    half = x.shape[-1] // 2
    inv = ROPE_THETA ** (-jnp.arange(half, dtype=jnp.float32) / half)
    ang = pos.astype(jnp.float32)[:, None] * inv[None, :]
    shape = (1, x.shape[1]) + (1,) * (x.ndim - 3) + (half,)
    cos = jnp.cos(ang).reshape(shape)
    sin = jnp.sin(ang).reshape(shape)
    xf = x.astype(jnp.float32)
    x1, x2 = xf[..., :half], xf[..., half:]
    return jnp.concatenate([x1 * cos - x2 * sin, x2 * cos + x1 * sin], axis=-1).astype(x.dtype)


def _rel_bias(table, dist):
    idx = jnp.clip(dist, -REL_CLIP, REL_CLIP) + REL_CLIP
    return table.astype(jnp.float32)[:, idx][:, None]


def _scores(q, k, bias):
    s = jnp.einsum('bqgrd,bkgd->bgrqk', q, k).astype(jnp.float32) * (q.shape[-1] ** -0.5)
    return s if bias is None else s + bias


def _attend(s, v, sinks):
    if sinks is None:
        p = jax.nn.softmax(s, axis=-1)
    else:
        sk = sinks.astype(jnp.float32).reshape(1, s.shape[1], s.shape[2], 1, 1)
        m = jnp.maximum(jnp.max(s, axis=-1, keepdims=True), sk)
        e = jnp.exp(s - m)
        p = e / (jnp.sum(e, axis=-1, keepdims=True) + jnp.exp(sk - m))
    return jnp.einsum('bgrqk,bkgd->bqgrd', p.astype(v.dtype), v)


def _band_attention_prompt(q, k, v, n_prev, bias, sinks):
    b, t, g, r, dh = q.shape
    nc = t // CHUNK
    kb = (n_prev + 1) * CHUNK
    pad = ((0, 0), (n_prev * CHUNK, 0), (0, 0), (0, 0))
    kp = jnp.pad(k, pad)
    vp = jnp.pad(v, pad)
    qc = jnp.moveaxis(q.reshape(b, nc, CHUNK, g, r, dh), 1, 0)
    key_off = jnp.arange(kb)

    def one_chunk(args):
        n, qn = args
        kn = lax.dynamic_slice_in_dim(kp, n * CHUNK, kb, axis=1)
        vn = lax.dynamic_slice_in_dim(vp, n * CHUNK, kb, axis=1)
        valid = (n - n_prev) * CHUNK + key_off >= 0
        s = jnp.where(valid, _scores(qn, kn, bias), -jnp.inf)
        return _attend(s, vn, sinks)

    o = lax.map(one_chunk, (jnp.arange(nc), qc))
    return jnp.moveaxis(o, 0, 1).reshape(b, t, g, r, dh)


def _band_attention_step(q, k_new, v_new, k_cache, v_cache, bias, sinks):
    k = jnp.concatenate([k_cache.astype(k_new.dtype), k_new], axis=1)
    v = jnp.concatenate([v_cache.astype(v_new.dtype), v_new], axis=1)
    return _attend(_scores(q, k, bias), v, sinks)


def _mix_prompt(qa, ka, va, qb, kb, vb, rel_table, sinks):
    t = qa.shape[1]
    dist = (A_ROWS + jnp.arange(CHUNK))[:, None] - jnp.arange(A_ROWS + CHUNK)[None, :]
    oa = _band_attention_prompt(qa, ka, va, A_PREV_CHUNKS, _rel_bias(rel_table, dist), None)
    ob = _band_attention_prompt(qb, kb, vb, B_PREV_CHUNKS, None, sinks)
    la = min(A_ROWS, t)
    lb = min(WINDOW, t)
    return oa, ob, (ka[:, t - la:], va[:, t - la:], kb[:, t - lb:], vb[:, t - lb:])


def _mix_step(qa, ka, va, qb, kb, vb, cache_ak, cache_av, cache_bk, cache_bv, rel_table, sinks):
    s = qa.shape[1]
    la = cache_ak.shape[1]
    dist = (la + jnp.arange(s))[:, None] - jnp.arange(la + s)[None, :]
    oa = _band_attention_step(qa, ka, va, cache_ak, cache_av, _rel_bias(rel_table, dist), None)
    ob = _band_attention_step(qb, kb, vb, cache_bk, cache_bv, None, sinks)
    return oa, ob, (ka, va, kb, vb)


def _moe(h, w_router, b_router, we_gate, we_up, we_down, ws_gate, ws_up, ws_down):
    t, d = h.shape
    scores = jax.nn.sigmoid(jnp.dot(h.astype(jnp.float32), w_router.astype(jnp.float32)))
    biased = scores + b_router.astype(jnp.float32)
    per_group = N_EXPERTS // N_GROUPS
    group_score = lax.top_k(biased.reshape(t, N_GROUPS, per_group), 2)[0].sum(-1)
    _, group_idx = lax.top_k(group_score, TOPK_GROUPS)
    group_keep = jnp.any(group_idx[:, :, None] == jnp.arange(N_GROUPS)[None, None, :], axis=1)
    cand = jnp.where(jnp.repeat(group_keep, per_group, axis=1), biased, -jnp.inf)
    _, expert_idx = lax.top_k(cand, TOP_K)
    gate = jnp.take_along_axis(scores, expert_idx, axis=1)
    gate = gate / jnp.sum(gate, axis=-1, keepdims=True) * ROUTED_SCALE

    n = t * TOP_K
    flat_e = expert_idx.reshape(n)
    order = jnp.argsort(flat_e, stable=True)
    e_sorted = flat_e[order]
    counts = jnp.bincount(flat_e, length=N_EXPERTS)
    padded = (counts + ROW_BLOCK - 1) // ROW_BLOCK * ROW_BLOCK
    pad_end = jnp.cumsum(padded)
    pad_start = pad_end - padded
    start = jnp.cumsum(counts) - counts
    dest = pad_start[e_sorted] + jnp.arange(n) - start[e_sorted]
    n_blocks = -(-n // ROW_BLOCK) + N_EXPERTS
    rows = n_blocks * ROW_BLOCK
    row_tok = jnp.zeros((rows,), jnp.int32).at[dest].set((order // TOP_K).astype(jnp.int32))
    row_gate = jnp.zeros((rows,), jnp.float32).at[dest].set(gate.reshape(n)[order])
    block_expert = jnp.minimum(
        jnp.searchsorted(pad_end, jnp.arange(n_blocks) * ROW_BLOCK, side='right'), N_EXPERTS - 1)

    def expert_block(args):
        tok, g, e = args
        xb = h[tok]
        y = jnp.dot(jax.nn.silu(jnp.dot(xb, we_gate[e])) * jnp.dot(xb, we_up[e]), we_down[e])
        return y.astype(jnp.float32) * g[:, None]

    y_rows = lax.map(expert_block, (row_tok.reshape(n_blocks, ROW_BLOCK),
                                    row_gate.reshape(n_blocks, ROW_BLOCK), block_expert))
    routed = jax.ops.segment_sum(y_rows.reshape(rows, d), row_tok, num_segments=t)
    shared = jnp.dot(jax.nn.silu(jnp.dot(h, ws_gate)) * jnp.dot(h, ws_up), ws_down)
    return (routed + shared.astype(jnp.float32)).astype(h.dtype)


def _layer(x, c, pos, mix, w_ada, b_ada, g_pre_mix, g_post_mix, w_in, w_branch_a, w_branch_b, w_out,
           g_pre_ffn, g_post_ffn, w_router, b_router, we_gate, we_up, we_down, ws_gate, ws_up, ws_down):
    b, t, d = x.shape
    mod = jnp.dot(jax.nn.silu(c), w_ada) + b_ada
    shift1, scale1, gate1, shift2, scale2, gate2 = [m[:, None, :] for m in jnp.split(mod, 6, axis=-1)]

    h = _rmsnorm(x, g_pre_mix) * (1 + scale1) + shift1
    proj = jnp.dot(h, w_in)
    parts = []
    off = 0
    for size in IN_SPLITS:
        parts.append(proj[..., off:off + size])
        off += size
    qa, ka, va, qb, kb, vb, gate_logits = parts
    qa = qa.reshape(b, t, A_HEADS, 1, HEAD_DIM)
    ka = ka.reshape(b, t, A_HEADS, HEAD_DIM)
    va = va.reshape(b, t, A_HEADS, HEAD_DIM)
    qb = _rope(qb.reshape(b, t, B_KV_HEADS, B_GROUP, HEAD_DIM), pos)
    kb = _rope(kb.reshape(b, t, B_KV_HEADS, HEAD_DIM), pos)
    vb = vb.reshape(b, t, B_KV_HEADS, HEAD_DIM)
    oa, ob, state = mix(qa, ka, va, qb, kb, vb)
    gate_a, gate_b = jnp.split(jax.nn.sigmoid(gate_logits), 2, axis=-1)
    merged = (gate_a * jnp.dot(oa.reshape(b, t, -1), w_branch_a)
              + gate_b * jnp.dot(ob.reshape(b, t, -1), w_branch_b))
    x = x + gate1 * _rmsnorm(jnp.dot(merged, w_out), g_post_mix)

    h = _rmsnorm(x, g_pre_ffn) * (1 + scale2) + shift2
    f = _moe(h.reshape(b * t, d), w_router, b_router, we_gate, we_up, we_down,
             ws_gate, ws_up, ws_down).reshape(b, t, d)
    x = x + gate2 * _rmsnorm(f, g_post_ffn)
    return x, state


def setup_inputs(seed: int = 0) -> dict:
    key = jax.random.key(seed)
    keys = iter(jax.random.split(key, 32))

    def nrm(shape, scale=1.0):
        return jax.random.normal(next(keys), shape, jnp.float32) * scale

    d = D_MODEL
    la = min(A_ROWS, PAST_LEN)
    lb = min(WINDOW, PAST_LEN)
    wa = A_HEADS * HEAD_DIM
    wb = B_HEADS * HEAD_DIM
    return {
        'x_prompt': nrm((BATCH, SEQ, d)),
        'x_sample': nrm((DEC_BATCH, DEC_SEQ, d)),
        'cache_a_k': nrm((DEPTH, DEC_BATCH, la, A_HEADS, HEAD_DIM)),
        'cache_a_v': nrm((DEPTH, DEC_BATCH, la, A_HEADS, HEAD_DIM)),
        'cache_b_k': nrm((DEPTH, DEC_BATCH, lb, B_KV_HEADS, HEAD_DIM)),
        'cache_b_v': nrm((DEPTH, DEC_BATCH, lb, B_KV_HEADS, HEAD_DIM)),
        'c_prompt': nrm((BATCH, d)),
        'c_sample': nrm((DEC_BATCH, d)),
        'w_ada': nrm((DEPTH, d, 6 * d), 0.5 * d ** -0.5),
        'b_ada': nrm((DEPTH, 6 * d), 0.02),
        'g_pre_mix': 1.0 + nrm((DEPTH, d), 0.05),
        'g_post_mix': 1.0 + nrm((DEPTH, d), 0.05),
        'w_in': nrm((DEPTH, d, N_IN), d ** -0.5),
        'rel_bias_a': nrm((DEPTH, A_HEADS, 2 * REL_CLIP + 1), 0.5),
        'sinks_b': nrm((DEPTH, B_HEADS), 1.0),
        'w_branch_a': nrm((DEPTH, wa, d), wa ** -0.5),
        'w_branch_b': nrm((DEPTH, wb, d), wb ** -0.5),
        'w_out': nrm((DEPTH, d, d), d ** -0.5),
        'g_pre_ffn': 1.0 + nrm((DEPTH, d), 0.05),
        'g_post_ffn': 1.0 + nrm((DEPTH, d), 0.05),
        'w_router': nrm((DEPTH, d, N_EXPERTS), d ** -0.5),
        'b_router': nrm((DEPTH, N_EXPERTS), 0.01),
        'we_gate': nrm((DEPTH, N_EXPERTS, d, EXPERT_DIM), d ** -0.5),
        'we_up': nrm((DEPTH, N_EXPERTS, d, EXPERT_DIM), d ** -0.5),
        'we_down': nrm((DEPTH, N_EXPERTS, EXPERT_DIM, d), EXPERT_DIM ** -0.5),
        'ws_gate': nrm((DEPTH, d, SHARED_DIM), d ** -0.5),
        'ws_up': nrm((DEPTH, d, SHARED_DIM), d ** -0.5),
        'ws_down': nrm((DEPTH, SHARED_DIM, d), SHARED_DIM ** -0.5),
    }


def reference(x_prompt, x_sample, cache_a_k, cache_a_v, cache_b_k, cache_b_v, c_prompt, c_sample,
              w_ada, b_ada, g_pre_mix, g_post_mix, w_in, rel_bias_a, sinks_b, w_branch_a, w_branch_b, w_out,
              g_pre_ffn, g_post_ffn, w_router, b_router, we_gate, we_up, we_down, ws_gate, ws_up, ws_down):
    pos_prompt = jnp.arange(x_prompt.shape[1])
    pos_sample = PAST_LEN + jnp.arange(x_sample.shape[1])
    y_prompt = x_prompt
    y_sample = x_sample
    st_prompt = []
    st_sample = []
    for l in range(DEPTH):
        lw = (w_ada[l], b_ada[l], g_pre_mix[l], g_post_mix[l], w_in[l], w_branch_a[l], w_branch_b[l], w_out[l],
              g_pre_ffn[l], g_post_ffn[l], w_router[l], b_router[l], we_gate[l], we_up[l], we_down[l],
              ws_gate[l], ws_up[l], ws_down[l])
        mix_p = functools.partial(_mix_prompt, rel_table=rel_bias_a[l], sinks=sinks_b[l])
        mix_s = functools.partial(_mix_step, cache_ak=cache_a_k[l], cache_av=cache_a_v[l],
                                  cache_bk=cache_b_k[l], cache_bv=cache_b_v[l],
                                  rel_table=rel_bias_a[l], sinks=sinks_b[l])
        y_prompt, sp = _layer(y_prompt, c_prompt, pos_prompt, mix_p, *lw)
        y_sample, ss = _layer(y_sample, c_sample, pos_sample, mix_s, *lw)
        st_prompt.append(sp)
        st_sample.append(ss)
    a_k_prompt = jnp.stack([s[0] for s in st_prompt])
    a_v_prompt = jnp.stack([s[1] for s in st_prompt])
    b_k_prompt = jnp.stack([s[2] for s in st_prompt])
    b_v_prompt = jnp.stack([s[3] for s in st_prompt])
    a_k_sample = jnp.stack([s[0] for s in st_sample])
    a_v_sample = jnp.stack([s[1] for s in st_sample])
    b_k_sample = jnp.stack([s[2] for s in st_sample])
    b_v_sample = jnp.stack([s[3] for s in st_sample])
    return (y_prompt, y_sample, a_k_prompt, a_v_prompt, b_k_prompt, b_v_prompt,
            a_k_sample, a_v_sample, b_k_sample, b_v_sample)
```

```python
import functools

import jax
import jax.numpy as jnp
from jax import lax
from jax.experimental import pallas as pl
from jax.experimental.pallas import tpu as pltpu

F32 = jnp.float32
BF16 = jnp.bfloat16
I32 = jnp.int32
U32 = jnp.uint32
HIGHEST = lax.Precision.HIGHEST

D_MODEL = 1024
CHUNK = 64
HEAD_DIM = 64
A_HEADS = 8
A_PREV_CHUNKS = 8
A_ROWS = A_PREV_CHUNKS * CHUNK
A_BAND = A_ROWS + CHUNK
REL_CLIP = 128
B_HEADS = 8
B_KV_HEADS = 2
B_GROUP = B_HEADS // B_KV_HEADS
WINDOW = 128
B_BAND = WINDOW + CHUNK
ROPE_THETA = 10000.0
N_EXPERTS = 256
TOP_K = 8
N_GROUPS = 8
PER_GROUP = N_EXPERTS // N_GROUPS
TOPK_GROUPS = 4
EXPERT_DIM = 256
ROUTED_SCALE = 2.5
EPS = 1e-6
PAST_LEN = 4096

WA = A_HEADS * HEAD_DIM
WB = B_HEADS * HEAD_DIM
WKB = B_KV_HEADS * HEAD_DIM
OFF_QA, OFF_KA, OFF_VA = 0, WA, 2 * WA
OFF_QB = 3 * WA
OFF_KB = OFF_QB + WB
OFF_VB = OFF_KB + WKB
OFF_G = OFF_VB + WKB
N_IN = OFF_G + 2 * D_MODEL

LANES = 128
TOK_TILE = 512
MOE_TILE = 256
EXPERT_ROWS = 256
HALF_D = D_MODEL // 2
VMEM_LIMIT = 56 * 1024 * 1024


def _cparams(sem, vmem=VMEM_LIMIT):
    return pltpu.CompilerParams(dimension_semantics=sem, vmem_limit_bytes=vmem)


def _rms(x):
    return x * lax.rsqrt(jnp.mean(x * x, axis=-1, keepdims=True) + EPS)


def _group_affine(y, mul, add):
    g = mul.shape[0]
    y3 = y.reshape(g, CHUNK, y.shape[-1]) * mul
    if add is not None:
        y3 = y3 + add
    return y3.reshape(g * CHUNK, y.shape[-1])


def _pack_halves(a, b):
    ua = lax.bitcast_convert_type(a.astype(BF16).astype(F32), U32)
    ub = lax.bitcast_convert_type(b.astype(BF16).astype(F32), U32)
    return (ua & jnp.uint32(0xFFFF0000)) | (ub >> 16)


def _unpack_halves(u):
    a = lax.bitcast_convert_type(u & jnp.uint32(0xFFFF0000), F32)
    b = lax.bitcast_convert_type(u << 16, F32)
    return a, b


def _mod_kernel(c_ref, w_ref, b_ref, o_ref):
    c = c_ref[...]
    s = c * jax.nn.sigmoid(c)
    o_ref[...] = jnp.dot(s, w_ref[...], precision=HIGHEST, preferred_element_type=F32) + b_ref[...]


def _modulation(c_all, w_ada, b_ada):
    rows = c_all.shape[0]
    n = w_ada.shape[1]
    tn = 512
    return pl.pallas_call(
        _mod_kernel,
        grid=(n // tn,),
        in_specs=[pl.BlockSpec((rows, D_MODEL), lambda j: (0, 0)),
                  pl.BlockSpec((D_MODEL, tn), lambda j: (0, j)),
                  pl.BlockSpec((1, tn), lambda j: (0, j))],
        out_specs=pl.BlockSpec((rows, tn), lambda j: (0, j)),
        out_shape=jax.ShapeDtypeStruct((rows, n), F32),
        compiler_params=_cparams(("arbitrary",)),
        name="modulation",
    )(c_all, w_ada, b_ada.reshape(1, n))


def _rope(x, cos, sin_signed):
    n = x.shape[-1]
    reps = n // LANES
    if reps > 1:
        cos = jnp.concatenate([cos] * reps, axis=1)
        sin_signed = jnp.concatenate([sin_signed] * reps, axis=1)
    lane = lax.broadcasted_iota(I32, x.shape, 1)
    first_half = (lane % HEAD_DIM) < (HEAD_DIM // 2)
    partner = jnp.where(first_half, pltpu.roll(x, n - HEAD_DIM // 2, 1), pltpu.roll(x, HEAD_DIM // 2, 1))
    return x * cos + partner * sin_signed


def _inproj_kernel(prompt_state, tiles_per_seq,
                   x_ref, sh_ref, sc_ref, g_ref, w_ref, cos_ref, sin_ref,
                   qa_ref, ka_ref, va_ref, qb_ref, kb_ref, vb_ref, gt_ref,
                   ska_ref, sva_ref, skb_ref, svb_ref):
    x = x_ref[...]
    h = _group_affine(_rms(x) * g_ref[...], 1.0 + sc_ref[...], sh_ref[...]).astype(BF16)

    def proj(off, width):
        return jnp.dot(h, w_ref[:, off:off + width], preferred_element_type=F32)

    cos = cos_ref[...]
    sin = sin_ref[...]
    scale = HEAD_DIM ** -0.5
    qa_ref[...] = (proj(OFF_QA, WA) * scale).astype(BF16)
    ka = proj(OFF_KA, WA)
    va = proj(OFF_VA, WA)
    ka_ref[...] = ka.astype(BF16)
    va_ref[...] = va.astype(BF16)
    qb_ref[...] = (_rope(proj(OFF_QB, WB), cos, sin) * scale).astype(BF16)
    kb = _rope(proj(OFF_KB, WKB), cos, sin)
    vb = proj(OFF_VB, WKB)
    kb_ref[...] = kb.astype(BF16)
    vb_ref[...] = vb.astype(BF16)
    gt_ref[...] = jax.nn.sigmoid(proj(OFF_G, 2 * D_MODEL)).astype(BF16)

    if prompt_state:
        @pl.when(pl.program_id(0) % tiles_per_seq == tiles_per_seq - 1)
        def _():
            ska_ref[...] = ka
            sva_ref[...] = va
            skb_ref[...] = kb[TOK_TILE - WINDOW:, :]
            svb_ref[...] = vb[TOK_TILE - WINDOW:, :]
    else:
        ska_ref[...] = ka
        sva_ref[...] = va
        skb_ref[...] = kb
        svb_ref[...] = vb


def _inproj(x2d, shift_g, scale_g, g_pre, w_in_bf, cos_tab, sin_tab, n_seq, prompt_state):
    n = x2d.shape[0]
    nt = n // TOK_TILE
    tiles_per_seq = nt // n_seq if prompt_state else 1
    tab_tiles = cos_tab.shape[0] // TOK_TILE
    gpt = TOK_TILE // CHUNK
    row = lambda i: (i, 0)
    grp = lambda i: (i, 0, 0)
    if prompt_state:
        st_shapes = [jax.ShapeDtypeStruct((n_seq, A_ROWS, WA), F32)] * 2 + \
                    [jax.ShapeDtypeStruct((n_seq, WINDOW, WKB), F32)] * 2
        st_specs = [pl.BlockSpec((None, A_ROWS, WA), lambda i: (i // tiles_per_seq, 0, 0))] * 2 + \
                   [pl.BlockSpec((None, WINDOW, WKB), lambda i: (i // tiles_per_seq, 0, 0))] * 2
    else:
        st_shapes = [jax.ShapeDtypeStruct((n, WA), F32)] * 2 + [jax.ShapeDtypeStruct((n, WKB), F32)] * 2
        st_specs = [pl.BlockSpec((TOK_TILE, WA), row)] * 2 + [pl.BlockSpec((TOK_TILE, WKB), row)] * 2
    out_shapes = [jax.ShapeDtypeStruct((n, WA), BF16)] * 4 + [jax.ShapeDtypeStruct((n, WKB), BF16)] * 2 + \
                 [jax.ShapeDtypeStruct((n, 2 * D_MODEL), BF16)]
    out_shapes = [out_shapes[0], out_shapes[1], out_shapes[2], out_shapes[3], out_shapes[4], out_shapes[5],
                  out_shapes[6]] + st_shapes
    out_specs = [pl.BlockSpec((TOK_TILE, WA), row)] * 4 + [pl.BlockSpec((TOK_TILE, WKB), row)] * 2 + \
                [pl.BlockSpec((TOK_TILE, 2 * D_MODEL), row)] + st_specs
    return pl.pallas_call(
        functools.partial(_inproj_kernel, prompt_state, tiles_per_seq),
        grid=(nt,),
        in_specs=[pl.BlockSpec((TOK_TILE, D_MODEL), row),
                  pl.BlockSpec((gpt, 1, D_MODEL), grp),
                  pl.BlockSpec((gpt, 1, D_MODEL), grp),
                  pl.BlockSpec((1, D_MODEL), lambda i: (0, 0)),
                  pl.BlockSpec((D_MODEL, N_IN), lambda i: (0, 0)),
                  pl.BlockSpec((TOK_TILE, LANES), lambda i: (i % tab_tiles, 0)),
                  pl.BlockSpec((TOK_TILE, LANES), lambda i: (i % tab_tiles, 0))],
        out_specs=out_specs,
        out_shape=out_shapes,
        compiler_params=_cparams(("arbitrary",)),
        name="inproj_prompt" if prompt_state else "inproj_sample",
    )(x2d, shift_g, scale_g, g_pre, w_in_bf, cos_tab, sin_tab)


def _attn_kernel(n_chunks, mask_first,
                 x_ref, qa_ref, qb_ref, gt_ref,
                 kap_ref, kac_ref, vap_ref, vac_ref, kbp_ref, kbc_ref, vbp_ref, vbc_ref,
                 bias_ref, sink_ref, wba_ref, wbb_ref, wout_ref, gpost_ref, gate1_ref,
                 o_ref,
                 ka_s, va_s, kb_s, vb_s, oa_s, ob_s):
    rows = n_chunks * CHUNK
    pb = kbp_ref.shape[0]
    ka_s[0:A_ROWS, :] = kap_ref[...].astype(BF16)
    va_s[0:A_ROWS, :] = vap_ref[...].astype(BF16)
    ka_s[A_ROWS:A_ROWS + rows, :] = kac_ref[...]
    va_s[A_ROWS:A_ROWS + rows, :] = vac_ref[...]
    kb_s[0:WINDOW, :] = kbp_ref[pb - WINDOW:pb, :].astype(BF16)
    vb_s[0:WINDOW, :] = vbp_ref[pb - WINDOW:pb, :].astype(BF16)
    kb_s[WINDOW:WINDOW + rows, :] = kbc_ref[...]
    vb_s[WINDOW:WINDOW + rows, :] = vbc_ref[...]

    if mask_first:
        not_first = pl.program_id(1) > 0
    lane_q = lax.broadcasted_iota(I32, (CHUNK, LANES), 1)
    nt_dims = (((1,), (1,)), ((), ()))

    def chunk_body(c, carry):
        c0 = pl.multiple_of(c * CHUNK, CHUNK)
        if mask_first:
            col_a = lax.broadcasted_iota(I32, (1, A_BAND), 1)
            valid_a = jnp.logical_or(not_first, c0 + col_a >= A_ROWS)
            col_b = lax.broadcasted_iota(I32, (1, B_BAND), 1)
            valid_b = jnp.logical_or(not_first, c0 + col_b >= WINDOW)

        for p in range(A_HEADS // 2):
            cols = slice(p * LANES, (p + 1) * LANES)
            q = qa_ref[pl.ds(c0, CHUNK), cols].astype(F32)
            qs = jnp.concatenate([jnp.where(lane_q < HEAD_DIM, q, 0.0),
                                  jnp.where(lane_q >= HEAD_DIM, q, 0.0)], axis=0).astype(BF16)
            k = ka_s[pl.ds(c0, A_BAND), cols]
            s = lax.dot_general(qs, k, nt_dims, preferred_element_type=F32) + bias_ref[p]
            if mask_first:
                s = jnp.where(valid_a, s, -jnp.inf)
            m = jnp.max(s, axis=1, keepdims=True)
            e = jnp.exp(s - m)
            l = jnp.sum(e, axis=1, keepdims=True)
            v = va_s[pl.ds(c0, A_BAND), cols]
            o = jnp.dot(e.astype(BF16), v, preferred_element_type=F32) / l
            oa_s[pl.ds(c0, CHUNK), cols] = jnp.where(lane_q < HEAD_DIM, o[:CHUNK], o[CHUNK:]).astype(BF16)

        for g in range(B_KV_HEADS):
            parts = []
            for r in range(B_GROUP):
                head = g * B_GROUP + r
                t, half = head // 2, head % 2
                q = qb_ref[pl.ds(c0, CHUNK), t * LANES:(t + 1) * LANES].astype(F32)
                if half != g:
                    q = pltpu.roll(q, HEAD_DIM, 1)
                in_g = (lane_q >= HEAD_DIM) if g else (lane_q < HEAD_DIM)
                parts.append(jnp.where(in_g, q, 0.0))
            qs = jnp.concatenate(parts, axis=0).astype(BF16)
            k = kb_s[pl.ds(c0, B_BAND), :]
            s = lax.dot_general(qs, k, nt_dims, preferred_element_type=F32)
            if mask_first:
                s = jnp.where(valid_b, s, -jnp.inf)
            sk = sink_ref[g]
            m = jnp.maximum(jnp.max(s, axis=1, keepdims=True), sk)
            e = jnp.exp(s - m)
            l = jnp.sum(e, axis=1, keepdims=True) + jnp.exp(sk - m)
            v = vb_s[pl.ds(c0, B_BAND), :]
            o = jnp.dot(e.astype(BF16), v, preferred_element_type=F32) / l
            for s2 in range(B_GROUP // 2):
                o_even = o[(2 * s2) * CHUNK:(2 * s2 + 1) * CHUNK]
                o_odd = o[(2 * s2 + 1) * CHUNK:(2 * s2 + 2) * CHUNK]
                if g == 0:
                    tile = jnp.where(lane_q < HEAD_DIM, o_even, pltpu.roll(o_odd, HEAD_DIM, 1))
                else:
                    tile = jnp.where(lane_q < HEAD_DIM, pltpu.roll(o_even, HEAD_DIM, 1), o_odd)
                t = g * (B_GROUP // 2) + s2
                ob_s[pl.ds(c0, CHUNK), t * LANES:(t + 1) * LANES] = tile.astype(BF16)
        return carry

    lax.fori_loop(0, n_chunks, chunk_body, 0)

    za = jnp.dot(oa_s[...], wba_ref[...], preferred_element_type=F32)
    zb = jnp.dot(ob_s[...], wbb_ref[...], preferred_element_type=F32)
    merged = gt_ref[:, :D_MODEL].astype(F32) * za + gt_ref[:, D_MODEL:].astype(F32) * zb
    mo = jnp.dot(merged.astype(BF16), wout_ref[...], preferred_element_type=F32)
    o_ref[...] = x_ref[...] + _group_affine(_rms(mo) * gpost_ref[...], gate1_ref[...], None)


def _attn_scratch(rows):
    return [pltpu.VMEM((A_ROWS + rows, WA), BF16), pltpu.VMEM((A_ROWS + rows, WA), BF16),
            pltpu.VMEM((WINDOW + rows, WKB), BF16), pltpu.VMEM((WINDOW + rows, WKB), BF16),
            pltpu.VMEM((rows, WA), BF16), pltpu.VMEM((rows, WB), BF16)]


def _const_specs(grid_rank):
    z2 = (lambda b, j: (0, 0)) if grid_rank == 2 else (lambda b: (0, 0))
    z3 = (lambda b, j: (0, 0, 0)) if grid_rank == 2 else (lambda b: (0, 0, 0))
    return [pl.BlockSpec((A_HEADS // 2, 2 * CHUNK, A_BAND), z3),
            pl.BlockSpec((B_KV_HEADS, B_GROUP * CHUNK, 1), z3),
            pl.BlockSpec((WA, D_MODEL), z2),
            pl.BlockSpec((WB, D_MODEL), z2),
            pl.BlockSpec((D_MODEL, D_MODEL), z2),
            pl.BlockSpec((1, D_MODEL), z2)]


def _attn_prompt(x2d, proj, consts, gate1_g, n_seq):
    qa, ka, va, qb, kb, vb, gt = proj
    n = x2d.shape[0]
    tps = n // n_seq // TOK_TILE
    gpt = TOK_TILE // CHUNK
    cur = lambda b, j: (b * tps + j, 0)
    prev = lambda b, j: (b * tps + jnp.maximum(j - 1, 0), 0)
    return pl.pallas_call(
        functools.partial(_attn_kernel, TOK_TILE // CHUNK, True),
        grid=(n_seq, tps),
        in_specs=[pl.BlockSpec((TOK_TILE, D_MODEL), cur),
                  pl.BlockSpec((TOK_TILE, WA), cur),
                  pl.BlockSpec((TOK_TILE, WB), cur),
                  pl.BlockSpec((TOK_TILE, 2 * D_MODEL), cur),
                  pl.BlockSpec((TOK_TILE, WA), prev), pl.BlockSpec((TOK_TILE, WA), cur),
                  pl.BlockSpec((TOK_TILE, WA), prev), pl.BlockSpec((TOK_TILE, WA), cur),
                  pl.BlockSpec((TOK_TILE, WKB), prev), pl.BlockSpec((TOK_TILE, WKB), cur),
                  pl.BlockSpec((TOK_TILE, WKB), prev), pl.BlockSpec((TOK_TILE, WKB), cur)]
                 + _const_specs(2)
                 + [pl.BlockSpec((gpt, 1, D_MODEL), lambda b, j: (b * tps + j, 0, 0))],
        out_specs=pl.BlockSpec((TOK_TILE, D_MODEL), cur),
        out_shape=jax.ShapeDtypeStruct((n, D_MODEL), F32),
        scratch_shapes=_attn_scratch(TOK_TILE),
        compiler_params=_cparams(("arbitrary", "arbitrary")),
        name="attn_prompt",
    )(x2d, qa, qb, gt, ka, ka, va, va, kb, kb, vb, vb, *consts, gate1_g)


def _attn_sample(x2d, proj, caches, consts, gate1_g):
    qa, ka, va, qb, kb, vb, gt = proj
    cak, cav, cbk, cbv = caches
    n_seq = cak.shape[0]
    cur = lambda b: (b, 0)
    cache = lambda b: (b, 0, 0)
    return pl.pallas_call(
        functools.partial(_attn_kernel, 1, False),
        grid=(n_seq,),
        in_specs=[pl.BlockSpec((CHUNK, D_MODEL), cur),
                  pl.BlockSpec((CHUNK, WA), cur),
                  pl.BlockSpec((CHUNK, WB), cur),
                  pl.BlockSpec((CHUNK, 2 * D_MODEL), cur),
                  pl.BlockSpec((None, A_ROWS, WA), cache), pl.BlockSpec((CHUNK, WA), cur),
                  pl.BlockSpec((None, A_ROWS, WA), cache), pl.BlockSpec((CHUNK, WA), cur),
                  pl.BlockSpec((None, WINDOW, WKB), cache), pl.BlockSpec((CHUNK, WKB), cur),
                  pl.BlockSpec((None, WINDOW, WKB), cache), pl.BlockSpec((CHUNK, WKB), cur)]
                 + _const_specs(1)
                 + [pl.BlockSpec((1, 1, D_MODEL), lambda b: (b, 0, 0))],
        out_specs=pl.BlockSpec((CHUNK, D_MODEL), cur),
        out_shape=jax.ShapeDtypeStruct(x2d.shape, F32),
        scratch_shapes=_attn_scratch(CHUNK),
        compiler_params=_cparams(("arbitrary",)),
        name="attn_sample",
    )(x2d, qa, qb, gt, cak, ka, cav, va, cbk, kb, cbv, vb, *consts, gate1_g)


def _two_part_tile(n_first_tiles, first_ref, second_ref):
    return jnp.where(pl.program_id(0) < n_first_tiles, first_ref[...], second_ref[...])


def _router_kernel(n_prompt_tiles, xp_ref, xs_ref, sh_ref, sc_ref, g_ref, wrt_ref, br_ref,
                   h_ref, idx_ref, gate_ref, rank_ref, cnt_ref, carry):
    i = pl.program_id(0)

    @pl.when(i == 0)
    def _():
        carry[...] = jnp.zeros_like(carry)

    tm = xp_ref.shape[0]
    x = _two_part_tile(n_prompt_tiles, xp_ref, xs_ref)
    h = _group_affine(_rms(x) * g_ref[...], 1.0 + sc_ref[...], sh_ref[...])
    h_ref[...] = _pack_halves(h[:, :HALF_D], h[:, HALF_D:])

    logits = lax.dot_general(wrt_ref[...], h, (((1,), (1,)), ((), ())),
                             precision=HIGHEST, preferred_element_type=F32)
    scores = jax.nn.sigmoid(logits)
    biased = scores + br_ref[...]
    neg = -jnp.inf

    sub = lax.broadcasted_iota(I32, (PER_GROUP, tm), 0).astype(F32)
    gs_rows = []
    for g in range(N_GROUPS):
        xg = biased[g * PER_GROUP:(g + 1) * PER_GROUP]
        m1 = jnp.max(xg, axis=0, keepdims=True)
        i1 = jnp.min(jnp.where(xg == m1, sub, float(PER_GROUP)), axis=0, keepdims=True)
        m2 = jnp.max(jnp.where(sub == i1, neg, xg), axis=0, keepdims=True)
        gs_rows.append(m1 + m2)
    gs = jnp.concatenate(gs_rows, axis=0)

    giota = lax.broadcasted_iota(I32, (N_GROUPS, tm), 0).astype(F32)
    keep = jnp.zeros((N_GROUPS, tm), F32)
    for _ in range(TOPK_GROUPS):
        m = jnp.max(gs, axis=0, keepdims=True)
        gi = jnp.min(jnp.where(gs == m, giota, float(N_GROUPS)), axis=0, keepdims=True)
        hit = giota == gi
        keep = jnp.where(hit, 1.0, keep)
        gs = jnp.where(hit, neg, gs)
    cand = jnp.concatenate(
        [jnp.where(keep[g:g + 1] > 0.0, biased[g * PER_GROUP:(g + 1) * PER_GROUP], neg)
         for g in range(N_GROUPS)], axis=0)

    eiota = lax.broadcasted_iota(I32, (N_EXPERTS, tm), 0).astype(F32)
    idx_rows, gate_rows = [], []
    chosen = jnp.zeros((N_EXPERTS, tm), F32)
    for _ in range(TOP_K):
        m = jnp.max(cand, axis=0, keepdims=True)
        ei = jnp.min(jnp.where(cand == m, eiota, float(N_EXPERTS)), axis=0, keepdims=True)
        sel = eiota == ei
        gate_rows.append(jnp.sum(jnp.where(sel, scores, 0.0), axis=0, keepdims=True))
        idx_rows.append(ei)
        chosen = jnp.where(sel, 1.0, chosen)
        cand = jnp.where(sel, neg, cand)
    gates = jnp.concatenate(gate_rows, axis=0)
    gates = gates / jnp.sum(gates, axis=0, keepdims=True) * ROUTED_SCALE
    gate_ref[...] = gates
    idx_ref[...] = jnp.concatenate(idx_rows, axis=0).astype(I32)

    r_i = lax.broadcasted_iota(I32, (tm, tm), 0)
    c_i = lax.broadcasted_iota(I32, (tm, tm), 1)
    upper = jnp.where(r_i < c_i, 1.0, 0.0).astype(BF16)
    before = jnp.dot(chosen.astype(BF16), upper, preferred_element_type=F32) + carry[...]
    rank_rows = [jnp.sum(jnp.where(eiota == idx_rows[k], before, 0.0), axis=0, keepdims=True)
                 for k in range(TOP_K)]
    rank_ref[...] = jnp.concatenate(rank_rows, axis=0).astype(I32)
    total = carry[...] + jnp.sum(chosen, axis=1, keepdims=True)
    carry[...] = total
    cnt_ref[...] = total


def _router(x1_p, x1_s, shift_g, scale_g, g_pre, w_router_t, b_router_col):
    n = x1_p.shape[0] + x1_s.shape[0]
    nt = n // MOE_TILE
    npt = x1_p.shape[0] // MOE_TILE
    gpt = MOE_TILE // CHUNK
    lane_blk = lambda i: (0, i)
    return pl.pallas_call(
        functools.partial(_router_kernel, npt),
        grid=(nt,),
        in_specs=[pl.BlockSpec((MOE_TILE, D_MODEL), lambda i: (jnp.minimum(i, npt - 1), 0)),
                  pl.BlockSpec((MOE_TILE, D_MODEL), lambda i: (jnp.maximum(i - npt, 0), 0)),
                  pl.BlockSpec((gpt, 1, D_MODEL), lambda i: (i, 0, 0)),
                  pl.BlockSpec((gpt, 1, D_MODEL), lambda i: (i, 0, 0)),
                  pl.BlockSpec((1, D_MODEL), lambda i: (0, 0)),
                  pl.BlockSpec((N_EXPERTS, D_MODEL), lambda i: (0, 0)),
                  pl.BlockSpec((N_EXPERTS, 1), lambda i: (0, 0))],
        out_specs=[pl.BlockSpec((MOE_TILE, HALF_D), lambda i: (i, 0)),
                   pl.BlockSpec((TOP_K, MOE_TILE), lane_blk),
                   pl.BlockSpec((TOP_K, MOE_TILE), lane_blk),
                   pl.BlockSpec((TOP_K, MOE_TILE), lane_blk),
                   pl.BlockSpec((N_EXPERTS, 1), lambda i: (0, 0))],
        out_shape=[jax.ShapeDtypeStruct((n, HALF_D), U32),
                   jax.ShapeDtypeStruct((TOP_K, n), I32),
                   jax.ShapeDtypeStruct((TOP_K, n), F32),
                   jax.ShapeDtypeStruct((TOP_K, n), I32),
                   jax.ShapeDtypeStruct((N_EXPERTS, 1), F32)],
        scratch_shapes=[pltpu.VMEM((N_EXPERTS, 1), F32)],
        compiler_params=_cparams(("arbitrary",)),
        name="router",
    )(x1_p, x1_s, shift_g, scale_g, g_pre, w_router_t, b_router_col)


def _dest_kernel(idx_ref, rank_ref, start_ref, o_ref):
    tm = idx_ref.shape[1]
    eiota = lax.broadcasted_iota(I32, (N_EXPERTS, tm), 0)
    start = start_ref[...]
    rows = [jnp.sum(jnp.where(eiota == idx_ref[k:k + 1, :], start, 0.0), axis=0, keepdims=True)
            for k in range(TOP_K)]
    o_ref[...] = jnp.concatenate(rows, axis=0).astype(I32) + rank_ref[...]


def _dest_rows(idx, rank, start_col):
    n = idx.shape[1]
    nt = n // MOE_TILE
    return pl.pallas_call(
        _dest_kernel,
        grid=(nt,),
        in_specs=[pl.BlockSpec((TOP_K, MOE_TILE), lambda i: (0, i)),
                  pl.BlockSpec((TOP_K, MOE_TILE), lambda i: (0, i)),
                  pl.BlockSpec((N_EXPERTS, 1), lambda i: (0, 0))],
        out_specs=pl.BlockSpec((None, TOP_K, MOE_TILE), lambda i: (i, 0, 0)),
        out_shape=jax.ShapeDtypeStruct((nt, TOP_K, MOE_TILE), I32),
        compiler_params=_cparams(("arbitrary",)),
        name="dest_rows",
    )(idx, rank, start_col)


def _fetch_dest(dest_hbm, dest_s, dsem, tile, slot):
    n = TOP_K * MOE_TILE
    return pltpu.make_async_copy(dest_hbm.at[pl.ds(tile * n, n)], dest_s.at[slot], dsem.at[slot])


def _stage_dest(dest_hbm, dest_s, dsem):
    i = pl.program_id(0)
    nt = pl.num_programs(0)
    slot = i % 2

    @pl.when(i == 0)
    def _():
        _fetch_dest(dest_hbm, dest_s, dsem, 0, 0).start()

    @pl.when(i + 1 < nt)
    def _():
        _fetch_dest(dest_hbm, dest_s, dsem, i + 1, 1 - slot).start()

    _fetch_dest(dest_hbm, dest_s, dsem, i, slot).wait()
    return slot


def _dispatch_kernel(dest_hbm, h_ref, xs_hbm, dest_s, dsem, rsem):
    slot = _stage_dest(dest_hbm, dest_s, dsem)

    def row_copy(t, k, d):
        return pltpu.make_async_copy(h_ref.at[pl.ds(t, 1)], xs_hbm.at[pl.ds(d, 1)], rsem.at[k])

    def body(t, carry):
        for k in range(TOP_K):
            row_copy(t, k, dest_s[slot, k * MOE_TILE + t]).start()
        return carry

    lax.fori_loop(0, MOE_TILE, body, 0)
    for k in range(TOP_K):
        pltpu.make_async_copy(h_ref, xs_hbm.at[pl.ds(0, MOE_TILE)], rsem.at[k]).wait()


def _dispatch(dest_flat, h_packed, n_rows):
    n = h_packed.shape[0]
    nt = n // MOE_TILE
    return pl.pallas_call(
        _dispatch_kernel,
        grid=(nt,),
        in_specs=[pl.BlockSpec(memory_space=pl.ANY),
                  pl.BlockSpec((MOE_TILE, HALF_D), lambda i: (i, 0))],
        out_specs=pl.BlockSpec(memory_space=pl.ANY),
        out_shape=jax.ShapeDtypeStruct((n_rows, HALF_D), U32),
        scratch_shapes=[pltpu.SMEM((2, TOP_K * MOE_TILE), I32),
                        pltpu.SemaphoreType.DMA((2,)),
                        pltpu.SemaphoreType.DMA((TOP_K,))],
        compiler_params=_cparams(("arbitrary",)),
        name="dispatch",
    )(dest_flat, h_packed)


def _expert_kernel(be_ref, bv_ref, na_ref, x_ref, wg_ref, wu_ref, wd_ref, y_ref, wg_s, wu_s, wd_s):
    i = pl.program_id(0)
    prev_e = be_ref[jnp.maximum(i - 1, 0)]
    changed = jnp.logical_or(i == 0, be_ref[i] != prev_e)

    @pl.when(changed)
    def _():
        wg_s[...] = wg_ref[...].astype(BF16)
        wu_s[...] = wu_ref[...].astype(BF16)
        wd_s[...] = wd_ref[...].astype(BF16)

    @pl.when(i < na_ref[0])
    def _():
        row = lax.broadcasted_iota(I32, (EXPERT_ROWS, HALF_D), 0)
        u = jnp.where(row < bv_ref[i], x_ref[...], jnp.uint32(0))
        xa, xb = _unpack_halves(u)
        xa = xa.astype(BF16)
        xb = xb.astype(BF16)

        def up(w_s):
            return (jnp.dot(xa, w_s[:HALF_D, :], preferred_element_type=F32)
                    + jnp.dot(xb, w_s[HALF_D:, :], preferred_element_type=F32))

        g = up(wg_s)
        a = (g * jax.nn.sigmoid(g) * up(wu_s)).astype(BF16)
        y = jnp.dot(a, wd_s[...], preferred_element_type=F32)
        y_ref[...] = _pack_halves(y[:, :HALF_D], y[:, HALF_D:])


def _experts(block_expert, block_valid, n_active, xs, we_gate, we_up, we_down):
    n_rows = xs.shape[0]
    nb = n_rows // EXPERT_ROWS
    xmap = lambda i, be, bv, na: (jnp.minimum(i, na[0] - 1), 0)
    wmap = lambda i, be, bv, na: (be[i], 0, 0)
    return pl.pallas_call(
        _expert_kernel,
        grid_spec=pltpu.PrefetchScalarGridSpec(
            num_scalar_prefetch=3,
            grid=(nb,),
            in_specs=[pl.BlockSpec((EXPERT_ROWS, HALF_D), xmap),
                      pl.BlockSpec((None, D_MODEL, EXPERT_DIM), wmap),
                      pl.BlockSpec((None, D_MODEL, EXPERT_DIM), wmap),
                      pl.BlockSpec((None, EXPERT_DIM, D_MODEL), wmap)],
            out_specs=pl.BlockSpec((EXPERT_ROWS, HALF_D), xmap),
            scratch_shapes=[pltpu.VMEM((D_MODEL, EXPERT_DIM), BF16),
                            pltpu.VMEM((D_MODEL, EXPERT_DIM), BF16),
                            pltpu.VMEM((EXPERT_DIM, D_MODEL), BF16)]),
        out_shape=jax.ShapeDtypeStruct((n_rows, HALF_D), U32),
        compiler_params=_cparams(("arbitrary",)),
        name="experts",
    )(block_expert, block_valid, n_active, xs, we_gate, we_up, we_down)


def _combine_kernel(n_prompt_tiles,
                    dest_hbm, ys_hbm, xp_ref, xs_ref, h_ref, gate_ref, wsg_ref, wsu_ref, wsd_ref, gpost_ref, gate2_ref,
                    yp_ref, ysm_ref, dest_s, buf, dsem, rsem):
    i = pl.program_id(0)
    slot = _stage_dest(dest_hbm, dest_s, dsem)

    def row_copy(t, k, d):
        return pltpu.make_async_copy(ys_hbm.at[pl.ds(d, 1)], buf.at[k, pl.ds(t, 1)], rsem.at[k])

    def body(t, carry):
        for k in range(TOP_K):
            row_copy(t, k, dest_s[slot, k * MOE_TILE + t]).start()
        return carry

    lax.fori_loop(0, MOE_TILE, body, 0)

    ha, hb = _unpack_halves(h_ref[...])
    ha = ha.astype(BF16)
    hb = hb.astype(BF16)

    def up(w_ref):
        return (jnp.dot(ha, w_ref[:HALF_D, :], preferred_element_type=F32)
                + jnp.dot(hb, w_ref[HALF_D:, :], preferred_element_type=F32))

    g = up(wsg_ref)
    a = (g * jax.nn.sigmoid(g) * up(wsu_ref)).astype(BF16)
    shared = jnp.dot(a, wsd_ref[...], preferred_element_type=F32)

    gpad = jnp.concatenate([gate_ref[...], jnp.zeros((LANES - TOP_K, MOE_TILE), F32)], axis=0)
    gcol = gpad.T

    acc_a = jnp.zeros((MOE_TILE, HALF_D), F32)
    acc_b = jnp.zeros((MOE_TILE, HALF_D), F32)
    for k in range(TOP_K):
        pltpu.make_async_copy(ys_hbm.at[pl.ds(0, MOE_TILE)], buf.at[k], rsem.at[k]).wait()
        ya, yb = _unpack_halves(buf[k])
        gk = gcol[:, k:k + 1]
        acc_a = acc_a + ya * gk
        acc_b = acc_b + yb * gk
    f = jnp.concatenate([acc_a, acc_b], axis=1) + shared
    x = _two_part_tile(n_prompt_tiles, xp_ref, xs_ref)
    y = x + _group_affine(_rms(f) * gpost_ref[...], gate2_ref[...], None)

    @pl.when(i < n_prompt_tiles)
    def _():
        yp_ref[...] = y

    @pl.when(i >= n_prompt_tiles)
    def _():
        ysm_ref[...] = y


def _combine(dest_flat, ys, x1_p, x1_s, h_packed, gates, ws_gate, ws_up, ws_down, g_post, gate2_g):
    n_prompt = x1_p.shape[0]
    n = n_prompt + x1_s.shape[0]
    nt = n // MOE_TILE
    npt = n_prompt // MOE_TILE
    gpt = MOE_TILE // CHUNK
    z2 = lambda i: (0, 0)
    return pl.pallas_call(
        functools.partial(_combine_kernel, npt),
        grid=(nt,),
        in_specs=[pl.BlockSpec(memory_space=pl.ANY),
                  pl.BlockSpec(memory_space=pl.ANY),
                  pl.BlockSpec((MOE_TILE, D_MODEL), lambda i: (jnp.minimum(i, npt - 1), 0)),
                  pl.BlockSpec((MOE_TILE, D_MODEL), lambda i: (jnp.maximum(i - npt, 0), 0)),
                  pl.BlockSpec((MOE_TILE, HALF_D), lambda i: (i, 0)),
                  pl.BlockSpec((TOP_K, MOE_TILE), lambda i: (0, i)),
                  pl.BlockSpec((D_MODEL, EXPERT_DIM), z2),
                  pl.BlockSpec((D_MODEL, EXPERT_DIM), z2),
                  pl.BlockSpec((EXPERT_DIM, D_MODEL), z2),
                  pl.BlockSpec((1, D_MODEL), z2),
                  pl.BlockSpec((gpt, 1, D_MODEL), lambda i: (i, 0, 0))],
        out_specs=[pl.BlockSpec((MOE_TILE, D_MODEL), lambda i: (jnp.minimum(i, npt - 1), 0)),
                   pl.BlockSpec((MOE_TILE, D_MODEL), lambda i: (jnp.maximum(i - npt, 0), 0))],
        out_shape=[jax.ShapeDtypeStruct((n_prompt, D_MODEL), F32),
                   jax.ShapeDtypeStruct((n - n_prompt, D_MODEL), F32)],
        scratch_shapes=[pltpu.SMEM((2, TOP_K * MOE_TILE), I32),
                        pltpu.VMEM((TOP_K, MOE_TILE, HALF_D), U32),
                        pltpu.SemaphoreType.DMA((2,)),
                        pltpu.SemaphoreType.DMA((TOP_K,))],
        compiler_params=_cparams(("arbitrary",)),
        name="combine",
    )(dest_flat, ys, x1_p, x1_s, h_packed, gates, ws_gate, ws_up, ws_down, g_post, gate2_g)


def _rope_tables(pos):
    half = HEAD_DIM // 2
    inv = ROPE_THETA ** (-jnp.arange(half, dtype=F32) / half)
    ang = pos.astype(F32)[:, None] * inv[None, :]
    cos, sin = jnp.cos(ang), jnp.sin(ang)
    return jnp.concatenate([cos] * 4, axis=1), jnp.concatenate([-sin, sin, -sin, sin], axis=1)


def _groups(vec_rows, reps):
    return jnp.repeat(vec_rows, reps, axis=0)[:, None, :]


def kernel(x_prompt, x_sample, cache_a_k, cache_a_v, cache_b_k, cache_b_v, c_prompt, c_sample, w_ada, b_ada,
           g_pre_mix, g_post_mix, w_in, rel_bias_a, sinks_b, w_branch_a, w_branch_b, w_out, g_pre_ffn,
           g_post_ffn, w_router, b_router, we_gate, we_up, we_down, ws_gate, ws_up, ws_down):
    assert w_ada.shape[0] == 1, "single layer"
    nb, seq, d = x_prompt.shape
    ns, dec = x_sample.shape[:2]
    assert d == D_MODEL and dec == CHUNK and seq % TOK_TILE == 0 and (ns * dec) == TOK_TILE
    n_p, n_s = nb * seq, ns * dec
    n_all = n_p + n_s
    assert n_all % MOE_TILE == 0 and n_p % MOE_TILE == 0

    c_all = jnp.concatenate([c_prompt, c_sample], axis=0)
    pad = (-c_all.shape[0]) % 8
    c_all = jnp.pad(c_all, ((0, pad), (0, 0)))
    mod = _modulation(c_all, w_ada[0], b_ada[0])
    mod_p, mod_s = mod[:nb], mod[nb:nb + ns]
    cpp = seq // CHUNK

    def part(k):
        return mod_p[:, k * d:(k + 1) * d], mod_s[:, k * d:(k + 1) * d]

    (sh1p, sh1s), (sc1p, sc1s), (g1p, g1s), (sh2p, sh2s), (sc2p, sc2s), (g2p, g2s) = [part(k) for k in range(6)]
    both = lambda p, s: jnp.concatenate([_groups(p, cpp), _groups(s, 1)], axis=0)

    w_in_bf = w_in[0].astype(BF16)
    g_pre = g_pre_mix[0].reshape(1, d)
    cos_p, sin_p = _rope_tables(jnp.arange(seq))
    cos_s, sin_s = _rope_tables(PAST_LEN + jnp.arange(dec))
    cos_s, sin_s = jnp.tile(cos_s, (ns, 1)), jnp.tile(sin_s, (ns, 1))

    xp2 = x_prompt.reshape(n_p, d)
    xs2 = x_sample.reshape(n_s, d)
    outs_p = _inproj(xp2, _groups(sh1p, cpp), _groups(sc1p, cpp), g_pre, w_in_bf, cos_p, sin_p, nb, True)
    outs_s = _inproj(xs2, _groups(sh1s, 1), _groups(sc1s, 1), g_pre, w_in_bf, cos_s, sin_s, ns, False)

    dist = (A_ROWS + jnp.arange(CHUNK))[:, None] - jnp.arange(A_BAND)[None, :]
    bias = rel_bias_a[0].astype(F32)[:, jnp.clip(dist, -REL_CLIP, REL_CLIP) + REL_CLIP]
    bias_pairs = bias.reshape(A_HEADS // 2, 2 * CHUNK, A_BAND)
    sink_rows = jnp.repeat(sinks_b[0].astype(F32).reshape(B_KV_HEADS, B_GROUP), CHUNK, axis=1)[:, :, None]
    consts = (bias_pairs, sink_rows, w_branch_a[0].astype(BF16), w_branch_b[0].astype(BF16),
              w_out[0].astype(BF16), g_post_mix[0].reshape(1, d))

    x1_p = _attn_prompt(xp2, outs_p[:7], consts, _groups(g1p, cpp), nb)
    caches = (cache_a_k[0].reshape(ns, A_ROWS, WA), cache_a_v[0].reshape(ns, A_ROWS, WA),
              cache_b_k[0].reshape(ns, WINDOW, WKB), cache_b_v[0].reshape(ns, WINDOW, WKB))
    x1_s = _attn_sample(xs2, outs_s[:7], caches, consts, _groups(g1s, 1))

    h_packed, idx, gates, rank, counts = _router(
        x1_p, x1_s, both(sh2p, sh2s), both(sc2p, sc2s), g_pre_ffn[0].reshape(1, d),
        w_router[0].astype(F32).T, b_router[0].astype(F32).reshape(N_EXPERTS, 1))
    n_blocks = (n_all * TOP_K) // EXPERT_ROWS + N_EXPERTS
    cnt = counts[:, 0].astype(I32)
    blocks_e = (cnt + EXPERT_ROWS - 1) // EXPERT_ROWS
    blk_end = jnp.cumsum(blocks_e)
    blk_start = blk_end - blocks_e
    n_active = blk_end[-1:]
    bid = jnp.minimum(jnp.arange(n_blocks, dtype=I32), n_active[0] - 1)
    block_expert = jnp.minimum(jnp.searchsorted(blk_end, bid, side="right"), N_EXPERTS - 1).astype(I32)
    block_valid = jnp.clip(cnt[block_expert] - (bid - blk_start[block_expert]) * EXPERT_ROWS, 0, EXPERT_ROWS)
    start_col = (blk_start * EXPERT_ROWS).astype(F32).reshape(N_EXPERTS, 1)

    dest_flat = _dest_rows(idx, rank, start_col).reshape(-1)
    xs = _dispatch(dest_flat, h_packed, n_blocks * EXPERT_ROWS)
    ys = _experts(block_expert, block_valid.astype(I32), n_active.astype(I32), xs, we_gate[0], we_up[0], we_down[0])
    y_p, y_s = _combine(dest_flat, ys, x1_p, x1_s, h_packed, gates, ws_gate[0].astype(BF16),
                        ws_up[0].astype(BF16), ws_down[0].astype(BF16), g_post_ffn[0].reshape(1, d),
                        both(g2p, g2s))

    a_heads = (A_HEADS, HEAD_DIM)
    b_heads = (B_KV_HEADS, HEAD_DIM)
    return (y_p.reshape(nb, seq, d), y_s.reshape(ns, dec, d),
            outs_p[7].reshape(1, nb, A_ROWS, *a_heads), outs_p[8].reshape(1, nb, A_ROWS, *a_heads),
            outs_p[9].reshape(1, nb, WINDOW, *b_heads), outs_p[10].reshape(1, nb, WINDOW, *b_heads),
            outs_s[7].reshape(1, ns, dec, *a_heads), outs_s[8].reshape(1, ns, dec, *a_heads),
            outs_s[9].reshape(1, ns, dec, *b_heads), outs_s[10].reshape(1, ns, dec, *b_heads))
```

```python
import functools

import jax
import jax.numpy as jnp
from jax import lax
from jax.experimental import pallas as pl
from jax.experimental.pallas import tpu as pltpu

F32 = jnp.float32
BF16 = jnp.bfloat16
I32 = jnp.int32
U32 = jnp.uint32
HIGHEST = lax.Precision.HIGHEST

D_MODEL = 1024
CHUNK = 64
HEAD_DIM = 64
A_HEADS = 8
A_PREV_CHUNKS = 8
A_ROWS = A_PREV_CHUNKS * CHUNK
A_BAND = A_ROWS + CHUNK
REL_CLIP = 128
B_HEADS = 8
B_KV_HEADS = 2
B_GROUP = B_HEADS // B_KV_HEADS
WINDOW = 128
B_BAND = WINDOW + CHUNK
ROPE_THETA = 10000.0
N_EXPERTS = 256
TOP_K = 8
N_GROUPS = 8
PER_GROUP = N_EXPERTS // N_GROUPS
TOPK_GROUPS = 4
EXPERT_DIM = 256
ROUTED_SCALE = 2.5
EPS = 1e-6
PAST_LEN = 4096

WA = A_HEADS * HEAD_DIM
WB = B_HEADS * HEAD_DIM
WKB = B_KV_HEADS * HEAD_DIM
OFF_QA, OFF_KA, OFF_VA = 0, WA, 2 * WA
OFF_QB = 3 * WA
OFF_KB = OFF_QB + WB
OFF_VB = OFF_KB + WKB
OFF_G = OFF_VB + WKB
N_IN = OFF_G + 2 * D_MODEL

LANES = 128
TOK_TILE = 512
MOE_TILE = 256
EXPERT_ROWS = 256
HALF_D = D_MODEL // 2
VMEM_LIMIT = 56 * 1024 * 1024


def _cparams(sem, vmem=VMEM_LIMIT):
    return pltpu.CompilerParams(dimension_semantics=sem, vmem_limit_bytes=vmem)


def _rms(x):
    return x * lax.rsqrt(jnp.mean(x * x, axis=-1, keepdims=True) + EPS)


def _group_affine(y, mul, add):
    g = mul.shape[0]
    y3 = y.reshape(g, CHUNK, y.shape[-1]) * mul
    if add is not None:
        y3 = y3 + add
    return y3.reshape(g * CHUNK, y.shape[-1])


def _pack_halves(a, b):
    ua = lax.bitcast_convert_type(a.astype(BF16).astype(F32), U32)
    ub = lax.bitcast_convert_type(b.astype(BF16).astype(F32), U32)
    return (ua & jnp.uint32(0xFFFF0000)) | (ub >> 16)


def _unpack_halves(u):
    a = lax.bitcast_convert_type(u & jnp.uint32(0xFFFF0000), F32)
    b = lax.bitcast_convert_type(u << 16, F32)
    return a, b


ROW_SUBLANES = HALF_D // LANES


def _store_packed_rows(ref, x):
    rows = x.shape[0]
    p = _pack_halves(x[:, :HALF_D], x[:, HALF_D:])
    for s in range(ROW_SUBLANES):
        ref[pl.ds(s, rows, stride=ROW_SUBLANES), :] = p[:, s * LANES:(s + 1) * LANES]


def _load_packed_rows(ref, rows, n_valid=None):
    his, los = [], []
    for s in range(ROW_SUBLANES):
        u = ref[pl.ds(s, rows, stride=ROW_SUBLANES), :]
        if n_valid is not None:
            u = jnp.where(lax.broadcasted_iota(I32, u.shape, 0) < n_valid, u, jnp.uint32(0))
        a, b = _unpack_halves(u)
        his.append(a)
        los.append(b)
    return jnp.concatenate(his, axis=1), jnp.concatenate(los, axis=1)


def _mod_kernel(c_ref, w_ref, b_ref, o_ref):
    c = c_ref[...]
    s = c * jax.nn.sigmoid(c)
    o_ref[...] = jnp.dot(s, w_ref[...], precision=HIGHEST, preferred_element_type=F32) + b_ref[...]


def _modulation(c_all, w_ada, b_ada):
    rows = c_all.shape[0]
    n = w_ada.shape[1]
    tn = 512
    return pl.pallas_call(
        _mod_kernel,
        grid=(n // tn,),
        in_specs=[pl.BlockSpec((rows, D_MODEL), lambda j: (0, 0)),
                  pl.BlockSpec((D_MODEL, tn), lambda j: (0, j)),
                  pl.BlockSpec((1, tn), lambda j: (0, j))],
        out_specs=pl.BlockSpec((rows, tn), lambda j: (0, j)),
        out_shape=jax.ShapeDtypeStruct((rows, n), F32),
        compiler_params=_cparams(("arbitrary",)),
        name="modulation",
    )(c_all, w_ada, b_ada.reshape(1, n))


def _rope(x, cos, sin_signed):
    n = x.shape[-1]
    reps = n // LANES
    if reps > 1:
        cos = jnp.concatenate([cos] * reps, axis=1)
        sin_signed = jnp.concatenate([sin_signed] * reps, axis=1)
    lane = lax.broadcasted_iota(I32, x.shape, 1)
    first_half = (lane % HEAD_DIM) < (HEAD_DIM // 2)
    partner = jnp.where(first_half, pltpu.roll(x, n - HEAD_DIM // 2, 1), pltpu.roll(x, HEAD_DIM // 2, 1))
    return x * cos + partner * sin_signed


def _inproj_kernel(prompt_state, tiles_per_seq,
                   x_ref, sh_ref, sc_ref, g_ref, w_ref, cos_ref, sin_ref,
                   qa_ref, ka_ref, va_ref, qb_ref, kb_ref, vb_ref, gt_ref,
                   ska_ref, sva_ref, skb_ref, svb_ref):
    x = x_ref[...]
    h = _group_affine(_rms(x) * g_ref[...], 1.0 + sc_ref[...], sh_ref[...]).astype(BF16)

    def proj(off, width):
        return jnp.dot(h, w_ref[:, off:off + width], preferred_element_type=F32)

    cos = cos_ref[...]
    sin = sin_ref[...]
    scale = HEAD_DIM ** -0.5
    qa_ref[...] = (proj(OFF_QA, WA) * scale).astype(BF16)
    ka = proj(OFF_KA, WA)
    va = proj(OFF_VA, WA)
    ka_ref[...] = ka.astype(BF16)
    va_ref[...] = va.astype(BF16)
    qb_ref[...] = (_rope(proj(OFF_QB, WB), cos, sin) * scale).astype(BF16)
    kb = _rope(proj(OFF_KB, WKB), cos, sin)
    vb = proj(OFF_VB, WKB)
    kb_ref[...] = kb.astype(BF16)
    vb_ref[...] = vb.astype(BF16)
    gt_ref[...] = jax.nn.sigmoid(proj(OFF_G, 2 * D_MODEL)).astype(BF16)

    if prompt_state:
        @pl.when(pl.program_id(0) % tiles_per_seq == tiles_per_seq - 1)
        def _():
            ska_ref[...] = ka
            sva_ref[...] = va
            skb_ref[...] = kb[TOK_TILE - WINDOW:, :]
            svb_ref[...] = vb[TOK_TILE - WINDOW:, :]
    else:
        ska_ref[...] = ka
        sva_ref[...] = va
        skb_ref[...] = kb
        svb_ref[...] = vb


def _inproj(x2d, shift_g, scale_g, g_pre, w_in_bf, cos_tab, sin_tab, n_seq, prompt_state):
    n = x2d.shape[0]
    nt = n // TOK_TILE
    tiles_per_seq = nt // n_seq if prompt_state else 1
    tab_tiles = cos_tab.shape[0] // TOK_TILE
    gpt = TOK_TILE // CHUNK
    row = lambda i: (i, 0)
    grp = lambda i: (i, 0, 0)
    if prompt_state:
        st_shapes = [jax.ShapeDtypeStruct((n_seq, A_ROWS, WA), F32)] * 2 + \
                    [jax.ShapeDtypeStruct((n_seq, WINDOW, WKB), F32)] * 2
        st_specs = [pl.BlockSpec((None, A_ROWS, WA), lambda i: (i // tiles_per_seq, 0, 0))] * 2 + \
                   [pl.BlockSpec((None, WINDOW, WKB), lambda i: (i // tiles_per_seq, 0, 0))] * 2
    else:
        st_shapes = [jax.ShapeDtypeStruct((n, WA), F32)] * 2 + [jax.ShapeDtypeStruct((n, WKB), F32)] * 2
        st_specs = [pl.BlockSpec((TOK_TILE, WA), row)] * 2 + [pl.BlockSpec((TOK_TILE, WKB), row)] * 2
    out_shapes = [jax.ShapeDtypeStruct((n, WA), BF16)] * 4 + [jax.ShapeDtypeStruct((n, WKB), BF16)] * 2 + \
                 [jax.ShapeDtypeStruct((n, 2 * D_MODEL), BF16)]
    out_shapes = [out_shapes[0], out_shapes[1], out_shapes[2], out_shapes[3], out_shapes[4], out_shapes[5],
                  out_shapes[6]] + st_shapes
    out_specs = [pl.BlockSpec((TOK_TILE, WA), row)] * 4 + [pl.BlockSpec((TOK_TILE, WKB), row)] * 2 + \
                [pl.BlockSpec((TOK_TILE, 2 * D_MODEL), row)] + st_specs
    return pl.pallas_call(
        functools.partial(_inproj_kernel, prompt_state, tiles_per_seq),
        grid=(nt,),
        in_specs=[pl.BlockSpec((TOK_TILE, D_MODEL), row),
                  pl.BlockSpec((gpt, 1, D_MODEL), grp),
                  pl.BlockSpec((gpt, 1, D_MODEL), grp),
                  pl.BlockSpec((1, D_MODEL), lambda i: (0, 0)),
                  pl.BlockSpec((D_MODEL, N_IN), lambda i: (0, 0)),
                  pl.BlockSpec((TOK_TILE, LANES), lambda i: (i % tab_tiles, 0)),
                  pl.BlockSpec((TOK_TILE, LANES), lambda i: (i % tab_tiles, 0))],
        out_specs=out_specs,
        out_shape=out_shapes,
        compiler_params=_cparams(("arbitrary",)),
        name="inproj_prompt" if prompt_state else "inproj_sample",
    )(x2d, shift_g, scale_g, g_pre, w_in_bf, cos_tab, sin_tab)


def _attn_kernel(n_chunks, mask_first,
                 x_ref, qa_ref, qb_ref, gt_ref,
                 kap_ref, kac_ref, vap_ref, vac_ref, kbp_ref, kbc_ref, vbp_ref, vbc_ref,
                 bias_ref, sink_ref, wba_ref, wbb_ref, wout_ref, gpost_ref, gate1_ref,
                 o_ref,
                 ka_s, va_s, kb_s, vb_s, oa_s, ob_s):
    rows = n_chunks * CHUNK
    pb = kbp_ref.shape[0]
    ka_s[0:A_ROWS, :] = kap_ref[...].astype(BF16)
    va_s[0:A_ROWS, :] = vap_ref[...].astype(BF16)
    ka_s[A_ROWS:A_ROWS + rows, :] = kac_ref[...]
    va_s[A_ROWS:A_ROWS + rows, :] = vac_ref[...]
    kb_s[0:WINDOW, :] = kbp_ref[pb - WINDOW:pb, :].astype(BF16)
    vb_s[0:WINDOW, :] = vbp_ref[pb - WINDOW:pb, :].astype(BF16)
    kb_s[WINDOW:WINDOW + rows, :] = kbc_ref[...]
    vb_s[WINDOW:WINDOW + rows, :] = vbc_ref[...]

    if mask_first:
        not_first = pl.program_id(1) > 0
    lane_q = lax.broadcasted_iota(I32, (CHUNK, LANES), 1)
    nt_dims = (((1,), (1,)), ((), ()))

    def chunk_body(c, carry):
        c0 = pl.multiple_of(c * CHUNK, CHUNK)
        if mask_first:
            col_a = lax.broadcasted_iota(I32, (1, A_BAND), 1)
            valid_a = jnp.logical_or(not_first, c0 + col_a >= A_ROWS)
            col_b = lax.broadcasted_iota(I32, (1, B_BAND), 1)
            valid_b = jnp.logical_or(not_first, c0 + col_b >= WINDOW)

        scores = []
        for p in range(A_HEADS // 2):
            cols = slice(p * LANES, (p + 1) * LANES)
            q = qa_ref[pl.ds(c0, CHUNK), cols].astype(F32)
            qs = jnp.concatenate([jnp.where(lane_q < HEAD_DIM, q, 0.0),
                                  jnp.where(lane_q >= HEAD_DIM, q, 0.0)], axis=0).astype(BF16)
            k = ka_s[pl.ds(c0, A_BAND), cols]
            s = lax.dot_general(qs, k, nt_dims, preferred_element_type=F32) + bias_ref[p]
            if mask_first:
                s = jnp.where(valid_a, s, -jnp.inf)
            scores.append(s)
        for g in range(B_KV_HEADS):
            parts = []
            for r in range(B_GROUP):
                head = g * B_GROUP + r
                t, half = head // 2, head % 2
                q = qb_ref[pl.ds(c0, CHUNK), t * LANES:(t + 1) * LANES].astype(F32)
                if half != g:
                    q = pltpu.roll(q, HEAD_DIM, 1)
                in_g = (lane_q >= HEAD_DIM) if g else (lane_q < HEAD_DIM)
                parts.append(jnp.where(in_g, q, 0.0))
            qs = jnp.concatenate(parts, axis=0).astype(BF16)
            k = kb_s[pl.ds(c0, B_BAND), :]
            s = lax.dot_general(qs, k, nt_dims, preferred_element_type=F32)
            if mask_first:
                s = jnp.where(valid_b, s, -jnp.inf)
            scores.append(s)

        numer, denom = [], []
        for n, s in enumerate(scores):
            m = jnp.max(s, axis=1, keepdims=True)
            if n >= A_HEADS // 2:
                sk = sink_ref[n - A_HEADS // 2]
                m = jnp.maximum(m, sk)
            e = jnp.exp(s - m)
            l = jnp.sum(e, axis=1, keepdims=True)
            if n >= A_HEADS // 2:
                l = l + jnp.exp(sk - m)
            numer.append(e.astype(BF16))
            denom.append(l)

        outs = []
        for n, e in enumerate(numer):
            if n < A_HEADS // 2:
                v = va_s[pl.ds(c0, A_BAND), n * LANES:(n + 1) * LANES]
            else:
                v = vb_s[pl.ds(c0, B_BAND), :]
            outs.append(jnp.dot(e, v, preferred_element_type=F32) / denom[n])

        for p in range(A_HEADS // 2):
            o = outs[p]
            oa_s[pl.ds(c0, CHUNK), p * LANES:(p + 1) * LANES] = jnp.where(
                lane_q < HEAD_DIM, o[:CHUNK], o[CHUNK:]).astype(BF16)
        for g in range(B_KV_HEADS):
            o = outs[A_HEADS // 2 + g]
            for s2 in range(B_GROUP // 2):
                o_even = o[(2 * s2) * CHUNK:(2 * s2 + 1) * CHUNK]
                o_odd = o[(2 * s2 + 1) * CHUNK:(2 * s2 + 2) * CHUNK]
                if g == 0:
                    tile = jnp.where(lane_q < HEAD_DIM, o_even, pltpu.roll(o_odd, HEAD_DIM, 1))
                else:
                    tile = jnp.where(lane_q < HEAD_DIM, pltpu.roll(o_even, HEAD_DIM, 1), o_odd)
                t = g * (B_GROUP // 2) + s2
                ob_s[pl.ds(c0, CHUNK), t * LANES:(t + 1) * LANES] = tile.astype(BF16)
        return carry

    lax.fori_loop(0, n_chunks, chunk_body, 0)

    za = jnp.dot(oa_s[...], wba_ref[...], preferred_element_type=F32)
    zb = jnp.dot(ob_s[...], wbb_ref[...], preferred_element_type=F32)
    merged = gt_ref[:, :D_MODEL].astype(F32) * za + gt_ref[:, D_MODEL:].astype(F32) * zb
    mo = jnp.dot(merged.astype(BF16), wout_ref[...], preferred_element_type=F32)
    o_ref[...] = x_ref[...] + _group_affine(_rms(mo) * gpost_ref[...], gate1_ref[...], None)


def _attn_scratch(rows):
    return [pltpu.VMEM((A_ROWS + rows, WA), BF16), pltpu.VMEM((A_ROWS + rows, WA), BF16),
            pltpu.VMEM((WINDOW + rows, WKB), BF16), pltpu.VMEM((WINDOW + rows, WKB), BF16),
            pltpu.VMEM((rows, WA), BF16), pltpu.VMEM((rows, WB), BF16)]


def _const_specs(grid_rank):
    z2 = (lambda b, j: (0, 0)) if grid_rank == 2 else (lambda b: (0, 0))
    z3 = (lambda b, j: (0, 0, 0)) if grid_rank == 2 else (lambda b: (0, 0, 0))
    return [pl.BlockSpec((A_HEADS // 2, 2 * CHUNK, A_BAND), z3),
            pl.BlockSpec((B_KV_HEADS, B_GROUP * CHUNK, 1), z3),
            pl.BlockSpec((WA, D_MODEL), z2),
            pl.BlockSpec((WB, D_MODEL), z2),
            pl.BlockSpec((D_MODEL, D_MODEL), z2),
            pl.BlockSpec((1, D_MODEL), z2)]


def _attn_prompt(x2d, proj, consts, gate1_g, n_seq):
    qa, ka, va, qb, kb, vb, gt = proj
    n = x2d.shape[0]
    tps = n // n_seq // TOK_TILE
    gpt = TOK_TILE // CHUNK
    cur = lambda b, j: (b * tps + j, 0)
    prev = lambda b, j: (b * tps + jnp.maximum(j - 1, 0), 0)
    return pl.pallas_call(
        functools.partial(_attn_kernel, TOK_TILE // CHUNK, True),
        grid=(n_seq, tps),
        in_specs=[pl.BlockSpec((TOK_TILE, D_MODEL), cur),
                  pl.BlockSpec((TOK_TILE, WA), cur),
                  pl.BlockSpec((TOK_TILE, WB), cur),
                  pl.BlockSpec((TOK_TILE, 2 * D_MODEL), cur),
                  pl.BlockSpec((TOK_TILE, WA), prev), pl.BlockSpec((TOK_TILE, WA), cur),
                  pl.BlockSpec((TOK_TILE, WA), prev), pl.BlockSpec((TOK_TILE, WA), cur),
                  pl.BlockSpec((TOK_TILE, WKB), prev), pl.BlockSpec((TOK_TILE, WKB), cur),
                  pl.BlockSpec((TOK_TILE, WKB), prev), pl.BlockSpec((TOK_TILE, WKB), cur)]
                 + _const_specs(2)
                 + [pl.BlockSpec((gpt, 1, D_MODEL), lambda b, j: (b * tps + j, 0, 0))],
        out_specs=pl.BlockSpec((TOK_TILE, D_MODEL), cur),
        out_shape=jax.ShapeDtypeStruct((n, D_MODEL), F32),
        scratch_shapes=_attn_scratch(TOK_TILE),
        compiler_params=_cparams(("arbitrary", "arbitrary")),
        name="attn_prompt",
    )(x2d, qa, qb, gt, ka, ka, va, va, kb, kb, vb, vb, *consts, gate1_g)


def _attn_sample(x2d, proj, caches, consts, gate1_g):
    qa, ka, va, qb, kb, vb, gt = proj
    cak, cav, cbk, cbv = caches
    n_seq = cak.shape[0]
    cur = lambda b: (b, 0)
    cache = lambda b: (b, 0, 0)
    return pl.pallas_call(
        functools.partial(_attn_kernel, 1, False),
        grid=(n_seq,),
        in_specs=[pl.BlockSpec((CHUNK, D_MODEL), cur),
                  pl.BlockSpec((CHUNK, WA), cur),
                  pl.BlockSpec((CHUNK, WB), cur),
                  pl.BlockSpec((CHUNK, 2 * D_MODEL), cur),
                  pl.BlockSpec((None, A_ROWS, WA), cache), pl.BlockSpec((CHUNK, WA), cur),
                  pl.BlockSpec((None, A_ROWS, WA), cache), pl.BlockSpec((CHUNK, WA), cur),
                  pl.BlockSpec((None, WINDOW, WKB), cache), pl.BlockSpec((CHUNK, WKB), cur),
                  pl.BlockSpec((None, WINDOW, WKB), cache), pl.BlockSpec((CHUNK, WKB), cur)]
                 + _const_specs(1)
                 + [pl.BlockSpec((1, 1, D_MODEL), lambda b: (b, 0, 0))],
        out_specs=pl.BlockSpec((CHUNK, D_MODEL), cur),
        out_shape=jax.ShapeDtypeStruct(x2d.shape, F32),
        scratch_shapes=_attn_scratch(CHUNK),
        compiler_params=_cparams(("arbitrary",)),
        name="attn_sample",
    )(x2d, qa, qb, gt, cak, ka, cav, va, cbk, kb, cbv, vb, *consts, gate1_g)


def _two_part_tile(n_first_tiles, first_ref, second_ref):
    return jnp.where(pl.program_id(0) < n_first_tiles, first_ref[...], second_ref[...])


def _router_kernel(n_prompt_tiles, xp_ref, xs_ref, sh_ref, sc_ref, g_ref, wrt_ref, br_ref,
                   h_ref, idx_ref, gate_ref, rank_ref, cnt_ref, carry):
    i = pl.program_id(0)

    @pl.when(i == 0)
    def _():
        carry[...] = jnp.zeros_like(carry)

    tm = xp_ref.shape[0]
    x = _two_part_tile(n_prompt_tiles, xp_ref, xs_ref)
    h = _group_affine(_rms(x) * g_ref[...], 1.0 + sc_ref[...], sh_ref[...])
    _store_packed_rows(h_ref, h)

    logits = lax.dot_general(wrt_ref[...], h, (((1,), (1,)), ((), ())),
                             precision=HIGHEST, preferred_element_type=F32)
    scores = jax.nn.sigmoid(logits)
    biased = scores + br_ref[...]
    neg = -jnp.inf

    sub = lax.broadcasted_iota(I32, (PER_GROUP, tm), 0).astype(F32)
    gs_rows = []
    for g in range(N_GROUPS):
        xg = biased[g * PER_GROUP:(g + 1) * PER_GROUP]
        m1 = jnp.max(xg, axis=0, keepdims=True)
        i1 = jnp.min(jnp.where(xg == m1, sub, float(PER_GROUP)), axis=0, keepdims=True)
        m2 = jnp.max(jnp.where(sub == i1, neg, xg), axis=0, keepdims=True)
        gs_rows.append(m1 + m2)
    gs = jnp.concatenate(gs_rows, axis=0)

    giota = lax.broadcasted_iota(I32, (N_GROUPS, tm), 0).astype(F32)
    keep = jnp.zeros((N_GROUPS, tm), F32)
    for _ in range(TOPK_GROUPS):
        m = jnp.max(gs, axis=0, keepdims=True)
        gi = jnp.min(jnp.where(gs == m, giota, float(N_GROUPS)), axis=0, keepdims=True)
        hit = giota == gi
        keep = jnp.where(hit, 1.0, keep)
        gs = jnp.where(hit, neg, gs)
    cand = jnp.concatenate(
        [jnp.where(keep[g:g + 1] > 0.0, biased[g * PER_GROUP:(g + 1) * PER_GROUP], neg)
         for g in range(N_GROUPS)], axis=0)

    eiota = lax.broadcasted_iota(I32, (N_EXPERTS, tm), 0).astype(F32)
    idx_rows, gate_rows = [], []
    chosen = jnp.zeros((N_EXPERTS, tm), F32)
    for _ in range(TOP_K):
        m = jnp.max(cand, axis=0, keepdims=True)
        ei = jnp.min(jnp.where(cand == m, eiota, float(N_EXPERTS)), axis=0, keepdims=True)
        sel = eiota == ei
        gate_rows.append(jnp.sum(jnp.where(sel, scores, 0.0), axis=0, keepdims=True))
        idx_rows.append(ei)
        chosen = jnp.where(sel, 1.0, chosen)
        cand = jnp.where(sel, neg, cand)
    gates = jnp.concatenate(gate_rows, axis=0)
    gates = gates / jnp.sum(gates, axis=0, keepdims=True) * ROUTED_SCALE
    gate_ref[...] = gates
    idx_ref[...] = jnp.concatenate(idx_rows, axis=0).astype(I32)

    r_i = lax.broadcasted_iota(I32, (tm, tm), 0)
    c_i = lax.broadcasted_iota(I32, (tm, tm), 1)
    upper = jnp.where(r_i < c_i, 1.0, 0.0).astype(BF16)
    before = jnp.dot(chosen.astype(BF16), upper, preferred_element_type=F32) + carry[...]
    rank_rows = [jnp.sum(jnp.where(eiota == idx_rows[k], before, 0.0), axis=0, keepdims=True)
                 for k in range(TOP_K)]
    rank_ref[...] = jnp.concatenate(rank_rows, axis=0).astype(I32)
    total = carry[...] + jnp.sum(chosen, axis=1, keepdims=True)
    carry[...] = total
    cnt_ref[...] = total


def _router(x1_p, x1_s, shift_g, scale_g, g_pre, w_router_t, b_router_col):
    n = x1_p.shape[0] + x1_s.shape[0]
    nt = n // MOE_TILE
    npt = x1_p.shape[0] // MOE_TILE
    gpt = MOE_TILE // CHUNK
    lane_blk = lambda i: (0, i)
    return pl.pallas_call(
        functools.partial(_router_kernel, npt),
        grid=(nt,),
        in_specs=[pl.BlockSpec((MOE_TILE, D_MODEL), lambda i: (jnp.minimum(i, npt - 1), 0)),
                  pl.BlockSpec((MOE_TILE, D_MODEL), lambda i: (jnp.maximum(i - npt, 0), 0)),
                  pl.BlockSpec((gpt, 1, D_MODEL), lambda i: (i, 0, 0)),
                  pl.BlockSpec((gpt, 1, D_MODEL), lambda i: (i, 0, 0)),
                  pl.BlockSpec((1, D_MODEL), lambda i: (0, 0)),
                  pl.BlockSpec((N_EXPERTS, D_MODEL), lambda i: (0, 0)),
                  pl.BlockSpec((N_EXPERTS, 1), lambda i: (0, 0))],
        out_specs=[pl.BlockSpec((MOE_TILE * ROW_SUBLANES, LANES), lambda i: (i, 0)),
                   pl.BlockSpec((TOP_K, MOE_TILE), lane_blk),
                   pl.BlockSpec((TOP_K, MOE_TILE), lane_blk),
                   pl.BlockSpec((TOP_K, MOE_TILE), lane_blk),
                   pl.BlockSpec((N_EXPERTS, 1), lambda i: (0, 0))],
        out_shape=[jax.ShapeDtypeStruct((n * ROW_SUBLANES, LANES), U32),
                   jax.ShapeDtypeStruct((TOP_K, n), I32),
                   jax.ShapeDtypeStruct((TOP_K, n), F32),
                   jax.ShapeDtypeStruct((TOP_K, n), I32),
                   jax.ShapeDtypeStruct((N_EXPERTS, 1), F32)],
        scratch_shapes=[pltpu.VMEM((N_EXPERTS, 1), F32)],
        compiler_params=_cparams(("arbitrary",)),
        name="router",
    )(x1_p, x1_s, shift_g, scale_g, g_pre, w_router_t, b_router_col)


def _dest_kernel(idx_ref, rank_ref, start_ref, o_ref):
    tm = idx_ref.shape[1]
    eiota = lax.broadcasted_iota(I32, (N_EXPERTS, tm), 0)
    start = start_ref[...]
    rows = [jnp.sum(jnp.where(eiota == idx_ref[k:k + 1, :], start, 0.0), axis=0, keepdims=True)
            for k in range(TOP_K)]
    o_ref[...] = jnp.concatenate(rows, axis=0).astype(I32) + rank_ref[...]


def _dest_rows(idx, rank, start_col):
    n = idx.shape[1]
    nt = n // MOE_TILE
    return pl.pallas_call(
        _dest_kernel,
        grid=(nt,),
        in_specs=[pl.BlockSpec((TOP_K, MOE_TILE), lambda i: (0, i)),
                  pl.BlockSpec((TOP_K, MOE_TILE), lambda i: (0, i)),
                  pl.BlockSpec((N_EXPERTS, 1), lambda i: (0, 0))],
        out_specs=pl.BlockSpec((None, TOP_K, MOE_TILE), lambda i: (i, 0, 0)),
        out_shape=jax.ShapeDtypeStruct((nt, TOP_K, MOE_TILE), I32),
        compiler_params=_cparams(("arbitrary",)),
        name="dest_rows",
    )(idx, rank, start_col)


DEST_PER_TILE = TOP_K * MOE_TILE


def _fetch_dest(dest_hbm, dest_s, dsem, tile, slot):
    return pltpu.make_async_copy(dest_hbm.at[pl.ds(tile * DEST_PER_TILE, DEST_PER_TILE)],
                                 dest_s.at[pl.ds(slot * DEST_PER_TILE, DEST_PER_TILE)], dsem.at[slot])


def _row_slice(ref, row):
    return ref.at[pl.ds(pl.multiple_of(row * ROW_SUBLANES, ROW_SUBLANES), ROW_SUBLANES)]


def _stage_dest(dest_hbm, dest_s, dsem):
    i = pl.program_id(0)
    nt = pl.num_programs(0)
    slot = i % 2

    @pl.when(i == 0)
    def _():
        _fetch_dest(dest_hbm, dest_s, dsem, 0, 0).start()

    @pl.when(i + 1 < nt)
    def _():
        _fetch_dest(dest_hbm, dest_s, dsem, i + 1, 1 - slot).start()

    _fetch_dest(dest_hbm, dest_s, dsem, i, slot).wait()
    return slot * DEST_PER_TILE


def _dispatch_kernel(dest_hbm, h_ref, xs_hbm, dest_s, dsem, rsem):
    base = _stage_dest(dest_hbm, dest_s, dsem)

    def body(t, carry):
        src = _row_slice(h_ref, t)
        for k in range(TOP_K):
            d = dest_s[base + k * MOE_TILE + t]
            pltpu.make_async_copy(src, _row_slice(xs_hbm, d), rsem.at[k]).start(priority=k % 2)
        return carry

    lax.fori_loop(0, MOE_TILE, body, 0)
    for k in range(TOP_K):
        pltpu.make_async_copy(h_ref, xs_hbm.at[pl.ds(0, MOE_TILE * ROW_SUBLANES)], rsem.at[k]).wait()


def _dispatch(dest_flat, h_packed, n_rows):
    nt = h_packed.shape[0] // (MOE_TILE * ROW_SUBLANES)
    return pl.pallas_call(
        _dispatch_kernel,
        grid=(nt,),
        in_specs=[pl.BlockSpec(memory_space=pl.ANY),
                  pl.BlockSpec((MOE_TILE * ROW_SUBLANES, LANES), lambda i: (i, 0))],
        out_specs=pl.BlockSpec(memory_space=pl.ANY),
        out_shape=jax.ShapeDtypeStruct((n_rows * ROW_SUBLANES, LANES), U32),
        scratch_shapes=[pltpu.SMEM((2 * DEST_PER_TILE,), I32),
                        pltpu.SemaphoreType.DMA((2,)),
                        pltpu.SemaphoreType.DMA((TOP_K,))],
        compiler_params=_cparams(("arbitrary",)),
        name="dispatch",
    )(dest_flat, h_packed)


def _expert_kernel(be_ref, bv_ref, na_ref, x_ref, wg_ref, wu_ref, wd_ref, y_ref, wg_s, wu_s, wd_s):
    i = pl.program_id(0)
    prev_e = be_ref[jnp.maximum(i - 1, 0)]
    changed = jnp.logical_or(i == 0, be_ref[i] != prev_e)

    @pl.when(changed)
    def _():
        wg_s[...] = wg_ref[...].astype(BF16)
        wu_s[...] = wu_ref[...].astype(BF16)
        wd_s[...] = wd_ref[...].astype(BF16)

    @pl.when(i < na_ref[0])
    def _():
        xa, xb = _load_packed_rows(x_ref, EXPERT_ROWS, bv_ref[i])
        xa = xa.astype(BF16)
        xb = xb.astype(BF16)

        def up(w_s):
            return (jnp.dot(xa, w_s[:HALF_D, :], preferred_element_type=F32)
                    + jnp.dot(xb, w_s[HALF_D:, :], preferred_element_type=F32))

        g = up(wg_s)
        a = (g * jax.nn.sigmoid(g) * up(wu_s)).astype(BF16)
        y = jnp.dot(a, wd_s[...], preferred_element_type=F32)
        _store_packed_rows(y_ref, y)


def _experts(block_expert, block_valid, n_active, xs, we_gate, we_up, we_down):
    nb = xs.shape[0] // (EXPERT_ROWS * ROW_SUBLANES)
    xmap = lambda i, be, bv, na: (jnp.minimum(i, na[0] - 1), 0)
    wmap = lambda i, be, bv, na: (be[i], 0, 0)
    return pl.pallas_call(
        _expert_kernel,
        grid_spec=pltpu.PrefetchScalarGridSpec(
            num_scalar_prefetch=3,
            grid=(nb,),
            in_specs=[pl.BlockSpec((EXPERT_ROWS * ROW_SUBLANES, LANES), xmap),
                      pl.BlockSpec((None, D_MODEL, EXPERT_DIM), wmap),
                      pl.BlockSpec((None, D_MODEL, EXPERT_DIM), wmap),
                      pl.BlockSpec((None, EXPERT_DIM, D_MODEL), wmap)],
            out_specs=pl.BlockSpec((EXPERT_ROWS * ROW_SUBLANES, LANES), xmap),
            scratch_shapes=[pltpu.VMEM((D_MODEL, EXPERT_DIM), BF16),
                            pltpu.VMEM((D_MODEL, EXPERT_DIM), BF16),
                            pltpu.VMEM((EXPERT_DIM, D_MODEL), BF16)]),
        out_shape=jax.ShapeDtypeStruct(xs.shape, U32),
        compiler_params=_cparams(("arbitrary",)),
        name="experts",
    )(block_expert, block_valid, n_active, xs, we_gate, we_up, we_down)


def _combine_kernel(n_prompt_tiles,
                    dest_hbm, ys_hbm, xp_ref, xs_ref, h_ref, gate_ref, wsg_ref, wsu_ref, wsd_ref, gpost_ref, gate2_ref,
                    yp_ref, ysm_ref, dest_s, buf, dsem, rsem):
    i = pl.program_id(0)
    base = _stage_dest(dest_hbm, dest_s, dsem)

    def body(t, carry):
        for k in range(TOP_K):
            d = dest_s[base + k * MOE_TILE + t]
            pltpu.make_async_copy(_row_slice(ys_hbm, d), _row_slice(buf.at[k], t), rsem.at[k]).start(priority=k % 2)
        return carry

    lax.fori_loop(0, MOE_TILE, body, 0)

    ha, hb = _load_packed_rows(h_ref, MOE_TILE)
    ha = ha.astype(BF16)
    hb = hb.astype(BF16)

    def up(w_ref):
        return (jnp.dot(ha, w_ref[:HALF_D, :], preferred_element_type=F32)
                + jnp.dot(hb, w_ref[HALF_D:, :], preferred_element_type=F32))

    g = up(wsg_ref)
    a = (g * jax.nn.sigmoid(g) * up(wsu_ref)).astype(BF16)
    shared = jnp.dot(a, wsd_ref[...], preferred_element_type=F32)

    gpad = jnp.concatenate([gate_ref[...], jnp.zeros((LANES - TOP_K, MOE_TILE), F32)], axis=0)
    gcol = gpad.T

    acc_a = jnp.zeros((MOE_TILE, HALF_D), F32)
    acc_b = jnp.zeros((MOE_TILE, HALF_D), F32)
    for k in range(TOP_K):
        pltpu.make_async_copy(ys_hbm.at[pl.ds(0, MOE_TILE * ROW_SUBLANES)], buf.at[k], rsem.at[k]).wait()
        ya, yb = _load_packed_rows(buf.at[k], MOE_TILE)
        gk = gcol[:, k:k + 1]
        acc_a = acc_a + ya * gk
        acc_b = acc_b + yb * gk
    f = jnp.concatenate([acc_a, acc_b], axis=1) + shared
    x = _two_part_tile(n_prompt_tiles, xp_ref, xs_ref)
    y = x + _group_affine(_rms(f) * gpost_ref[...], gate2_ref[...], None)

    @pl.when(i < n_prompt_tiles)
    def _():
        yp_ref[...] = y

    @pl.when(i >= n_prompt_tiles)
    def _():
        ysm_ref[...] = y


def _combine(dest_flat, ys, x1_p, x1_s, h_packed, gates, ws_gate, ws_up, ws_down, g_post, gate2_g):
    n_prompt = x1_p.shape[0]
    n = n_prompt + x1_s.shape[0]
    nt = n // MOE_TILE
    npt = n_prompt // MOE_TILE
    gpt = MOE_TILE // CHUNK
    z2 = lambda i: (0, 0)
    return pl.pallas_call(
        functools.partial(_combine_kernel, npt),
        grid=(nt,),
        in_specs=[pl.BlockSpec(memory_space=pl.ANY),
                  pl.BlockSpec(memory_space=pl.ANY),
                  pl.BlockSpec((MOE_TILE, D_MODEL), lambda i: (jnp.minimum(i, npt - 1), 0)),
                  pl.BlockSpec((MOE_TILE, D_MODEL), lambda i: (jnp.maximum(i - npt, 0), 0)),
                  pl.BlockSpec((MOE_TILE * ROW_SUBLANES, LANES), lambda i: (i, 0)),
                  pl.BlockSpec((TOP_K, MOE_TILE), lambda i: (0, i)),
                  pl.BlockSpec((D_MODEL, EXPERT_DIM), z2),
                  pl.BlockSpec((D_MODEL, EXPERT_DIM), z2),
                  pl.BlockSpec((EXPERT_DIM, D_MODEL), z2),
                  pl.BlockSpec((1, D_MODEL), z2),
                  pl.BlockSpec((gpt, 1, D_MODEL), lambda i: (i, 0, 0))],
        out_specs=[pl.BlockSpec((MOE_TILE, D_MODEL), lambda i: (jnp.minimum(i, npt - 1), 0)),
                   pl.BlockSpec((MOE_TILE, D_MODEL), lambda i: (jnp.maximum(i - npt, 0), 0))],
        out_shape=[jax.ShapeDtypeStruct((n_prompt, D_MODEL), F32),
                   jax.ShapeDtypeStruct((n - n_prompt, D_MODEL), F32)],
        scratch_shapes=[pltpu.SMEM((2 * DEST_PER_TILE,), I32),
                        pltpu.VMEM((TOP_K, MOE_TILE * ROW_SUBLANES, LANES), U32),
                        pltpu.SemaphoreType.DMA((2,)),
                        pltpu.SemaphoreType.DMA((TOP_K,))],
        compiler_params=_cparams(("arbitrary",)),
        name="combine",
    )(dest_flat, ys, x1_p, x1_s, h_packed, gates, ws_gate, ws_up, ws_down, g_post, gate2_g)


def _rope_tables(pos):
    half = HEAD_DIM // 2
    inv = ROPE_THETA ** (-jnp.arange(half, dtype=F32) / half)
    ang = pos.astype(F32)[:, None] * inv[None, :]
    cos, sin = jnp.cos(ang), jnp.sin(ang)
    return jnp.concatenate([cos] * 4, axis=1), jnp.concatenate([-sin, sin, -sin, sin], axis=1)


def _groups(vec_rows, reps):
    rows, width = vec_rows.shape
    return jnp.broadcast_to(vec_rows[:, None, :], (rows, reps, width)).reshape(rows * reps, 1, width)


def kernel(x_prompt, x_sample, cache_a_k, cache_a_v, cache_b_k, cache_b_v, c_prompt, c_sample, w_ada, b_ada,
           g_pre_mix, g_post_mix, w_in, rel_bias_a, sinks_b, w_branch_a, w_branch_b, w_out, g_pre_ffn,
           g_post_ffn, w_router, b_router, we_gate, we_up, we_down, ws_gate, ws_up, ws_down):
    assert w_ada.shape[0] == 1, "single layer"
    nb, seq, d = x_prompt.shape
    ns, dec = x_sample.shape[:2]
    assert d == D_MODEL and dec == CHUNK and seq % TOK_TILE == 0 and (ns * dec) == TOK_TILE
    n_p, n_s = nb * seq, ns * dec
    n_all = n_p + n_s
    assert n_all % MOE_TILE == 0 and n_p % MOE_TILE == 0

    c_all = jnp.concatenate([c_prompt, c_sample], axis=0)
    pad = (-c_all.shape[0]) % 8
    c_all = jnp.pad(c_all, ((0, pad), (0, 0)))
    mod = _modulation(c_all, w_ada[0], b_ada[0])
    mod_p, mod_s = mod[:nb], mod[nb:nb + ns]
    cpp = seq // CHUNK

    def part(k):
        return mod_p[:, k * d:(k + 1) * d], mod_s[:, k * d:(k + 1) * d]

    (sh1p, sh1s), (sc1p, sc1s), (g1p, g1s), (sh2p, sh2s), (sc2p, sc2s), (g2p, g2s) = [part(k) for k in range(6)]
    both = lambda p, s: jnp.concatenate([_groups(p, cpp), _groups(s, 1)], axis=0)

    w_in_bf = w_in[0].astype(BF16)
    g_pre = g_pre_mix[0].reshape(1, d)
    cos_p, sin_p = _rope_tables(jnp.arange(seq))
    cos_s, sin_s = _rope_tables(PAST_LEN + jnp.arange(dec))
    cos_s, sin_s = jnp.tile(cos_s, (ns, 1)), jnp.tile(sin_s, (ns, 1))

    xp2 = x_prompt.reshape(n_p, d)
    xs2 = x_sample.reshape(n_s, d)
    outs_p = _inproj(xp2, _groups(sh1p, cpp), _groups(sc1p, cpp), g_pre, w_in_bf, cos_p, sin_p, nb, True)
    outs_s = _inproj(xs2, _groups(sh1s, 1), _groups(sc1s, 1), g_pre, w_in_bf, cos_s, sin_s, ns, False)

    table = rel_bias_a[0].astype(F32)
    n_far = A_BAND - 1 - REL_CLIP
    ext = jnp.concatenate([jnp.broadcast_to(table[:, 2 * REL_CLIP:], (A_HEADS, n_far)),
                           jnp.flip(table[:, REL_CLIP - (CHUNK - 1):], axis=1)], axis=1)
    bias = jnp.stack([ext[:, CHUNK - 1 - q:CHUNK - 1 - q + A_BAND] for q in range(CHUNK)], axis=1)
    bias_pairs = bias.reshape(A_HEADS // 2, 2 * CHUNK, A_BAND)
    sink_rows = jnp.broadcast_to(sinks_b[0].astype(F32).reshape(B_KV_HEADS, B_GROUP, 1),
                                 (B_KV_HEADS, B_GROUP, CHUNK)).reshape(B_KV_HEADS, B_GROUP * CHUNK, 1)
    consts = (bias_pairs, sink_rows, w_branch_a[0].astype(BF16), w_branch_b[0].astype(BF16),
              w_out[0].astype(BF16), g_post_mix[0].reshape(1, d))

    x1_p = _attn_prompt(xp2, outs_p[:7], consts, _groups(g1p, cpp), nb)
    caches = (cache_a_k[0].reshape(ns, A_ROWS, WA), cache_a_v[0].reshape(ns, A_ROWS, WA),
              cache_b_k[0].reshape(ns, WINDOW, WKB), cache_b_v[0].reshape(ns, WINDOW, WKB))
    x1_s = _attn_sample(xs2, outs_s[:7], caches, consts, _groups(g1s, 1))

    h_packed, idx, gates, rank, counts = _router(
        x1_p, x1_s, both(sh2p, sh2s), both(sc2p, sc2s), g_pre_ffn[0].reshape(1, d),
        w_router[0].astype(F32).T, b_router[0].astype(F32).reshape(N_EXPERTS, 1))
    n_blocks = (n_all * TOP_K) // EXPERT_ROWS + N_EXPERTS
    cnt = counts[:, 0].astype(I32)
    blocks_e = (cnt + EXPERT_ROWS - 1) // EXPERT_ROWS
    blk_end = jnp.cumsum(blocks_e)
    blk_start = blk_end - blocks_e
    n_active = blk_end[-1:]
    bid = jnp.minimum(jnp.arange(n_blocks, dtype=I32), n_active[0] - 1)
    block_expert = jnp.minimum(jnp.sum((blk_end[None, :] <= bid[:, None]).astype(I32), axis=1), N_EXPERTS - 1)
    block_valid = jnp.clip(cnt[block_expert] - (bid - blk_start[block_expert]) * EXPERT_ROWS, 0, EXPERT_ROWS)
    start_col = (blk_start * EXPERT_ROWS).astype(F32).reshape(N_EXPERTS, 1)

    dest_flat = _dest_rows(idx, rank, start_col).reshape(-1)
    xs = _dispatch(dest_flat, h_packed, n_blocks * EXPERT_ROWS)
    ys = _experts(block_expert, block_valid.astype(I32), n_active.astype(I32), xs, we_gate[0], we_up[0], we_down[0])
    y_p, y_s = _combine(dest_flat, ys, x1_p, x1_s, h_packed, gates, ws_gate[0].astype(BF16),
                        ws_up[0].astype(BF16), ws_down[0].astype(BF16), g_post_ffn[0].reshape(1, d),
                        both(g2p, g2s))

    a_heads = (A_HEADS, HEAD_DIM)
    b_heads = (B_KV_HEADS, HEAD_DIM)
    return (y_p.reshape(nb, seq, d), y_s.reshape(ns, dec, d),
            outs_p[7].reshape(1, nb, A_ROWS, *a_heads), outs_p[8].reshape(1, nb, A_ROWS, *a_heads),
            outs_p[9].reshape(1, nb, WINDOW, *b_heads), outs_p[10].reshape(1, nb, WINDOW, *b_heads),
            outs_s[7].reshape(1, ns, dec, *a_heads), outs_s[8].reshape(1, ns, dec, *a_heads),
            outs_s[9].reshape(1, ns, dec, *b_heads), outs_s[10].reshape(1, ns, dec, *b_heads))
```

```python
import functools

import jax
import jax.numpy as jnp
from jax import lax
from jax.experimental import pallas as pl
from jax.experimental.pallas import tpu as pltpu

F32 = jnp.float32
BF16 = jnp.bfloat16
I32 = jnp.int32
U32 = jnp.uint32
HIGHEST = lax.Precision.HIGHEST

D_MODEL = 1024
CHUNK = 64
HEAD_DIM = 64
A_HEADS = 8
A_PREV_CHUNKS = 8
A_ROWS = A_PREV_CHUNKS * CHUNK
A_BAND = A_ROWS + CHUNK
REL_CLIP = 128
B_HEADS = 8
B_KV_HEADS = 2
B_GROUP = B_HEADS // B_KV_HEADS
WINDOW = 128
B_BAND = WINDOW + CHUNK
ROPE_THETA = 10000.0
N_EXPERTS = 256
TOP_K = 8
N_GROUPS = 8
PER_GROUP = N_EXPERTS // N_GROUPS
TOPK_GROUPS = 4
EXPERT_DIM = 256
ROUTED_SCALE = 2.5
EPS = 1e-6
PAST_LEN = 4096

WA = A_HEADS * HEAD_DIM
WB = B_HEADS * HEAD_DIM
WKB = B_KV_HEADS * HEAD_DIM
OFF_QA, OFF_KA, OFF_VA = 0, WA, 2 * WA
OFF_QB = 3 * WA
OFF_KB = OFF_QB + WB
OFF_VB = OFF_KB + WKB
OFF_G = OFF_VB + WKB
N_IN = OFF_G + 2 * D_MODEL

LANES = 128
TOK_TILE = 512
MOE_TILE = 256
EXPERT_ROWS = 256
HALF_D = D_MODEL // 2
VMEM_LIMIT = 56 * 1024 * 1024


def _cparams(sem, vmem=VMEM_LIMIT):
    return pltpu.CompilerParams(dimension_semantics=sem, vmem_limit_bytes=vmem)


def _rms(x):
    return x * lax.rsqrt(jnp.mean(x * x, axis=-1, keepdims=True) + EPS)


def _group_affine(y, mul, add):
    g = mul.shape[0]
    y3 = y.reshape(g, CHUNK, y.shape[-1]) * mul
    if add is not None:
        y3 = y3 + add
    return y3.reshape(g * CHUNK, y.shape[-1])


def _pack_halves(a, b):
    ua = lax.bitcast_convert_type(a.astype(BF16).astype(F32), U32)
    ub = lax.bitcast_convert_type(b.astype(BF16).astype(F32), U32)
    return (ua & jnp.uint32(0xFFFF0000)) | (ub >> 16)


def _unpack_halves(u):
    a = lax.bitcast_convert_type(u & jnp.uint32(0xFFFF0000), F32)
    b = lax.bitcast_convert_type(u << 16, F32)
    return a, b


ROW_SUBLANES = HALF_D // LANES


def _store_packed_rows(ref, x):
    rows = x.shape[0]
    p = _pack_halves(x[:, :HALF_D], x[:, HALF_D:])
    for s in range(ROW_SUBLANES):
        ref[pl.ds(s, rows, stride=ROW_SUBLANES), :] = p[:, s * LANES:(s + 1) * LANES]


def _load_packed_rows(ref, rows, n_valid=None):
    his, los = [], []
    for s in range(ROW_SUBLANES):
        u = ref[pl.ds(s, rows, stride=ROW_SUBLANES), :]
        if n_valid is not None:
            u = jnp.where(lax.broadcasted_iota(I32, u.shape, 0) < n_valid, u, jnp.uint32(0))
        a, b = _unpack_halves(u)
        his.append(a)
        los.append(b)
    return jnp.concatenate(his, axis=1), jnp.concatenate(los, axis=1)


def _mod_kernel(c_ref, w_ref, b_ref, o_ref):
    c = c_ref[...]
    s = c * jax.nn.sigmoid(c)
    o_ref[...] = jnp.dot(s, w_ref[...], precision=HIGHEST, preferred_element_type=F32) + b_ref[...]


def _modulation(c_all, w_ada, b_ada):
    rows = c_all.shape[0]
    n = w_ada.shape[1]
    tn = 512
    return pl.pallas_call(
        _mod_kernel,
        grid=(n // tn,),
        in_specs=[pl.BlockSpec((rows, D_MODEL), lambda j: (0, 0)),
                  pl.BlockSpec((D_MODEL, tn), lambda j: (0, j)),
                  pl.BlockSpec((1, tn), lambda j: (0, j))],
        out_specs=pl.BlockSpec((rows, tn), lambda j: (0, j)),
        out_shape=jax.ShapeDtypeStruct((rows, n), F32),
        compiler_params=_cparams(("arbitrary",)),
        name="modulation",
    )(c_all, w_ada, b_ada.reshape(1, n))


def _rope(x, cos, sin_signed):
    n = x.shape[-1]
    reps = n // LANES
    if reps > 1:
        cos = jnp.concatenate([cos] * reps, axis=1)
        sin_signed = jnp.concatenate([sin_signed] * reps, axis=1)
    lane = lax.broadcasted_iota(I32, x.shape, 1)
    first_half = (lane % HEAD_DIM) < (HEAD_DIM // 2)
    partner = jnp.where(first_half, pltpu.roll(x, n - HEAD_DIM // 2, 1), pltpu.roll(x, HEAD_DIM // 2, 1))
    return x * cos + partner * sin_signed


def _inproj_kernel(prompt_state, tiles_per_seq,
                   x_ref, sh_ref, sc_ref, g_ref, w_ref, cos_ref, sin_ref,
                   qa_ref, ka_ref, va_ref, qb_ref, kb_ref, vb_ref, gt_ref,
                   ska_ref, sva_ref, skb_ref, svb_ref):
    x = x_ref[...]
    h = _group_affine(_rms(x) * g_ref[...], 1.0 + sc_ref[...], sh_ref[...]).astype(BF16)

    def proj(off, width):
        return jnp.dot(h, w_ref[:, off:off + width], preferred_element_type=F32)

    cos = cos_ref[...]
    sin = sin_ref[...]
    scale = HEAD_DIM ** -0.5
    qa_ref[...] = (proj(OFF_QA, WA) * scale).astype(BF16)
    ka = proj(OFF_KA, WA)
    va = proj(OFF_VA, WA)
    ka_ref[...] = ka.astype(BF16)
    va_ref[...] = va.astype(BF16)
    qb_ref[...] = (_rope(proj(OFF_QB, WB), cos, sin) * scale).astype(BF16)
    kb = _rope(proj(OFF_KB, WKB), cos, sin)
    vb = proj(OFF_VB, WKB)
    kb_ref[...] = kb.astype(BF16)
    vb_ref[...] = vb.astype(BF16)
    gt_ref[...] = jax.nn.sigmoid(proj(OFF_G, 2 * D_MODEL)).astype(BF16)

    if prompt_state:
        @pl.when(pl.program_id(0) % tiles_per_seq == tiles_per_seq - 1)
        def _():
            ska_ref[...] = ka
            sva_ref[...] = va
            skb_ref[...] = kb[TOK_TILE - WINDOW:, :]
            svb_ref[...] = vb[TOK_TILE - WINDOW:, :]
    else:
        ska_ref[...] = ka
        sva_ref[...] = va
        skb_ref[...] = kb
        svb_ref[...] = vb


def _inproj(x2d, shift_g, scale_g, g_pre, w_in_bf, cos_tab, sin_tab, n_seq, prompt_state):
    n = x2d.shape[0]
    nt = n // TOK_TILE
    tiles_per_seq = nt // n_seq if prompt_state else 1
    tab_tiles = cos_tab.shape[0] // TOK_TILE
    gpt = TOK_TILE // CHUNK
    row = lambda i: (i, 0)
    grp = lambda i: (i, 0, 0)
    if prompt_state:
        st_shapes = [jax.ShapeDtypeStruct((n_seq, A_ROWS, WA), F32)] * 2 + \
                    [jax.ShapeDtypeStruct((n_seq, WINDOW, WKB), F32)] * 2
        st_specs = [pl.BlockSpec((None, A_ROWS, WA), lambda i: (i // tiles_per_seq, 0, 0))] * 2 + \
                   [pl.BlockSpec((None, WINDOW, WKB), lambda i: (i // tiles_per_seq, 0, 0))] * 2
    else:
        st_shapes = [jax.ShapeDtypeStruct((n, WA), F32)] * 2 + [jax.ShapeDtypeStruct((n, WKB), F32)] * 2
        st_specs = [pl.BlockSpec((TOK_TILE, WA), row)] * 2 + [pl.BlockSpec((TOK_TILE, WKB), row)] * 2
    out_shapes = [jax.ShapeDtypeStruct((n, WA), BF16)] * 4 + [jax.ShapeDtypeStruct((n, WKB), BF16)] * 2 + \
                 [jax.ShapeDtypeStruct((n, 2 * D_MODEL), BF16)]
    out_shapes = [out_shapes[0], out_shapes[1], out_shapes[2], out_shapes[3], out_shapes[4], out_shapes[5],
                  out_shapes[6]] + st_shapes
    out_specs = [pl.BlockSpec((TOK_TILE, WA), row)] * 4 + [pl.BlockSpec((TOK_TILE, WKB), row)] * 2 + \
                [pl.BlockSpec((TOK_TILE, 2 * D_MODEL), row)] + st_specs
    return pl.pallas_call(
        functools.partial(_inproj_kernel, prompt_state, tiles_per_seq),
        grid=(nt,),
        in_specs=[pl.BlockSpec((TOK_TILE, D_MODEL), row),
                  pl.BlockSpec((gpt, 1, D_MODEL), grp),
                  pl.BlockSpec((gpt, 1, D_MODEL), grp),
                  pl.BlockSpec((1, D_MODEL), lambda i: (0, 0)),
                  pl.BlockSpec((D_MODEL, N_IN), lambda i: (0, 0)),
                  pl.BlockSpec((TOK_TILE, LANES), lambda i: (i % tab_tiles, 0)),
                  pl.BlockSpec((TOK_TILE, LANES), lambda i: (i % tab_tiles, 0))],
        out_specs=out_specs,
        out_shape=out_shapes,
        compiler_params=_cparams(("arbitrary",)),
        name="inproj_prompt" if prompt_state else "inproj_sample",
    )(x2d, shift_g, scale_g, g_pre, w_in_bf, cos_tab, sin_tab)


def _attn_kernel(n_chunks, mask_first,
                 x_ref, qa_ref, qb_ref, gt_ref,
                 kap_ref, kac_ref, vap_ref, vac_ref, kbp_ref, kbc_ref, vbp_ref, vbc_ref,
                 bias_ref, sink_ref, wba_ref, wbb_ref, wout_ref, gpost_ref, gate1_ref,
                 o_ref,
                 ka_s, va_s, kb_s, vb_s, oa_s, ob_s):
    rows = n_chunks * CHUNK
    pb = kbp_ref.shape[0]
    ka_s[0:A_ROWS, :] = kap_ref[...].astype(BF16)
    va_s[0:A_ROWS, :] = vap_ref[...].astype(BF16)
    ka_s[A_ROWS:A_ROWS + rows, :] = kac_ref[...]
    va_s[A_ROWS:A_ROWS + rows, :] = vac_ref[...]
    kb_s[0:WINDOW, :] = kbp_ref[pb - WINDOW:pb, :].astype(BF16)
    vb_s[0:WINDOW, :] = vbp_ref[pb - WINDOW:pb, :].astype(BF16)
    kb_s[WINDOW:WINDOW + rows, :] = kbc_ref[...]
    vb_s[WINDOW:WINDOW + rows, :] = vbc_ref[...]

    if mask_first:
        not_first = pl.program_id(1) > 0
    lane_q = lax.broadcasted_iota(I32, (CHUNK, LANES), 1)
    nt_dims = (((1,), (1,)), ((), ()))

    def chunk_body(c, carry):
        c0 = pl.multiple_of(c * CHUNK, CHUNK)
        if mask_first:
            col_a = lax.broadcasted_iota(I32, (1, A_BAND), 1)
            valid_a = jnp.logical_or(not_first, c0 + col_a >= A_ROWS)
            col_b = lax.broadcasted_iota(I32, (1, B_BAND), 1)
            valid_b = jnp.logical_or(not_first, c0 + col_b >= WINDOW)

        scores = []
        for p in range(A_HEADS // 2):
            cols = slice(p * LANES, (p + 1) * LANES)
            q = qa_ref[pl.ds(c0, CHUNK), cols].astype(F32)
            qs = jnp.concatenate([jnp.where(lane_q < HEAD_DIM, q, 0.0),
                                  jnp.where(lane_q >= HEAD_DIM, q, 0.0)], axis=0).astype(BF16)
            k = ka_s[pl.ds(c0, A_BAND), cols]
            s = lax.dot_general(qs, k, nt_dims, preferred_element_type=F32) + bias_ref[p]
            if mask_first:
                s = jnp.where(valid_a, s, -jnp.inf)
            scores.append(s)
        for g in range(B_KV_HEADS):
            parts = []
            for r in range(B_GROUP):
                head = g * B_GROUP + r
                t, half = head // 2, head % 2
                q = qb_ref[pl.ds(c0, CHUNK), t * LANES:(t + 1) * LANES].astype(F32)
                if half != g:
                    q = pltpu.roll(q, HEAD_DIM, 1)
                in_g = (lane_q >= HEAD_DIM) if g else (lane_q < HEAD_DIM)
                parts.append(jnp.where(in_g, q, 0.0))
            qs = jnp.concatenate(parts, axis=0).astype(BF16)
            k = kb_s[pl.ds(c0, B_BAND), :]
            s = lax.dot_general(qs, k, nt_dims, preferred_element_type=F32)
            if mask_first:
                s = jnp.where(valid_b, s, -jnp.inf)
            scores.append(s)

        numer, denom = [], []
        for n, s in enumerate(scores):
            m = jnp.max(s, axis=1, keepdims=True)
            if n >= A_HEADS // 2:
                sk = sink_ref[n - A_HEADS // 2]
                m = jnp.maximum(m, sk)
            e = jnp.exp(s - m)
            l = jnp.sum(e, axis=1, keepdims=True)
            if n >= A_HEADS // 2:
                l = l + jnp.exp(sk - m)
            numer.append(e.astype(BF16))
            denom.append(l)

        outs = []
        for n, e in enumerate(numer):
            if n < A_HEADS // 2:
                v = va_s[pl.ds(c0, A_BAND), n * LANES:(n + 1) * LANES]
            else:
                v = vb_s[pl.ds(c0, B_BAND), :]
            outs.append(jnp.dot(e, v, preferred_element_type=F32) / denom[n])

        for p in range(A_HEADS // 2):
            o = outs[p]
            oa_s[pl.ds(c0, CHUNK), p * LANES:(p + 1) * LANES] = jnp.where(
                lane_q < HEAD_DIM, o[:CHUNK], o[CHUNK:]).astype(BF16)
        for g in range(B_KV_HEADS):
            o = outs[A_HEADS // 2 + g]
            for s2 in range(B_GROUP // 2):
                o_even = o[(2 * s2) * CHUNK:(2 * s2 + 1) * CHUNK]
                o_odd = o[(2 * s2 + 1) * CHUNK:(2 * s2 + 2) * CHUNK]
                if g == 0:
                    tile = jnp.where(lane_q < HEAD_DIM, o_even, pltpu.roll(o_odd, HEAD_DIM, 1))
                else:
                    tile = jnp.where(lane_q < HEAD_DIM, pltpu.roll(o_even, HEAD_DIM, 1), o_odd)
                t = g * (B_GROUP // 2) + s2
                ob_s[pl.ds(c0, CHUNK), t * LANES:(t + 1) * LANES] = tile.astype(BF16)
        return carry

    lax.fori_loop(0, n_chunks, chunk_body, 0, unroll=2 if n_chunks > 1 else 1)

    za = jnp.dot(oa_s[...], wba_ref[...], preferred_element_type=F32)
    zb = jnp.dot(ob_s[...], wbb_ref[...], preferred_element_type=F32)
    merged = gt_ref[:, :D_MODEL].astype(F32) * za + gt_ref[:, D_MODEL:].astype(F32) * zb
    mo = jnp.dot(merged.astype(BF16), wout_ref[...], preferred_element_type=F32)
    o_ref[...] = x_ref[...] + _group_affine(_rms(mo) * gpost_ref[...], gate1_ref[...], None)


def _attn_scratch(rows):
    return [pltpu.VMEM((A_ROWS + rows, WA), BF16), pltpu.VMEM((A_ROWS + rows, WA), BF16),
            pltpu.VMEM((WINDOW + rows, WKB), BF16), pltpu.VMEM((WINDOW + rows, WKB), BF16),
            pltpu.VMEM((rows, WA), BF16), pltpu.VMEM((rows, WB), BF16)]


def _const_specs(grid_rank):
    z2 = (lambda b, j: (0, 0)) if grid_rank == 2 else (lambda b: (0, 0))
    z3 = (lambda b, j: (0, 0, 0)) if grid_rank == 2 else (lambda b: (0, 0, 0))
    return [pl.BlockSpec((A_HEADS // 2, 2 * CHUNK, A_BAND), z3),
            pl.BlockSpec((B_KV_HEADS, B_GROUP * CHUNK, 1), z3),
            pl.BlockSpec((WA, D_MODEL), z2),
            pl.BlockSpec((WB, D_MODEL), z2),
            pl.BlockSpec((D_MODEL, D_MODEL), z2),
            pl.BlockSpec((1, D_MODEL), z2)]


def _attn_prompt(x2d, proj, consts, gate1_g, n_seq):
    qa, ka, va, qb, kb, vb, gt = proj
    n = x2d.shape[0]
    tps = n // n_seq // TOK_TILE
    gpt = TOK_TILE // CHUNK
    cur = lambda b, j: (b * tps + j, 0)
    prev = lambda b, j: (b * tps + jnp.maximum(j - 1, 0), 0)
    return pl.pallas_call(
        functools.partial(_attn_kernel, TOK_TILE // CHUNK, True),
        grid=(n_seq, tps),
        in_specs=[pl.BlockSpec((TOK_TILE, D_MODEL), cur),
                  pl.BlockSpec((TOK_TILE, WA), cur),
                  pl.BlockSpec((TOK_TILE, WB), cur),
                  pl.BlockSpec((TOK_TILE, 2 * D_MODEL), cur),
                  pl.BlockSpec((TOK_TILE, WA), prev), pl.BlockSpec((TOK_TILE, WA), cur),
                  pl.BlockSpec((TOK_TILE, WA), prev), pl.BlockSpec((TOK_TILE, WA), cur),
                  pl.BlockSpec((TOK_TILE, WKB), prev), pl.BlockSpec((TOK_TILE, WKB), cur),
                  pl.BlockSpec((TOK_TILE, WKB), prev), pl.BlockSpec((TOK_TILE, WKB), cur)]
                 + _const_specs(2)
                 + [pl.BlockSpec((gpt, 1, D_MODEL), lambda b, j: (b * tps + j, 0, 0))],
        out_specs=pl.BlockSpec((TOK_TILE, D_MODEL), cur),
        out_shape=jax.ShapeDtypeStruct((n, D_MODEL), F32),
        scratch_shapes=_attn_scratch(TOK_TILE),
        compiler_params=_cparams(("arbitrary", "arbitrary")),
        name="attn_prompt",
    )(x2d, qa, qb, gt, ka, ka, va, va, kb, kb, vb, vb, *consts, gate1_g)


def _attn_sample(x2d, proj, caches, consts, gate1_g):
    qa, ka, va, qb, kb, vb, gt = proj
    cak, cav, cbk, cbv = caches
    n_seq = cak.shape[0]
    cur = lambda b: (b, 0)
    cache = lambda b: (b, 0, 0)
    return pl.pallas_call(
        functools.partial(_attn_kernel, 1, False),
        grid=(n_seq,),
        in_specs=[pl.BlockSpec((CHUNK, D_MODEL), cur),
                  pl.BlockSpec((CHUNK, WA), cur),
                  pl.BlockSpec((CHUNK, WB), cur),
                  pl.BlockSpec((CHUNK, 2 * D_MODEL), cur),
                  pl.BlockSpec((None, A_ROWS, WA), cache), pl.BlockSpec((CHUNK, WA), cur),
                  pl.BlockSpec((None, A_ROWS, WA), cache), pl.BlockSpec((CHUNK, WA), cur),
                  pl.BlockSpec((None, WINDOW, WKB), cache), pl.BlockSpec((CHUNK, WKB), cur),
                  pl.BlockSpec((None, WINDOW, WKB), cache), pl.BlockSpec((CHUNK, WKB), cur)]
                 + _const_specs(1)
                 + [pl.BlockSpec((1, 1, D_MODEL), lambda b: (b, 0, 0))],
        out_specs=pl.BlockSpec((CHUNK, D_MODEL), cur),
        out_shape=jax.ShapeDtypeStruct(x2d.shape, F32),
        scratch_shapes=_attn_scratch(CHUNK),
        compiler_params=_cparams(("arbitrary",)),
        name="attn_sample",
    )(x2d, qa, qb, gt, cak, ka, cav, va, cbk, kb, cbv, vb, *consts, gate1_g)


def _two_part_tile(n_first_tiles, first_ref, second_ref):
    return jnp.where(pl.program_id(0) < n_first_tiles, first_ref[...], second_ref[...])


def _router_kernel(n_prompt_tiles, xp_ref, xs_ref, sh_ref, sc_ref, g_ref, wrt_ref, br_ref,
                   h_ref, idx_ref, gate_ref, rank_ref, cnt_ref, carry):
    i = pl.program_id(0)

    @pl.when(i == 0)
    def _():
        carry[...] = jnp.zeros_like(carry)

    tm = xp_ref.shape[0]
    x = _two_part_tile(n_prompt_tiles, xp_ref, xs_ref)
    h = _group_affine(_rms(x) * g_ref[...], 1.0 + sc_ref[...], sh_ref[...])
    _store_packed_rows(h_ref, h)

    logits = lax.dot_general(wrt_ref[...], h, (((1,), (1,)), ((), ())),
                             precision=HIGHEST, preferred_element_type=F32)
    scores = jax.nn.sigmoid(logits)
    biased = scores + br_ref[...]
    neg = -jnp.inf

    sub = lax.broadcasted_iota(I32, (PER_GROUP, tm), 0).astype(F32)
    gs_rows = []
    for g in range(N_GROUPS):
        xg = biased[g * PER_GROUP:(g + 1) * PER_GROUP]
        m1 = jnp.max(xg, axis=0, keepdims=True)
        i1 = jnp.min(jnp.where(xg == m1, sub, float(PER_GROUP)), axis=0, keepdims=True)
        m2 = jnp.max(jnp.where(sub == i1, neg, xg), axis=0, keepdims=True)
        gs_rows.append(m1 + m2)
    gs = jnp.concatenate(gs_rows, axis=0)

    giota = lax.broadcasted_iota(I32, (N_GROUPS, tm), 0).astype(F32)
    keep = jnp.zeros((N_GROUPS, tm), F32)
    for _ in range(TOPK_GROUPS):
        m = jnp.max(gs, axis=0, keepdims=True)
        gi = jnp.min(jnp.where(gs == m, giota, float(N_GROUPS)), axis=0, keepdims=True)
        hit = giota == gi
        keep = jnp.where(hit, 1.0, keep)
        gs = jnp.where(hit, neg, gs)
    cand = jnp.concatenate(
        [jnp.where(keep[g:g + 1] > 0.0, biased[g * PER_GROUP:(g + 1) * PER_GROUP], neg)
         for g in range(N_GROUPS)], axis=0)

    eiota = lax.broadcasted_iota(I32, (N_EXPERTS, tm), 0).astype(F32)
    idx_rows, gate_rows = [], []
    chosen = jnp.zeros((N_EXPERTS, tm), F32)
    for _ in range(TOP_K):
        m = jnp.max(cand, axis=0, keepdims=True)
        ei = jnp.min(jnp.where(cand == m, eiota, float(N_EXPERTS)), axis=0, keepdims=True)
        sel = eiota == ei
        gate_rows.append(jnp.sum(jnp.where(sel, scores, 0.0), axis=0, keepdims=True))
        idx_rows.append(ei)
        chosen = jnp.where(sel, 1.0, chosen)
        cand = jnp.where(sel, neg, cand)
    gates = jnp.concatenate(gate_rows, axis=0)
    gates = gates / jnp.sum(gates, axis=0, keepdims=True) * ROUTED_SCALE
    gate_ref[...] = gates
    idx_ref[...] = jnp.concatenate(idx_rows, axis=0).astype(I32)

    r_i = lax.broadcasted_iota(I32, (tm, tm), 0)
    c_i = lax.broadcasted_iota(I32, (tm, tm), 1)
    upper = jnp.where(r_i < c_i, 1.0, 0.0).astype(BF16)
    before = jnp.dot(chosen.astype(BF16), upper, preferred_element_type=F32) + carry[...]
    rank_rows = [jnp.sum(jnp.where(eiota == idx_rows[k], before, 0.0), axis=0, keepdims=True)
                 for k in range(TOP_K)]
    rank_ref[...] = jnp.concatenate(rank_rows, axis=0).astype(I32)
    total = carry[...] + jnp.sum(chosen, axis=1, keepdims=True)
    carry[...] = total
    cnt_ref[...] = total


def _router(x1_p, x1_s, shift_g, scale_g, g_pre, w_router_t, b_router_col):
    n = x1_p.shape[0] + x1_s.shape[0]
    nt = n // MOE_TILE
    npt = x1_p.shape[0] // MOE_TILE
    gpt = MOE_TILE // CHUNK
    lane_blk = lambda i: (0, i)
    return pl.pallas_call(
        functools.partial(_router_kernel, npt),
        grid=(nt,),
        in_specs=[pl.BlockSpec((MOE_TILE, D_MODEL), lambda i: (jnp.minimum(i, npt - 1), 0)),
                  pl.BlockSpec((MOE_TILE, D_MODEL), lambda i: (jnp.maximum(i - npt, 0), 0)),
                  pl.BlockSpec((gpt, 1, D_MODEL), lambda i: (i, 0, 0)),
                  pl.BlockSpec((gpt, 1, D_MODEL), lambda i: (i, 0, 0)),
                  pl.BlockSpec((1, D_MODEL), lambda i: (0, 0)),
                  pl.BlockSpec((N_EXPERTS, D_MODEL), lambda i: (0, 0)),
                  pl.BlockSpec((N_EXPERTS, 1), lambda i: (0, 0))],
        out_specs=[pl.BlockSpec((MOE_TILE * ROW_SUBLANES, LANES), lambda i: (i, 0)),
                   pl.BlockSpec((TOP_K, MOE_TILE), lane_blk),
                   pl.BlockSpec((TOP_K, MOE_TILE), lane_blk),
                   pl.BlockSpec((TOP_K, MOE_TILE), lane_blk),
                   pl.BlockSpec((N_EXPERTS, 1), lambda i: (0, 0))],
        out_shape=[jax.ShapeDtypeStruct((n * ROW_SUBLANES, LANES), U32),
                   jax.ShapeDtypeStruct((TOP_K, n), I32),
                   jax.ShapeDtypeStruct((TOP_K, n), F32),
                   jax.ShapeDtypeStruct((TOP_K, n), I32),
                   jax.ShapeDtypeStruct((N_EXPERTS, 1), F32)],
        scratch_shapes=[pltpu.VMEM((N_EXPERTS, 1), F32)],
        compiler_params=_cparams(("arbitrary",)),
        name="router",
    )(x1_p, x1_s, shift_g, scale_g, g_pre, w_router_t, b_router_col)


def _dest_kernel(idx_ref, rank_ref, start_ref, o_ref):
    eiota = lax.broadcasted_iota(I32, (N_EXPERTS, MOE_TILE), 0)
    start = start_ref[...]
    for sub in range(o_ref.shape[0]):
        cols = slice(sub * MOE_TILE, (sub + 1) * MOE_TILE)
        rows = [jnp.sum(jnp.where(eiota == idx_ref[k:k + 1, cols], start, 0.0), axis=0, keepdims=True)
                for k in range(TOP_K)]
        o_ref[sub] = jnp.concatenate(rows, axis=0).astype(I32) + rank_ref[:, cols]


def _dest_rows(idx, rank, start_col):
    n = idx.shape[1]
    nt = n // MOE_TILE
    per_step = next(c for c in (10, 8, 5, 4, 2, 1) if nt % c == 0)
    return pl.pallas_call(
        _dest_kernel,
        grid=(nt // per_step,),
        in_specs=[pl.BlockSpec((TOP_K, per_step * MOE_TILE), lambda i: (0, i)),
                  pl.BlockSpec((TOP_K, per_step * MOE_TILE), lambda i: (0, i)),
                  pl.BlockSpec((N_EXPERTS, 1), lambda i: (0, 0))],
        out_specs=pl.BlockSpec((per_step, TOP_K, MOE_TILE), lambda i: (i, 0, 0)),
        out_shape=jax.ShapeDtypeStruct((nt, TOP_K, MOE_TILE), I32),
        compiler_params=_cparams(("arbitrary",)),
        name="dest_rows",
    )(idx, rank, start_col)


DEST_PER_TILE = TOP_K * MOE_TILE


def _fetch_dest(dest_hbm, dest_s, dsem, tile, slot):
    return pltpu.make_async_copy(dest_hbm.at[pl.ds(tile * DEST_PER_TILE, DEST_PER_TILE)],
                                 dest_s.at[pl.ds(slot * DEST_PER_TILE, DEST_PER_TILE)], dsem.at[slot])


def _row_slice(ref, row):
    return ref.at[pl.ds(pl.multiple_of(row * ROW_SUBLANES, ROW_SUBLANES), ROW_SUBLANES)]


def _stage_dest(dest_hbm, dest_s, dsem):
    i = pl.program_id(0)
    nt = pl.num_programs(0)
    slot = i % 2

    @pl.when(i == 0)
    def _():
        _fetch_dest(dest_hbm, dest_s, dsem, 0, 0).start()

    @pl.when(i + 1 < nt)
    def _():
        _fetch_dest(dest_hbm, dest_s, dsem, i + 1, 1 - slot).start()

    _fetch_dest(dest_hbm, dest_s, dsem, i, slot).wait()
    return slot * DEST_PER_TILE


def _dispatch_kernel(dest_hbm, h_ref, xs_hbm, dest_s, dsem, rsem):
    base = _stage_dest(dest_hbm, dest_s, dsem)

    def body(t, carry):
        src = _row_slice(h_ref, t)
        for k in range(TOP_K):
            d = dest_s[base + k * MOE_TILE + t]
            pltpu.make_async_copy(src, _row_slice(xs_hbm, d), rsem.at[k]).start(priority=k % 2)
        return carry

    lax.fori_loop(0, MOE_TILE, body, 0, unroll=4)
    for k in range(TOP_K):
        pltpu.make_async_copy(h_ref, xs_hbm.at[pl.ds(0, MOE_TILE * ROW_SUBLANES)], rsem.at[k]).wait()


def _dispatch(dest_flat, h_packed, n_rows):
    nt = h_packed.shape[0] // (MOE_TILE * ROW_SUBLANES)
    return pl.pallas_call(
        _dispatch_kernel,
        grid=(nt,),
        in_specs=[pl.BlockSpec(memory_space=pl.ANY),
                  pl.BlockSpec((MOE_TILE * ROW_SUBLANES, LANES), lambda i: (i, 0))],
        out_specs=pl.BlockSpec(memory_space=pl.ANY),
        out_shape=jax.ShapeDtypeStruct((n_rows * ROW_SUBLANES, LANES), U32),
        scratch_shapes=[pltpu.SMEM((2 * DEST_PER_TILE,), I32),
                        pltpu.SemaphoreType.DMA((2,)),
                        pltpu.SemaphoreType.DMA((TOP_K,))],
        compiler_params=_cparams(("arbitrary",)),
        name="dispatch",
    )(dest_flat, h_packed)


BLOCK_SUBLANES = EXPERT_ROWS * ROW_SUBLANES


def _expert_kernel(first_ref, last_ref, cnt_ref, na_ref, xs_hbm, wg_ref, wu_ref, wd_ref, ys_hbm,
                   wg_s, wu_s, wd_s, xbuf, ybuf, isem, osem):
    e = pl.program_id(0)
    n_active = na_ref[0]
    first, last = first_ref[e], last_ref[e]

    def block_rows(ref, g):
        return ref.at[pl.ds(pl.multiple_of(g * BLOCK_SUBLANES, BLOCK_SUBLANES), BLOCK_SUBLANES)]

    def fetch(g, slot):
        return pltpu.make_async_copy(block_rows(xs_hbm, g), xbuf.at[slot], isem.at[slot])

    def flush(g, slot):
        return pltpu.make_async_copy(ybuf.at[slot], block_rows(ys_hbm, g), osem.at[slot])

    @pl.when(jnp.logical_and(e == 0, n_active > 0))
    def _():
        fetch(0, 0).start()

    @pl.when(last > first)
    def _():
        wg_s[...] = wg_ref[...].astype(BF16)
        wu_s[...] = wu_ref[...].astype(BF16)
        wd_s[...] = wd_ref[...].astype(BF16)

    def body(g, carry):
        slot = g % 2
        fetch(g, slot).wait()

        @pl.when(g + 1 < n_active)
        def _():
            fetch(g + 1, 1 - slot).start()

        @pl.when(g >= 2)
        def _():
            flush(g - 2, slot).wait()

        n_valid = cnt_ref[e] - (g - first) * EXPERT_ROWS
        xa, xb = _load_packed_rows(xbuf.at[slot], EXPERT_ROWS, n_valid)
        xa = xa.astype(BF16)
        xb = xb.astype(BF16)

        def up(w_s):
            return (jnp.dot(xa, w_s[:HALF_D, :], preferred_element_type=F32)
                    + jnp.dot(xb, w_s[HALF_D:, :], preferred_element_type=F32))

        gate = up(wg_s)
        a = (gate * jax.nn.sigmoid(gate) * up(wu_s)).astype(BF16)
        y = jnp.dot(a, wd_s[...], preferred_element_type=F32)
        _store_packed_rows(ybuf.at[slot], y)
        flush(g, slot).start()
        return carry

    lax.fori_loop(first, last, body, 0)

    @pl.when(e == pl.num_programs(0) - 1)
    def _():
        for back in (2, 1):
            @pl.when(n_active >= back)
            def _():
                flush(n_active - back, (n_active - back) % 2).wait()


def _experts(blk_first, blk_last, counts, n_active, xs, we_gate, we_up, we_down):
    wmap = lambda e, *_: (e, 0, 0)
    return pl.pallas_call(
        _expert_kernel,
        grid_spec=pltpu.PrefetchScalarGridSpec(
            num_scalar_prefetch=4,
            grid=(N_EXPERTS,),
            in_specs=[pl.BlockSpec(memory_space=pl.ANY),
                      pl.BlockSpec((None, D_MODEL, EXPERT_DIM), wmap),
                      pl.BlockSpec((None, D_MODEL, EXPERT_DIM), wmap),
                      pl.BlockSpec((None, EXPERT_DIM, D_MODEL), wmap)],
            out_specs=pl.BlockSpec(memory_space=pl.ANY),
            scratch_shapes=[pltpu.VMEM((D_MODEL, EXPERT_DIM), BF16),
                            pltpu.VMEM((D_MODEL, EXPERT_DIM), BF16),
                            pltpu.VMEM((EXPERT_DIM, D_MODEL), BF16),
                            pltpu.VMEM((2, BLOCK_SUBLANES, LANES), U32),
                            pltpu.VMEM((2, BLOCK_SUBLANES, LANES), U32),
                            pltpu.SemaphoreType.DMA((2,)),
                            pltpu.SemaphoreType.DMA((2,))]),
        out_shape=jax.ShapeDtypeStruct(xs.shape, U32),
        compiler_params=_cparams(("arbitrary",)),
        name="experts",
    )(blk_first, blk_last, counts, n_active, xs, we_gate, we_up, we_down)


def _combine_kernel(n_prompt_tiles,
                    dest_hbm, ys_hbm, xp_ref, xs_ref, h_ref, gate_ref, wsg_ref, wsu_ref, wsd_ref, gpost_ref, gate2_ref,
                    yp_ref, ysm_ref, dest_s, buf, dsem, rsem):
    i = pl.program_id(0)
    base = _stage_dest(dest_hbm, dest_s, dsem)

    def body(t, carry):
        for k in range(TOP_K):
            d = dest_s[base + k * MOE_TILE + t]
            pltpu.make_async_copy(_row_slice(ys_hbm, d), _row_slice(buf.at[k], t), rsem.at[k]).start(priority=k % 2)
        return carry

    lax.fori_loop(0, MOE_TILE, body, 0, unroll=4)

    ha, hb = _load_packed_rows(h_ref, MOE_TILE)
    ha = ha.astype(BF16)
    hb = hb.astype(BF16)

    def up(w_ref):
        return (jnp.dot(ha, w_ref[:HALF_D, :], preferred_element_type=F32)
                + jnp.dot(hb, w_ref[HALF_D:, :], preferred_element_type=F32))

    g = up(wsg_ref)
    a = (g * jax.nn.sigmoid(g) * up(wsu_ref)).astype(BF16)
    shared = jnp.dot(a, wsd_ref[...], preferred_element_type=F32)

    gpad = jnp.concatenate([gate_ref[...], jnp.zeros((LANES - TOP_K, MOE_TILE), F32)], axis=0)
    gcol = gpad.T

    acc_a = jnp.zeros((MOE_TILE, HALF_D), F32)
    acc_b = jnp.zeros((MOE_TILE, HALF_D), F32)
    for k in range(TOP_K):
        pltpu.make_async_copy(ys_hbm.at[pl.ds(0, MOE_TILE * ROW_SUBLANES)], buf.at[k], rsem.at[k]).wait()
        ya, yb = _load_packed_rows(buf.at[k], MOE_TILE)
        gk = gcol[:, k:k + 1]
        acc_a = acc_a + ya * gk
        acc_b = acc_b + yb * gk
    f = jnp.concatenate([acc_a, acc_b], axis=1) + shared
    x = _two_part_tile(n_prompt_tiles, xp_ref, xs_ref)
    y = x + _group_affine(_rms(f) * gpost_ref[...], gate2_ref[...], None)

    @pl.when(i < n_prompt_tiles)
    def _():
        yp_ref[...] = y

    @pl.when(i >= n_prompt_tiles)
    def _():
        ysm_ref[...] = y


def _combine(dest_flat, ys, x1_p, x1_s, h_packed, gates, ws_gate, ws_up, ws_down, g_post, gate2_g):
    n_prompt = x1_p.shape[0]
    n = n_prompt + x1_s.shape[0]
    nt = n // MOE_TILE
    npt = n_prompt // MOE_TILE
    gpt = MOE_TILE // CHUNK
    z2 = lambda i: (0, 0)
    return pl.pallas_call(
        functools.partial(_combine_kernel, npt),
        grid=(nt,),
        in_specs=[pl.BlockSpec(memory_space=pl.ANY),
                  pl.BlockSpec(memory_space=pl.ANY),
                  pl.BlockSpec((MOE_TILE, D_MODEL), lambda i: (jnp.minimum(i, npt - 1), 0)),
                  pl.BlockSpec((MOE_TILE, D_MODEL), lambda i: (jnp.maximum(i - npt, 0), 0)),
                  pl.BlockSpec((MOE_TILE * ROW_SUBLANES, LANES), lambda i: (i, 0)),
                  pl.BlockSpec((TOP_K, MOE_TILE), lambda i: (0, i)),
                  pl.BlockSpec((D_MODEL, EXPERT_DIM), z2),
                  pl.BlockSpec((D_MODEL, EXPERT_DIM), z2),
                  pl.BlockSpec((EXPERT_DIM, D_MODEL), z2),
                  pl.BlockSpec((1, D_MODEL), z2),
                  pl.BlockSpec((gpt, 1, D_MODEL), lambda i: (i, 0, 0))],
        out_specs=[pl.BlockSpec((MOE_TILE, D_MODEL), lambda i: (jnp.minimum(i, npt - 1), 0)),
                   pl.BlockSpec((MOE_TILE, D_MODEL), lambda i: (jnp.maximum(i - npt, 0), 0))],
        out_shape=[jax.ShapeDtypeStruct((n_prompt, D_MODEL), F32),
                   jax.ShapeDtypeStruct((n - n_prompt, D_MODEL), F32)],
        scratch_shapes=[pltpu.SMEM((2 * DEST_PER_TILE,), I32),
                        pltpu.VMEM((TOP_K, MOE_TILE * ROW_SUBLANES, LANES), U32),
                        pltpu.SemaphoreType.DMA((2,)),
                        pltpu.SemaphoreType.DMA((TOP_K,))],
        compiler_params=_cparams(("arbitrary",)),
        name="combine",
    )(dest_flat, ys, x1_p, x1_s, h_packed, gates, ws_gate, ws_up, ws_down, g_post, gate2_g)


def _rope_tables(pos):
    half = HEAD_DIM // 2
    inv = ROPE_THETA ** (-jnp.arange(half, dtype=F32) / half)
    ang = pos.astype(F32)[:, None] * inv[None, :]
    cos, sin = jnp.cos(ang), jnp.sin(ang)
    return jnp.concatenate([cos] * 4, axis=1), jnp.concatenate([-sin, sin, -sin, sin], axis=1)


def _groups(vec_rows, reps):
    rows, width = vec_rows.shape
    return jnp.broadcast_to(vec_rows[:, None, :], (rows, reps, width)).reshape(rows * reps, 1, width)


def kernel(x_prompt, x_sample, cache_a_k, cache_a_v, cache_b_k, cache_b_v, c_prompt, c_sample, w_ada, b_ada,
           g_pre_mix, g_post_mix, w_in, rel_bias_a, sinks_b, w_branch_a, w_branch_b, w_out, g_pre_ffn,
           g_post_ffn, w_router, b_router, we_gate, we_up, we_down, ws_gate, ws_up, ws_down):
    assert w_ada.shape[0] == 1, "single layer"
    nb, seq, d = x_prompt.shape
    ns, dec = x_sample.shape[:2]
    assert d == D_MODEL and dec == CHUNK and seq % TOK_TILE == 0 and (ns * dec) == TOK_TILE
    n_p, n_s = nb * seq, ns * dec
    n_all = n_p + n_s
    assert n_all % MOE_TILE == 0 and n_p % MOE_TILE == 0

    c_all = jnp.concatenate([c_prompt, c_sample], axis=0)
    pad = (-c_all.shape[0]) % 8
    c_all = jnp.pad(c_all, ((0, pad), (0, 0)))
    mod = _modulation(c_all, w_ada[0], b_ada[0])
    mod_p, mod_s = mod[:nb], mod[nb:nb + ns]
    cpp = seq // CHUNK

    def part(k):
        return mod_p[:, k * d:(k + 1) * d], mod_s[:, k * d:(k + 1) * d]

    (sh1p, sh1s), (sc1p, sc1s), (g1p, g1s), (sh2p, sh2s), (sc2p, sc2s), (g2p, g2s) = [part(k) for k in range(6)]
    both = lambda p, s: jnp.concatenate([_groups(p, cpp), _groups(s, 1)], axis=0)

    w_in_bf = w_in[0].astype(BF16)
    g_pre = g_pre_mix[0].reshape(1, d)
    cos_p, sin_p = _rope_tables(jnp.arange(seq))
    cos_s, sin_s = _rope_tables(PAST_LEN + jnp.arange(dec))
    cos_s, sin_s = jnp.tile(cos_s, (ns, 1)), jnp.tile(sin_s, (ns, 1))

    xp2 = x_prompt.reshape(n_p, d)
    xs2 = x_sample.reshape(n_s, d)
    outs_p = _inproj(xp2, _groups(sh1p, cpp), _groups(sc1p, cpp), g_pre, w_in_bf, cos_p, sin_p, nb, True)
    outs_s = _inproj(xs2, _groups(sh1s, 1), _groups(sc1s, 1), g_pre, w_in_bf, cos_s, sin_s, ns, False)

    table = rel_bias_a[0].astype(F32)
    n_far = A_BAND - 1 - REL_CLIP
    ext = jnp.concatenate([jnp.broadcast_to(table[:, 2 * REL_CLIP:], (A_HEADS, n_far)),
                           jnp.flip(table[:, REL_CLIP - (CHUNK - 1):], axis=1)], axis=1)
    bias = jnp.stack([ext[:, CHUNK - 1 - q:CHUNK - 1 - q + A_BAND] for q in range(CHUNK)], axis=1)
    bias_pairs = bias.reshape(A_HEADS // 2, 2 * CHUNK, A_BAND)
    sink_rows = jnp.broadcast_to(sinks_b[0].astype(F32).reshape(B_KV_HEADS, B_GROUP, 1),
                                 (B_KV_HEADS, B_GROUP, CHUNK)).reshape(B_KV_HEADS, B_GROUP * CHUNK, 1)
    consts = (bias_pairs, sink_rows, w_branch_a[0].astype(BF16), w_branch_b[0].astype(BF16),
              w_out[0].astype(BF16), g_post_mix[0].reshape(1, d))

    x1_p = _attn_prompt(xp2, outs_p[:7], consts, _groups(g1p, cpp), nb)
    caches = (cache_a_k[0].reshape(ns, A_ROWS, WA), cache_a_v[0].reshape(ns, A_ROWS, WA),
              cache_b_k[0].reshape(ns, WINDOW, WKB), cache_b_v[0].reshape(ns, WINDOW, WKB))
    x1_s = _attn_sample(xs2, outs_s[:7], caches, consts, _groups(g1s, 1))

    h_packed, idx, gates, rank, counts = _router(
        x1_p, x1_s, both(sh2p, sh2s), both(sc2p, sc2s), g_pre_ffn[0].reshape(1, d),
        w_router[0].astype(F32).T, b_router[0].astype(F32).reshape(N_EXPERTS, 1))
    n_blocks = (n_all * TOP_K) // EXPERT_ROWS + N_EXPERTS
    cnt = counts[:, 0].astype(I32)
    blocks_e = (cnt + EXPERT_ROWS - 1) // EXPERT_ROWS
    blk_end = jnp.cumsum(blocks_e)
    blk_start = blk_end - blocks_e
    n_active = blk_end[-1:]
    start_col = (blk_start * EXPERT_ROWS).astype(F32).reshape(N_EXPERTS, 1)

    dest_flat = _dest_rows(idx, rank, start_col).reshape(-1)
    xs = _dispatch(dest_flat, h_packed, n_blocks * EXPERT_ROWS)
    ys = _experts(blk_start.astype(I32), blk_end.astype(I32), cnt, n_active.astype(I32), xs,
                  we_gate[0], we_up[0], we_down[0])
    y_p, y_s = _combine(dest_flat, ys, x1_p, x1_s, h_packed, gates, ws_gate[0].astype(BF16),
                        ws_up[0].astype(BF16), ws_down[0].astype(BF16), g_post_ffn[0].reshape(1, d),
                        both(g2p, g2s))

    a_heads = (A_HEADS, HEAD_DIM)
    b_heads = (B_KV_HEADS, HEAD_DIM)
    return (y_p.reshape(nb, seq, d), y_s.reshape(ns, dec, d),
            outs_p[7].reshape(1, nb, A_ROWS, *a_heads), outs_p[8].reshape(1, nb, A_ROWS, *a_heads),
            outs_p[9].reshape(1, nb, WINDOW, *b_heads), outs_p[10].reshape(1, nb, WINDOW, *b_heads),
            outs_s[7].reshape(1, ns, dec, *a_heads), outs_s[8].reshape(1, ns, dec, *a_heads),
            outs_s[9].reshape(1, ns, dec, *b_heads), outs_s[10].reshape(1, ns, dec, *b_heads))
```

```python
import functools

import jax
import jax.numpy as jnp
from jax import lax
from jax.experimental import pallas as pl
from jax.experimental.pallas import tpu as pltpu

F32 = jnp.float32
BF16 = jnp.bfloat16
I32 = jnp.int32
U32 = jnp.uint32
HIGHEST = lax.Precision.HIGHEST

D_MODEL = 1024
CHUNK = 64
HEAD_DIM = 64
A_HEADS = 8
A_PREV_CHUNKS = 8
A_ROWS = A_PREV_CHUNKS * CHUNK
A_BAND = A_ROWS + CHUNK
REL_CLIP = 128
B_HEADS = 8
B_KV_HEADS = 2
B_GROUP = B_HEADS // B_KV_HEADS
WINDOW = 128
B_BAND = WINDOW + CHUNK
ROPE_THETA = 10000.0
N_EXPERTS = 256
TOP_K = 8
N_GROUPS = 8
PER_GROUP = N_EXPERTS // N_GROUPS
TOPK_GROUPS = 4
EXPERT_DIM = 256
ROUTED_SCALE = 2.5
EPS = 1e-6
PAST_LEN = 4096

WA = A_HEADS * HEAD_DIM
WB = B_HEADS * HEAD_DIM
WKB = B_KV_HEADS * HEAD_DIM
OFF_QA, OFF_KA, OFF_VA = 0, WA, 2 * WA
OFF_QB = 3 * WA
OFF_KB = OFF_QB + WB
OFF_VB = OFF_KB + WKB
OFF_G = OFF_VB + WKB
N_IN = OFF_G + 2 * D_MODEL

LANES = 128
TOK_TILE = 512
MOE_TILE = 256
EXPERT_ROWS = 256
HALF_D = D_MODEL // 2
VMEM_LIMIT = 56 * 1024 * 1024


def _cparams(sem, vmem=VMEM_LIMIT):
    return pltpu.CompilerParams(dimension_semantics=sem, vmem_limit_bytes=vmem)


def _rms(x):
    return x * lax.rsqrt(jnp.mean(x * x, axis=-1, keepdims=True) + EPS)


def _group_affine(y, mul, add):
    g = mul.shape[0]
    y3 = y.reshape(g, CHUNK, y.shape[-1]) * mul
    if add is not None:
        y3 = y3 + add
    return y3.reshape(g * CHUNK, y.shape[-1])


def _pack_halves(a, b):
    ua = lax.bitcast_convert_type(a.astype(BF16).astype(F32), U32)
    ub = lax.bitcast_convert_type(b.astype(BF16).astype(F32), U32)
    return (ua & jnp.uint32(0xFFFF0000)) | (ub >> 16)


def _unpack_halves(u):
    a = lax.bitcast_convert_type(u & jnp.uint32(0xFFFF0000), F32)
    b = lax.bitcast_convert_type(u << 16, F32)
    return a, b


ROW_SUBLANES = HALF_D // LANES


def _store_packed_rows(ref, x):
    rows = x.shape[0]
    p = _pack_halves(x[:, :HALF_D], x[:, HALF_D:])
    for s in range(ROW_SUBLANES):
        ref[pl.ds(s, rows, stride=ROW_SUBLANES), :] = p[:, s * LANES:(s + 1) * LANES]


def _load_packed_rows(ref, rows, n_valid=None):
    his, los = [], []
    for s in range(ROW_SUBLANES):
        u = ref[pl.ds(s, rows, stride=ROW_SUBLANES), :]
        if n_valid is not None:
            u = jnp.where(lax.broadcasted_iota(I32, u.shape, 0) < n_valid, u, jnp.uint32(0))
        a, b = _unpack_halves(u)
        his.append(a)
        los.append(b)
    return jnp.concatenate(his, axis=1), jnp.concatenate(los, axis=1)


def _mod_kernel(c_ref, w_ref, b_ref, o_ref):
    c = c_ref[...]
    s = c * jax.nn.sigmoid(c)
    o_ref[...] = jnp.dot(s, w_ref[...], precision=HIGHEST, preferred_element_type=F32) + b_ref[...]


def _modulation(c_all, w_ada, b_ada):
    rows = c_all.shape[0]
    n = w_ada.shape[1]
    tn = 512
    return pl.pallas_call(
        _mod_kernel,
        grid=(n // tn,),
        in_specs=[pl.BlockSpec((rows, D_MODEL), lambda j: (0, 0)),
                  pl.BlockSpec((D_MODEL, tn), lambda j: (0, j)),
                  pl.BlockSpec((1, tn), lambda j: (0, j))],
        out_specs=pl.BlockSpec((rows, tn), lambda j: (0, j)),
        out_shape=jax.ShapeDtypeStruct((rows, n), F32),
        compiler_params=_cparams(("arbitrary",)),
        name="modulation",
    )(c_all, w_ada, b_ada.reshape(1, n))


def _rope(x, cos, sin_signed):
    n = x.shape[-1]
    reps = n // LANES
    if reps > 1:
        cos = jnp.concatenate([cos] * reps, axis=1)
        sin_signed = jnp.concatenate([sin_signed] * reps, axis=1)
    lane = lax.broadcasted_iota(I32, x.shape, 1)
    first_half = (lane % HEAD_DIM) < (HEAD_DIM // 2)
    partner = jnp.where(first_half, pltpu.roll(x, n - HEAD_DIM // 2, 1), pltpu.roll(x, HEAD_DIM // 2, 1))
    return x * cos + partner * sin_signed


def _inproj_kernel(prompt_state, tiles_per_seq,
                   x_ref, sh_ref, sc_ref, g_ref, w_ref, cos_ref, sin_ref,
                   qa_ref, ka_ref, va_ref, qb_ref, kb_ref, vb_ref, gt_ref,
                   ska_ref, sva_ref, skb_ref, svb_ref):
    x = x_ref[...]
    h = _group_affine(_rms(x) * g_ref[...], 1.0 + sc_ref[...], sh_ref[...]).astype(BF16)

    def proj(off, width):
        return jnp.dot(h, w_ref[:, off:off + width], preferred_element_type=F32)

    cos = cos_ref[...]
    sin = sin_ref[...]
    scale = HEAD_DIM ** -0.5
    qa_ref[...] = (proj(OFF_QA, WA) * scale).astype(BF16)
    ka = proj(OFF_KA, WA)
    va = proj(OFF_VA, WA)
    ka_ref[...] = ka.astype(BF16)
    va_ref[...] = va.astype(BF16)
    qb_ref[...] = (_rope(proj(OFF_QB, WB), cos, sin) * scale).astype(BF16)
    kb = _rope(proj(OFF_KB, WKB), cos, sin)
    vb = proj(OFF_VB, WKB)
    kb_ref[...] = kb.astype(BF16)
    vb_ref[...] = vb.astype(BF16)
    gt_ref[...] = jax.nn.sigmoid(proj(OFF_G, 2 * D_MODEL)).astype(BF16)

    if prompt_state:
        @pl.when(pl.program_id(0) % tiles_per_seq == tiles_per_seq - 1)
        def _():
            ska_ref[...] = ka
            sva_ref[...] = va
            skb_ref[...] = kb[TOK_TILE - WINDOW:, :]
            svb_ref[...] = vb[TOK_TILE - WINDOW:, :]
    else:
        ska_ref[...] = ka
        sva_ref[...] = va
        skb_ref[...] = kb
        svb_ref[...] = vb


def _inproj(x2d, shift_g, scale_g, g_pre, w_in_bf, cos_tab, sin_tab, n_seq, prompt_state):
    n = x2d.shape[0]
    nt = n // TOK_TILE
    tiles_per_seq = nt // n_seq if prompt_state else 1
    tab_tiles = cos_tab.shape[0] // TOK_TILE
    gpt = TOK_TILE // CHUNK
    row = lambda i: (i, 0)
    grp = lambda i: (i, 0, 0)
    if prompt_state:
        st_shapes = [jax.ShapeDtypeStruct((n_seq, A_ROWS, WA), F32)] * 2 + \
                    [jax.ShapeDtypeStruct((n_seq, WINDOW, WKB), F32)] * 2
        st_specs = [pl.BlockSpec((None, A_ROWS, WA), lambda i: (i // tiles_per_seq, 0, 0))] * 2 + \
                   [pl.BlockSpec((None, WINDOW, WKB), lambda i: (i // tiles_per_seq, 0, 0))] * 2
    else:
        st_shapes = [jax.ShapeDtypeStruct((n, WA), F32)] * 2 + [jax.ShapeDtypeStruct((n, WKB), F32)] * 2
        st_specs = [pl.BlockSpec((TOK_TILE, WA), row)] * 2 + [pl.BlockSpec((TOK_TILE, WKB), row)] * 2
    out_shapes = [jax.ShapeDtypeStruct((n, WA), BF16)] * 4 + [jax.ShapeDtypeStruct((n, WKB), BF16)] * 2 + \
                 [jax.ShapeDtypeStruct((n, 2 * D_MODEL), BF16)]
    out_shapes = [out_shapes[0], out_shapes[1], out_shapes[2], out_shapes[3], out_shapes[4], out_shapes[5],
                  out_shapes[6]] + st_shapes
    out_specs = [pl.BlockSpec((TOK_TILE, WA), row)] * 4 + [pl.BlockSpec((TOK_TILE, WKB), row)] * 2 + \
                [pl.BlockSpec((TOK_TILE, 2 * D_MODEL), row)] + st_specs
    return pl.pallas_call(
        functools.partial(_inproj_kernel, prompt_state, tiles_per_seq),
        grid=(nt,),
        in_specs=[pl.BlockSpec((TOK_TILE, D_MODEL), row),
                  pl.BlockSpec((gpt, 1, D_MODEL), grp),
                  pl.BlockSpec((gpt, 1, D_MODEL), grp),
                  pl.BlockSpec((1, D_MODEL), lambda i: (0, 0)),
                  pl.BlockSpec((D_MODEL, N_IN), lambda i: (0, 0)),
                  pl.BlockSpec((TOK_TILE, LANES), lambda i: (i % tab_tiles, 0)),
                  pl.BlockSpec((TOK_TILE, LANES), lambda i: (i % tab_tiles, 0))],
        out_specs=out_specs,
        out_shape=out_shapes,
        compiler_params=_cparams(("arbitrary",)),
        name="inproj_prompt" if prompt_state else "inproj_sample",
    )(x2d, shift_g, scale_g, g_pre, w_in_bf, cos_tab, sin_tab)


def _attn_kernel(n_chunks, mask_first,
                 x_ref, qa_ref, qb_ref, gt_ref,
                 kap_ref, kac_ref, vap_ref, vac_ref, kbp_ref, kbc_ref, vbp_ref, vbc_ref,
                 bias_ref, sink_ref, wba_ref, wbb_ref, wout_ref, gpost_ref, gate1_ref,
                 o_ref,
                 ka_s, va_s, kb_s, vb_s, oa_s, ob_s):
    rows = n_chunks * CHUNK
    pb = kbp_ref.shape[0]
    ka_s[0:A_ROWS, :] = kap_ref[...].astype(BF16)
    va_s[0:A_ROWS, :] = vap_ref[...].astype(BF16)
    ka_s[A_ROWS:A_ROWS + rows, :] = kac_ref[...]
    va_s[A_ROWS:A_ROWS + rows, :] = vac_ref[...]
    kb_s[0:WINDOW, :] = kbp_ref[pb - WINDOW:pb, :].astype(BF16)
    vb_s[0:WINDOW, :] = vbp_ref[pb - WINDOW:pb, :].astype(BF16)
    kb_s[WINDOW:WINDOW + rows, :] = kbc_ref[...]
    vb_s[WINDOW:WINDOW + rows, :] = vbc_ref[...]

    if mask_first:
        not_first = pl.program_id(1) > 0
    lane_q = lax.broadcasted_iota(I32, (CHUNK, LANES), 1)
    nt_dims = (((1,), (1,)), ((), ()))

    def chunk_body(c, carry):
        c0 = pl.multiple_of(c * CHUNK, CHUNK)
        if mask_first:
            col_a = lax.broadcasted_iota(I32, (1, A_BAND), 1)
            valid_a = jnp.logical_or(not_first, c0 + col_a >= A_ROWS)
            col_b = lax.broadcasted_iota(I32, (1, B_BAND), 1)
            valid_b = jnp.logical_or(not_first, c0 + col_b >= WINDOW)

        scores = []
        for p in range(A_HEADS // 2):
            cols = slice(p * LANES, (p + 1) * LANES)
            q = qa_ref[pl.ds(c0, CHUNK), cols].astype(F32)
            qs = jnp.concatenate([jnp.where(lane_q < HEAD_DIM, q, 0.0),
                                  jnp.where(lane_q >= HEAD_DIM, q, 0.0)], axis=0).astype(BF16)
            k = ka_s[pl.ds(c0, A_BAND), cols]
            s = lax.dot_general(qs, k, nt_dims, preferred_element_type=F32) + bias_ref[p]
            if mask_first:
                s = jnp.where(valid_a, s, -jnp.inf)
            scores.append(s)
        for g in range(B_KV_HEADS):
            parts = []
            for r in range(B_GROUP):
                head = g * B_GROUP + r
                t, half = head // 2, head % 2
                q = qb_ref[pl.ds(c0, CHUNK), t * LANES:(t + 1) * LANES].astype(F32)
                if half != g:
                    q = pltpu.roll(q, HEAD_DIM, 1)
                in_g = (lane_q >= HEAD_DIM) if g else (lane_q < HEAD_DIM)
                parts.append(jnp.where(in_g, q, 0.0))
            qs = jnp.concatenate(parts, axis=0).astype(BF16)
            k = kb_s[pl.ds(c0, B_BAND), :]
            s = lax.dot_general(qs, k, nt_dims, preferred_element_type=F32)
            if mask_first:
                s = jnp.where(valid_b, s, -jnp.inf)
            scores.append(s)

        numer, denom = [], []
        for n, s in enumerate(scores):
            m = jnp.max(s, axis=1, keepdims=True)
            if n >= A_HEADS // 2:
                sk = sink_ref[n - A_HEADS // 2]
                m = jnp.maximum(m, sk)
            e = jnp.exp(s - m)
            l = jnp.sum(e, axis=1, keepdims=True)
            if n >= A_HEADS // 2:
                l = l + jnp.exp(sk - m)
            numer.append(e.astype(BF16))
            denom.append(l)

        outs = []
        for n, e in enumerate(numer):
            if n < A_HEADS // 2:
                v = va_s[pl.ds(c0, A_BAND), n * LANES:(n + 1) * LANES]
            else:
                v = vb_s[pl.ds(c0, B_BAND), :]
            outs.append(jnp.dot(e, v, preferred_element_type=F32) / denom[n])

        for p in range(A_HEADS // 2):
            o = outs[p]
            oa_s[pl.ds(c0, CHUNK), p * LANES:(p + 1) * LANES] = jnp.where(
                lane_q < HEAD_DIM, o[:CHUNK], o[CHUNK:]).astype(BF16)
        for g in range(B_KV_HEADS):
            o = outs[A_HEADS // 2 + g]
            for s2 in range(B_GROUP // 2):
                o_even = o[(2 * s2) * CHUNK:(2 * s2 + 1) * CHUNK]
                o_odd = o[(2 * s2 + 1) * CHUNK:(2 * s2 + 2) * CHUNK]
                if g == 0:
                    tile = jnp.where(lane_q < HEAD_DIM, o_even, pltpu.roll(o_odd, HEAD_DIM, 1))
                else:
                    tile = jnp.where(lane_q < HEAD_DIM, pltpu.roll(o_even, HEAD_DIM, 1), o_odd)
                t = g * (B_GROUP // 2) + s2
                ob_s[pl.ds(c0, CHUNK), t * LANES:(t + 1) * LANES] = tile.astype(BF16)
        return carry

    lax.fori_loop(0, n_chunks, chunk_body, 0, unroll=2 if n_chunks > 1 else 1)

    za = jnp.dot(oa_s[...], wba_ref[...], preferred_element_type=F32)
    zb = jnp.dot(ob_s[...], wbb_ref[...], preferred_element_type=F32)
    merged = gt_ref[:, :D_MODEL].astype(F32) * za + gt_ref[:, D_MODEL:].astype(F32) * zb
    mo = jnp.dot(merged.astype(BF16), wout_ref[...], preferred_element_type=F32)
    o_ref[...] = x_ref[...] + _group_affine(_rms(mo) * gpost_ref[...], gate1_ref[...], None)


def _attn_scratch(rows):
    return [pltpu.VMEM((A_ROWS + rows, WA), BF16), pltpu.VMEM((A_ROWS + rows, WA), BF16),
            pltpu.VMEM((WINDOW + rows, WKB), BF16), pltpu.VMEM((WINDOW + rows, WKB), BF16),
            pltpu.VMEM((rows, WA), BF16), pltpu.VMEM((rows, WB), BF16)]


def _const_specs(grid_rank):
    z2 = (lambda b, j: (0, 0)) if grid_rank == 2 else (lambda b: (0, 0))
    z3 = (lambda b, j: (0, 0, 0)) if grid_rank == 2 else (lambda b: (0, 0, 0))
    return [pl.BlockSpec((A_HEADS // 2, 2 * CHUNK, A_BAND), z3),
            pl.BlockSpec((B_KV_HEADS, B_GROUP * CHUNK, 1), z3),
            pl.BlockSpec((WA, D_MODEL), z2),
            pl.BlockSpec((WB, D_MODEL), z2),
            pl.BlockSpec((D_MODEL, D_MODEL), z2),
            pl.BlockSpec((1, D_MODEL), z2)]


def _attn_prompt(x2d, proj, consts, gate1_g, n_seq):
    qa, ka, va, qb, kb, vb, gt = proj
    n = x2d.shape[0]
    tps = n // n_seq // TOK_TILE
    gpt = TOK_TILE // CHUNK
    cur = lambda b, j: (b * tps + j, 0)
    prev = lambda b, j: (b * tps + jnp.maximum(j - 1, 0), 0)
    return pl.pallas_call(
        functools.partial(_attn_kernel, TOK_TILE // CHUNK, True),
        grid=(n_seq, tps),
        in_specs=[pl.BlockSpec((TOK_TILE, D_MODEL), cur),
                  pl.BlockSpec((TOK_TILE, WA), cur),
                  pl.BlockSpec((TOK_TILE, WB), cur),
                  pl.BlockSpec((TOK_TILE, 2 * D_MODEL), cur),
                  pl.BlockSpec((TOK_TILE, WA), prev), pl.BlockSpec((TOK_TILE, WA), cur),
                  pl.BlockSpec((TOK_TILE, WA), prev), pl.BlockSpec((TOK_TILE, WA), cur),
                  pl.BlockSpec((TOK_TILE, WKB), prev), pl.BlockSpec((TOK_TILE, WKB), cur),
                  pl.BlockSpec((TOK_TILE, WKB), prev), pl.BlockSpec((TOK_TILE, WKB), cur)]
                 + _const_specs(2)
                 + [pl.BlockSpec((gpt, 1, D_MODEL), lambda b, j: (b * tps + j, 0, 0))],
        out_specs=pl.BlockSpec((TOK_TILE, D_MODEL), cur),
        out_shape=jax.ShapeDtypeStruct((n, D_MODEL), F32),
        scratch_shapes=_attn_scratch(TOK_TILE),
        compiler_params=_cparams(("arbitrary", "arbitrary")),
        name="attn_prompt",
    )(x2d, qa, qb, gt, ka, ka, va, va, kb, kb, vb, vb, *consts, gate1_g)


def _attn_sample(x2d, proj, caches, consts, gate1_g):
    qa, ka, va, qb, kb, vb, gt = proj
    cak, cav, cbk, cbv = caches
    n_seq = cak.shape[0]
    cur = lambda b: (b, 0)
    cache = lambda b: (b, 0, 0)
    return pl.pallas_call(
        functools.partial(_attn_kernel, 1, False),
        grid=(n_seq,),
        in_specs=[pl.BlockSpec((CHUNK, D_MODEL), cur),
                  pl.BlockSpec((CHUNK, WA), cur),
                  pl.BlockSpec((CHUNK, WB), cur),
                  pl.BlockSpec((CHUNK, 2 * D_MODEL), cur),
                  pl.BlockSpec((None, A_ROWS, WA), cache), pl.BlockSpec((CHUNK, WA), cur),
                  pl.BlockSpec((None, A_ROWS, WA), cache), pl.BlockSpec((CHUNK, WA), cur),
                  pl.BlockSpec((None, WINDOW, WKB), cache), pl.BlockSpec((CHUNK, WKB), cur),
                  pl.BlockSpec((None, WINDOW, WKB), cache), pl.BlockSpec((CHUNK, WKB), cur)]
                 + _const_specs(1)
                 + [pl.BlockSpec((1, 1, D_MODEL), lambda b: (b, 0, 0))],
        out_specs=pl.BlockSpec((CHUNK, D_MODEL), cur),
        out_shape=jax.ShapeDtypeStruct(x2d.shape, F32),
        scratch_shapes=_attn_scratch(CHUNK),
        compiler_params=_cparams(("arbitrary",)),
        name="attn_sample",
    )(x2d, qa, qb, gt, cak, ka, cav, va, cbk, kb, cbv, vb, *consts, gate1_g)


def _two_part_tile(n_first_tiles, first_ref, second_ref):
    return jnp.where(pl.program_id(0) < n_first_tiles, first_ref[...], second_ref[...])


def _router_kernel(n_prompt_tiles, xp_ref, xs_ref, sh_ref, sc_ref, g_ref, wrt_ref, br_ref,
                   h_ref, idx_ref, gate_ref, rank_ref, cnt_ref, carry):
    i = pl.program_id(0)

    @pl.when(i == 0)
    def _():
        carry[...] = jnp.zeros_like(carry)

    tm = xp_ref.shape[0]
    x = _two_part_tile(n_prompt_tiles, xp_ref, xs_ref)
    h = _group_affine(_rms(x) * g_ref[...], 1.0 + sc_ref[...], sh_ref[...])
    _store_packed_rows(h_ref, h)

    logits = lax.dot_general(wrt_ref[...], h, (((1,), (1,)), ((), ())),
                             precision=HIGHEST, preferred_element_type=F32)
    scores = jax.nn.sigmoid(logits)
    biased = scores + br_ref[...]
    neg = -jnp.inf

    sub = lax.broadcasted_iota(I32, (PER_GROUP, tm), 0).astype(F32)
    gs_rows = []
    for g in range(N_GROUPS):
        xg = biased[g * PER_GROUP:(g + 1) * PER_GROUP]
        m1 = jnp.max(xg, axis=0, keepdims=True)
        i1 = jnp.min(jnp.where(xg == m1, sub, float(PER_GROUP)), axis=0, keepdims=True)
        m2 = jnp.max(jnp.where(sub == i1, neg, xg), axis=0, keepdims=True)
        gs_rows.append(m1 + m2)
    gs = jnp.concatenate(gs_rows, axis=0)

    giota = lax.broadcasted_iota(I32, (N_GROUPS, tm), 0).astype(F32)
    keep = jnp.zeros((N_GROUPS, tm), F32)
    for _ in range(TOPK_GROUPS):
        m = jnp.max(gs, axis=0, keepdims=True)
        gi = jnp.min(jnp.where(gs == m, giota, float(N_GROUPS)), axis=0, keepdims=True)
        hit = giota == gi
        keep = jnp.where(hit, 1.0, keep)
        gs = jnp.where(hit, neg, gs)
    cand = jnp.concatenate(
        [jnp.where(keep[g:g + 1] > 0.0, biased[g * PER_GROUP:(g + 1) * PER_GROUP], neg)
         for g in range(N_GROUPS)], axis=0)

    eiota = lax.broadcasted_iota(I32, (N_EXPERTS, tm), 0).astype(F32)
    idx_rows, gate_rows = [], []
    chosen = jnp.zeros((N_EXPERTS, tm), F32)
    for _ in range(TOP_K):
        m = jnp.max(cand, axis=0, keepdims=True)
        ei = jnp.min(jnp.where(cand == m, eiota, float(N_EXPERTS)), axis=0, keepdims=True)
        sel = eiota == ei
        gate_rows.append(jnp.sum(jnp.where(sel, scores, 0.0), axis=0, keepdims=True))
        idx_rows.append(ei)
        chosen = jnp.where(sel, 1.0, chosen)
        cand = jnp.where(sel, neg, cand)
    gates = jnp.concatenate(gate_rows, axis=0)
    gates = gates / jnp.sum(gates, axis=0, keepdims=True) * ROUTED_SCALE
    gate_ref[...] = gates
    idx_ref[...] = jnp.concatenate(idx_rows, axis=0).astype(I32)

    r_i = lax.broadcasted_iota(I32, (tm, tm), 0)
    c_i = lax.broadcasted_iota(I32, (tm, tm), 1)
    upper = jnp.where(r_i < c_i, 1.0, 0.0).astype(BF16)
    before = jnp.dot(chosen.astype(BF16), upper, preferred_element_type=F32) + carry[...]
    rank_rows = [jnp.sum(jnp.where(eiota == idx_rows[k], before, 0.0), axis=0, keepdims=True)
                 for k in range(TOP_K)]
    rank_ref[...] = jnp.concatenate(rank_rows, axis=0).astype(I32)
    total = carry[...] + jnp.sum(chosen, axis=1, keepdims=True)
    carry[...] = total
    cnt_ref[...] = total


def _router(x1_p, x1_s, shift_g, scale_g, g_pre, w_router_t, b_router_col):
    n = x1_p.shape[0] + x1_s.shape[0]
    nt = n // MOE_TILE
    npt = x1_p.shape[0] // MOE_TILE
    gpt = MOE_TILE // CHUNK
    lane_blk = lambda i: (0, i)
    return pl.pallas_call(
        functools.partial(_router_kernel, npt),
        grid=(nt,),
        in_specs=[pl.BlockSpec((MOE_TILE, D_MODEL), lambda i: (jnp.minimum(i, npt - 1), 0)),
                  pl.BlockSpec((MOE_TILE, D_MODEL), lambda i: (jnp.maximum(i - npt, 0), 0)),
                  pl.BlockSpec((gpt, 1, D_MODEL), lambda i: (i, 0, 0)),
                  pl.BlockSpec((gpt, 1, D_MODEL), lambda i: (i, 0, 0)),
                  pl.BlockSpec((1, D_MODEL), lambda i: (0, 0)),
                  pl.BlockSpec((N_EXPERTS, D_MODEL), lambda i: (0, 0)),
                  pl.BlockSpec((N_EXPERTS, 1), lambda i: (0, 0))],
        out_specs=[pl.BlockSpec((MOE_TILE * ROW_SUBLANES, LANES), lambda i: (i, 0)),
                   pl.BlockSpec((TOP_K, MOE_TILE), lane_blk),
                   pl.BlockSpec((TOP_K, MOE_TILE), lane_blk),
                   pl.BlockSpec((TOP_K, MOE_TILE), lane_blk),
                   pl.BlockSpec((N_EXPERTS, 1), lambda i: (0, 0))],
        out_shape=[jax.ShapeDtypeStruct((n * ROW_SUBLANES, LANES), U32),
                   jax.ShapeDtypeStruct((TOP_K, n), I32),
                   jax.ShapeDtypeStruct((TOP_K, n), F32),
                   jax.ShapeDtypeStruct((TOP_K, n), I32),
                   jax.ShapeDtypeStruct((N_EXPERTS, 1), F32)],
        scratch_shapes=[pltpu.VMEM((N_EXPERTS, 1), F32)],
        compiler_params=_cparams(("arbitrary",)),
        name="router",
    )(x1_p, x1_s, shift_g, scale_g, g_pre, w_router_t, b_router_col)


def _dest_kernel(idx_ref, rank_ref, start_ref, o_ref):
    eiota = lax.broadcasted_iota(I32, (N_EXPERTS, MOE_TILE), 0)
    start = start_ref[...]
    for sub in range(o_ref.shape[0]):
        cols = slice(sub * MOE_TILE, (sub + 1) * MOE_TILE)
        rows = [jnp.sum(jnp.where(eiota == idx_ref[k:k + 1, cols], start, 0.0), axis=0, keepdims=True)
                for k in range(TOP_K)]
        o_ref[sub] = jnp.concatenate(rows, axis=0).astype(I32) + rank_ref[:, cols]


def _dest_rows(idx, rank, start_col):
    n = idx.shape[1]
    nt = n // MOE_TILE
    per_step = next(c for c in (10, 8, 5, 4, 2, 1) if nt % c == 0)
    return pl.pallas_call(
        _dest_kernel,
        grid=(nt // per_step,),
        in_specs=[pl.BlockSpec((TOP_K, per_step * MOE_TILE), lambda i: (0, i)),
                  pl.BlockSpec((TOP_K, per_step * MOE_TILE), lambda i: (0, i)),
                  pl.BlockSpec((N_EXPERTS, 1), lambda i: (0, 0))],
        out_specs=pl.BlockSpec((per_step, TOP_K, MOE_TILE), lambda i: (i, 0, 0)),
        out_shape=jax.ShapeDtypeStruct((nt, TOP_K, MOE_TILE), I32),
        compiler_params=_cparams(("arbitrary",)),
        name="dest_rows",
    )(idx, rank, start_col)


DEST_PER_TILE = TOP_K * MOE_TILE


def _row_slice(ref, row):
    return ref.at[pl.ds(pl.multiple_of(row * ROW_SUBLANES, ROW_SUBLANES), ROW_SUBLANES)]


TILE_SUBLANES = MOE_TILE * ROW_SUBLANES


def _dispatch_kernel(dest_hbm, h_hbm, xs_hbm, dest_s, hbuf, dsem, hsem, rsem):
    i = pl.program_id(0)
    nt = pl.num_programs(0)

    def load(tile):
        dst = dest_s.at[pl.ds((tile % 2) * DEST_PER_TILE, DEST_PER_TILE)]
        rows = h_hbm.at[pl.ds(pl.multiple_of(tile * TILE_SUBLANES, TILE_SUBLANES), TILE_SUBLANES)]
        return (pltpu.make_async_copy(dest_hbm.at[pl.ds(tile * DEST_PER_TILE, DEST_PER_TILE)], dst, dsem.at[tile % 2]),
                pltpu.make_async_copy(rows, hbuf.at[tile % 3], hsem.at[tile % 3]))

    def drain(tile):
        for k in range(TOP_K):
            pltpu.make_async_copy(hbuf.at[tile % 3], xs_hbm.at[pl.ds(0, TILE_SUBLANES)],
                                  rsem.at[(tile % 2) * TOP_K + k]).wait()

    @pl.when(i == 0)
    def _():
        for cp in load(0):
            cp.start()

    @pl.when(i + 1 < nt)
    def _():
        for cp in load(i + 1):
            cp.start()

    for cp in load(i):
        cp.wait()
    base = (i % 2) * DEST_PER_TILE
    src_buf = hbuf.at[i % 3]
    sem0 = (i % 2) * TOP_K

    def body(t, carry):
        src = _row_slice(src_buf, t)
        for k in range(TOP_K):
            d = dest_s[base + k * MOE_TILE + t]
            pltpu.make_async_copy(src, _row_slice(xs_hbm, d), rsem.at[sem0 + k]).start(priority=k % 2)
        return carry

    lax.fori_loop(0, MOE_TILE, body, 0, unroll=4)

    @pl.when(i >= 1)
    def _():
        drain(i - 1)

    @pl.when(i == nt - 1)
    def _():
        drain(i)


def _dispatch(dest_flat, h_packed, n_rows):
    nt = h_packed.shape[0] // TILE_SUBLANES
    return pl.pallas_call(
        _dispatch_kernel,
        grid=(nt,),
        in_specs=[pl.BlockSpec(memory_space=pl.ANY),
                  pl.BlockSpec(memory_space=pl.ANY)],
        out_specs=pl.BlockSpec(memory_space=pl.ANY),
        out_shape=jax.ShapeDtypeStruct((n_rows * ROW_SUBLANES, LANES), U32),
        scratch_shapes=[pltpu.SMEM((2 * DEST_PER_TILE,), I32),
                        pltpu.VMEM((3, TILE_SUBLANES, LANES), U32),
                        pltpu.SemaphoreType.DMA((2,)),
                        pltpu.SemaphoreType.DMA((3,)),
                        pltpu.SemaphoreType.DMA((2 * TOP_K,))],
        compiler_params=_cparams(("arbitrary",)),
        name="dispatch",
    )(dest_flat, h_packed)


BLOCK_SUBLANES = EXPERT_ROWS * ROW_SUBLANES
EXPERT_SLOTS = 4


def _expert_kernel(first_ref, last_ref, cnt_ref, na_ref, xs_hbm, wg_ref, wu_ref, wd_ref, ys_hbm,
                   wg_s, wu_s, wd_s, xbuf, ybuf, isem, osem):
    e = pl.program_id(0)
    n_active = na_ref[0]
    first, last = first_ref[e], last_ref[e]

    def block_rows(ref, g):
        return ref.at[pl.ds(pl.multiple_of(g * BLOCK_SUBLANES, BLOCK_SUBLANES), BLOCK_SUBLANES)]

    def fetch(g, slot):
        return pltpu.make_async_copy(block_rows(xs_hbm, g), xbuf.at[slot], isem.at[slot])

    def flush(g, slot):
        return pltpu.make_async_copy(ybuf.at[slot], block_rows(ys_hbm, g), osem.at[slot])

    ahead = EXPERT_SLOTS - 1

    @pl.when(e == 0)
    def _():
        for g0 in range(ahead):
            @pl.when(g0 < n_active)
            def _():
                fetch(g0, g0).start()

    @pl.when(last > first)
    def _():
        wg_s[...] = wg_ref[...].astype(BF16)
        wu_s[...] = wu_ref[...].astype(BF16)
        wd_s[...] = wd_ref[...].astype(BF16)

    def body(g, carry):
        slot = g % EXPERT_SLOTS
        fetch(g, slot).wait()

        @pl.when(g + ahead < n_active)
        def _():
            fetch(g + ahead, (g + ahead) % EXPERT_SLOTS).start()

        @pl.when(g >= EXPERT_SLOTS)
        def _():
            flush(g - EXPERT_SLOTS, slot).wait()

        n_valid = cnt_ref[e] - (g - first) * EXPERT_ROWS
        xa, xb = _load_packed_rows(xbuf.at[slot], EXPERT_ROWS, n_valid)
        xa = xa.astype(BF16)
        xb = xb.astype(BF16)

        def up(w_s):
            return (jnp.dot(xa, w_s[:HALF_D, :], preferred_element_type=F32)
                    + jnp.dot(xb, w_s[HALF_D:, :], preferred_element_type=F32))

        gate = up(wg_s)
        a = (gate * jax.nn.sigmoid(gate) * up(wu_s)).astype(BF16)
        y = jnp.dot(a, wd_s[...], preferred_element_type=F32)
        _store_packed_rows(ybuf.at[slot], y)
        flush(g, slot).start()
        return carry

    lax.fori_loop(first, last, body, 0)

    @pl.when(e == pl.num_programs(0) - 1)
    def _():
        for back in range(EXPERT_SLOTS, 0, -1):
            @pl.when(n_active >= back)
            def _():
                flush(n_active - back, (n_active - back) % EXPERT_SLOTS).wait()


def _experts(blk_first, blk_last, counts, n_active, xs, we_gate, we_up, we_down):
    wmap = lambda e, *_: (e, 0, 0)
    return pl.pallas_call(
        _expert_kernel,
        grid_spec=pltpu.PrefetchScalarGridSpec(
            num_scalar_prefetch=4,
            grid=(N_EXPERTS,),
            in_specs=[pl.BlockSpec(memory_space=pl.ANY),
                      pl.BlockSpec((None, D_MODEL, EXPERT_DIM), wmap),
                      pl.BlockSpec((None, D_MODEL, EXPERT_DIM), wmap),
                      pl.BlockSpec((None, EXPERT_DIM, D_MODEL), wmap)],
            out_specs=pl.BlockSpec(memory_space=pl.ANY),
            scratch_shapes=[pltpu.VMEM((D_MODEL, EXPERT_DIM), BF16),
                            pltpu.VMEM((D_MODEL, EXPERT_DIM), BF16),
                            pltpu.VMEM((EXPERT_DIM, D_MODEL), BF16),
                            pltpu.VMEM((EXPERT_SLOTS, BLOCK_SUBLANES, LANES), U32),
                            pltpu.VMEM((EXPERT_SLOTS, BLOCK_SUBLANES, LANES), U32),
                            pltpu.SemaphoreType.DMA((EXPERT_SLOTS,)),
                            pltpu.SemaphoreType.DMA((EXPERT_SLOTS,))]),
        out_shape=jax.ShapeDtypeStruct(xs.shape, U32),
        compiler_params=_cparams(("arbitrary",)),
        name="experts",
    )(blk_first, blk_last, counts, n_active, xs, we_gate, we_up, we_down)


def _combine_kernel(n_prompt_tiles,
                    dest_hbm, ys_hbm, xp_ref, xs_ref, h_ref, gate_ref, wsg_ref, wsu_ref, wsd_ref, gpost_ref, gate2_ref,
                    yp_ref, ysm_ref, dest_s, buf, dsem, rsem):
    i = pl.program_id(0)
    nt = pl.num_programs(0)

    def dest_copy(tile):
        slot = tile % 3
        return pltpu.make_async_copy(dest_hbm.at[pl.ds(tile * DEST_PER_TILE, DEST_PER_TILE)],
                                     dest_s.at[pl.ds(slot * DEST_PER_TILE, DEST_PER_TILE)], dsem.at[slot])

    def issue_gathers(tile):
        base = (tile % 3) * DEST_PER_TILE
        bset = (tile % 2) * TOP_K

        def body(t, carry):
            for k in range(TOP_K):
                d = dest_s[base + k * MOE_TILE + t]
                pltpu.make_async_copy(_row_slice(ys_hbm, d), _row_slice(buf.at[bset + k], t),
                                      rsem.at[bset + k]).start(priority=k % 2)
            return carry

        lax.fori_loop(0, MOE_TILE, body, 0, unroll=4)

    @pl.when(i == 0)
    def _():
        dest_copy(0).start()

        @pl.when(nt > 1)
        def _():
            dest_copy(1).start()

        dest_copy(0).wait()
        issue_gathers(0)

    @pl.when(i + 2 < nt)
    def _():
        dest_copy(i + 2).start()

    @pl.when(i + 1 < nt)
    def _():
        dest_copy(i + 1).wait()
        issue_gathers(i + 1)

    bset = (i % 2) * TOP_K

    ha, hb = _load_packed_rows(h_ref, MOE_TILE)
    ha = ha.astype(BF16)
    hb = hb.astype(BF16)

    def up(w_ref):
        return (jnp.dot(ha, w_ref[:HALF_D, :], preferred_element_type=F32)
                + jnp.dot(hb, w_ref[HALF_D:, :], preferred_element_type=F32))

    g = up(wsg_ref)
    a = (g * jax.nn.sigmoid(g) * up(wsu_ref)).astype(BF16)
    shared = jnp.dot(a, wsd_ref[...], preferred_element_type=F32)

    gpad = jnp.concatenate([gate_ref[...], jnp.zeros((LANES - TOP_K, MOE_TILE), F32)], axis=0)
    gcol = gpad.T

    acc_a = jnp.zeros((MOE_TILE, HALF_D), F32)
    acc_b = jnp.zeros((MOE_TILE, HALF_D), F32)
    for k in range(TOP_K):
        pltpu.make_async_copy(ys_hbm.at[pl.ds(0, MOE_TILE * ROW_SUBLANES)], buf.at[bset + k],
                              rsem.at[bset + k]).wait()
        ya, yb = _load_packed_rows(buf.at[bset + k], MOE_TILE)
        gk = gcol[:, k:k + 1]
        acc_a = acc_a + ya * gk
        acc_b = acc_b + yb * gk
    f = jnp.concatenate([acc_a, acc_b], axis=1) + shared
    x = _two_part_tile(n_prompt_tiles, xp_ref, xs_ref)
    y = x + _group_affine(_rms(f) * gpost_ref[...], gate2_ref[...], None)

    @pl.when(i < n_prompt_tiles)
    def _():
        yp_ref[...] = y

    @pl.when(i >= n_prompt_tiles)
    def _():
        ysm_ref[...] = y


def _combine(dest_flat, ys, x1_p, x1_s, h_packed, gates, ws_gate, ws_up, ws_down, g_post, gate2_g):
    n_prompt = x1_p.shape[0]
    n = n_prompt + x1_s.shape[0]
    nt = n // MOE_TILE
    npt = n_prompt // MOE_TILE
    gpt = MOE_TILE // CHUNK
    z2 = lambda i: (0, 0)
    return pl.pallas_call(
        functools.partial(_combine_kernel, npt),
        grid=(nt,),
        in_specs=[pl.BlockSpec(memory_space=pl.ANY),
                  pl.BlockSpec(memory_space=pl.ANY),
                  pl.BlockSpec((MOE_TILE, D_MODEL), lambda i: (jnp.minimum(i, npt - 1), 0)),
                  pl.BlockSpec((MOE_TILE, D_MODEL), lambda i: (jnp.maximum(i - npt, 0), 0)),
                  pl.BlockSpec((MOE_TILE * ROW_SUBLANES, LANES), lambda i: (i, 0)),
                  pl.BlockSpec((TOP_K, MOE_TILE), lambda i: (0, i)),
                  pl.BlockSpec((D_MODEL, EXPERT_DIM), z2),
                  pl.BlockSpec((D_MODEL, EXPERT_DIM), z2),
                  pl.BlockSpec((EXPERT_DIM, D_MODEL), z2),
                  pl.BlockSpec((1, D_MODEL), z2),
                  pl.BlockSpec((gpt, 1, D_MODEL), lambda i: (i, 0, 0))],
        out_specs=[pl.BlockSpec((MOE_TILE, D_MODEL), lambda i: (jnp.minimum(i, npt - 1), 0)),
                   pl.BlockSpec((MOE_TILE, D_MODEL), lambda i: (jnp.maximum(i - npt, 0), 0))],
        out_shape=[jax.ShapeDtypeStruct((n_prompt, D_MODEL), F32),
                   jax.ShapeDtypeStruct((n - n_prompt, D_MODEL), F32)],
        scratch_shapes=[pltpu.SMEM((3 * DEST_PER_TILE,), I32),
                        pltpu.VMEM((2 * TOP_K, MOE_TILE * ROW_SUBLANES, LANES), U32),
                        pltpu.SemaphoreType.DMA((3,)),
                        pltpu.SemaphoreType.DMA((2 * TOP_K,))],
        compiler_params=_cparams(("arbitrary",)),
        name="combine",
    )(dest_flat, ys, x1_p, x1_s, h_packed, gates, ws_gate, ws_up, ws_down, g_post, gate2_g)


def _rope_tables(pos):
    half = HEAD_DIM // 2
    inv = ROPE_THETA ** (-jnp.arange(half, dtype=F32) / half)
    ang = pos.astype(F32)[:, None] * inv[None, :]
    cos, sin = jnp.cos(ang), jnp.sin(ang)
    return jnp.concatenate([cos] * 4, axis=1), jnp.concatenate([-sin, sin, -sin, sin], axis=1)


def _groups(vec_rows, reps):
    rows, width = vec_rows.shape
    return jnp.broadcast_to(vec_rows[:, None, :], (rows, reps, width)).reshape(rows * reps, 1, width)


def kernel(x_prompt, x_sample, cache_a_k, cache_a_v, cache_b_k, cache_b_v, c_prompt, c_sample, w_ada, b_ada,
           g_pre_mix, g_post_mix, w_in, rel_bias_a, sinks_b, w_branch_a, w_branch_b, w_out, g_pre_ffn,
           g_post_ffn, w_router, b_router, we_gate, we_up, we_down, ws_gate, ws_up, ws_down):
    assert w_ada.shape[0] == 1, "single layer"
    nb, seq, d = x_prompt.shape
    ns, dec = x_sample.shape[:2]
    assert d == D_MODEL and dec == CHUNK and seq % TOK_TILE == 0 and (ns * dec) == TOK_TILE
    n_p, n_s = nb * seq, ns * dec
    n_all = n_p + n_s
    assert n_all % MOE_TILE == 0 and n_p % MOE_TILE == 0

    c_all = jnp.concatenate([c_prompt, c_sample], axis=0)
    pad = (-c_all.shape[0]) % 8
    c_all = jnp.pad(c_all, ((0, pad), (0, 0)))
    mod = _modulation(c_all, w_ada[0], b_ada[0])
    mod_p, mod_s = mod[:nb], mod[nb:nb + ns]
    cpp = seq // CHUNK

    def part(k):
        return mod_p[:, k * d:(k + 1) * d], mod_s[:, k * d:(k + 1) * d]

    (sh1p, sh1s), (sc1p, sc1s), (g1p, g1s), (sh2p, sh2s), (sc2p, sc2s), (g2p, g2s) = [part(k) for k in range(6)]
    both = lambda p, s: jnp.concatenate([_groups(p, cpp), _groups(s, 1)], axis=0)

    w_in_bf = w_in[0].astype(BF16)
    g_pre = g_pre_mix[0].reshape(1, d)
    cos_p, sin_p = _rope_tables(jnp.arange(seq))
    cos_s, sin_s = _rope_tables(PAST_LEN + jnp.arange(dec))
    cos_s, sin_s = jnp.tile(cos_s, (ns, 1)), jnp.tile(sin_s, (ns, 1))

    xp2 = x_prompt.reshape(n_p, d)
    xs2 = x_sample.reshape(n_s, d)
    outs_p = _inproj(xp2, _groups(sh1p, cpp), _groups(sc1p, cpp), g_pre, w_in_bf, cos_p, sin_p, nb, True)
    outs_s = _inproj(xs2, _groups(sh1s, 1), _groups(sc1s, 1), g_pre, w_in_bf, cos_s, sin_s, ns, False)

    table = rel_bias_a[0].astype(F32)
    n_far = A_BAND - 1 - REL_CLIP
    ext = jnp.concatenate([jnp.broadcast_to(table[:, 2 * REL_CLIP:], (A_HEADS, n_far)),
                           jnp.flip(table[:, REL_CLIP - (CHUNK - 1):], axis=1)], axis=1)
    bias = jnp.stack([ext[:, CHUNK - 1 - q:CHUNK - 1 - q + A_BAND] for q in range(CHUNK)], axis=1)
    bias_pairs = bias.reshape(A_HEADS // 2, 2 * CHUNK, A_BAND)
    sink_rows = jnp.broadcast_to(sinks_b[0].astype(F32).reshape(B_KV_HEADS, B_GROUP, 1),
                                 (B_KV_HEADS, B_GROUP, CHUNK)).reshape(B_KV_HEADS, B_GROUP * CHUNK, 1)
    consts = (bias_pairs, sink_rows, w_branch_a[0].astype(BF16), w_branch_b[0].astype(BF16),
              w_out[0].astype(BF16), g_post_mix[0].reshape(1, d))

    x1_p = _attn_prompt(xp2, outs_p[:7], consts, _groups(g1p, cpp), nb)
    caches = (cache_a_k[0].reshape(ns, A_ROWS, WA), cache_a_v[0].reshape(ns, A_ROWS, WA),
              cache_b_k[0].reshape(ns, WINDOW, WKB), cache_b_v[0].reshape(ns, WINDOW, WKB))
    x1_s = _attn_sample(xs2, outs_s[:7], caches, consts, _groups(g1s, 1))

    h_packed, idx, gates, rank, counts = _router(
        x1_p, x1_s, both(sh2p, sh2s), both(sc2p, sc2s), g_pre_ffn[0].reshape(1, d),
        w_router[0].astype(F32).T, b_router[0].astype(F32).reshape(N_EXPERTS, 1))
    n_blocks = (n_all * TOP_K) // EXPERT_ROWS + N_EXPERTS
    cnt = counts[:, 0].astype(I32)
    blocks_e = (cnt + EXPERT_ROWS - 1) // EXPERT_ROWS
    blk_end = jnp.cumsum(blocks_e)
    blk_start = blk_end - blocks_e
    n_active = blk_end[-1:]
    start_col = (blk_start * EXPERT_ROWS).astype(F32).reshape(N_EXPERTS, 1)

    dest_flat = _dest_rows(idx, rank, start_col).reshape(-1)
    xs = _dispatch(dest_flat, h_packed, n_blocks * EXPERT_ROWS)
    ys = _experts(blk_start.astype(I32), blk_end.astype(I32), cnt, n_active.astype(I32), xs,
                  we_gate[0], we_up[0], we_down[0])
    y_p, y_s = _combine(dest_flat, ys, x1_p, x1_s, h_packed, gates, ws_gate[0].astype(BF16),
                        ws_up[0].astype(BF16), ws_down[0].astype(BF16), g_post_ffn[0].reshape(1, d),
                        both(g2p, g2s))

    a_heads = (A_HEADS, HEAD_DIM)
    b_heads = (B_KV_HEADS, HEAD_DIM)
    return (y_p.reshape(nb, seq, d), y_s.reshape(ns, dec, d),
            outs_p[7].reshape(1, nb, A_ROWS, *a_heads), outs_p[8].reshape(1, nb, A_ROWS, *a_heads),
            outs_p[9].reshape(1, nb, WINDOW, *b_heads), outs_p[10].reshape(1, nb, WINDOW, *b_heads),
            outs_s[7].reshape(1, ns, dec, *a_heads), outs_s[8].reshape(1, ns, dec, *a_heads),
            outs_s[9].reshape(1, ns, dec, *b_heads), outs_s[10].reshape(1, ns, dec, *b_heads))
```

```python
import functools

import jax
import jax.numpy as jnp
from jax import lax
from jax.experimental import pallas as pl
from jax.experimental.pallas import tpu as pltpu

F32 = jnp.float32
BF16 = jnp.bfloat16
I32 = jnp.int32
U32 = jnp.uint32
HIGHEST = lax.Precision.HIGHEST

D_MODEL = 1024
CHUNK = 64
HEAD_DIM = 64
A_HEADS = 8
A_PREV_CHUNKS = 8
A_ROWS = A_PREV_CHUNKS * CHUNK
A_BAND = A_ROWS + CHUNK
REL_CLIP = 128
B_HEADS = 8
B_KV_HEADS = 2
B_GROUP = B_HEADS // B_KV_HEADS
WINDOW = 128
B_BAND = WINDOW + CHUNK
ROPE_THETA = 10000.0
N_EXPERTS = 256
TOP_K = 8
N_GROUPS = 8
PER_GROUP = N_EXPERTS // N_GROUPS
TOPK_GROUPS = 4
EXPERT_DIM = 256
ROUTED_SCALE = 2.5
EPS = 1e-6
PAST_LEN = 4096

WA = A_HEADS * HEAD_DIM
WB = B_HEADS * HEAD_DIM
WKB = B_KV_HEADS * HEAD_DIM
OFF_QA, OFF_KA, OFF_VA = 0, WA, 2 * WA
OFF_QB = 3 * WA
OFF_KB = OFF_QB + WB
OFF_VB = OFF_KB + WKB
OFF_G = OFF_VB + WKB
N_IN = OFF_G + 2 * D_MODEL

LANES = 128
TOK_TILE = 512
MOE_TILE = 256
EXPERT_ROWS = 256
HALF_D = D_MODEL // 2
VMEM_LIMIT = 56 * 1024 * 1024


def _cparams(sem, vmem=VMEM_LIMIT):
    return pltpu.CompilerParams(dimension_semantics=sem, vmem_limit_bytes=vmem)


def _rms(x):
    return x * lax.rsqrt(jnp.mean(x * x, axis=-1, keepdims=True) + EPS)


def _group_affine(y, mul, add):
    g = mul.shape[0]
    y3 = y.reshape(g, CHUNK, y.shape[-1]) * mul
    if add is not None:
        y3 = y3 + add
    return y3.reshape(g * CHUNK, y.shape[-1])


def _pack_halves(a, b):
    ua = lax.bitcast_convert_type(a.astype(BF16).astype(F32), U32)
    ub = lax.bitcast_convert_type(b.astype(BF16).astype(F32), U32)
    return (ua & jnp.uint32(0xFFFF0000)) | (ub >> 16)


def _unpack_halves(u):
    a = lax.bitcast_convert_type(u & jnp.uint32(0xFFFF0000), F32)
    b = lax.bitcast_convert_type(u << 16, F32)
    return a, b


ROW_SUBLANES = HALF_D // LANES


def _store_packed_rows(ref, x):
    rows = x.shape[0]
    p = _pack_halves(x[:, :HALF_D], x[:, HALF_D:])
    for s in range(ROW_SUBLANES):
        ref[pl.ds(s, rows, stride=ROW_SUBLANES), :] = p[:, s * LANES:(s + 1) * LANES]


def _load_packed_rows(ref, rows, n_valid=None):
    his, los = [], []
    for s in range(ROW_SUBLANES):
        u = ref[pl.ds(s, rows, stride=ROW_SUBLANES), :]
        if n_valid is not None:
            u = jnp.where(lax.broadcasted_iota(I32, u.shape, 0) < n_valid, u, jnp.uint32(0))
        a, b = _unpack_halves(u)
        his.append(a)
        los.append(b)
    return jnp.concatenate(his, axis=1), jnp.concatenate(los, axis=1)


def _mod_kernel(c_ref, w_ref, b_ref, o_ref):
    c = c_ref[...]
    s = c * jax.nn.sigmoid(c)
    o_ref[...] = jnp.dot(s, w_ref[...], precision=HIGHEST, preferred_element_type=F32) + b_ref[...]


def _modulation(c_all, w_ada, b_ada):
    rows = c_all.shape[0]
    n = w_ada.shape[1]
    tn = 512
    return pl.pallas_call(
        _mod_kernel,
        grid=(n // tn,),
        in_specs=[pl.BlockSpec((rows, D_MODEL), lambda j: (0, 0)),
                  pl.BlockSpec((D_MODEL, tn), lambda j: (0, j)),
                  pl.BlockSpec((1, tn), lambda j: (0, j))],
        out_specs=pl.BlockSpec((rows, tn), lambda j: (0, j)),
        out_shape=jax.ShapeDtypeStruct((rows, n), F32),
        compiler_params=_cparams(("arbitrary",)),
        name="modulation",
    )(c_all, w_ada, b_ada.reshape(1, n))


def _rope(x, cos, sin_signed):
    n = x.shape[-1]
    reps = n // LANES
    if reps > 1:
        cos = jnp.concatenate([cos] * reps, axis=1)
        sin_signed = jnp.concatenate([sin_signed] * reps, axis=1)
    lane = lax.broadcasted_iota(I32, x.shape, 1)
    first_half = (lane % HEAD_DIM) < (HEAD_DIM // 2)
    partner = jnp.where(first_half, pltpu.roll(x, n - HEAD_DIM // 2, 1), pltpu.roll(x, HEAD_DIM // 2, 1))
    return x * cos + partner * sin_signed


def _inproj_kernel(prompt_state, tiles_per_seq,
                   x_ref, sh_ref, sc_ref, g_ref, w_ref, cos_ref, sin_ref,
                   qa_ref, ka_ref, va_ref, qb_ref, kb_ref, vb_ref, gt_ref,
                   ska_ref, sva_ref, skb_ref, svb_ref):
    x = x_ref[...]
    h = _group_affine(_rms(x) * g_ref[...], 1.0 + sc_ref[...], sh_ref[...]).astype(BF16)

    def proj(off, width):
        return jnp.dot(h, w_ref[:, off:off + width], preferred_element_type=F32)

    cos = cos_ref[...]
    sin = sin_ref[...]
    scale = HEAD_DIM ** -0.5
    qa_ref[...] = (proj(OFF_QA, WA) * scale).astype(BF16)
    ka = proj(OFF_KA, WA)
    va = proj(OFF_VA, WA)
    ka_ref[...] = ka.astype(BF16)
    va_ref[...] = va.astype(BF16)
    qb_ref[...] = (_rope(proj(OFF_QB, WB), cos, sin) * scale).astype(BF16)
    kb = _rope(proj(OFF_KB, WKB), cos, sin)
    vb = proj(OFF_VB, WKB)
    kb_ref[...] = kb.astype(BF16)
    vb_ref[...] = vb.astype(BF16)
    gt_ref[...] = jax.nn.sigmoid(proj(OFF_G, 2 * D_MODEL)).astype(BF16)

    if prompt_state:
        @pl.when(pl.program_id(0) % tiles_per_seq == tiles_per_seq - 1)
        def _():
            ska_ref[...] = ka
            sva_ref[...] = va
            skb_ref[...] = kb[TOK_TILE - WINDOW:, :]
            svb_ref[...] = vb[TOK_TILE - WINDOW:, :]
    else:
        ska_ref[...] = ka
        sva_ref[...] = va
        skb_ref[...] = kb
        svb_ref[...] = vb


def _inproj(x2d, shift_g, scale_g, g_pre, w_in_bf, cos_tab, sin_tab, n_seq, prompt_state):
    n = x2d.shape[0]
    nt = n // TOK_TILE
    tiles_per_seq = nt // n_seq if prompt_state else 1
    tab_tiles = cos_tab.shape[0] // TOK_TILE
    gpt = TOK_TILE // CHUNK
    row = lambda i: (i, 0)
    grp = lambda i: (i, 0, 0)
    if prompt_state:
        st_shapes = [jax.ShapeDtypeStruct((n_seq, A_ROWS, WA), F32)] * 2 + \
                    [jax.ShapeDtypeStruct((n_seq, WINDOW, WKB), F32)] * 2
        st_specs = [pl.BlockSpec((None, A_ROWS, WA), lambda i: (i // tiles_per_seq, 0, 0))] * 2 + \
                   [pl.BlockSpec((None, WINDOW, WKB), lambda i: (i // tiles_per_seq, 0, 0))] * 2
    else:
        st_shapes = [jax.ShapeDtypeStruct((n, WA), F32)] * 2 + [jax.ShapeDtypeStruct((n, WKB), F32)] * 2
        st_specs = [pl.BlockSpec((TOK_TILE, WA), row)] * 2 + [pl.BlockSpec((TOK_TILE, WKB), row)] * 2
    out_shapes = [jax.ShapeDtypeStruct((n, WA), BF16)] * 4 + [jax.ShapeDtypeStruct((n, WKB), BF16)] * 2 + \
                 [jax.ShapeDtypeStruct((n, 2 * D_MODEL), BF16)]
    out_shapes = [out_shapes[0], out_shapes[1], out_shapes[2], out_shapes[3], out_shapes[4], out_shapes[5],
                  out_shapes[6]] + st_shapes
    out_specs = [pl.BlockSpec((TOK_TILE, WA), row)] * 4 + [pl.BlockSpec((TOK_TILE, WKB), row)] * 2 + \
                [pl.BlockSpec((TOK_TILE, 2 * D_MODEL), row)] + st_specs
    return pl.pallas_call(
        functools.partial(_inproj_kernel, prompt_state, tiles_per_seq),
        grid=(nt,),
        in_specs=[pl.BlockSpec((TOK_TILE, D_MODEL), row),
                  pl.BlockSpec((gpt, 1, D_MODEL), grp),
                  pl.BlockSpec((gpt, 1, D_MODEL), grp),
                  pl.BlockSpec((1, D_MODEL), lambda i: (0, 0)),
                  pl.BlockSpec((D_MODEL, N_IN), lambda i: (0, 0)),
                  pl.BlockSpec((TOK_TILE, LANES), lambda i: (i % tab_tiles, 0)),
                  pl.BlockSpec((TOK_TILE, LANES), lambda i: (i % tab_tiles, 0))],
        out_specs=out_specs,
        out_shape=out_shapes,
        compiler_params=_cparams(("arbitrary",)),
        name="inproj_prompt" if prompt_state else "inproj_sample",
    )(x2d, shift_g, scale_g, g_pre, w_in_bf, cos_tab, sin_tab)


def _attn_kernel(n_chunks, mask_first,
                 x_ref, qa_ref, qb_ref, gt_ref,
                 kap_ref, kac_ref, vap_ref, vac_ref, kbp_ref, kbc_ref, vbp_ref, vbc_ref,
                 bias_ref, sink_ref, wba_ref, wbb_ref, wout_ref, gpost_ref, gate1_ref,
                 o_ref,
                 ka_s, va_s, kb_s, vb_s, oa_s, ob_s):
    rows = n_chunks * CHUNK
    pb = kbp_ref.shape[0]
    ka_s[0:A_ROWS, :] = kap_ref[...].astype(BF16)
    va_s[0:A_ROWS, :] = vap_ref[...].astype(BF16)
    ka_s[A_ROWS:A_ROWS + rows, :] = kac_ref[...]
    va_s[A_ROWS:A_ROWS + rows, :] = vac_ref[...]
    kb_s[0:WINDOW, :] = kbp_ref[pb - WINDOW:pb, :].astype(BF16)
    vb_s[0:WINDOW, :] = vbp_ref[pb - WINDOW:pb, :].astype(BF16)
    kb_s[WINDOW:WINDOW + rows, :] = kbc_ref[...]
    vb_s[WINDOW:WINDOW + rows, :] = vbc_ref[...]

    lane_q = lax.broadcasted_iota(I32, (CHUNK, LANES), 1)
    nt_dims = (((1,), (1,)), ((), ()))

    def chunk_body(masked, c, carry):
        c0 = pl.multiple_of(c * CHUNK, CHUNK)
        if masked:
            valid_a = c0 + lax.broadcasted_iota(I32, (1, A_BAND), 1) >= A_ROWS
            valid_b = c0 + lax.broadcasted_iota(I32, (1, B_BAND), 1) >= WINDOW

        scores = []
        for p in range(A_HEADS // 2):
            cols = slice(p * LANES, (p + 1) * LANES)
            q = qa_ref[pl.ds(c0, CHUNK), cols].astype(F32)
            qs = jnp.concatenate([jnp.where(lane_q < HEAD_DIM, q, 0.0),
                                  jnp.where(lane_q >= HEAD_DIM, q, 0.0)], axis=0).astype(BF16)
            k = ka_s[pl.ds(c0, A_BAND), cols]
            s = lax.dot_general(qs, k, nt_dims, preferred_element_type=F32) + bias_ref[p]
            if masked:
                s = jnp.where(valid_a, s, -jnp.inf)
            scores.append(s)
        for g in range(B_KV_HEADS):
            parts = []
            for r in range(B_GROUP):
                head = g * B_GROUP + r
                t, half = head // 2, head % 2
                q = qb_ref[pl.ds(c0, CHUNK), t * LANES:(t + 1) * LANES].astype(F32)
                if half != g:
                    q = pltpu.roll(q, HEAD_DIM, 1)
                in_g = (lane_q >= HEAD_DIM) if g else (lane_q < HEAD_DIM)
                parts.append(jnp.where(in_g, q, 0.0))
            qs = jnp.concatenate(parts, axis=0).astype(BF16)
            k = kb_s[pl.ds(c0, B_BAND), :]
            s = lax.dot_general(qs, k, nt_dims, preferred_element_type=F32)
            if masked:
                s = jnp.where(valid_b, s, -jnp.inf)
            scores.append(s)

        numer, denom = [], []
        for n, s in enumerate(scores):
            m = jnp.max(s, axis=1, keepdims=True)
            if n >= A_HEADS // 2:
                sk = sink_ref[n - A_HEADS // 2]
                m = jnp.maximum(m, sk)
            e = jnp.exp(s - m)
            l = jnp.sum(e, axis=1, keepdims=True)
            if n >= A_HEADS // 2:
                l = l + jnp.exp(sk - m)
            numer.append(e.astype(BF16))
            denom.append(l)

        outs = []
        for n, e in enumerate(numer):
            if n < A_HEADS // 2:
                v = va_s[pl.ds(c0, A_BAND), n * LANES:(n + 1) * LANES]
            else:
                v = vb_s[pl.ds(c0, B_BAND), :]
            outs.append(jnp.dot(e, v, preferred_element_type=F32) / denom[n])

        for p in range(A_HEADS // 2):
            o = outs[p]
            oa_s[pl.ds(c0, CHUNK), p * LANES:(p + 1) * LANES] = jnp.where(
                lane_q < HEAD_DIM, o[:CHUNK], o[CHUNK:]).astype(BF16)
        for g in range(B_KV_HEADS):
            o = outs[A_HEADS // 2 + g]
            for s2 in range(B_GROUP // 2):
                o_even = o[(2 * s2) * CHUNK:(2 * s2 + 1) * CHUNK]
                o_odd = o[(2 * s2 + 1) * CHUNK:(2 * s2 + 2) * CHUNK]
                if g == 0:
                    tile = jnp.where(lane_q < HEAD_DIM, o_even, pltpu.roll(o_odd, HEAD_DIM, 1))
                else:
                    tile = jnp.where(lane_q < HEAD_DIM, pltpu.roll(o_even, HEAD_DIM, 1), o_odd)
                t = g * (B_GROUP // 2) + s2
                ob_s[pl.ds(c0, CHUNK), t * LANES:(t + 1) * LANES] = tile.astype(BF16)
        return carry

    def all_chunks(masked):
        lax.fori_loop(0, n_chunks, functools.partial(chunk_body, masked), 0, unroll=2 if n_chunks > 1 else 1)

    if mask_first:
        pl.when(pl.program_id(1) == 0)(lambda: all_chunks(True))
        pl.when(pl.program_id(1) > 0)(lambda: all_chunks(False))
    else:
        all_chunks(False)

    za = jnp.dot(oa_s[...], wba_ref[...], preferred_element_type=F32)
    zb = jnp.dot(ob_s[...], wbb_ref[...], preferred_element_type=F32)
    merged = gt_ref[:, :D_MODEL].astype(F32) * za + gt_ref[:, D_MODEL:].astype(F32) * zb
    mo = jnp.dot(merged.astype(BF16), wout_ref[...], preferred_element_type=F32)
    o_ref[...] = x_ref[...] + _group_affine(_rms(mo) * gpost_ref[...], gate1_ref[...], None)


def _attn_scratch(rows):
    return [pltpu.VMEM((A_ROWS + rows, WA), BF16), pltpu.VMEM((A_ROWS + rows, WA), BF16),
            pltpu.VMEM((WINDOW + rows, WKB), BF16), pltpu.VMEM((WINDOW + rows, WKB), BF16),
            pltpu.VMEM((rows, WA), BF16), pltpu.VMEM((rows, WB), BF16)]


def _const_specs(grid_rank):
    z2 = (lambda b, j: (0, 0)) if grid_rank == 2 else (lambda b: (0, 0))
    z3 = (lambda b, j: (0, 0, 0)) if grid_rank == 2 else (lambda b: (0, 0, 0))
    return [pl.BlockSpec((A_HEADS // 2, 2 * CHUNK, A_BAND), z3),
            pl.BlockSpec((B_KV_HEADS, B_GROUP * CHUNK, 1), z3),
            pl.BlockSpec((WA, D_MODEL), z2),
            pl.BlockSpec((WB, D_MODEL), z2),
            pl.BlockSpec((D_MODEL, D_MODEL), z2),
            pl.BlockSpec((1, D_MODEL), z2)]


def _attn_prompt(x2d, proj, consts, gate1_g, n_seq):
    qa, ka, va, qb, kb, vb, gt = proj
    n = x2d.shape[0]
    tps = n // n_seq // TOK_TILE
    gpt = TOK_TILE // CHUNK
    cur = lambda b, j: (b * tps + j, 0)
    prev = lambda b, j: (b * tps + jnp.maximum(j - 1, 0), 0)
    return pl.pallas_call(
        functools.partial(_attn_kernel, TOK_TILE // CHUNK, True),
        grid=(n_seq, tps),
        in_specs=[pl.BlockSpec((TOK_TILE, D_MODEL), cur),
                  pl.BlockSpec((TOK_TILE, WA), cur),
                  pl.BlockSpec((TOK_TILE, WB), cur),
                  pl.BlockSpec((TOK_TILE, 2 * D_MODEL), cur),
                  pl.BlockSpec((TOK_TILE, WA), prev), pl.BlockSpec((TOK_TILE, WA), cur),
                  pl.BlockSpec((TOK_TILE, WA), prev), pl.BlockSpec((TOK_TILE, WA), cur),
                  pl.BlockSpec((TOK_TILE, WKB), prev), pl.BlockSpec((TOK_TILE, WKB), cur),
                  pl.BlockSpec((TOK_TILE, WKB), prev), pl.BlockSpec((TOK_TILE, WKB), cur)]
                 + _const_specs(2)
                 + [pl.BlockSpec((gpt, 1, D_MODEL), lambda b, j: (b * tps + j, 0, 0))],
        out_specs=pl.BlockSpec((TOK_TILE, D_MODEL), cur),
        out_shape=jax.ShapeDtypeStruct((n, D_MODEL), F32),
        scratch_shapes=_attn_scratch(TOK_TILE),
        compiler_params=_cparams(("arbitrary", "arbitrary")),
        name="attn_prompt",
    )(x2d, qa, qb, gt, ka, ka, va, va, kb, kb, vb, vb, *consts, gate1_g)


def _attn_sample(x2d, proj, caches, consts, gate1_g):
    qa, ka, va, qb, kb, vb, gt = proj
    cak, cav, cbk, cbv = caches
    n_seq = cak.shape[0]
    cur = lambda b: (b, 0)
    cache = lambda b: (b, 0, 0)
    return pl.pallas_call(
        functools.partial(_attn_kernel, 1, False),
        grid=(n_seq,),
        in_specs=[pl.BlockSpec((CHUNK, D_MODEL), cur),
                  pl.BlockSpec((CHUNK, WA), cur),
                  pl.BlockSpec((CHUNK, WB), cur),
                  pl.BlockSpec((CHUNK, 2 * D_MODEL), cur),
                  pl.BlockSpec((None, A_ROWS, WA), cache), pl.BlockSpec((CHUNK, WA), cur),
                  pl.BlockSpec((None, A_ROWS, WA), cache), pl.BlockSpec((CHUNK, WA), cur),
                  pl.BlockSpec((None, WINDOW, WKB), cache), pl.BlockSpec((CHUNK, WKB), cur),
                  pl.BlockSpec((None, WINDOW, WKB), cache), pl.BlockSpec((CHUNK, WKB), cur)]
                 + _const_specs(1)
                 + [pl.BlockSpec((1, 1, D_MODEL), lambda b: (b, 0, 0))],
        out_specs=pl.BlockSpec((CHUNK, D_MODEL), cur),
        out_shape=jax.ShapeDtypeStruct(x2d.shape, F32),
        scratch_shapes=_attn_scratch(CHUNK),
        compiler_params=_cparams(("arbitrary",)),
        name="attn_sample",
    )(x2d, qa, qb, gt, cak, ka, cav, va, cbk, kb, cbv, vb, *consts, gate1_g)


def _two_part_tile(n_first_tiles, first_ref, second_ref):
    return jnp.where(pl.program_id(0) < n_first_tiles, first_ref[...], second_ref[...])


def _router_kernel(n_prompt_tiles, xp_ref, xs_ref, sh_ref, sc_ref, g_ref, wrt_ref, br_ref,
                   h_ref, idx_ref, gate_ref, rank_ref, cnt_ref, carry):
    i = pl.program_id(0)

    @pl.when(i == 0)
    def _():
        carry[...] = jnp.zeros_like(carry)

    tm = xp_ref.shape[0]
    x = _two_part_tile(n_prompt_tiles, xp_ref, xs_ref)
    h = _group_affine(_rms(x) * g_ref[...], 1.0 + sc_ref[...], sh_ref[...])
    _store_packed_rows(h_ref, h)

    logits = lax.dot_general(wrt_ref[...], h, (((1,), (1,)), ((), ())),
                             precision=HIGHEST, preferred_element_type=F32)
    scores = jax.nn.sigmoid(logits)
    biased = scores + br_ref[...]
    neg = -jnp.inf

    sub = lax.broadcasted_iota(I32, (PER_GROUP, tm), 0).astype(F32)
    gs_rows = []
    for g in range(N_GROUPS):
        xg = biased[g * PER_GROUP:(g + 1) * PER_GROUP]
        m1 = jnp.max(xg, axis=0, keepdims=True)
        i1 = jnp.min(jnp.where(xg == m1, sub, float(PER_GROUP)), axis=0, keepdims=True)
        m2 = jnp.max(jnp.where(sub == i1, neg, xg), axis=0, keepdims=True)
        gs_rows.append(m1 + m2)
    gs = jnp.concatenate(gs_rows, axis=0)

    giota = lax.broadcasted_iota(I32, (N_GROUPS, tm), 0).astype(F32)
    keep = jnp.zeros((N_GROUPS, tm), F32)
    for _ in range(TOPK_GROUPS):
        m = jnp.max(gs, axis=0, keepdims=True)
        gi = jnp.min(jnp.where(gs == m, giota, float(N_GROUPS)), axis=0, keepdims=True)
        hit = giota == gi
        keep = jnp.where(hit, 1.0, keep)
        gs = jnp.where(hit, neg, gs)
    cand = jnp.concatenate(
        [jnp.where(keep[g:g + 1] > 0.0, biased[g * PER_GROUP:(g + 1) * PER_GROUP], neg)
         for g in range(N_GROUPS)], axis=0)

    eiota = lax.broadcasted_iota(I32, (N_EXPERTS, tm), 0).astype(F32)
    idx_rows, gate_rows = [], []
    chosen = jnp.zeros((N_EXPERTS, tm), F32)
    for _ in range(TOP_K):
        m = jnp.max(cand, axis=0, keepdims=True)
        ei = jnp.min(jnp.where(cand == m, eiota, float(N_EXPERTS)), axis=0, keepdims=True)
        sel = eiota == ei
        gate_rows.append(jnp.sum(jnp.where(sel, scores, 0.0), axis=0, keepdims=True))
        idx_rows.append(ei)
        chosen = jnp.where(sel, 1.0, chosen)
        cand = jnp.where(sel, neg, cand)
    gates = jnp.concatenate(gate_rows, axis=0)
    gates = gates / jnp.sum(gates, axis=0, keepdims=True) * ROUTED_SCALE
    gate_ref[...] = gates
    idx_ref[...] = jnp.concatenate(idx_rows, axis=0).astype(I32)

    r_i = lax.broadcasted_iota(I32, (tm, tm), 0)
    c_i = lax.broadcasted_iota(I32, (tm, tm), 1)
    upper = jnp.where(r_i < c_i, 1.0, 0.0).astype(BF16)
    before = jnp.dot(chosen.astype(BF16), upper, preferred_element_type=F32) + carry[...]
    rank_rows = [jnp.sum(jnp.where(eiota == idx_rows[k], before, 0.0), axis=0, keepdims=True)
                 for k in range(TOP_K)]
    rank_ref[...] = jnp.concatenate(rank_rows, axis=0).astype(I32)
    total = carry[...] + jnp.sum(chosen, axis=1, keepdims=True)
    carry[...] = total
    cnt_ref[...] = total


def _router(x1_p, x1_s, shift_g, scale_g, g_pre, w_router_t, b_router_col):
    n = x1_p.shape[0] + x1_s.shape[0]
    nt = n // MOE_TILE
    npt = x1_p.shape[0] // MOE_TILE
    gpt = MOE_TILE // CHUNK
    lane_blk = lambda i: (0, i)
    return pl.pallas_call(
        functools.partial(_router_kernel, npt),
        grid=(nt,),
        in_specs=[pl.BlockSpec((MOE_TILE, D_MODEL), lambda i: (jnp.minimum(i, npt - 1), 0)),
                  pl.BlockSpec((MOE_TILE, D_MODEL), lambda i: (jnp.maximum(i - npt, 0), 0)),
                  pl.BlockSpec((gpt, 1, D_MODEL), lambda i: (i, 0, 0)),
                  pl.BlockSpec((gpt, 1, D_MODEL), lambda i: (i, 0, 0)),
                  pl.BlockSpec((1, D_MODEL), lambda i: (0, 0)),
                  pl.BlockSpec((N_EXPERTS, D_MODEL), lambda i: (0, 0)),
                  pl.BlockSpec((N_EXPERTS, 1), lambda i: (0, 0))],
        out_specs=[pl.BlockSpec((MOE_TILE * ROW_SUBLANES, LANES), lambda i: (i, 0)),
                   pl.BlockSpec((TOP_K, MOE_TILE), lane_blk),
                   pl.BlockSpec((TOP_K, MOE_TILE), lane_blk),
                   pl.BlockSpec((TOP_K, MOE_TILE), lane_blk),
                   pl.BlockSpec((N_EXPERTS, 1), lambda i: (0, 0))],
        out_shape=[jax.ShapeDtypeStruct((n * ROW_SUBLANES, LANES), U32),
                   jax.ShapeDtypeStruct((TOP_K, n), I32),
                   jax.ShapeDtypeStruct((TOP_K, n), F32),
                   jax.ShapeDtypeStruct((TOP_K, n), I32),
                   jax.ShapeDtypeStruct((N_EXPERTS, 1), F32)],
        scratch_shapes=[pltpu.VMEM((N_EXPERTS, 1), F32)],
        compiler_params=_cparams(("arbitrary",)),
        name="router",
    )(x1_p, x1_s, shift_g, scale_g, g_pre, w_router_t, b_router_col)


def _dest_kernel(idx_ref, rank_ref, start_ref, o_ref):
    eiota = lax.broadcasted_iota(I32, (N_EXPERTS, MOE_TILE), 0)
    start = start_ref[...]
    for sub in range(o_ref.shape[0]):
        cols = slice(sub * MOE_TILE, (sub + 1) * MOE_TILE)
        rows = [jnp.sum(jnp.where(eiota == idx_ref[k:k + 1, cols], start, 0.0), axis=0, keepdims=True)
                for k in range(TOP_K)]
        o_ref[sub] = jnp.concatenate(rows, axis=0).astype(I32) + rank_ref[:, cols]


def _dest_rows(idx, rank, start_col):
    n = idx.shape[1]
    nt = n // MOE_TILE
    per_step = next(c for c in (10, 8, 5, 4, 2, 1) if nt % c == 0)
    return pl.pallas_call(
        _dest_kernel,
        grid=(nt // per_step,),
        in_specs=[pl.BlockSpec((TOP_K, per_step * MOE_TILE), lambda i: (0, i)),
                  pl.BlockSpec((TOP_K, per_step * MOE_TILE), lambda i: (0, i)),
                  pl.BlockSpec((N_EXPERTS, 1), lambda i: (0, 0))],
        out_specs=pl.BlockSpec((per_step, TOP_K, MOE_TILE), lambda i: (i, 0, 0)),
        out_shape=jax.ShapeDtypeStruct((nt, TOP_K, MOE_TILE), I32),
        compiler_params=_cparams(("arbitrary",)),
        name="dest_rows",
    )(idx, rank, start_col)


DEST_PER_TILE = TOP_K * MOE_TILE


def _row_slice(ref, row):
    return ref.at[pl.ds(pl.multiple_of(row * ROW_SUBLANES, ROW_SUBLANES), ROW_SUBLANES)]


TILE_SUBLANES = MOE_TILE * ROW_SUBLANES


def _dispatch_kernel(dest_hbm, h_hbm, xs_hbm, dest_s, hbuf, dsem, hsem, rsem):
    i = pl.program_id(0)
    nt = pl.num_programs(0)

    def load(tile):
        dst = dest_s.at[pl.ds((tile % 2) * DEST_PER_TILE, DEST_PER_TILE)]
        rows = h_hbm.at[pl.ds(pl.multiple_of(tile * TILE_SUBLANES, TILE_SUBLANES), TILE_SUBLANES)]
        return (pltpu.make_async_copy(dest_hbm.at[pl.ds(tile * DEST_PER_TILE, DEST_PER_TILE)], dst, dsem.at[tile % 2]),
                pltpu.make_async_copy(rows, hbuf.at[tile % 3], hsem.at[tile % 3]))

    def drain(tile):
        for k in range(TOP_K):
            pltpu.make_async_copy(hbuf.at[tile % 3], xs_hbm.at[pl.ds(0, TILE_SUBLANES)],
                                  rsem.at[(tile % 2) * TOP_K + k]).wait()

    @pl.when(i == 0)
    def _():
        for cp in load(0):
            cp.start()

    @pl.when(i + 1 < nt)
    def _():
        for cp in load(i + 1):
            cp.start()

    for cp in load(i):
        cp.wait()
    base = (i % 2) * DEST_PER_TILE
    src_buf = hbuf.at[i % 3]
    sem0 = (i % 2) * TOP_K

    def body(t, carry):
        src = _row_slice(src_buf, t)
        for k in range(TOP_K):
            d = dest_s[base + k * MOE_TILE + t]
            pltpu.make_async_copy(src, _row_slice(xs_hbm, d), rsem.at[sem0 + k]).start(priority=k % 2)
        return carry

    lax.fori_loop(0, MOE_TILE, body, 0, unroll=4)

    @pl.when(i >= 1)
    def _():
        drain(i - 1)

    @pl.when(i == nt - 1)
    def _():
        drain(i)


def _dispatch(dest_flat, h_packed, n_rows):
    nt = h_packed.shape[0] // TILE_SUBLANES
    return pl.pallas_call(
        _dispatch_kernel,
        grid=(nt,),
        in_specs=[pl.BlockSpec(memory_space=pl.ANY),
                  pl.BlockSpec(memory_space=pl.ANY)],
        out_specs=pl.BlockSpec(memory_space=pl.ANY),
        out_shape=jax.ShapeDtypeStruct((n_rows * ROW_SUBLANES, LANES), U32),
        scratch_shapes=[pltpu.SMEM((2 * DEST_PER_TILE,), I32),
                        pltpu.VMEM((3, TILE_SUBLANES, LANES), U32),
                        pltpu.SemaphoreType.DMA((2,)),
                        pltpu.SemaphoreType.DMA((3,)),
                        pltpu.SemaphoreType.DMA((2 * TOP_K,))],
        compiler_params=_cparams(("arbitrary",)),
        name="dispatch",
    )(dest_flat, h_packed)


BLOCK_SUBLANES = EXPERT_ROWS * ROW_SUBLANES
EXPERT_AHEAD = 4
EXPERT_SLOTS = EXPERT_AHEAD + 2


def _expert_kernel(first_ref, last_ref, cnt_ref, na_ref, xs_hbm, wg_ref, wu_ref, wd_ref, ys_hbm,
                   wg_s, wu_s, wd_s, xbuf, ybuf, isem, osem):
    e = pl.program_id(0)
    n_active = na_ref[0]
    first, last = first_ref[e], last_ref[e]

    def block_rows(ref, g):
        return ref.at[pl.ds(pl.multiple_of(g * BLOCK_SUBLANES, BLOCK_SUBLANES), BLOCK_SUBLANES)]

    def fetch(g, slot):
        return pltpu.make_async_copy(block_rows(xs_hbm, g), xbuf.at[slot], isem.at[slot])

    def flush(g, slot):
        return pltpu.make_async_copy(ybuf.at[slot], block_rows(ys_hbm, g), osem.at[slot])

    @pl.when(e == 0)
    def _():
        for g0 in range(EXPERT_AHEAD):
            @pl.when(g0 < n_active)
            def _():
                fetch(g0, g0).start()

    @pl.when(last > first)
    def _():
        wg_s[...] = wg_ref[...].astype(BF16)
        wu_s[...] = wu_ref[...].astype(BF16)
        wd_s[...] = wd_ref[...].astype(BF16)

    def enter(g):
        slot = g % EXPERT_SLOTS
        fetch(g, slot).wait()

        @pl.when(g + EXPERT_AHEAD < n_active)
        def _():
            fetch(g + EXPERT_AHEAD, (g + EXPERT_AHEAD) % EXPERT_SLOTS).start()

        @pl.when(g >= EXPERT_SLOTS)
        def _():
            flush(g - EXPERT_SLOTS, slot).wait()
        return slot

    def swiglu(halves):
        xa = jnp.concatenate([h[0] for h in halves], axis=0).astype(BF16)
        xb = jnp.concatenate([h[1] for h in halves], axis=0).astype(BF16)

        def up(w_s):
            return (jnp.dot(xa, w_s[:HALF_D, :], preferred_element_type=F32)
                    + jnp.dot(xb, w_s[HALF_D:, :], preferred_element_type=F32))

        gate = up(wg_s)
        a = (gate * jax.nn.sigmoid(gate) * up(wu_s)).astype(BF16)
        return jnp.dot(a, wd_s[...], preferred_element_type=F32)

    def n_valid(g):
        return cnt_ref[e] - (g - first) * EXPERT_ROWS

    def pair_body(p, carry):
        g = first + 2 * p
        s0 = enter(g)
        s1 = enter(g + 1)
        y = swiglu([_load_packed_rows(xbuf.at[s0], EXPERT_ROWS),
                    _load_packed_rows(xbuf.at[s1], EXPERT_ROWS, n_valid(g + 1))])
        _store_packed_rows(ybuf.at[s0], y[:EXPERT_ROWS])
        _store_packed_rows(ybuf.at[s1], y[EXPERT_ROWS:])
        flush(g, s0).start()
        flush(g + 1, s1).start()
        return carry

    n_pairs = lax.shift_right_logical(last - first, 1)
    lax.fori_loop(0, n_pairs, pair_body, 0)

    @pl.when((last - first) % 2 == 1)
    def _():
        g = last - 1
        s0 = enter(g)
        y = swiglu([_load_packed_rows(xbuf.at[s0], EXPERT_ROWS, n_valid(g))])
        _store_packed_rows(ybuf.at[s0], y)
        flush(g, s0).start()

    @pl.when(e == pl.num_programs(0) - 1)
    def _():
        for back in range(EXPERT_SLOTS, 0, -1):
            @pl.when(n_active >= back)
            def _():
                flush(n_active - back, (n_active - back) % EXPERT_SLOTS).wait()


def _experts(blk_first, blk_last, counts, n_active, xs, we_gate, we_up, we_down):
    wmap = lambda e, *_: (e, 0, 0)
    return pl.pallas_call(
        _expert_kernel,
        grid_spec=pltpu.PrefetchScalarGridSpec(
            num_scalar_prefetch=4,
            grid=(N_EXPERTS,),
            in_specs=[pl.BlockSpec(memory_space=pl.ANY),
                      pl.BlockSpec((None, D_MODEL, EXPERT_DIM), wmap),
                      pl.BlockSpec((None, D_MODEL, EXPERT_DIM), wmap),
                      pl.BlockSpec((None, EXPERT_DIM, D_MODEL), wmap)],
            out_specs=pl.BlockSpec(memory_space=pl.ANY),
            scratch_shapes=[pltpu.VMEM((D_MODEL, EXPERT_DIM), BF16),
                            pltpu.VMEM((D_MODEL, EXPERT_DIM), BF16),
                            pltpu.VMEM((EXPERT_DIM, D_MODEL), BF16),
                            pltpu.VMEM((EXPERT_SLOTS, BLOCK_SUBLANES, LANES), U32),
                            pltpu.VMEM((EXPERT_SLOTS, BLOCK_SUBLANES, LANES), U32),
                            pltpu.SemaphoreType.DMA((EXPERT_SLOTS,)),
                            pltpu.SemaphoreType.DMA((EXPERT_SLOTS,))]),
        out_shape=jax.ShapeDtypeStruct(xs.shape, U32),
        compiler_params=_cparams(("arbitrary",)),
        name="experts",
    )(blk_first, blk_last, counts, n_active, xs, we_gate, we_up, we_down)


COMBINE_ROWS = 8


def _combine_kernel(n_prompt_tiles,
                    dest_hbm, ys_hbm, xp_ref, xs_ref, h_ref, gate_ref, wsg_ref, wsu_ref, wsd_ref, gpost_ref, gate2_ref,
                    yp_ref, ysm_ref, dest_s, buf, shared_s, gcol_s, y_s, dsem, rsem):
    i = pl.program_id(0)
    nt = pl.num_programs(0)

    def dest_copy(tile):
        slot = tile % 3
        return pltpu.make_async_copy(dest_hbm.at[pl.ds(tile * DEST_PER_TILE, DEST_PER_TILE)],
                                     dest_s.at[pl.ds(slot * DEST_PER_TILE, DEST_PER_TILE)], dsem.at[slot])

    def gather_token(base, bset, t):
        dst0 = bset * TILE_SUBLANES + t * ROW_SUBLANES
        for k in range(TOP_K):
            d = dest_s[base + t + k * MOE_TILE]
            dst = buf.at[pl.ds(pl.multiple_of(dst0 + k * TILE_SUBLANES, ROW_SUBLANES), ROW_SUBLANES)]
            pltpu.make_async_copy(_row_slice(ys_hbm, d), dst, rsem.at[bset + k]).start(priority=k % 2)

    def wait_rows(bset):
        for k in range(TOP_K):
            pltpu.make_async_copy(ys_hbm.at[pl.ds(0, TILE_SUBLANES)], buf.at[pl.ds(0, TILE_SUBLANES)],
                                  rsem.at[bset + k]).wait()

    @pl.when(i == 0)
    def _():
        dest_copy(0).start()

        @pl.when(nt > 1)
        def _():
            dest_copy(1).start()

        dest_copy(0).wait()

        def first_tile(t, carry):
            gather_token(0, 0, t)
            return carry

        lax.fori_loop(0, MOE_TILE, first_tile, 0, unroll=4)

    @pl.when(i + 2 < nt)
    def _():
        dest_copy(i + 2).start()

    @pl.when(i + 1 < nt)
    def _():
        dest_copy(i + 1).wait()

    nxt = jnp.minimum(i + 1, nt - 1)
    base_next = (nxt % 3) * DEST_PER_TILE
    bset = (i % 2) * TOP_K
    bset_next = ((i + 1) % 2) * TOP_K

    ha, hb = _load_packed_rows(h_ref, MOE_TILE)
    ha = ha.astype(BF16)
    hb = hb.astype(BF16)

    def up(w_ref):
        return (jnp.dot(ha, w_ref[:HALF_D, :], preferred_element_type=F32)
                + jnp.dot(hb, w_ref[HALF_D:, :], preferred_element_type=F32))

    g = up(wsg_ref)
    a = (g * jax.nn.sigmoid(g) * up(wsu_ref)).astype(BF16)
    shared_s[...] = jnp.dot(a, wsd_ref[...], preferred_element_type=F32)

    gpad = jnp.concatenate([gate_ref[...], jnp.zeros((LANES - TOP_K, MOE_TILE), F32)], axis=0)
    gcol_s[...] = gpad.T

    wait_rows(bset)
    in_first = i < n_prompt_tiles

    def chunk(j, carry):
        r0 = pl.multiple_of(j * COMBINE_ROWS, COMBINE_ROWS)
        rows = pl.ds(r0, COMBINE_ROWS)
        acc_hi = [jnp.zeros((COMBINE_ROWS, LANES), F32) for _ in range(ROW_SUBLANES)]
        acc_lo = [jnp.zeros((COMBINE_ROWS, LANES), F32) for _ in range(ROW_SUBLANES)]
        src0 = bset * TILE_SUBLANES + r0 * ROW_SUBLANES
        for k in range(TOP_K):
            gk = jnp.broadcast_to(gcol_s[rows, k:k + 1], (COMBINE_ROWS, LANES))
            for s in range(ROW_SUBLANES):
                hi, lo = _unpack_halves(
                    buf[pl.ds(src0 + k * TILE_SUBLANES + s, COMBINE_ROWS, stride=ROW_SUBLANES), :])
                acc_hi[s] = acc_hi[s] + hi * gk
                acc_lo[s] = acc_lo[s] + lo * gk
        f = jnp.concatenate(acc_hi + acc_lo, axis=1) + shared_s[rows, :]
        x = jnp.where(in_first, xp_ref[rows, :], xs_ref[rows, :])
        gate2 = gate2_ref[lax.shift_right_logical(j * COMBINE_ROWS, CHUNK.bit_length() - 1)]
        y = x + _rms(f) * gpost_ref[...] * gate2
        for t in range(COMBINE_ROWS):
            gather_token(base_next, bset_next, r0 + t)
        y_s[rows, :] = y
        return carry

    lax.fori_loop(0, MOE_TILE // COMBINE_ROWS, chunk, 0)

    @pl.when(in_first)
    def _():
        yp_ref[...] = y_s[...]

    @pl.when(jnp.logical_not(in_first))
    def _():
        ysm_ref[...] = y_s[...]

    @pl.when(i == nt - 1)
    def _():
        wait_rows(bset_next)


def _combine(dest_flat, ys, x1_p, x1_s, h_packed, gates, ws_gate, ws_up, ws_down, g_post, gate2_g):
    n_prompt = x1_p.shape[0]
    n = n_prompt + x1_s.shape[0]
    nt = n // MOE_TILE
    npt = n_prompt // MOE_TILE
    gpt = MOE_TILE // CHUNK
    z2 = lambda i: (0, 0)
    return pl.pallas_call(
        functools.partial(_combine_kernel, npt),
        grid=(nt,),
        in_specs=[pl.BlockSpec(memory_space=pl.ANY),
                  pl.BlockSpec(memory_space=pl.ANY),
                  pl.BlockSpec((MOE_TILE, D_MODEL), lambda i: (jnp.minimum(i, npt - 1), 0)),
                  pl.BlockSpec((MOE_TILE, D_MODEL), lambda i: (jnp.maximum(i - npt, 0), 0)),
                  pl.BlockSpec((MOE_TILE * ROW_SUBLANES, LANES), lambda i: (i, 0)),
                  pl.BlockSpec((TOP_K, MOE_TILE), lambda i: (0, i)),
                  pl.BlockSpec((D_MODEL, EXPERT_DIM), z2),
                  pl.BlockSpec((D_MODEL, EXPERT_DIM), z2),
                  pl.BlockSpec((EXPERT_DIM, D_MODEL), z2),
                  pl.BlockSpec((1, D_MODEL), z2),
                  pl.BlockSpec((gpt, 1, D_MODEL), lambda i: (i, 0, 0))],
        out_specs=[pl.BlockSpec((MOE_TILE, D_MODEL), lambda i: (jnp.minimum(i, npt - 1), 0)),
                   pl.BlockSpec((MOE_TILE, D_MODEL), lambda i: (jnp.maximum(i - npt, 0), 0))],
        out_shape=[jax.ShapeDtypeStruct((n_prompt, D_MODEL), F32),
                   jax.ShapeDtypeStruct((n - n_prompt, D_MODEL), F32)],
        scratch_shapes=[pltpu.SMEM((3 * DEST_PER_TILE,), I32),
                        pltpu.VMEM((2 * TOP_K * TILE_SUBLANES, LANES), U32),
                        pltpu.VMEM((MOE_TILE, D_MODEL), F32),
                        pltpu.VMEM((MOE_TILE, LANES), F32),
                        pltpu.VMEM((MOE_TILE, D_MODEL), F32),
                        pltpu.SemaphoreType.DMA((3,)),
                        pltpu.SemaphoreType.DMA((2 * TOP_K,))],
        compiler_params=_cparams(("arbitrary",)),
        name="combine",
    )(dest_flat, ys, x1_p, x1_s, h_packed, gates, ws_gate, ws_up, ws_down, g_post, gate2_g)


def _rope_tables(pos):
    half = HEAD_DIM // 2
    inv = ROPE_THETA ** (-jnp.arange(half, dtype=F32) / half)
    ang = pos.astype(F32)[:, None] * inv[None, :]
    cos, sin = jnp.cos(ang), jnp.sin(ang)
    return jnp.concatenate([cos] * 4, axis=1), jnp.concatenate([-sin, sin, -sin, sin], axis=1)


def _groups(vec_rows, reps):
    rows, width = vec_rows.shape
    return jnp.broadcast_to(vec_rows[:, None, :], (rows, reps, width)).reshape(rows * reps, 1, width)


def kernel(x_prompt, x_sample, cache_a_k, cache_a_v, cache_b_k, cache_b_v, c_prompt, c_sample, w_ada, b_ada,
           g_pre_mix, g_post_mix, w_in, rel_bias_a, sinks_b, w_branch_a, w_branch_b, w_out, g_pre_ffn,
           g_post_ffn, w_router, b_router, we_gate, we_up, we_down, ws_gate, ws_up, ws_down):
    assert w_ada.shape[0] == 1, "single layer"
    nb, seq, d = x_prompt.shape
    ns, dec = x_sample.shape[:2]
    assert d == D_MODEL and dec == CHUNK and seq % TOK_TILE == 0 and (ns * dec) == TOK_TILE
    n_p, n_s = nb * seq, ns * dec
    n_all = n_p + n_s
    assert n_all % MOE_TILE == 0 and n_p % MOE_TILE == 0

    c_all = jnp.concatenate([c_prompt, c_sample], axis=0)
    pad = (-c_all.shape[0]) % 8
    c_all = jnp.pad(c_all, ((0, pad), (0, 0)))
    mod = _modulation(c_all, w_ada[0], b_ada[0])
    mod_p, mod_s = mod[:nb], mod[nb:nb + ns]
    cpp = seq // CHUNK

    def part(k):
        return mod_p[:, k * d:(k + 1) * d], mod_s[:, k * d:(k + 1) * d]

    (sh1p, sh1s), (sc1p, sc1s), (g1p, g1s), (sh2p, sh2s), (sc2p, sc2s), (g2p, g2s) = [part(k) for k in range(6)]
    both = lambda p, s: jnp.concatenate([_groups(p, cpp), _groups(s, 1)], axis=0)

    w_in_bf = w_in[0].astype(BF16)
    g_pre = g_pre_mix[0].reshape(1, d)
    cos_p, sin_p = _rope_tables(jnp.arange(seq))
    cos_s, sin_s = _rope_tables(PAST_LEN + jnp.arange(dec))
    cos_s, sin_s = jnp.tile(cos_s, (ns, 1)), jnp.tile(sin_s, (ns, 1))

    xp2 = x_prompt.reshape(n_p, d)
    xs2 = x_sample.reshape(n_s, d)
    outs_p = _inproj(xp2, _groups(sh1p, cpp), _groups(sc1p, cpp), g_pre, w_in_bf, cos_p, sin_p, nb, True)
    outs_s = _inproj(xs2, _groups(sh1s, 1), _groups(sc1s, 1), g_pre, w_in_bf, cos_s, sin_s, ns, False)

    table = rel_bias_a[0].astype(F32)
    n_far = A_BAND - 1 - REL_CLIP
    ext = jnp.concatenate([jnp.broadcast_to(table[:, 2 * REL_CLIP:], (A_HEADS, n_far)),
                           jnp.flip(table[:, REL_CLIP - (CHUNK - 1):], axis=1)], axis=1)
    bias = jnp.stack([ext[:, CHUNK - 1 - q:CHUNK - 1 - q + A_BAND] for q in range(CHUNK)], axis=1)
    bias_pairs = bias.reshape(A_HEADS // 2, 2 * CHUNK, A_BAND)
    sink_rows = jnp.broadcast_to(sinks_b[0].astype(F32).reshape(B_KV_HEADS, B_GROUP, 1),
                                 (B_KV_HEADS, B_GROUP, CHUNK)).reshape(B_KV_HEADS, B_GROUP * CHUNK, 1)
    consts = (bias_pairs, sink_rows, w_branch_a[0].astype(BF16), w_branch_b[0].astype(BF16),
              w_out[0].astype(BF16), g_post_mix[0].reshape(1, d))

    x1_p = _attn_prompt(xp2, outs_p[:7], consts, _groups(g1p, cpp), nb)
    caches = (cache_a_k[0].reshape(ns, A_ROWS, WA), cache_a_v[0].reshape(ns, A_ROWS, WA),
              cache_b_k[0].reshape(ns, WINDOW, WKB), cache_b_v[0].reshape(ns, WINDOW, WKB))
    x1_s = _attn_sample(xs2, outs_s[:7], caches, consts, _groups(g1s, 1))

    h_packed, idx, gates, rank, counts = _router(
        x1_p, x1_s, both(sh2p, sh2s), both(sc2p, sc2s), g_pre_ffn[0].reshape(1, d),
        w_router[0].astype(F32).T, b_router[0].astype(F32).reshape(N_EXPERTS, 1))
    n_blocks = (n_all * TOP_K) // EXPERT_ROWS + N_EXPERTS
    cnt = counts[:, 0].astype(I32)
    blocks_e = (cnt + EXPERT_ROWS - 1) // EXPERT_ROWS
    blk_end = jnp.cumsum(blocks_e)
    blk_start = blk_end - blocks_e
    n_active = blk_end[-1:]
    start_col = (blk_start * EXPERT_ROWS).astype(F32).reshape(N_EXPERTS, 1)

    dest_flat = _dest_rows(idx, rank, start_col).reshape(-1)
    xs = _dispatch(dest_flat, h_packed, n_blocks * EXPERT_ROWS)
    ys = _experts(blk_start.astype(I32), blk_end.astype(I32), cnt, n_active.astype(I32), xs,
                  we_gate[0], we_up[0], we_down[0])
    y_p, y_s = _combine(dest_flat, ys, x1_p, x1_s, h_packed, gates, ws_gate[0].astype(BF16),
                        ws_up[0].astype(BF16), ws_down[0].astype(BF16), g_post_ffn[0].reshape(1, d),
                        both(g2p, g2s))

    a_heads = (A_HEADS, HEAD_DIM)
    b_heads = (B_KV_HEADS, HEAD_DIM)
    return (y_p.reshape(nb, seq, d), y_s.reshape(ns, dec, d),
            outs_p[7].reshape(1, nb, A_ROWS, *a_heads), outs_p[8].reshape(1, nb, A_ROWS, *a_heads),
            outs_p[9].reshape(1, nb, WINDOW, *b_heads), outs_p[10].reshape(1, nb, WINDOW, *b_heads),
            outs_s[7].reshape(1, ns, dec, *a_heads), outs_s[8].reshape(1, ns, dec, *a_heads),
            outs_s[9].reshape(1, ns, dec, *b_heads), outs_s[10].reshape(1, ns, dec, *b_heads))
```

```python
import functools

import jax
import jax.numpy as jnp
from jax import lax
from jax.experimental import pallas as pl
from jax.experimental.pallas import tpu as pltpu

F32 = jnp.float32
BF16 = jnp.bfloat16
I32 = jnp.int32
U32 = jnp.uint32
HIGHEST = lax.Precision.HIGHEST
LOG2_E = 1.4426950408889634

D_MODEL = 1024
CHUNK = 64
HEAD_DIM = 64
A_HEADS = 8
A_PREV_CHUNKS = 8
A_ROWS = A_PREV_CHUNKS * CHUNK
A_BAND = A_ROWS + CHUNK
REL_CLIP = 128
B_HEADS = 8
B_KV_HEADS = 2
B_GROUP = B_HEADS // B_KV_HEADS
WINDOW = 128
B_BAND = WINDOW + CHUNK
ROPE_THETA = 10000.0
N_EXPERTS = 256
TOP_K = 8
N_GROUPS = 8
PER_GROUP = N_EXPERTS // N_GROUPS
TOPK_GROUPS = 4
EXPERT_DIM = 256
ROUTED_SCALE = 2.5
EPS = 1e-6
PAST_LEN = 4096

WA = A_HEADS * HEAD_DIM
WB = B_HEADS * HEAD_DIM
WKB = B_KV_HEADS * HEAD_DIM
OFF_QA, OFF_KA, OFF_VA = 0, WA, 2 * WA
OFF_QB = 3 * WA
OFF_KB = OFF_QB + WB
OFF_VB = OFF_KB + WKB
OFF_G = OFF_VB + WKB
N_IN = OFF_G + 2 * D_MODEL

LANES = 128
TOK_TILE = 512
MOE_TILE = 256
EXPERT_ROWS = 256
HALF_D = D_MODEL // 2
VMEM_LIMIT = 56 * 1024 * 1024


def _cparams(sem, vmem=VMEM_LIMIT):
    return pltpu.CompilerParams(dimension_semantics=sem, vmem_limit_bytes=vmem)


def _rms(x):
    return x * lax.rsqrt(jnp.mean(x * x, axis=-1, keepdims=True) + EPS)


def _group_affine(y, mul, add):
    g = mul.shape[0]
    y3 = y.reshape(g, CHUNK, y.shape[-1]) * mul
    if add is not None:
        y3 = y3 + add
    return y3.reshape(g * CHUNK, y.shape[-1])


def _pack_halves(a, b):
    ua = lax.bitcast_convert_type(a.astype(BF16).astype(F32), U32)
    ub = lax.bitcast_convert_type(b.astype(BF16).astype(F32), U32)
    return (ua & jnp.uint32(0xFFFF0000)) | (ub >> 16)


def _unpack_halves(u):
    a = lax.bitcast_convert_type(u & jnp.uint32(0xFFFF0000), F32)
    b = lax.bitcast_convert_type(u << 16, F32)
    return a, b


ROW_SUBLANES = HALF_D // LANES


def _store_packed_rows(ref, x):
    rows = x.shape[0]
    p = _pack_halves(x[:, :HALF_D], x[:, HALF_D:])
    for s in range(ROW_SUBLANES):
        ref[pl.ds(s, rows, stride=ROW_SUBLANES), :] = p[:, s * LANES:(s + 1) * LANES]


def _load_packed_rows(ref, rows, n_valid=None):
    his, los = [], []
    for s in range(ROW_SUBLANES):
        u = ref[pl.ds(s, rows, stride=ROW_SUBLANES), :]
        if n_valid is not None:
            u = jnp.where(lax.broadcasted_iota(I32, u.shape, 0) < n_valid, u, jnp.uint32(0))
        a, b = _unpack_halves(u)
        his.append(a)
        los.append(b)
    return jnp.concatenate(his, axis=1), jnp.concatenate(los, axis=1)


def _mod_kernel(c_ref, w_ref, b_ref, o_ref):
    c = c_ref[...]
    s = c * jax.nn.sigmoid(c)
    o_ref[...] = jnp.dot(s, w_ref[...], precision=HIGHEST, preferred_element_type=F32) + b_ref[...]


def _modulation(c_all, w_ada, b_ada):
    rows = c_all.shape[0]
    n = w_ada.shape[1]
    tn = 512
    return pl.pallas_call(
        _mod_kernel,
        grid=(n // tn,),
        in_specs=[pl.BlockSpec((rows, D_MODEL), lambda j: (0, 0)),
                  pl.BlockSpec((D_MODEL, tn), lambda j: (0, j)),
                  pl.BlockSpec((1, tn), lambda j: (0, j))],
        out_specs=pl.BlockSpec((rows, tn), lambda j: (0, j)),
        out_shape=jax.ShapeDtypeStruct((rows, n), F32),
        compiler_params=_cparams(("arbitrary",)),
        name="modulation",
    )(c_all, w_ada, b_ada.reshape(1, n))


def _rope(x, cos, sin_signed):
    n = x.shape[-1]
    reps = n // LANES
    if reps > 1:
        cos = jnp.concatenate([cos] * reps, axis=1)
        sin_signed = jnp.concatenate([sin_signed] * reps, axis=1)
    lane = lax.broadcasted_iota(I32, x.shape, 1)
    first_half = (lane % HEAD_DIM) < (HEAD_DIM // 2)
    partner = jnp.where(first_half, pltpu.roll(x, n - HEAD_DIM // 2, 1), pltpu.roll(x, HEAD_DIM // 2, 1))
    return x * cos + partner * sin_signed


def _inproj_kernel(prompt_state, tiles_per_seq,
                   x_ref, sh_ref, sc_ref, g_ref, w_ref, cos_ref, sin_ref,
                   qa_ref, ka_ref, va_ref, qb_ref, kb_ref, vb_ref, gt_ref,
                   ska_ref, sva_ref, skb_ref, svb_ref):
    x = x_ref[...]
    h = _group_affine(_rms(x) * g_ref[...], 1.0 + sc_ref[...], sh_ref[...]).astype(BF16)

    def proj(off, width):
        return jnp.dot(h, w_ref[:, off:off + width], preferred_element_type=F32)

    cos = cos_ref[...]
    sin = sin_ref[...]
    scale = HEAD_DIM ** -0.5 * LOG2_E
    qa_ref[...] = (proj(OFF_QA, WA) * scale).astype(BF16)
    ka = proj(OFF_KA, WA)
    va = proj(OFF_VA, WA)
    ka_ref[...] = ka.astype(BF16)
    va_ref[...] = va.astype(BF16)
    qb_ref[...] = (_rope(proj(OFF_QB, WB), cos, sin) * scale).astype(BF16)
    kb = _rope(proj(OFF_KB, WKB), cos, sin)
    vb = proj(OFF_VB, WKB)
    kb_ref[...] = kb.astype(BF16)
    vb_ref[...] = vb.astype(BF16)
    gt_ref[...] = jax.nn.sigmoid(proj(OFF_G, 2 * D_MODEL)).astype(BF16)

    if prompt_state:
        @pl.when(pl.program_id(0) % tiles_per_seq == tiles_per_seq - 1)
        def _():
            ska_ref[...] = ka
            sva_ref[...] = va
            skb_ref[...] = kb[TOK_TILE - WINDOW:, :]
            svb_ref[...] = vb[TOK_TILE - WINDOW:, :]
    else:
        ska_ref[...] = ka
        sva_ref[...] = va
        skb_ref[...] = kb
        svb_ref[...] = vb


def _inproj(x2d, shift_g, scale_g, g_pre, w_in_bf, cos_tab, sin_tab, n_seq, prompt_state):
    n = x2d.shape[0]
    nt = n // TOK_TILE
    tiles_per_seq = nt // n_seq if prompt_state else 1
    tab_tiles = cos_tab.shape[0] // TOK_TILE
    gpt = TOK_TILE // CHUNK
    row = lambda i: (i, 0)
    grp = lambda i: (i, 0, 0)
    if prompt_state:
        st_shapes = [jax.ShapeDtypeStruct((n_seq, A_ROWS, WA), F32)] * 2 + \
                    [jax.ShapeDtypeStruct((n_seq, WINDOW, WKB), F32)] * 2
        st_specs = [pl.BlockSpec((None, A_ROWS, WA), lambda i: (i // tiles_per_seq, 0, 0))] * 2 + \
                   [pl.BlockSpec((None, WINDOW, WKB), lambda i: (i // tiles_per_seq, 0, 0))] * 2
    else:
        st_shapes = [jax.ShapeDtypeStruct((n, WA), F32)] * 2 + [jax.ShapeDtypeStruct((n, WKB), F32)] * 2
        st_specs = [pl.BlockSpec((TOK_TILE, WA), row)] * 2 + [pl.BlockSpec((TOK_TILE, WKB), row)] * 2
    out_shapes = [jax.ShapeDtypeStruct((n, WA), BF16)] * 4 + [jax.ShapeDtypeStruct((n, WKB), BF16)] * 2 + \
                 [jax.ShapeDtypeStruct((n, 2 * D_MODEL), BF16)]
    out_shapes = [out_shapes[0], out_shapes[1], out_shapes[2], out_shapes[3], out_shapes[4], out_shapes[5],
                  out_shapes[6]] + st_shapes
    out_specs = [pl.BlockSpec((TOK_TILE, WA), row)] * 4 + [pl.BlockSpec((TOK_TILE, WKB), row)] * 2 + \
                [pl.BlockSpec((TOK_TILE, 2 * D_MODEL), row)] + st_specs
    return pl.pallas_call(
        functools.partial(_inproj_kernel, prompt_state, tiles_per_seq),
        grid=(nt,),
        in_specs=[pl.BlockSpec((TOK_TILE, D_MODEL), row),
                  pl.BlockSpec((gpt, 1, D_MODEL), grp),
                  pl.BlockSpec((gpt, 1, D_MODEL), grp),
                  pl.BlockSpec((1, D_MODEL), lambda i: (0, 0)),
                  pl.BlockSpec((D_MODEL, N_IN), lambda i: (0, 0)),
                  pl.BlockSpec((TOK_TILE, LANES), lambda i: (i % tab_tiles, 0)),
                  pl.BlockSpec((TOK_TILE, LANES), lambda i: (i % tab_tiles, 0))],
        out_specs=out_specs,
        out_shape=out_shapes,
        compiler_params=_cparams(("arbitrary",)),
        name="inproj_prompt" if prompt_state else "inproj_sample",
    )(x2d, shift_g, scale_g, g_pre, w_in_bf, cos_tab, sin_tab)


def _attn_kernel(n_chunks, mask_first,
                 x_ref, qa_ref, qb_ref, gt_ref,
                 kap_ref, kac_ref, vap_ref, vac_ref, kbp_ref, kbc_ref, vbp_ref, vbc_ref,
                 bias_ref, sink_ref, wba_ref, wbb_ref, wout_ref, gpost_ref, gate1_ref,
                 o_ref,
                 ka_s, va_s, kb_s, vb_s, oa_s, ob_s):
    rows = n_chunks * CHUNK
    pb = kbp_ref.shape[0]
    ka_s[0:A_ROWS, :] = kap_ref[...].astype(BF16)
    va_s[0:A_ROWS, :] = vap_ref[...].astype(BF16)
    ka_s[A_ROWS:A_ROWS + rows, :] = kac_ref[...]
    va_s[A_ROWS:A_ROWS + rows, :] = vac_ref[...]
    kb_s[0:WINDOW, :] = kbp_ref[pb - WINDOW:pb, :].astype(BF16)
    vb_s[0:WINDOW, :] = vbp_ref[pb - WINDOW:pb, :].astype(BF16)
    kb_s[WINDOW:WINDOW + rows, :] = kbc_ref[...]
    vb_s[WINDOW:WINDOW + rows, :] = vbc_ref[...]

    lane_q = lax.broadcasted_iota(I32, (CHUNK, LANES), 1)
    nt_dims = (((1,), (1,)), ((), ()))

    def chunk_body(masked, c, carry):
        c0 = pl.multiple_of(c * CHUNK, CHUNK)
        if masked:
            valid_a = c0 + lax.broadcasted_iota(I32, (1, A_BAND), 1) >= A_ROWS
            valid_b = c0 + lax.broadcasted_iota(I32, (1, B_BAND), 1) >= WINDOW

        scores = []
        for p in range(A_HEADS // 2):
            cols = slice(p * LANES, (p + 1) * LANES)
            q = qa_ref[pl.ds(c0, CHUNK), cols].astype(F32)
            qs = jnp.concatenate([jnp.where(lane_q < HEAD_DIM, q, 0.0),
                                  jnp.where(lane_q >= HEAD_DIM, q, 0.0)], axis=0).astype(BF16)
            k = ka_s[pl.ds(c0, A_BAND), cols]
            s = lax.dot_general(qs, k, nt_dims, preferred_element_type=F32) + bias_ref[p]
            if masked:
                s = jnp.where(valid_a, s, -jnp.inf)
            scores.append(s)
        for g in range(B_KV_HEADS):
            parts = []
            for r in range(B_GROUP):
                head = g * B_GROUP + r
                t, half = head // 2, head % 2
                q = qb_ref[pl.ds(c0, CHUNK), t * LANES:(t + 1) * LANES].astype(F32)
                if half != g:
                    q = pltpu.roll(q, HEAD_DIM, 1)
                in_g = (lane_q >= HEAD_DIM) if g else (lane_q < HEAD_DIM)
                parts.append(jnp.where(in_g, q, 0.0))
            qs = jnp.concatenate(parts, axis=0).astype(BF16)
            k = kb_s[pl.ds(c0, B_BAND), :]
            s = lax.dot_general(qs, k, nt_dims, preferred_element_type=F32)
            if masked:
                s = jnp.where(valid_b, s, -jnp.inf)
            scores.append(s)

        numer, denom = [], []
        for n, s in enumerate(scores):
            m = jnp.max(s, axis=1, keepdims=True)
            if n >= A_HEADS // 2:
                sk = sink_ref[n - A_HEADS // 2]
                m = jnp.maximum(m, sk)
            e = jnp.exp2(s - m)
            l = jnp.sum(e, axis=1, keepdims=True)
            if n >= A_HEADS // 2:
                l = l + jnp.exp2(sk - m)
            numer.append(e.astype(BF16))
            denom.append(l)

        outs = []
        for n, e in enumerate(numer):
            if n < A_HEADS // 2:
                v = va_s[pl.ds(c0, A_BAND), n * LANES:(n + 1) * LANES]
            else:
                v = vb_s[pl.ds(c0, B_BAND), :]
            outs.append(jnp.dot(e, v, preferred_element_type=F32) / denom[n])

        for p in range(A_HEADS // 2):
            o = outs[p]
            oa_s[pl.ds(c0, CHUNK), p * LANES:(p + 1) * LANES] = jnp.where(
                lane_q < HEAD_DIM, o[:CHUNK], o[CHUNK:]).astype(BF16)
        for g in range(B_KV_HEADS):
            o = outs[A_HEADS // 2 + g]
            for s2 in range(B_GROUP // 2):
                o_even = o[(2 * s2) * CHUNK:(2 * s2 + 1) * CHUNK]
                o_odd = o[(2 * s2 + 1) * CHUNK:(2 * s2 + 2) * CHUNK]
                if g == 0:
                    tile = jnp.where(lane_q < HEAD_DIM, o_even, pltpu.roll(o_odd, HEAD_DIM, 1))
                else:
                    tile = jnp.where(lane_q < HEAD_DIM, pltpu.roll(o_even, HEAD_DIM, 1), o_odd)
                t = g * (B_GROUP // 2) + s2
                ob_s[pl.ds(c0, CHUNK), t * LANES:(t + 1) * LANES] = tile.astype(BF16)
        return carry

    def all_chunks(masked):
        lax.fori_loop(0, n_chunks, functools.partial(chunk_body, masked), 0, unroll=2 if n_chunks > 1 else 1)

    if mask_first:
        pl.when(pl.program_id(1) == 0)(lambda: all_chunks(True))
        pl.when(pl.program_id(1) > 0)(lambda: all_chunks(False))
    else:
        all_chunks(False)

    za = jnp.dot(oa_s[...], wba_ref[...], preferred_element_type=F32)
    zb = jnp.dot(ob_s[...], wbb_ref[...], preferred_element_type=F32)
    merged = gt_ref[:, :D_MODEL].astype(F32) * za + gt_ref[:, D_MODEL:].astype(F32) * zb
    mo = jnp.dot(merged.astype(BF16), wout_ref[...], preferred_element_type=F32)
    o_ref[...] = x_ref[...] + _group_affine(_rms(mo) * gpost_ref[...], gate1_ref[...], None)


def _attn_scratch(rows):
    return [pltpu.VMEM((A_ROWS + rows, WA), BF16), pltpu.VMEM((A_ROWS + rows, WA), BF16),
            pltpu.VMEM((WINDOW + rows, WKB), BF16), pltpu.VMEM((WINDOW + rows, WKB), BF16),
            pltpu.VMEM((rows, WA), BF16), pltpu.VMEM((rows, WB), BF16)]


def _const_specs(grid_rank):
    z2 = (lambda b, j: (0, 0)) if grid_rank == 2 else (lambda b: (0, 0))
    z3 = (lambda b, j: (0, 0, 0)) if grid_rank == 2 else (lambda b: (0, 0, 0))
    return [pl.BlockSpec((A_HEADS // 2, 2 * CHUNK, A_BAND), z3),
            pl.BlockSpec((B_KV_HEADS, B_GROUP * CHUNK, 1), z3),
            pl.BlockSpec((WA, D_MODEL), z2),
            pl.BlockSpec((WB, D_MODEL), z2),
            pl.BlockSpec((D_MODEL, D_MODEL), z2),
            pl.BlockSpec((1, D_MODEL), z2)]


def _attn_prompt(x2d, proj, consts, gate1_g, n_seq):
    qa, ka, va, qb, kb, vb, gt = proj
    n = x2d.shape[0]
    tps = n // n_seq // TOK_TILE
    gpt = TOK_TILE // CHUNK
    cur = lambda b, j: (b * tps + j, 0)
    prev = lambda b, j: (b * tps + jnp.maximum(j - 1, 0), 0)
    return pl.pallas_call(
        functools.partial(_attn_kernel, TOK_TILE // CHUNK, True),
        grid=(n_seq, tps),
        in_specs=[pl.BlockSpec((TOK_TILE, D_MODEL), cur),
                  pl.BlockSpec((TOK_TILE, WA), cur),
                  pl.BlockSpec((TOK_TILE, WB), cur),
                  pl.BlockSpec((TOK_TILE, 2 * D_MODEL), cur),
                  pl.BlockSpec((TOK_TILE, WA), prev), pl.BlockSpec((TOK_TILE, WA), cur),
                  pl.BlockSpec((TOK_TILE, WA), prev), pl.BlockSpec((TOK_TILE, WA), cur),
                  pl.BlockSpec((TOK_TILE, WKB), prev), pl.BlockSpec((TOK_TILE, WKB), cur),
                  pl.BlockSpec((TOK_TILE, WKB), prev), pl.BlockSpec((TOK_TILE, WKB), cur)]
                 + _const_specs(2)
                 + [pl.BlockSpec((gpt, 1, D_MODEL), lambda b, j: (b * tps + j, 0, 0))],
        out_specs=pl.BlockSpec((TOK_TILE, D_MODEL), cur),
        out_shape=jax.ShapeDtypeStruct((n, D_MODEL), F32),
        scratch_shapes=_attn_scratch(TOK_TILE),
        compiler_params=_cparams(("arbitrary", "arbitrary")),
        name="attn_prompt",
    )(x2d, qa, qb, gt, ka, ka, va, va, kb, kb, vb, vb, *consts, gate1_g)


def _attn_sample(x2d, proj, caches, consts, gate1_g):
    qa, ka, va, qb, kb, vb, gt = proj
    cak, cav, cbk, cbv = caches
    n_seq = cak.shape[0]
    cur = lambda b: (b, 0)
    cache = lambda b: (b, 0, 0)
    return pl.pallas_call(
        functools.partial(_attn_kernel, 1, False),
        grid=(n_seq,),
        in_specs=[pl.BlockSpec((CHUNK, D_MODEL), cur),
                  pl.BlockSpec((CHUNK, WA), cur),
                  pl.BlockSpec((CHUNK, WB), cur),
                  pl.BlockSpec((CHUNK, 2 * D_MODEL), cur),
                  pl.BlockSpec((None, A_ROWS, WA), cache), pl.BlockSpec((CHUNK, WA), cur),
                  pl.BlockSpec((None, A_ROWS, WA), cache), pl.BlockSpec((CHUNK, WA), cur),
                  pl.BlockSpec((None, WINDOW, WKB), cache), pl.BlockSpec((CHUNK, WKB), cur),
                  pl.BlockSpec((None, WINDOW, WKB), cache), pl.BlockSpec((CHUNK, WKB), cur)]
                 + _const_specs(1)
                 + [pl.BlockSpec((1, 1, D_MODEL), lambda b: (b, 0, 0))],
        out_specs=pl.BlockSpec((CHUNK, D_MODEL), cur),
        out_shape=jax.ShapeDtypeStruct(x2d.shape, F32),
        scratch_shapes=_attn_scratch(CHUNK),
        compiler_params=_cparams(("arbitrary",)),
        name="attn_sample",
    )(x2d, qa, qb, gt, cak, ka, cav, va, cbk, kb, cbv, vb, *consts, gate1_g)


def _two_part_tile(n_first_tiles, first_ref, second_ref):
    return jnp.where(pl.program_id(0) < n_first_tiles, first_ref[...], second_ref[...])


def _router_kernel(n_prompt_tiles, xp_ref, xs_ref, sh_ref, sc_ref, g_ref, wrt_ref, br_ref,
                   h_ref, idx_ref, gate_ref, rank_ref, cnt_ref, carry):
    i = pl.program_id(0)

    @pl.when(i == 0)
    def _():
        carry[...] = jnp.zeros_like(carry)

    tm = xp_ref.shape[0]
    x = _two_part_tile(n_prompt_tiles, xp_ref, xs_ref)
    h = _group_affine(_rms(x) * g_ref[...], 1.0 + sc_ref[...], sh_ref[...])
    _store_packed_rows(h_ref, h)

    logits = lax.dot_general(wrt_ref[...], h.astype(BF16), (((1,), (1,)), ((), ())),
                             preferred_element_type=F32)
    scores = jax.nn.sigmoid(logits)
    biased = scores + br_ref[...]
    neg = -jnp.inf

    sub = lax.broadcasted_iota(I32, (PER_GROUP, tm), 0).astype(F32)
    gs_rows = []
    for g in range(N_GROUPS):
        xg = biased[g * PER_GROUP:(g + 1) * PER_GROUP]
        m1 = jnp.max(xg, axis=0, keepdims=True)
        i1 = jnp.min(jnp.where(xg == m1, sub, float(PER_GROUP)), axis=0, keepdims=True)
        m2 = jnp.max(jnp.where(sub == i1, neg, xg), axis=0, keepdims=True)
        gs_rows.append(m1 + m2)
    gs = jnp.concatenate(gs_rows, axis=0)

    giota = lax.broadcasted_iota(I32, (N_GROUPS, tm), 0).astype(F32)
    keep = jnp.zeros((N_GROUPS, tm), F32)
    for _ in range(TOPK_GROUPS):
        m = jnp.max(gs, axis=0, keepdims=True)
        gi = jnp.min(jnp.where(gs == m, giota, float(N_GROUPS)), axis=0, keepdims=True)
        hit = giota == gi
        keep = jnp.where(hit, 1.0, keep)
        gs = jnp.where(hit, neg, gs)
    cand = jnp.concatenate(
        [jnp.where(keep[g:g + 1] > 0.0, biased[g * PER_GROUP:(g + 1) * PER_GROUP], neg)
         for g in range(N_GROUPS)], axis=0)

    eiota = lax.broadcasted_iota(I32, (N_EXPERTS, tm), 0).astype(F32)
    idx_rows, gate_rows = [], []
    chosen = jnp.zeros((N_EXPERTS, tm), F32)
    for _ in range(TOP_K):
        m = jnp.max(cand, axis=0, keepdims=True)
        ei = jnp.min(jnp.where(cand == m, eiota, float(N_EXPERTS)), axis=0, keepdims=True)
        sel = eiota == ei
        gate_rows.append(jnp.sum(jnp.where(sel, scores, 0.0), axis=0, keepdims=True))
        idx_rows.append(ei)
        chosen = jnp.where(sel, 1.0, chosen)
        cand = jnp.where(sel, neg, cand)
    gates = jnp.concatenate(gate_rows, axis=0)
    gates = gates / jnp.sum(gates, axis=0, keepdims=True) * ROUTED_SCALE
    gate_ref[...] = gates
    idx_ref[...] = jnp.concatenate(idx_rows, axis=0).astype(I32)

    r_i = lax.broadcasted_iota(I32, (tm, tm), 0)
    c_i = lax.broadcasted_iota(I32, (tm, tm), 1)
    upper = jnp.where(r_i < c_i, 1.0, 0.0).astype(BF16)
    before = jnp.dot(chosen.astype(BF16), upper, preferred_element_type=F32) + carry[...]
    rank_rows = [jnp.sum(jnp.where(eiota == idx_rows[k], before, 0.0), axis=0, keepdims=True)
                 for k in range(TOP_K)]
    rank_ref[...] = jnp.concatenate(rank_rows, axis=0).astype(I32)
    total = carry[...] + jnp.sum(chosen, axis=1, keepdims=True)
    carry[...] = total
    cnt_ref[...] = total


def _router(x1_p, x1_s, shift_g, scale_g, g_pre, w_router_t, b_router_col):
    n = x1_p.shape[0] + x1_s.shape[0]
    nt = n // MOE_TILE
    npt = x1_p.shape[0] // MOE_TILE
    gpt = MOE_TILE // CHUNK
    lane_blk = lambda i: (0, i)
    return pl.pallas_call(
        functools.partial(_router_kernel, npt),
        grid=(nt,),
        in_specs=[pl.BlockSpec((MOE_TILE, D_MODEL), lambda i: (jnp.minimum(i, npt - 1), 0)),
                  pl.BlockSpec((MOE_TILE, D_MODEL), lambda i: (jnp.maximum(i - npt, 0), 0)),
                  pl.BlockSpec((gpt, 1, D_MODEL), lambda i: (i, 0, 0)),
                  pl.BlockSpec((gpt, 1, D_MODEL), lambda i: (i, 0, 0)),
                  pl.BlockSpec((1, D_MODEL), lambda i: (0, 0)),
                  pl.BlockSpec((N_EXPERTS, D_MODEL), lambda i: (0, 0)),
                  pl.BlockSpec((N_EXPERTS, 1), lambda i: (0, 0))],
        out_specs=[pl.BlockSpec((MOE_TILE * ROW_SUBLANES, LANES), lambda i: (i, 0)),
                   pl.BlockSpec((TOP_K, MOE_TILE), lane_blk),
                   pl.BlockSpec((TOP_K, MOE_TILE), lane_blk),
                   pl.BlockSpec((TOP_K, MOE_TILE), lane_blk),
                   pl.BlockSpec((N_EXPERTS, 1), lambda i: (0, 0))],
        out_shape=[jax.ShapeDtypeStruct((n * ROW_SUBLANES, LANES), U32),
                   jax.ShapeDtypeStruct((TOP_K, n), I32),
                   jax.ShapeDtypeStruct((TOP_K, n), F32),
                   jax.ShapeDtypeStruct((TOP_K, n), I32),
                   jax.ShapeDtypeStruct((N_EXPERTS, 1), F32)],
        scratch_shapes=[pltpu.VMEM((N_EXPERTS, 1), F32)],
        compiler_params=_cparams(("arbitrary",)),
        name="router",
    )(x1_p, x1_s, shift_g, scale_g, g_pre, w_router_t, b_router_col)


def _dest_kernel(idx_ref, rank_ref, start_ref, o_ref):
    eiota = lax.broadcasted_iota(I32, (N_EXPERTS, MOE_TILE), 0)
    start = start_ref[...]
    for sub in range(o_ref.shape[0]):
        cols = slice(sub * MOE_TILE, (sub + 1) * MOE_TILE)
        rows = [jnp.sum(jnp.where(eiota == idx_ref[k:k + 1, cols], start, 0.0), axis=0, keepdims=True)
                for k in range(TOP_K)]
        o_ref[sub] = jnp.concatenate(rows, axis=0).astype(I32) + rank_ref[:, cols]


def _dest_rows(idx, rank, start_col):
    n = idx.shape[1]
    nt = n // MOE_TILE
    per_step = next(c for c in (10, 8, 5, 4, 2, 1) if nt % c == 0)
    return pl.pallas_call(
        _dest_kernel,
        grid=(nt // per_step,),
        in_specs=[pl.BlockSpec((TOP_K, per_step * MOE_TILE), lambda i: (0, i)),
                  pl.BlockSpec((TOP_K, per_step * MOE_TILE), lambda i: (0, i)),
                  pl.BlockSpec((N_EXPERTS, 1), lambda i: (0, 0))],
        out_specs=pl.BlockSpec((per_step, TOP_K, MOE_TILE), lambda i: (i, 0, 0)),
        out_shape=jax.ShapeDtypeStruct((nt, TOP_K, MOE_TILE), I32),
        compiler_params=_cparams(("arbitrary",)),
        name="dest_rows",
    )(idx, rank, start_col)


DEST_PER_TILE = TOP_K * MOE_TILE


def _row_slice(ref, row):
    return ref.at[pl.ds(pl.multiple_of(row * ROW_SUBLANES, ROW_SUBLANES), ROW_SUBLANES)]


TILE_SUBLANES = MOE_TILE * ROW_SUBLANES


def _dispatch_kernel(dest_hbm, h_hbm, xs_hbm, dest_s, hbuf, dsem, hsem, rsem):
    i = pl.program_id(0)
    nt = pl.num_programs(0)

    def load(tile):
        dst = dest_s.at[pl.ds((tile % 2) * DEST_PER_TILE, DEST_PER_TILE)]
        rows = h_hbm.at[pl.ds(pl.multiple_of(tile * TILE_SUBLANES, TILE_SUBLANES), TILE_SUBLANES)]
        return (pltpu.make_async_copy(dest_hbm.at[pl.ds(tile * DEST_PER_TILE, DEST_PER_TILE)], dst, dsem.at[tile % 2]),
                pltpu.make_async_copy(rows, hbuf.at[tile % 3], hsem.at[tile % 3]))

    def drain(tile):
        for k in range(TOP_K):
            pltpu.make_async_copy(hbuf.at[tile % 3], xs_hbm.at[pl.ds(0, TILE_SUBLANES)],
                                  rsem.at[(tile % 2) * TOP_K + k]).wait()

    @pl.when(i == 0)
    def _():
        for cp in load(0):
            cp.start()

    @pl.when(i + 1 < nt)
    def _():
        for cp in load(i + 1):
            cp.start()

    for cp in load(i):
        cp.wait()
    base = (i % 2) * DEST_PER_TILE
    src_buf = hbuf.at[i % 3]
    sem0 = (i % 2) * TOP_K

    def body(t, carry):
        src = _row_slice(src_buf, t)
        for k in range(TOP_K):
            d = dest_s[base + k * MOE_TILE + t]
            pltpu.make_async_copy(src, _row_slice(xs_hbm, d), rsem.at[sem0 + k]).start(priority=k % 2)
        return carry

    lax.fori_loop(0, MOE_TILE, body, 0, unroll=4)

    @pl.when(i >= 1)
    def _():
        drain(i - 1)

    @pl.when(i == nt - 1)
    def _():
        drain(i)


def _dispatch(dest_flat, h_packed, n_rows):
    nt = h_packed.shape[0] // TILE_SUBLANES
    return pl.pallas_call(
        _dispatch_kernel,
        grid=(nt,),
        in_specs=[pl.BlockSpec(memory_space=pl.ANY),
                  pl.BlockSpec(memory_space=pl.ANY)],
        out_specs=pl.BlockSpec(memory_space=pl.ANY),
        out_shape=jax.ShapeDtypeStruct((n_rows * ROW_SUBLANES, LANES), U32),
        scratch_shapes=[pltpu.SMEM((2 * DEST_PER_TILE,), I32),
                        pltpu.VMEM((3, TILE_SUBLANES, LANES), U32),
                        pltpu.SemaphoreType.DMA((2,)),
                        pltpu.SemaphoreType.DMA((3,)),
                        pltpu.SemaphoreType.DMA((2 * TOP_K,))],
        compiler_params=_cparams(("arbitrary",)),
        name="dispatch",
    )(dest_flat, h_packed)


BLOCK_SUBLANES = EXPERT_ROWS * ROW_SUBLANES
EXPERT_AHEAD = 4
EXPERT_SLOTS = EXPERT_AHEAD + 2


def _expert_kernel(first_ref, last_ref, cnt_ref, na_ref, xs_hbm, wg_ref, wu_ref, wd_ref, ys_hbm,
                   wg_s, wu_s, wd_s, xbuf, ybuf, isem, osem):
    e = pl.program_id(0)
    n_active = na_ref[0]
    first, last = first_ref[e], last_ref[e]

    def block_rows(ref, g):
        return ref.at[pl.ds(pl.multiple_of(g * BLOCK_SUBLANES, BLOCK_SUBLANES), BLOCK_SUBLANES)]

    def fetch(g, slot):
        return pltpu.make_async_copy(block_rows(xs_hbm, g), xbuf.at[slot], isem.at[slot])

    def flush(g, slot):
        return pltpu.make_async_copy(ybuf.at[slot], block_rows(ys_hbm, g), osem.at[slot])

    @pl.when(e == 0)
    def _():
        for g0 in range(EXPERT_AHEAD):
            @pl.when(g0 < n_active)
            def _():
                fetch(g0, g0).start()

    @pl.when(last > first)
    def _():
        wg_s[...] = wg_ref[...].astype(BF16)
        wu_s[...] = wu_ref[...].astype(BF16)
        wd_s[...] = wd_ref[...].astype(BF16)

    def enter(g):
        slot = g % EXPERT_SLOTS
        fetch(g, slot).wait()

        @pl.when(g + EXPERT_AHEAD < n_active)
        def _():
            fetch(g + EXPERT_AHEAD, (g + EXPERT_AHEAD) % EXPERT_SLOTS).start()

        @pl.when(g >= EXPERT_SLOTS)
        def _():
            flush(g - EXPERT_SLOTS, slot).wait()
        return slot

    def swiglu(halves):
        xa = jnp.concatenate([h[0] for h in halves], axis=0).astype(BF16)
        xb = jnp.concatenate([h[1] for h in halves], axis=0).astype(BF16)

        def up(w_s):
            return (jnp.dot(xa, w_s[:HALF_D, :], preferred_element_type=F32)
                    + jnp.dot(xb, w_s[HALF_D:, :], preferred_element_type=F32))

        gate = up(wg_s)
        a = (gate * jax.nn.sigmoid(gate) * up(wu_s)).astype(BF16)
        return jnp.dot(a, wd_s[...], preferred_element_type=F32)

    def n_valid(g):
        return cnt_ref[e] - (g - first) * EXPERT_ROWS

    def pair_body(p, carry):
        g = first + 2 * p
        s0 = enter(g)
        s1 = enter(g + 1)
        y = swiglu([_load_packed_rows(xbuf.at[s0], EXPERT_ROWS),
                    _load_packed_rows(xbuf.at[s1], EXPERT_ROWS, n_valid(g + 1))])
        _store_packed_rows(ybuf.at[s0], y[:EXPERT_ROWS])
        _store_packed_rows(ybuf.at[s1], y[EXPERT_ROWS:])
        flush(g, s0).start()
        flush(g + 1, s1).start()
        return carry

    n_pairs = lax.shift_right_logical(last - first, 1)
    lax.fori_loop(0, n_pairs, pair_body, 0)

    @pl.when((last - first) % 2 == 1)
    def _():
        g = last - 1
        s0 = enter(g)
        y = swiglu([_load_packed_rows(xbuf.at[s0], EXPERT_ROWS, n_valid(g))])
        _store_packed_rows(ybuf.at[s0], y)
        flush(g, s0).start()

    @pl.when(e == pl.num_programs(0) - 1)
    def _():
        for back in range(EXPERT_SLOTS, 0, -1):
            @pl.when(n_active >= back)
            def _():
                flush(n_active - back, (n_active - back) % EXPERT_SLOTS).wait()


def _experts(blk_first, blk_last, counts, n_active, xs, we_gate, we_up, we_down):
    wmap = lambda e, *_: (e, 0, 0)
    return pl.pallas_call(
        _expert_kernel,
        grid_spec=pltpu.PrefetchScalarGridSpec(
            num_scalar_prefetch=4,
            grid=(N_EXPERTS,),
            in_specs=[pl.BlockSpec(memory_space=pl.ANY),
                      pl.BlockSpec((None, D_MODEL, EXPERT_DIM), wmap),
                      pl.BlockSpec((None, D_MODEL, EXPERT_DIM), wmap),
                      pl.BlockSpec((None, EXPERT_DIM, D_MODEL), wmap)],
            out_specs=pl.BlockSpec(memory_space=pl.ANY),
            scratch_shapes=[pltpu.VMEM((D_MODEL, EXPERT_DIM), BF16),
                            pltpu.VMEM((D_MODEL, EXPERT_DIM), BF16),
                            pltpu.VMEM((EXPERT_DIM, D_MODEL), BF16),
                            pltpu.VMEM((EXPERT_SLOTS, BLOCK_SUBLANES, LANES), U32),
                            pltpu.VMEM((EXPERT_SLOTS, BLOCK_SUBLANES, LANES), U32),
                            pltpu.SemaphoreType.DMA((EXPERT_SLOTS,)),
                            pltpu.SemaphoreType.DMA((EXPERT_SLOTS,))]),
        out_shape=jax.ShapeDtypeStruct(xs.shape, U32),
        compiler_params=_cparams(("arbitrary",)),
        name="experts",
    )(blk_first, blk_last, counts, n_active, xs, we_gate, we_up, we_down)


COMBINE_ROWS = 8


def _combine_kernel(n_prompt_tiles,
                    dest_hbm, ys_hbm, xp_ref, xs_ref, h_ref, gate_ref, wsg_ref, wsu_ref, wsd_ref, gpost_ref, gate2_ref,
                    yp_ref, ysm_ref, dest_s, buf, shared_s, gcol_s, y_s, dsem, rsem):
    i = pl.program_id(0)
    nt = pl.num_programs(0)

    def dest_copy(tile):
        slot = tile % 3
        return pltpu.make_async_copy(dest_hbm.at[pl.ds(tile * DEST_PER_TILE, DEST_PER_TILE)],
                                     dest_s.at[pl.ds(slot * DEST_PER_TILE, DEST_PER_TILE)], dsem.at[slot])

    def gather_token(base, bset, t):
        dst0 = bset * TILE_SUBLANES + t * ROW_SUBLANES
        for k in range(TOP_K):
            d = dest_s[base + t + k * MOE_TILE]
            dst = buf.at[pl.ds(pl.multiple_of(dst0 + k * TILE_SUBLANES, ROW_SUBLANES), ROW_SUBLANES)]
            pltpu.make_async_copy(_row_slice(ys_hbm, d), dst, rsem.at[bset + k]).start(priority=k % 2)

    def wait_rows(bset):
        for k in range(TOP_K):
            pltpu.make_async_copy(ys_hbm.at[pl.ds(0, TILE_SUBLANES)], buf.at[pl.ds(0, TILE_SUBLANES)],
                                  rsem.at[bset + k]).wait()

    @pl.when(i == 0)
    def _():
        dest_copy(0).start()

        @pl.when(nt > 1)
        def _():
            dest_copy(1).start()

        dest_copy(0).wait()

        def first_tile(t, carry):
            gather_token(0, 0, t)
            return carry

        lax.fori_loop(0, MOE_TILE, first_tile, 0, unroll=4)

    @pl.when(i + 2 < nt)
    def _():
        dest_copy(i + 2).start()

    @pl.when(i + 1 < nt)
    def _():
        dest_copy(i + 1).wait()

    nxt = jnp.minimum(i + 1, nt - 1)
    base_next = (nxt % 3) * DEST_PER_TILE
    bset = (i % 2) * TOP_K
    bset_next = ((i + 1) % 2) * TOP_K

    ha, hb = _load_packed_rows(h_ref, MOE_TILE)
    ha = ha.astype(BF16)
    hb = hb.astype(BF16)

    def up(w_ref):
        return (jnp.dot(ha, w_ref[:HALF_D, :], preferred_element_type=F32)
                + jnp.dot(hb, w_ref[HALF_D:, :], preferred_element_type=F32))

    g = up(wsg_ref)
    a = (g * jax.nn.sigmoid(g) * up(wsu_ref)).astype(BF16)
    shared_s[...] = jnp.dot(a, wsd_ref[...], preferred_element_type=F32)

    gpad = jnp.concatenate([gate_ref[...], jnp.zeros((LANES - TOP_K, MOE_TILE), F32)], axis=0)
    gcol_s[...] = gpad.T

    wait_rows(bset)
    in_first = i < n_prompt_tiles

    def chunk(j, carry):
        r0 = pl.multiple_of(j * COMBINE_ROWS, COMBINE_ROWS)
        rows = pl.ds(r0, COMBINE_ROWS)
        acc_hi = [jnp.zeros((COMBINE_ROWS, LANES), F32) for _ in range(ROW_SUBLANES)]
        acc_lo = [jnp.zeros((COMBINE_ROWS, LANES), F32) for _ in range(ROW_SUBLANES)]
        src0 = bset * TILE_SUBLANES + r0 * ROW_SUBLANES
        for k in range(TOP_K):
            gk = jnp.broadcast_to(gcol_s[rows, k:k + 1], (COMBINE_ROWS, LANES))
            for s in range(ROW_SUBLANES):
                hi, lo = _unpack_halves(
                    buf[pl.ds(src0 + k * TILE_SUBLANES + s, COMBINE_ROWS, stride=ROW_SUBLANES), :])
                acc_hi[s] = acc_hi[s] + hi * gk
                acc_lo[s] = acc_lo[s] + lo * gk
        f = jnp.concatenate(acc_hi + acc_lo, axis=1) + shared_s[rows, :]
        x = jnp.where(in_first, xp_ref[rows, :], xs_ref[rows, :])
        gate2 = gate2_ref[lax.shift_right_logical(j * COMBINE_ROWS, CHUNK.bit_length() - 1)]
        y = x + _rms(f) * gpost_ref[...] * gate2
        for t in range(COMBINE_ROWS):
            gather_token(base_next, bset_next, r0 + t)
        y_s[rows, :] = y
        return carry

    lax.fori_loop(0, MOE_TILE // COMBINE_ROWS, chunk, 0)

    @pl.when(in_first)
    def _():
        yp_ref[...] = y_s[...]

    @pl.when(jnp.logical_not(in_first))
    def _():
        ysm_ref[...] = y_s[...]

    @pl.when(i == nt - 1)
    def _():
        wait_rows(bset_next)


def _combine(dest_flat, ys, x1_p, x1_s, h_packed, gates, ws_gate, ws_up, ws_down, g_post, gate2_g):
    n_prompt = x1_p.shape[0]
    n = n_prompt + x1_s.shape[0]
    nt = n // MOE_TILE
    npt = n_prompt // MOE_TILE
    gpt = MOE_TILE // CHUNK
    z2 = lambda i: (0, 0)
    return pl.pallas_call(
        functools.partial(_combine_kernel, npt),
        grid=(nt,),
        in_specs=[pl.BlockSpec(memory_space=pl.ANY),
                  pl.BlockSpec(memory_space=pl.ANY),
                  pl.BlockSpec((MOE_TILE, D_MODEL), lambda i: (jnp.minimum(i, npt - 1), 0)),
                  pl.BlockSpec((MOE_TILE, D_MODEL), lambda i: (jnp.maximum(i - npt, 0), 0)),
                  pl.BlockSpec((MOE_TILE * ROW_SUBLANES, LANES), lambda i: (i, 0)),
                  pl.BlockSpec((TOP_K, MOE_TILE), lambda i: (0, i)),
                  pl.BlockSpec((D_MODEL, EXPERT_DIM), z2),
                  pl.BlockSpec((D_MODEL, EXPERT_DIM), z2),
                  pl.BlockSpec((EXPERT_DIM, D_MODEL), z2),
                  pl.BlockSpec((1, D_MODEL), z2),
                  pl.BlockSpec((gpt, 1, D_MODEL), lambda i: (i, 0, 0))],
        out_specs=[pl.BlockSpec((MOE_TILE, D_MODEL), lambda i: (jnp.minimum(i, npt - 1), 0)),
                   pl.BlockSpec((MOE_TILE, D_MODEL), lambda i: (jnp.maximum(i - npt, 0), 0))],
        out_shape=[jax.ShapeDtypeStruct((n_prompt, D_MODEL), F32),
                   jax.ShapeDtypeStruct((n - n_prompt, D_MODEL), F32)],
        scratch_shapes=[pltpu.SMEM((3 * DEST_PER_TILE,), I32),
                        pltpu.VMEM((2 * TOP_K * TILE_SUBLANES, LANES), U32),
                        pltpu.VMEM((MOE_TILE, D_MODEL), F32),
                        pltpu.VMEM((MOE_TILE, LANES), F32),
                        pltpu.VMEM((MOE_TILE, D_MODEL), F32),
                        pltpu.SemaphoreType.DMA((3,)),
                        pltpu.SemaphoreType.DMA((2 * TOP_K,))],
        compiler_params=_cparams(("arbitrary",)),
        name="combine",
    )(dest_flat, ys, x1_p, x1_s, h_packed, gates, ws_gate, ws_up, ws_down, g_post, gate2_g)


def _rope_tables(pos):
    half = HEAD_DIM // 2
    inv = ROPE_THETA ** (-jnp.arange(half, dtype=F32) / half)
    ang = pos.astype(F32)[:, None] * inv[None, :]
    cos, sin = jnp.cos(ang), jnp.sin(ang)
    return jnp.concatenate([cos] * 4, axis=1), jnp.concatenate([-sin, sin, -sin, sin], axis=1)


def _groups(vec_rows, reps):
    rows, width = vec_rows.shape
    return jnp.broadcast_to(vec_rows[:, None, :], (rows, reps, width)).reshape(rows * reps, 1, width)


def kernel(x_prompt, x_sample, cache_a_k, cache_a_v, cache_b_k, cache_b_v, c_prompt, c_sample, w_ada, b_ada,
           g_pre_mix, g_post_mix, w_in, rel_bias_a, sinks_b, w_branch_a, w_branch_b, w_out, g_pre_ffn,
           g_post_ffn, w_router, b_router, we_gate, we_up, we_down, ws_gate, ws_up, ws_down):
    assert w_ada.shape[0] == 1, "single layer"
    nb, seq, d = x_prompt.shape
    ns, dec = x_sample.shape[:2]
    assert d == D_MODEL and dec == CHUNK and seq % TOK_TILE == 0 and (ns * dec) == TOK_TILE
    n_p, n_s = nb * seq, ns * dec
    n_all = n_p + n_s
    assert n_all % MOE_TILE == 0 and n_p % MOE_TILE == 0

    c_all = jnp.concatenate([c_prompt, c_sample], axis=0)
    pad = (-c_all.shape[0]) % 8
    c_all = jnp.pad(c_all, ((0, pad), (0, 0)))
    mod = _modulation(c_all, w_ada[0], b_ada[0])
    mod_p, mod_s = mod[:nb], mod[nb:nb + ns]
    cpp = seq // CHUNK

    def part(k):
        return mod_p[:, k * d:(k + 1) * d], mod_s[:, k * d:(k + 1) * d]

    (sh1p, sh1s), (sc1p, sc1s), (g1p, g1s), (sh2p, sh2s), (sc2p, sc2s), (g2p, g2s) = [part(k) for k in range(6)]
    both = lambda p, s: jnp.concatenate([_groups(p, cpp), _groups(s, 1)], axis=0)

    w_in_bf = w_in[0].astype(BF16)
    g_pre = g_pre_mix[0].reshape(1, d)
    cos_p, sin_p = _rope_tables(jnp.arange(seq))
    cos_s, sin_s = _rope_tables(PAST_LEN + jnp.arange(dec))
    cos_s, sin_s = jnp.tile(cos_s, (ns, 1)), jnp.tile(sin_s, (ns, 1))

    xp2 = x_prompt.reshape(n_p, d)
    xs2 = x_sample.reshape(n_s, d)
    outs_p = _inproj(xp2, _groups(sh1p, cpp), _groups(sc1p, cpp), g_pre, w_in_bf, cos_p, sin_p, nb, True)
    outs_s = _inproj(xs2, _groups(sh1s, 1), _groups(sc1s, 1), g_pre, w_in_bf, cos_s, sin_s, ns, False)

    table = rel_bias_a[0].astype(F32)
    n_far = A_BAND - 1 - REL_CLIP
    ext = jnp.concatenate([jnp.broadcast_to(table[:, 2 * REL_CLIP:], (A_HEADS, n_far)),
                           jnp.flip(table[:, REL_CLIP - (CHUNK - 1):], axis=1)], axis=1)
    bias = jnp.stack([ext[:, CHUNK - 1 - q:CHUNK - 1 - q + A_BAND] for q in range(CHUNK)], axis=1)
    bias_pairs = bias.reshape(A_HEADS // 2, 2 * CHUNK, A_BAND) * LOG2_E
    sink_rows = jnp.broadcast_to((sinks_b[0].astype(F32) * LOG2_E).reshape(B_KV_HEADS, B_GROUP, 1),
                                 (B_KV_HEADS, B_GROUP, CHUNK)).reshape(B_KV_HEADS, B_GROUP * CHUNK, 1)
    consts = (bias_pairs, sink_rows, w_branch_a[0].astype(BF16), w_branch_b[0].astype(BF16),
              w_out[0].astype(BF16), g_post_mix[0].reshape(1, d))

    x1_p = _attn_prompt(xp2, outs_p[:7], consts, _groups(g1p, cpp), nb)
    caches = (cache_a_k[0].reshape(ns, A_ROWS, WA), cache_a_v[0].reshape(ns, A_ROWS, WA),
              cache_b_k[0].reshape(ns, WINDOW, WKB), cache_b_v[0].reshape(ns, WINDOW, WKB))
    x1_s = _attn_sample(xs2, outs_s[:7], caches, consts, _groups(g1s, 1))

    h_packed, idx, gates, rank, counts = _router(
        x1_p, x1_s, both(sh2p, sh2s), both(sc2p, sc2s), g_pre_ffn[0].reshape(1, d),
        w_router[0].T.astype(BF16), b_router[0].astype(F32).reshape(N_EXPERTS, 1))
    n_blocks = (n_all * TOP_K) // EXPERT_ROWS + N_EXPERTS
    cnt = counts[:, 0].astype(I32)
    blocks_e = (cnt + EXPERT_ROWS - 1) // EXPERT_ROWS
    blk_end = jnp.cumsum(blocks_e)
    blk_start = blk_end - blocks_e
    n_active = blk_end[-1:]
    start_col = (blk_start * EXPERT_ROWS).astype(F32).reshape(N_EXPERTS, 1)

    dest_flat = _dest_rows(idx, rank, start_col).reshape(-1)
    xs = _dispatch(dest_flat, h_packed, n_blocks * EXPERT_ROWS)
    ys = _experts(blk_start.astype(I32), blk_end.astype(I32), cnt, n_active.astype(I32), xs,
                  we_gate[0], we_up[0], we_down[0])
    y_p, y_s = _combine(dest_flat, ys, x1_p, x1_s, h_packed, gates, ws_gate[0].astype(BF16),
                        ws_up[0].astype(BF16), ws_down[0].astype(BF16), g_post_ffn[0].reshape(1, d),
                        both(g2p, g2s))

    a_heads = (A_HEADS, HEAD_DIM)
    b_heads = (B_KV_HEADS, HEAD_DIM)
    return (y_p.reshape(nb, seq, d), y_s.reshape(ns, dec, d),
            outs_p[7].reshape(1, nb, A_ROWS, *a_heads), outs_p[8].reshape(1, nb, A_ROWS, *a_heads),
            outs_p[9].reshape(1, nb, WINDOW, *b_heads), outs_p[10].reshape(1, nb, WINDOW, *b_heads),
            outs_s[7].reshape(1, ns, dec, *a_heads), outs_s[8].reshape(1, ns, dec, *a_heads),
            outs_s[9].reshape(1, ns, dec, *b_heads), outs_s[10].reshape(1, ns, dec, *b_heads))
```

```python
import functools

import jax
import jax.numpy as jnp
from jax import lax
from jax.experimental import pallas as pl
from jax.experimental.pallas import tpu as pltpu

F32 = jnp.float32
BF16 = jnp.bfloat16
I32 = jnp.int32
U32 = jnp.uint32
HIGHEST = lax.Precision.HIGHEST
LOG2_E = 1.4426950408889634

D_MODEL = 1024
CHUNK = 64
HEAD_DIM = 64
A_HEADS = 8
A_PREV_CHUNKS = 8
A_ROWS = A_PREV_CHUNKS * CHUNK
A_BAND = A_ROWS + CHUNK
REL_CLIP = 128
B_HEADS = 8
B_KV_HEADS = 2
B_GROUP = B_HEADS // B_KV_HEADS
WINDOW = 128
B_BAND = WINDOW + CHUNK
ROPE_THETA = 10000.0
N_EXPERTS = 256
TOP_K = 8
N_GROUPS = 8
PER_GROUP = N_EXPERTS // N_GROUPS
TOPK_GROUPS = 4
EXPERT_DIM = 256
ROUTED_SCALE = 2.5
EPS = 1e-6
PAST_LEN = 4096

WA = A_HEADS * HEAD_DIM
WB = B_HEADS * HEAD_DIM
WKB = B_KV_HEADS * HEAD_DIM
OFF_QA, OFF_KA, OFF_VA = 0, WA, 2 * WA
OFF_QB = 3 * WA
OFF_KB = OFF_QB + WB
OFF_VB = OFF_KB + WKB
OFF_G = OFF_VB + WKB
N_IN = OFF_G + 2 * D_MODEL

LANES = 128
TOK_TILE = 512
MOE_TILE = 256
EXPERT_ROWS = 256
HALF_D = D_MODEL // 2
VMEM_LIMIT = 56 * 1024 * 1024


def _cparams(sem, vmem=VMEM_LIMIT):
    return pltpu.CompilerParams(dimension_semantics=sem, vmem_limit_bytes=vmem)


def _rms(x):
    return x * lax.rsqrt(jnp.mean(x * x, axis=-1, keepdims=True) + EPS)


def _group_affine(y, mul, add):
    g = mul.shape[0]
    y3 = y.reshape(g, CHUNK, y.shape[-1]) * mul
    if add is not None:
        y3 = y3 + add
    return y3.reshape(g * CHUNK, y.shape[-1])


def _pack_halves(a, b):
    ua = lax.bitcast_convert_type(a.astype(BF16).astype(F32), U32)
    ub = lax.bitcast_convert_type(b.astype(BF16).astype(F32), U32)
    return (ua & jnp.uint32(0xFFFF0000)) | (ub >> 16)


def _unpack_halves(u):
    a = lax.bitcast_convert_type(u & jnp.uint32(0xFFFF0000), F32)
    b = lax.bitcast_convert_type(u << 16, F32)
    return a, b


ROW_SUBLANES = HALF_D // LANES


def _store_packed_rows(ref, x):
    rows = x.shape[0]
    p = _pack_halves(x[:, :HALF_D], x[:, HALF_D:])
    for s in range(ROW_SUBLANES):
        ref[pl.ds(s, rows, stride=ROW_SUBLANES), :] = p[:, s * LANES:(s + 1) * LANES]


def _load_packed_rows(ref, rows, n_valid=None):
    his, los = [], []
    for s in range(ROW_SUBLANES):
        u = ref[pl.ds(s, rows, stride=ROW_SUBLANES), :]
        if n_valid is not None:
            u = jnp.where(lax.broadcasted_iota(I32, u.shape, 0) < n_valid, u, jnp.uint32(0))
        a, b = _unpack_halves(u)
        his.append(a)
        los.append(b)
    return jnp.concatenate(his, axis=1), jnp.concatenate(los, axis=1)


def _mod_kernel(c_ref, w_ref, b_ref, o_ref):
    c = c_ref[...]
    s = c * jax.nn.sigmoid(c)
    o_ref[...] = jnp.dot(s, w_ref[...], precision=HIGHEST, preferred_element_type=F32) + b_ref[...]


def _modulation(c_all, w_ada, b_ada):
    rows = c_all.shape[0]
    n = w_ada.shape[1]
    tn = 512
    return pl.pallas_call(
        _mod_kernel,
        grid=(n // tn,),
        in_specs=[pl.BlockSpec((rows, D_MODEL), lambda j: (0, 0)),
                  pl.BlockSpec((D_MODEL, tn), lambda j: (0, j)),
                  pl.BlockSpec((1, tn), lambda j: (0, j))],
        out_specs=pl.BlockSpec((rows, tn), lambda j: (0, j)),
        out_shape=jax.ShapeDtypeStruct((rows, n), F32),
        compiler_params=_cparams(("arbitrary",)),
        name="modulation",
    )(c_all, w_ada, b_ada.reshape(1, n))


def _rope(x, cos, sin_signed):
    n = x.shape[-1]
    reps = n // LANES
    if reps > 1:
        cos = jnp.concatenate([cos] * reps, axis=1)
        sin_signed = jnp.concatenate([sin_signed] * reps, axis=1)
    lane = lax.broadcasted_iota(I32, x.shape, 1)
    first_half = (lane % HEAD_DIM) < (HEAD_DIM // 2)
    partner = jnp.where(first_half, pltpu.roll(x, n - HEAD_DIM // 2, 1), pltpu.roll(x, HEAD_DIM // 2, 1))
    return x * cos + partner * sin_signed


def _inproj_kernel(prompt_state, tiles_per_seq,
                   x_ref, sh_ref, sc_ref, g_ref, w_ref, cos_ref, sin_ref,
                   qa_ref, ka_ref, va_ref, qb_ref, kb_ref, vb_ref, gt_ref,
                   ska_ref, sva_ref, skb_ref, svb_ref):
    x = x_ref[...]
    h = _group_affine(_rms(x) * g_ref[...], 1.0 + sc_ref[...], sh_ref[...]).astype(BF16)

    def proj(off, width):
        return jnp.dot(h, w_ref[:, off:off + width], preferred_element_type=F32)

    cos = cos_ref[...]
    sin = sin_ref[...]
    scale = HEAD_DIM ** -0.5 * LOG2_E
    qa_ref[...] = (proj(OFF_QA, WA) * scale).astype(BF16)
    ka = proj(OFF_KA, WA)
    va = proj(OFF_VA, WA)
    ka_ref[...] = ka.astype(BF16)
    va_ref[...] = va.astype(BF16)
    qb_ref[...] = (_rope(proj(OFF_QB, WB), cos, sin) * scale).astype(BF16)
    kb = _rope(proj(OFF_KB, WKB), cos, sin)
    vb = proj(OFF_VB, WKB)
    kb_ref[...] = kb.astype(BF16)
    vb_ref[...] = vb.astype(BF16)
    gt_ref[...] = jax.nn.sigmoid(proj(OFF_G, 2 * D_MODEL)).astype(BF16)

    if prompt_state:
        @pl.when(pl.program_id(0) % tiles_per_seq == tiles_per_seq - 1)
        def _():
            ska_ref[...] = ka
            sva_ref[...] = va
            skb_ref[...] = kb[TOK_TILE - WINDOW:, :]
            svb_ref[...] = vb[TOK_TILE - WINDOW:, :]
    else:
        ska_ref[...] = ka
        sva_ref[...] = va
        skb_ref[...] = kb
        svb_ref[...] = vb


def _inproj(x2d, shift_g, scale_g, g_pre, w_in_bf, cos_tab, sin_tab, n_seq, prompt_state):
    n = x2d.shape[0]
    nt = n // TOK_TILE
    tiles_per_seq = nt // n_seq if prompt_state else 1
    tab_tiles = cos_tab.shape[0] // TOK_TILE
    gpt = TOK_TILE // CHUNK
    row = lambda i: (i, 0)
    grp = lambda i: (i, 0, 0)
    if prompt_state:
        st_shapes = [jax.ShapeDtypeStruct((n_seq, A_ROWS, WA), F32)] * 2 + \
                    [jax.ShapeDtypeStruct((n_seq, WINDOW, WKB), F32)] * 2
        st_specs = [pl.BlockSpec((None, A_ROWS, WA), lambda i: (i // tiles_per_seq, 0, 0))] * 2 + \
                   [pl.BlockSpec((None, WINDOW, WKB), lambda i: (i // tiles_per_seq, 0, 0))] * 2
    else:
        st_shapes = [jax.ShapeDtypeStruct((n, WA), F32)] * 2 + [jax.ShapeDtypeStruct((n, WKB), F32)] * 2
        st_specs = [pl.BlockSpec((TOK_TILE, WA), row)] * 2 + [pl.BlockSpec((TOK_TILE, WKB), row)] * 2
    out_shapes = [jax.ShapeDtypeStruct((n, WA), BF16)] * 4 + [jax.ShapeDtypeStruct((n, WKB), BF16)] * 2 + \
                 [jax.ShapeDtypeStruct((n, 2 * D_MODEL), BF16)]
    out_shapes = [out_shapes[0], out_shapes[1], out_shapes[2], out_shapes[3], out_shapes[4], out_shapes[5],
                  out_shapes[6]] + st_shapes
    out_specs = [pl.BlockSpec((TOK_TILE, WA), row)] * 4 + [pl.BlockSpec((TOK_TILE, WKB), row)] * 2 + \
                [pl.BlockSpec((TOK_TILE, 2 * D_MODEL), row)] + st_specs
    return pl.pallas_call(
        functools.partial(_inproj_kernel, prompt_state, tiles_per_seq),
        grid=(nt,),
        in_specs=[pl.BlockSpec((TOK_TILE, D_MODEL), row),
                  pl.BlockSpec((gpt, 1, D_MODEL), grp),
                  pl.BlockSpec((gpt, 1, D_MODEL), grp),
                  pl.BlockSpec((1, D_MODEL), lambda i: (0, 0)),
                  pl.BlockSpec((D_MODEL, N_IN), lambda i: (0, 0)),
                  pl.BlockSpec((TOK_TILE, LANES), lambda i: (i % tab_tiles, 0)),
                  pl.BlockSpec((TOK_TILE, LANES), lambda i: (i % tab_tiles, 0))],
        out_specs=out_specs,
        out_shape=out_shapes,
        compiler_params=_cparams(("arbitrary",)),
        name="inproj_prompt" if prompt_state else "inproj_sample",
    )(x2d, shift_g, scale_g, g_pre, w_in_bf, cos_tab, sin_tab)


def _attn_kernel(n_chunks, mask_first,
                 x_ref, qa_ref, qb_ref, gt_ref,
                 kap_ref, kac_ref, vap_ref, vac_ref, kbp_ref, kbc_ref, vbp_ref, vbc_ref,
                 bias_ref, sink_ref, wba_ref, wbb_ref, wout_ref, gpost_ref, gate1_ref,
                 o_ref,
                 ka_s, va_s, kb_s, vb_s, oa_s, ob_s):
    rows = n_chunks * CHUNK
    pb = kbp_ref.shape[0]
    ka_s[0:A_ROWS, :] = kap_ref[...].astype(BF16)
    va_s[0:A_ROWS, :] = vap_ref[...].astype(BF16)
    ka_s[A_ROWS:A_ROWS + rows, :] = kac_ref[...]
    va_s[A_ROWS:A_ROWS + rows, :] = vac_ref[...]
    kb_s[0:WINDOW, :] = kbp_ref[pb - WINDOW:pb, :].astype(BF16)
    vb_s[0:WINDOW, :] = vbp_ref[pb - WINDOW:pb, :].astype(BF16)
    kb_s[WINDOW:WINDOW + rows, :] = kbc_ref[...]
    vb_s[WINDOW:WINDOW + rows, :] = vbc_ref[...]

    lane_q = lax.broadcasted_iota(I32, (CHUNK, LANES), 1)
    nt_dims = (((1,), (1,)), ((), ()))

    def chunk_body(masked, c, carry):
        c0 = pl.multiple_of(c * CHUNK, CHUNK)
        if masked:
            valid_a = c0 + lax.broadcasted_iota(I32, (1, A_BAND), 1) >= A_ROWS
            valid_b = c0 + lax.broadcasted_iota(I32, (1, B_BAND), 1) >= WINDOW

        scores = []
        for p in range(A_HEADS // 2):
            cols = slice(p * LANES, (p + 1) * LANES)
            q = qa_ref[pl.ds(c0, CHUNK), cols].astype(F32)
            qs = jnp.concatenate([jnp.where(lane_q < HEAD_DIM, q, 0.0),
                                  jnp.where(lane_q >= HEAD_DIM, q, 0.0)], axis=0).astype(BF16)
            k = ka_s[pl.ds(c0, A_BAND), cols]
            s = lax.dot_general(qs, k, nt_dims, preferred_element_type=F32) + bias_ref[p]
            if masked:
                s = jnp.where(valid_a, s, -jnp.inf)
            scores.append(s)
        for g in range(B_KV_HEADS):
            parts = []
            for r in range(B_GROUP):
                head = g * B_GROUP + r
                t, half = head // 2, head % 2
                q = qb_ref[pl.ds(c0, CHUNK), t * LANES:(t + 1) * LANES].astype(F32)
                if half != g:
                    q = pltpu.roll(q, HEAD_DIM, 1)
                in_g = (lane_q >= HEAD_DIM) if g else (lane_q < HEAD_DIM)
                parts.append(jnp.where(in_g, q, 0.0))
            qs = jnp.concatenate(parts, axis=0).astype(BF16)
            k = kb_s[pl.ds(c0, B_BAND), :]
            s = lax.dot_general(qs, k, nt_dims, preferred_element_type=F32)
            if masked:
                s = jnp.where(valid_b, s, -jnp.inf)
            scores.append(s)

        numer, denom = [], []
        for n, s in enumerate(scores):
            m = jnp.max(s, axis=1, keepdims=True)
            if n >= A_HEADS // 2:
                sk = sink_ref[n - A_HEADS // 2]
                m = jnp.maximum(m, sk)
            e = jnp.exp2(s - m)
            l = jnp.sum(e, axis=1, keepdims=True)
            if n >= A_HEADS // 2:
                l = l + jnp.exp2(sk - m)
            numer.append(e.astype(BF16))
            denom.append(l)

        outs = []
        for n, e in enumerate(numer):
            if n < A_HEADS // 2:
                v = va_s[pl.ds(c0, A_BAND), n * LANES:(n + 1) * LANES]
            else:
                v = vb_s[pl.ds(c0, B_BAND), :]
            outs.append(jnp.dot(e, v, preferred_element_type=F32) / denom[n])

        for p in range(A_HEADS // 2):
            o = outs[p]
            oa_s[pl.ds(c0, CHUNK), p * LANES:(p + 1) * LANES] = jnp.where(
                lane_q < HEAD_DIM, o[:CHUNK], o[CHUNK:]).astype(BF16)
        for g in range(B_KV_HEADS):
            o = outs[A_HEADS // 2 + g]
            for s2 in range(B_GROUP // 2):
                o_even = o[(2 * s2) * CHUNK:(2 * s2 + 1) * CHUNK]
                o_odd = o[(2 * s2 + 1) * CHUNK:(2 * s2 + 2) * CHUNK]
                if g == 0:
                    tile = jnp.where(lane_q < HEAD_DIM, o_even, pltpu.roll(o_odd, HEAD_DIM, 1))
                else:
                    tile = jnp.where(lane_q < HEAD_DIM, pltpu.roll(o_even, HEAD_DIM, 1), o_odd)
                t = g * (B_GROUP // 2) + s2
                ob_s[pl.ds(c0, CHUNK), t * LANES:(t + 1) * LANES] = tile.astype(BF16)
        return carry

    def all_chunks(masked):
        lax.fori_loop(0, n_chunks, functools.partial(chunk_body, masked), 0, unroll=4 if n_chunks > 1 else 1)

    if mask_first:
        pl.when(pl.program_id(1) == 0)(lambda: all_chunks(True))
        pl.when(pl.program_id(1) > 0)(lambda: all_chunks(False))
    else:
        all_chunks(False)

    za = jnp.dot(oa_s[...], wba_ref[...], preferred_element_type=F32)
    zb = jnp.dot(ob_s[...], wbb_ref[...], preferred_element_type=F32)
    merged = gt_ref[:, :D_MODEL].astype(F32) * za + gt_ref[:, D_MODEL:].astype(F32) * zb
    mo = jnp.dot(merged.astype(BF16), wout_ref[...], preferred_element_type=F32)
    o_ref[...] = x_ref[...] + _group_affine(_rms(mo) * gpost_ref[...], gate1_ref[...], None)


def _attn_scratch(rows):
    return [pltpu.VMEM((A_ROWS + rows, WA), BF16), pltpu.VMEM((A_ROWS + rows, WA), BF16),
            pltpu.VMEM((WINDOW + rows, WKB), BF16), pltpu.VMEM((WINDOW + rows, WKB), BF16),
            pltpu.VMEM((rows, WA), BF16), pltpu.VMEM((rows, WB), BF16)]


def _const_specs(grid_rank):
    z2 = (lambda b, j: (0, 0)) if grid_rank == 2 else (lambda b: (0, 0))
    z3 = (lambda b, j: (0, 0, 0)) if grid_rank == 2 else (lambda b: (0, 0, 0))
    return [pl.BlockSpec((A_HEADS // 2, 2 * CHUNK, A_BAND), z3),
            pl.BlockSpec((B_KV_HEADS, B_GROUP * CHUNK, 1), z3),
            pl.BlockSpec((WA, D_MODEL), z2),
            pl.BlockSpec((WB, D_MODEL), z2),
            pl.BlockSpec((D_MODEL, D_MODEL), z2),
            pl.BlockSpec((1, D_MODEL), z2)]


def _attn_prompt(x2d, proj, consts, gate1_g, n_seq):
    qa, ka, va, qb, kb, vb, gt = proj
    n = x2d.shape[0]
    tps = n // n_seq // TOK_TILE
    gpt = TOK_TILE // CHUNK
    cur = lambda b, j: (b * tps + j, 0)
    prev = lambda b, j: (b * tps + jnp.maximum(j - 1, 0), 0)
    return pl.pallas_call(
        functools.partial(_attn_kernel, TOK_TILE // CHUNK, True),
        grid=(n_seq, tps),
        in_specs=[pl.BlockSpec((TOK_TILE, D_MODEL), cur),
                  pl.BlockSpec((TOK_TILE, WA), cur),
                  pl.BlockSpec((TOK_TILE, WB), cur),
                  pl.BlockSpec((TOK_TILE, 2 * D_MODEL), cur),
                  pl.BlockSpec((TOK_TILE, WA), prev), pl.BlockSpec((TOK_TILE, WA), cur),
                  pl.BlockSpec((TOK_TILE, WA), prev), pl.BlockSpec((TOK_TILE, WA), cur),
                  pl.BlockSpec((TOK_TILE, WKB), prev), pl.BlockSpec((TOK_TILE, WKB), cur),
                  pl.BlockSpec((TOK_TILE, WKB), prev), pl.BlockSpec((TOK_TILE, WKB), cur)]
                 + _const_specs(2)
                 + [pl.BlockSpec((gpt, 1, D_MODEL), lambda b, j: (b * tps + j, 0, 0))],
        out_specs=pl.BlockSpec((TOK_TILE, D_MODEL), cur),
        out_shape=jax.ShapeDtypeStruct((n, D_MODEL), F32),
        scratch_shapes=_attn_scratch(TOK_TILE),
        compiler_params=_cparams(("arbitrary", "arbitrary")),
        name="attn_prompt",
    )(x2d, qa, qb, gt, ka, ka, va, va, kb, kb, vb, vb, *consts, gate1_g)


def _attn_sample(x2d, proj, caches, consts, gate1_g):
    qa, ka, va, qb, kb, vb, gt = proj
    cak, cav, cbk, cbv = caches
    n_seq = cak.shape[0]
    cur = lambda b: (b, 0)
    cache = lambda b: (b, 0, 0)
    return pl.pallas_call(
        functools.partial(_attn_kernel, 1, False),
        grid=(n_seq,),
        in_specs=[pl.BlockSpec((CHUNK, D_MODEL), cur),
                  pl.BlockSpec((CHUNK, WA), cur),
                  pl.BlockSpec((CHUNK, WB), cur),
                  pl.BlockSpec((CHUNK, 2 * D_MODEL), cur),
                  pl.BlockSpec((None, A_ROWS, WA), cache), pl.BlockSpec((CHUNK, WA), cur),
                  pl.BlockSpec((None, A_ROWS, WA), cache), pl.BlockSpec((CHUNK, WA), cur),
                  pl.BlockSpec((None, WINDOW, WKB), cache), pl.BlockSpec((CHUNK, WKB), cur),
                  pl.BlockSpec((None, WINDOW, WKB), cache), pl.BlockSpec((CHUNK, WKB), cur)]
                 + _const_specs(1)
                 + [pl.BlockSpec((1, 1, D_MODEL), lambda b: (b, 0, 0))],
        out_specs=pl.BlockSpec((CHUNK, D_MODEL), cur),
        out_shape=jax.ShapeDtypeStruct(x2d.shape, F32),
        scratch_shapes=_attn_scratch(CHUNK),
        compiler_params=_cparams(("arbitrary",)),
        name="attn_sample",
    )(x2d, qa, qb, gt, cak, ka, cav, va, cbk, kb, cbv, vb, *consts, gate1_g)


def _two_part_tile(n_first_tiles, first_ref, second_ref):
    return jnp.where(pl.program_id(0) < n_first_tiles, first_ref[...], second_ref[...])


def _router_kernel(n_prompt_tiles, xp_ref, xs_ref, sh_ref, sc_ref, g_ref, wrt_ref, br_ref,
                   h_ref, idx_ref, gate_ref, rank_ref, cnt_ref, carry):
    i = pl.program_id(0)

    @pl.when(i == 0)
    def _():
        carry[...] = jnp.zeros_like(carry)

    tm = xp_ref.shape[0]
    x = _two_part_tile(n_prompt_tiles, xp_ref, xs_ref)
    h = _group_affine(_rms(x) * g_ref[...], 1.0 + sc_ref[...], sh_ref[...])
    _store_packed_rows(h_ref, h)

    logits = lax.dot_general(wrt_ref[...], h.astype(BF16), (((1,), (1,)), ((), ())),
                             preferred_element_type=F32)
    scores = jax.nn.sigmoid(logits)
    biased = scores + br_ref[...]
    neg = -jnp.inf

    sub = lax.broadcasted_iota(I32, (PER_GROUP, tm), 0).astype(F32)
    gs_rows = []
    for g in range(N_GROUPS):
        xg = biased[g * PER_GROUP:(g + 1) * PER_GROUP]
        m1 = jnp.max(xg, axis=0, keepdims=True)
        i1 = jnp.min(jnp.where(xg == m1, sub, float(PER_GROUP)), axis=0, keepdims=True)
        m2 = jnp.max(jnp.where(sub == i1, neg, xg), axis=0, keepdims=True)
        gs_rows.append(m1 + m2)
    gs = jnp.concatenate(gs_rows, axis=0)

    giota = lax.broadcasted_iota(I32, (N_GROUPS, tm), 0).astype(F32)
    keep = jnp.zeros((N_GROUPS, tm), F32)
    for _ in range(TOPK_GROUPS):
        m = jnp.max(gs, axis=0, keepdims=True)
        gi = jnp.min(jnp.where(gs == m, giota, float(N_GROUPS)), axis=0, keepdims=True)
        hit = giota == gi
        keep = jnp.where(hit, 1.0, keep)
        gs = jnp.where(hit, neg, gs)
    cand = jnp.concatenate(
        [jnp.where(keep[g:g + 1] > 0.0, biased[g * PER_GROUP:(g + 1) * PER_GROUP], neg)
         for g in range(N_GROUPS)], axis=0)

    eiota = lax.broadcasted_iota(I32, (N_EXPERTS, tm), 0).astype(F32)
    idx_rows, gate_rows = [], []
    chosen = jnp.zeros((N_EXPERTS, tm), F32)
    for _ in range(TOP_K):
        m = jnp.max(cand, axis=0, keepdims=True)
        ei = jnp.min(jnp.where(cand == m, eiota, float(N_EXPERTS)), axis=0, keepdims=True)
        sel = eiota == ei
        gate_rows.append(jnp.sum(jnp.where(sel, scores, 0.0), axis=0, keepdims=True))
        idx_rows.append(ei)
        chosen = jnp.where(sel, 1.0, chosen)
        cand = jnp.where(sel, neg, cand)
    gates = jnp.concatenate(gate_rows, axis=0)
    gates = gates / jnp.sum(gates, axis=0, keepdims=True) * ROUTED_SCALE
    gate_ref[...] = gates
    idx_ref[...] = jnp.concatenate(idx_rows, axis=0).astype(I32)

    r_i = lax.broadcasted_iota(I32, (tm, tm), 0)
    c_i = lax.broadcasted_iota(I32, (tm, tm), 1)
    upper = jnp.where(r_i < c_i, 1.0, 0.0).astype(BF16)
    before = jnp.dot(chosen.astype(BF16), upper, preferred_element_type=F32) + carry[...]
    rank_rows = [jnp.sum(jnp.where(eiota == idx_rows[k], before, 0.0), axis=0, keepdims=True)
                 for k in range(TOP_K)]
    rank_ref[...] = jnp.concatenate(rank_rows, axis=0).astype(I32)
    total = carry[...] + jnp.sum(chosen, axis=1, keepdims=True)
    carry[...] = total
    cnt_ref[...] = total


def _router(x1_p, x1_s, shift_g, scale_g, g_pre, w_router_t, b_router_col):
    n = x1_p.shape[0] + x1_s.shape[0]
    nt = n // MOE_TILE
    npt = x1_p.shape[0] // MOE_TILE
    gpt = MOE_TILE // CHUNK
    lane_blk = lambda i: (0, i)
    return pl.pallas_call(
        functools.partial(_router_kernel, npt),
        grid=(nt,),
        in_specs=[pl.BlockSpec((MOE_TILE, D_MODEL), lambda i: (jnp.minimum(i, npt - 1), 0)),
                  pl.BlockSpec((MOE_TILE, D_MODEL), lambda i: (jnp.maximum(i - npt, 0), 0)),
                  pl.BlockSpec((gpt, 1, D_MODEL), lambda i: (i, 0, 0)),
                  pl.BlockSpec((gpt, 1, D_MODEL), lambda i: (i, 0, 0)),
                  pl.BlockSpec((1, D_MODEL), lambda i: (0, 0)),
                  pl.BlockSpec((N_EXPERTS, D_MODEL), lambda i: (0, 0)),
                  pl.BlockSpec((N_EXPERTS, 1), lambda i: (0, 0))],
        out_specs=[pl.BlockSpec((MOE_TILE * ROW_SUBLANES, LANES), lambda i: (i, 0)),
                   pl.BlockSpec((TOP_K, MOE_TILE), lane_blk),
                   pl.BlockSpec((TOP_K, MOE_TILE), lane_blk),
                   pl.BlockSpec((TOP_K, MOE_TILE), lane_blk),
                   pl.BlockSpec((N_EXPERTS, 1), lambda i: (0, 0))],
        out_shape=[jax.ShapeDtypeStruct((n * ROW_SUBLANES, LANES), U32),
                   jax.ShapeDtypeStruct((TOP_K, n), I32),
                   jax.ShapeDtypeStruct((TOP_K, n), F32),
                   jax.ShapeDtypeStruct((TOP_K, n), I32),
                   jax.ShapeDtypeStruct((N_EXPERTS, 1), F32)],
        scratch_shapes=[pltpu.VMEM((N_EXPERTS, 1), F32)],
        compiler_params=_cparams(("arbitrary",)),
        name="router",
    )(x1_p, x1_s, shift_g, scale_g, g_pre, w_router_t, b_router_col)


def _dest_kernel(idx_ref, rank_ref, start_ref, o_ref):
    eiota = lax.broadcasted_iota(I32, (N_EXPERTS, MOE_TILE), 0)
    start = start_ref[...]
    for sub in range(o_ref.shape[0]):
        cols = slice(sub * MOE_TILE, (sub + 1) * MOE_TILE)
        rows = [jnp.sum(jnp.where(eiota == idx_ref[k:k + 1, cols], start, 0.0), axis=0, keepdims=True)
                for k in range(TOP_K)]
        o_ref[sub] = jnp.concatenate(rows, axis=0).astype(I32) + rank_ref[:, cols]


def _dest_rows(idx, rank, start_col):
    n = idx.shape[1]
    nt = n // MOE_TILE
    per_step = next(c for c in (10, 8, 5, 4, 2, 1) if nt % c == 0)
    return pl.pallas_call(
        _dest_kernel,
        grid=(nt // per_step,),
        in_specs=[pl.BlockSpec((TOP_K, per_step * MOE_TILE), lambda i: (0, i)),
                  pl.BlockSpec((TOP_K, per_step * MOE_TILE), lambda i: (0, i)),
                  pl.BlockSpec((N_EXPERTS, 1), lambda i: (0, 0))],
        out_specs=pl.BlockSpec((per_step, TOP_K, MOE_TILE), lambda i: (i, 0, 0)),
        out_shape=jax.ShapeDtypeStruct((nt, TOP_K, MOE_TILE), I32),
        compiler_params=_cparams(("arbitrary",)),
        name="dest_rows",
    )(idx, rank, start_col)


DEST_PER_TILE = TOP_K * MOE_TILE


def _row_slice(ref, row):
    return ref.at[pl.ds(pl.multiple_of(row * ROW_SUBLANES, ROW_SUBLANES), ROW_SUBLANES)]


TILE_SUBLANES = MOE_TILE * ROW_SUBLANES


def _dispatch_kernel(n_cast_steps, dest_hbm, h_hbm, wg_ref, wu_ref, wd_ref, xs_hbm, wgb_ref, wub_ref, wdb_ref,
                     dest_s, hbuf, dsem, hsem, rsem):
    i = pl.program_id(0)
    nt = pl.num_programs(0)

    @pl.when(i < n_cast_steps)
    def _():
        wgb_ref[...] = wg_ref[...].astype(BF16)
        wub_ref[...] = wu_ref[...].astype(BF16)
        wdb_ref[...] = wd_ref[...].astype(BF16)

    def load(tile):
        dst = dest_s.at[pl.ds((tile % 2) * DEST_PER_TILE, DEST_PER_TILE)]
        rows = h_hbm.at[pl.ds(pl.multiple_of(tile * TILE_SUBLANES, TILE_SUBLANES), TILE_SUBLANES)]
        return (pltpu.make_async_copy(dest_hbm.at[pl.ds(tile * DEST_PER_TILE, DEST_PER_TILE)], dst, dsem.at[tile % 2]),
                pltpu.make_async_copy(rows, hbuf.at[tile % 3], hsem.at[tile % 3]))

    def drain(tile):
        for k in range(TOP_K):
            pltpu.make_async_copy(hbuf.at[tile % 3], xs_hbm.at[pl.ds(0, TILE_SUBLANES)],
                                  rsem.at[(tile % 2) * TOP_K + k]).wait()

    @pl.when(i == 0)
    def _():
        for cp in load(0):
            cp.start()

    @pl.when(i + 1 < nt)
    def _():
        for cp in load(i + 1):
            cp.start()

    for cp in load(i):
        cp.wait()
    base = (i % 2) * DEST_PER_TILE
    src_buf = hbuf.at[i % 3]
    sem0 = (i % 2) * TOP_K

    def body(t, carry):
        src = _row_slice(src_buf, t)
        for k in range(TOP_K):
            d = dest_s[base + k * MOE_TILE + t]
            pltpu.make_async_copy(src, _row_slice(xs_hbm, d), rsem.at[sem0 + k]).start(priority=k % 2)
        return carry

    lax.fori_loop(0, MOE_TILE, body, 0, unroll=4)

    @pl.when(i >= 1)
    def _():
        drain(i - 1)

    @pl.when(i == nt - 1)
    def _():
        drain(i)


def _dispatch(dest_flat, h_packed, we_gate, we_up, we_down, n_rows):
    nt = h_packed.shape[0] // TILE_SUBLANES
    per_step = -(-N_EXPERTS // nt)
    while N_EXPERTS % per_step:
        per_step += 1
    n_cast = N_EXPERTS // per_step
    wmap = lambda i: (jnp.minimum(i, n_cast - 1), 0, 0)
    up_blk = pl.BlockSpec((per_step, D_MODEL, EXPERT_DIM), wmap)
    down_blk = pl.BlockSpec((per_step, EXPERT_DIM, D_MODEL), wmap)
    return pl.pallas_call(
        functools.partial(_dispatch_kernel, n_cast),
        grid=(nt,),
        in_specs=[pl.BlockSpec(memory_space=pl.ANY),
                  pl.BlockSpec(memory_space=pl.ANY),
                  up_blk, up_blk, down_blk],
        out_specs=[pl.BlockSpec(memory_space=pl.ANY), up_blk, up_blk, down_blk],
        out_shape=[jax.ShapeDtypeStruct((n_rows * ROW_SUBLANES, LANES), U32),
                   jax.ShapeDtypeStruct(we_gate.shape, BF16),
                   jax.ShapeDtypeStruct(we_up.shape, BF16),
                   jax.ShapeDtypeStruct(we_down.shape, BF16)],
        scratch_shapes=[pltpu.SMEM((2 * DEST_PER_TILE,), I32),
                        pltpu.VMEM((3, TILE_SUBLANES, LANES), U32),
                        pltpu.SemaphoreType.DMA((2,)),
                        pltpu.SemaphoreType.DMA((3,)),
                        pltpu.SemaphoreType.DMA((2 * TOP_K,))],
        compiler_params=_cparams(("arbitrary",)),
        name="dispatch",
    )(dest_flat, h_packed, we_gate, we_up, we_down)


BLOCK_SUBLANES = EXPERT_ROWS * ROW_SUBLANES
EXPERT_AHEAD = 4
EXPERT_SLOTS = EXPERT_AHEAD + 2


def _expert_kernel(first_ref, last_ref, cnt_ref, na_ref, xs_hbm, wg_ref, wu_ref, wd_ref, ys_hbm,
                   xbuf, ybuf, isem, osem):
    e = pl.program_id(0)
    n_active = na_ref[0]
    first, last = first_ref[e], last_ref[e]

    def block_rows(ref, g):
        return ref.at[pl.ds(pl.multiple_of(g * BLOCK_SUBLANES, BLOCK_SUBLANES), BLOCK_SUBLANES)]

    def fetch(g, slot):
        return pltpu.make_async_copy(block_rows(xs_hbm, g), xbuf.at[slot], isem.at[slot])

    def flush(g, slot):
        return pltpu.make_async_copy(ybuf.at[slot], block_rows(ys_hbm, g), osem.at[slot])

    @pl.when(e == 0)
    def _():
        for g0 in range(EXPERT_AHEAD):
            @pl.when(g0 < n_active)
            def _():
                fetch(g0, g0).start()

    def enter(g):
        slot = g % EXPERT_SLOTS
        fetch(g, slot).wait()

        @pl.when(g + EXPERT_AHEAD < n_active)
        def _():
            fetch(g + EXPERT_AHEAD, (g + EXPERT_AHEAD) % EXPERT_SLOTS).start()

        @pl.when(g >= EXPERT_SLOTS)
        def _():
            flush(g - EXPERT_SLOTS, slot).wait()
        return slot

    def swiglu(halves):
        xa = jnp.concatenate([h[0] for h in halves], axis=0).astype(BF16)
        xb = jnp.concatenate([h[1] for h in halves], axis=0).astype(BF16)

        def up(w_ref):
            return (jnp.dot(xa, w_ref[:HALF_D, :], preferred_element_type=F32)
                    + jnp.dot(xb, w_ref[HALF_D:, :], preferred_element_type=F32))

        gate = up(wg_ref)
        a = (gate * jax.nn.sigmoid(gate) * up(wu_ref)).astype(BF16)
        return jnp.dot(a, wd_ref[...], preferred_element_type=F32)

    def n_valid(g):
        return cnt_ref[e] - (g - first) * EXPERT_ROWS

    def pair_body(p, carry):
        g = first + 2 * p
        s0 = enter(g)
        s1 = enter(g + 1)
        y = swiglu([_load_packed_rows(xbuf.at[s0], EXPERT_ROWS),
                    _load_packed_rows(xbuf.at[s1], EXPERT_ROWS, n_valid(g + 1))])
        _store_packed_rows(ybuf.at[s0], y[:EXPERT_ROWS])
        _store_packed_rows(ybuf.at[s1], y[EXPERT_ROWS:])
        flush(g, s0).start()
        flush(g + 1, s1).start()
        return carry

    n_pairs = lax.shift_right_logical(last - first, 1)
    lax.fori_loop(0, n_pairs, pair_body, 0)

    @pl.when((last - first) % 2 == 1)
    def _():
        g = last - 1
        s0 = enter(g)
        y = swiglu([_load_packed_rows(xbuf.at[s0], EXPERT_ROWS, n_valid(g))])
        _store_packed_rows(ybuf.at[s0], y)
        flush(g, s0).start()

    @pl.when(e == pl.num_programs(0) - 1)
    def _():
        for back in range(EXPERT_SLOTS, 0, -1):
            @pl.when(n_active >= back)
            def _():
                flush(n_active - back, (n_active - back) % EXPERT_SLOTS).wait()


def _experts(blk_first, blk_last, counts, n_active, xs, we_gate, we_up, we_down):
    wmap = lambda e, *_: (e, 0, 0)
    return pl.pallas_call(
        _expert_kernel,
        grid_spec=pltpu.PrefetchScalarGridSpec(
            num_scalar_prefetch=4,
            grid=(N_EXPERTS,),
            in_specs=[pl.BlockSpec(memory_space=pl.ANY),
                      pl.BlockSpec((None, D_MODEL, EXPERT_DIM), wmap),
                      pl.BlockSpec((None, D_MODEL, EXPERT_DIM), wmap),
                      pl.BlockSpec((None, EXPERT_DIM, D_MODEL), wmap)],
            out_specs=pl.BlockSpec(memory_space=pl.ANY),
            scratch_shapes=[pltpu.VMEM((EXPERT_SLOTS, BLOCK_SUBLANES, LANES), U32),
                            pltpu.VMEM((EXPERT_SLOTS, BLOCK_SUBLANES, LANES), U32),
                            pltpu.SemaphoreType.DMA((EXPERT_SLOTS,)),
                            pltpu.SemaphoreType.DMA((EXPERT_SLOTS,))]),
        out_shape=jax.ShapeDtypeStruct(xs.shape, U32),
        compiler_params=_cparams(("arbitrary",)),
        name="experts",
    )(blk_first, blk_last, counts, n_active, xs, we_gate, we_up, we_down)


COMBINE_ROWS = 8


def _combine_kernel(n_prompt_tiles,
                    dest_hbm, ys_hbm, xp_ref, xs_ref, h_ref, gate_ref, wsg_ref, wsu_ref, wsd_ref, gpost_ref, gate2_ref,
                    yp_ref, ysm_ref, dest_s, buf, shared_s, gcol_s, y_s, dsem, rsem):
    i = pl.program_id(0)
    nt = pl.num_programs(0)

    def dest_copy(tile):
        slot = tile % 3
        return pltpu.make_async_copy(dest_hbm.at[pl.ds(tile * DEST_PER_TILE, DEST_PER_TILE)],
                                     dest_s.at[pl.ds(slot * DEST_PER_TILE, DEST_PER_TILE)], dsem.at[slot])

    def gather_token(base, bset, t):
        dst0 = bset * TILE_SUBLANES + t * ROW_SUBLANES
        for k in range(TOP_K):
            d = dest_s[base + t + k * MOE_TILE]
            dst = buf.at[pl.ds(pl.multiple_of(dst0 + k * TILE_SUBLANES, ROW_SUBLANES), ROW_SUBLANES)]
            pltpu.make_async_copy(_row_slice(ys_hbm, d), dst, rsem.at[bset + k]).start(priority=k % 2)

    def wait_rows(bset):
        for k in range(TOP_K):
            pltpu.make_async_copy(ys_hbm.at[pl.ds(0, TILE_SUBLANES)], buf.at[pl.ds(0, TILE_SUBLANES)],
                                  rsem.at[bset + k]).wait()

    @pl.when(i == 0)
    def _():
        dest_copy(0).start()

        @pl.when(nt > 1)
        def _():
            dest_copy(1).start()

        dest_copy(0).wait()

        def first_tile(t, carry):
            gather_token(0, 0, t)
            return carry

        lax.fori_loop(0, MOE_TILE, first_tile, 0, unroll=4)

    @pl.when(i + 2 < nt)
    def _():
        dest_copy(i + 2).start()

    @pl.when(i + 1 < nt)
    def _():
        dest_copy(i + 1).wait()

    nxt = jnp.minimum(i + 1, nt - 1)
    base_next = (nxt % 3) * DEST_PER_TILE
    bset = (i % 2) * TOP_K
    bset_next = ((i + 1) % 2) * TOP_K

    ha, hb = _load_packed_rows(h_ref, MOE_TILE)
    ha = ha.astype(BF16)
    hb = hb.astype(BF16)

    def up(w_ref):
        return (jnp.dot(ha, w_ref[:HALF_D, :], preferred_element_type=F32)
                + jnp.dot(hb, w_ref[HALF_D:, :], preferred_element_type=F32))

    g = up(wsg_ref)
    a = (g * jax.nn.sigmoid(g) * up(wsu_ref)).astype(BF16)
    shared_s[...] = jnp.dot(a, wsd_ref[...], preferred_element_type=F32)

    gpad = jnp.concatenate([gate_ref[...], jnp.zeros((LANES - TOP_K, MOE_TILE), F32)], axis=0)
    gcol_s[...] = gpad.T

    wait_rows(bset)
    in_first = i < n_prompt_tiles

    def chunk(j, carry):
        r0 = pl.multiple_of(j * COMBINE_ROWS, COMBINE_ROWS)
        rows = pl.ds(r0, COMBINE_ROWS)
        acc_hi = [jnp.zeros((COMBINE_ROWS, LANES), F32) for _ in range(ROW_SUBLANES)]
        acc_lo = [jnp.zeros((COMBINE_ROWS, LANES), F32) for _ in range(ROW_SUBLANES)]
        src0 = bset * TILE_SUBLANES + r0 * ROW_SUBLANES
        for k in range(TOP_K):
            gk = jnp.broadcast_to(gcol_s[rows, k:k + 1], (COMBINE_ROWS, LANES))
            for s in range(ROW_SUBLANES):
                hi, lo = _unpack_halves(
                    buf[pl.ds(src0 + k * TILE_SUBLANES + s, COMBINE_ROWS, stride=ROW_SUBLANES), :])
                acc_hi[s] = acc_hi[s] + hi * gk
                acc_lo[s] = acc_lo[s] + lo * gk
        f = jnp.concatenate(acc_hi + acc_lo, axis=1) + shared_s[rows, :]
        x = jnp.where(in_first, xp_ref[rows, :], xs_ref[rows, :])
        gate2 = gate2_ref[lax.shift_right_logical(j * COMBINE_ROWS, CHUNK.bit_length() - 1)]
        y = x + _rms(f) * gpost_ref[...] * gate2
        for t in range(COMBINE_ROWS):
            gather_token(base_next, bset_next, r0 + t)
        y_s[rows, :] = y
        return carry

    lax.fori_loop(0, MOE_TILE // COMBINE_ROWS, chunk, 0)

    @pl.when(in_first)
    def _():
        yp_ref[...] = y_s[...]

    @pl.when(jnp.logical_not(in_first))
    def _():
        ysm_ref[...] = y_s[...]

    @pl.when(i == nt - 1)
    def _():
        wait_rows(bset_next)


def _combine(dest_flat, ys, x1_p, x1_s, h_packed, gates, ws_gate, ws_up, ws_down, g_post, gate2_g):
    n_prompt = x1_p.shape[0]
    n = n_prompt + x1_s.shape[0]
    nt = n // MOE_TILE
    npt = n_prompt // MOE_TILE
    gpt = MOE_TILE // CHUNK
    z2 = lambda i: (0, 0)
    return pl.pallas_call(
        functools.partial(_combine_kernel, npt),
        grid=(nt,),
        in_specs=[pl.BlockSpec(memory_space=pl.ANY),
                  pl.BlockSpec(memory_space=pl.ANY),
                  pl.BlockSpec((MOE_TILE, D_MODEL), lambda i: (jnp.minimum(i, npt - 1), 0)),
                  pl.BlockSpec((MOE_TILE, D_MODEL), lambda i: (jnp.maximum(i - npt, 0), 0)),
                  pl.BlockSpec((MOE_TILE * ROW_SUBLANES, LANES), lambda i: (i, 0)),
                  pl.BlockSpec((TOP_K, MOE_TILE), lambda i: (0, i)),
                  pl.BlockSpec((D_MODEL, EXPERT_DIM), z2),
                  pl.BlockSpec((D_MODEL, EXPERT_DIM), z2),
                  pl.BlockSpec((EXPERT_DIM, D_MODEL), z2),
                  pl.BlockSpec((1, D_MODEL), z2),
                  pl.BlockSpec((gpt, 1, D_MODEL), lambda i: (i, 0, 0))],
        out_specs=[pl.BlockSpec((MOE_TILE, D_MODEL), lambda i: (jnp.minimum(i, npt - 1), 0)),
                   pl.BlockSpec((MOE_TILE, D_MODEL), lambda i: (jnp.maximum(i - npt, 0), 0))],
        out_shape=[jax.ShapeDtypeStruct((n_prompt, D_MODEL), F32),
                   jax.ShapeDtypeStruct((n - n_prompt, D_MODEL), F32)],
        scratch_shapes=[pltpu.SMEM((3 * DEST_PER_TILE,), I32),
                        pltpu.VMEM((2 * TOP_K * TILE_SUBLANES, LANES), U32),
                        pltpu.VMEM((MOE_TILE, D_MODEL), F32),
                        pltpu.VMEM((MOE_TILE, LANES), F32),
                        pltpu.VMEM((MOE_TILE, D_MODEL), F32),
                        pltpu.SemaphoreType.DMA((3,)),
                        pltpu.SemaphoreType.DMA((2 * TOP_K,))],
        compiler_params=_cparams(("arbitrary",)),
        name="combine",
    )(dest_flat, ys, x1_p, x1_s, h_packed, gates, ws_gate, ws_up, ws_down, g_post, gate2_g)


def _rope_tables(pos):
    half = HEAD_DIM // 2
    inv = ROPE_THETA ** (-jnp.arange(half, dtype=F32) / half)
    ang = pos.astype(F32)[:, None] * inv[None, :]
    cos, sin = jnp.cos(ang), jnp.sin(ang)
    return jnp.concatenate([cos] * 4, axis=1), jnp.concatenate([-sin, sin, -sin, sin], axis=1)


def _groups(vec_rows, reps):
    rows, width = vec_rows.shape
    return jnp.broadcast_to(vec_rows[:, None, :], (rows, reps, width)).reshape(rows * reps, 1, width)


def kernel(x_prompt, x_sample, cache_a_k, cache_a_v, cache_b_k, cache_b_v, c_prompt, c_sample, w_ada, b_ada,
           g_pre_mix, g_post_mix, w_in, rel_bias_a, sinks_b, w_branch_a, w_branch_b, w_out, g_pre_ffn,
           g_post_ffn, w_router, b_router, we_gate, we_up, we_down, ws_gate, ws_up, ws_down):
    assert w_ada.shape[0] == 1, "single layer"
    nb, seq, d = x_prompt.shape
    ns, dec = x_sample.shape[:2]
    assert d == D_MODEL and dec == CHUNK and seq % TOK_TILE == 0 and (ns * dec) == TOK_TILE
    n_p, n_s = nb * seq, ns * dec
    n_all = n_p + n_s
    assert n_all % MOE_TILE == 0 and n_p % MOE_TILE == 0

    c_all = jnp.concatenate([c_prompt, c_sample], axis=0)
    pad = (-c_all.shape[0]) % 8
    c_all = jnp.pad(c_all, ((0, pad), (0, 0)))
    mod = _modulation(c_all, w_ada[0], b_ada[0])
    mod_p, mod_s = mod[:nb], mod[nb:nb + ns]
    cpp = seq // CHUNK

    def part(k):
        return mod_p[:, k * d:(k + 1) * d], mod_s[:, k * d:(k + 1) * d]

    (sh1p, sh1s), (sc1p, sc1s), (g1p, g1s), (sh2p, sh2s), (sc2p, sc2s), (g2p, g2s) = [part(k) for k in range(6)]
    both = lambda p, s: jnp.concatenate([_groups(p, cpp), _groups(s, 1)], axis=0)

    w_in_bf = w_in[0].astype(BF16)
    g_pre = g_pre_mix[0].reshape(1, d)
    cos_p, sin_p = _rope_tables(jnp.arange(seq))
    cos_s, sin_s = _rope_tables(PAST_LEN + jnp.arange(dec))
    cos_s, sin_s = jnp.tile(cos_s, (ns, 1)), jnp.tile(sin_s, (ns, 1))

    xp2 = x_prompt.reshape(n_p, d)
    xs2 = x_sample.reshape(n_s, d)
    outs_p = _inproj(xp2, _groups(sh1p, cpp), _groups(sc1p, cpp), g_pre, w_in_bf, cos_p, sin_p, nb, True)
    outs_s = _inproj(xs2, _groups(sh1s, 1), _groups(sc1s, 1), g_pre, w_in_bf, cos_s, sin_s, ns, False)

    table = rel_bias_a[0].astype(F32)
    n_far = A_BAND - 1 - REL_CLIP
    ext = jnp.concatenate([jnp.broadcast_to(table[:, 2 * REL_CLIP:], (A_HEADS, n_far)),
                           jnp.flip(table[:, REL_CLIP - (CHUNK - 1):], axis=1)], axis=1)
    bias = jnp.stack([ext[:, CHUNK - 1 - q:CHUNK - 1 - q + A_BAND] for q in range(CHUNK)], axis=1)
    bias_pairs = bias.reshape(A_HEADS // 2, 2 * CHUNK, A_BAND) * LOG2_E
    sink_rows = jnp.broadcast_to((sinks_b[0].astype(F32) * LOG2_E).reshape(B_KV_HEADS, B_GROUP, 1),
                                 (B_KV_HEADS, B_GROUP, CHUNK)).reshape(B_KV_HEADS, B_GROUP * CHUNK, 1)
    consts = (bias_pairs, sink_rows, w_branch_a[0].astype(BF16), w_branch_b[0].astype(BF16),
              w_out[0].astype(BF16), g_post_mix[0].reshape(1, d))

    x1_p = _attn_prompt(xp2, outs_p[:7], consts, _groups(g1p, cpp), nb)
    caches = (cache_a_k[0].reshape(ns, A_ROWS, WA), cache_a_v[0].reshape(ns, A_ROWS, WA),
              cache_b_k[0].reshape(ns, WINDOW, WKB), cache_b_v[0].reshape(ns, WINDOW, WKB))
    x1_s = _attn_sample(xs2, outs_s[:7], caches, consts, _groups(g1s, 1))

    h_packed, idx, gates, rank, counts = _router(
        x1_p, x1_s, both(sh2p, sh2s), both(sc2p, sc2s), g_pre_ffn[0].reshape(1, d),
        w_router[0].T.astype(BF16), b_router[0].astype(F32).reshape(N_EXPERTS, 1))
    n_blocks = (n_all * TOP_K) // EXPERT_ROWS + N_EXPERTS
    cnt = counts[:, 0].astype(I32)
    blocks_e = (cnt + EXPERT_ROWS - 1) // EXPERT_ROWS
    blk_end = jnp.cumsum(blocks_e)
    blk_start = blk_end - blocks_e
    n_active = blk_end[-1:]
    start_col = (blk_start * EXPERT_ROWS).astype(F32).reshape(N_EXPERTS, 1)

    dest_flat = _dest_rows(idx, rank, start_col).reshape(-1)
    xs, wg_bf, wu_bf, wd_bf = _dispatch(dest_flat, h_packed, we_gate[0], we_up[0], we_down[0],
                                        n_blocks * EXPERT_ROWS)
    ys = _experts(blk_start.astype(I32), blk_end.astype(I32), cnt, n_active.astype(I32), xs, wg_bf, wu_bf, wd_bf)
    y_p, y_s = _combine(dest_flat, ys, x1_p, x1_s, h_packed, gates, ws_gate[0].astype(BF16),
                        ws_up[0].astype(BF16), ws_down[0].astype(BF16), g_post_ffn[0].reshape(1, d),
                        both(g2p, g2s))

    a_heads = (A_HEADS, HEAD_DIM)
    b_heads = (B_KV_HEADS, HEAD_DIM)
    return (y_p.reshape(nb, seq, d), y_s.reshape(ns, dec, d),
            outs_p[7].reshape(1, nb, A_ROWS, *a_heads), outs_p[8].reshape(1, nb, A_ROWS, *a_heads),
            outs_p[9].reshape(1, nb, WINDOW, *b_heads), outs_p[10].reshape(1, nb, WINDOW, *b_heads),
            outs_s[7].reshape(1, ns, dec, *a_heads), outs_s[8].reshape(1, ns, dec, *a_heads),
            outs_s[9].reshape(1, ns, dec, *b_heads), outs_s[10].reshape(1, ns, dec, *b_heads))
```

```python
import functools

import jax
import jax.numpy as jnp
from jax import lax
from jax.experimental import pallas as pl
from jax.experimental.pallas import tpu as pltpu

F32 = jnp.float32
BF16 = jnp.bfloat16
I32 = jnp.int32
U32 = jnp.uint32
HIGHEST = lax.Precision.HIGHEST
LOG2_E = 1.4426950408889634

D_MODEL = 1024
CHUNK = 64
HEAD_DIM = 64
A_HEADS = 8
A_PREV_CHUNKS = 8
A_ROWS = A_PREV_CHUNKS * CHUNK
A_BAND = A_ROWS + CHUNK
REL_CLIP = 128
B_HEADS = 8
B_KV_HEADS = 2
B_GROUP = B_HEADS // B_KV_HEADS
WINDOW = 128
B_BAND = WINDOW + CHUNK
ROPE_THETA = 10000.0
N_EXPERTS = 256
TOP_K = 8
N_GROUPS = 8
PER_GROUP = N_EXPERTS // N_GROUPS
TOPK_GROUPS = 4
EXPERT_DIM = 256
ROUTED_SCALE = 2.5
EPS = 1e-6
PAST_LEN = 4096

WA = A_HEADS * HEAD_DIM
WB = B_HEADS * HEAD_DIM
WKB = B_KV_HEADS * HEAD_DIM
OFF_QA, OFF_KA, OFF_VA = 0, WA, 2 * WA
OFF_QB = 3 * WA
OFF_KB = OFF_QB + WB
OFF_VB = OFF_KB + WKB
OFF_G = OFF_VB + WKB
N_IN = OFF_G + 2 * D_MODEL

LANES = 128
TOK_TILE = 512
MOE_TILE = 256
EXPERT_ROWS = 256
HALF_D = D_MODEL // 2
VMEM_LIMIT = 56 * 1024 * 1024


def _cparams(sem, vmem=VMEM_LIMIT):
    return pltpu.CompilerParams(dimension_semantics=sem, vmem_limit_bytes=vmem)


def _rms(x):
    return x * lax.rsqrt(jnp.mean(x * x, axis=-1, keepdims=True) + EPS)


def _group_affine(y, mul, add):
    g = mul.shape[0]
    y3 = y.reshape(g, CHUNK, y.shape[-1]) * mul
    if add is not None:
        y3 = y3 + add
    return y3.reshape(g * CHUNK, y.shape[-1])


def _pack_halves(a, b):
    ua = lax.bitcast_convert_type(a.astype(BF16).astype(F32), U32)
    ub = lax.bitcast_convert_type(b.astype(BF16).astype(F32), U32)
    return (ua & jnp.uint32(0xFFFF0000)) | (ub >> 16)


def _unpack_halves(u):
    a = lax.bitcast_convert_type(u & jnp.uint32(0xFFFF0000), F32)
    b = lax.bitcast_convert_type(u << 16, F32)
    return a, b


ROW_SUBLANES = HALF_D // LANES


def _store_packed_rows(ref, x):
    rows = x.shape[0]
    p = _pack_halves(x[:, :HALF_D], x[:, HALF_D:])
    for s in range(ROW_SUBLANES):
        ref[pl.ds(s, rows, stride=ROW_SUBLANES), :] = p[:, s * LANES:(s + 1) * LANES]


def _load_packed_rows(ref, rows, n_valid=None):
    his, los = [], []
    for s in range(ROW_SUBLANES):
        u = ref[pl.ds(s, rows, stride=ROW_SUBLANES), :]
        if n_valid is not None:
            u = jnp.where(lax.broadcasted_iota(I32, u.shape, 0) < n_valid, u, jnp.uint32(0))
        a, b = _unpack_halves(u)
        his.append(a)
        los.append(b)
    return jnp.concatenate(his, axis=1), jnp.concatenate(los, axis=1)


def _mod_kernel(c_ref, w_ref, b_ref, o_ref):
    c = c_ref[...]
    s = c * jax.nn.sigmoid(c)
    o_ref[...] = jnp.dot(s, w_ref[...], precision=HIGHEST, preferred_element_type=F32) + b_ref[...]


def _modulation(c_all, w_ada, b_ada):
    rows = c_all.shape[0]
    n = w_ada.shape[1]
    tn = 512
    return pl.pallas_call(
        _mod_kernel,
        grid=(n // tn,),
        in_specs=[pl.BlockSpec((rows, D_MODEL), lambda j: (0, 0)),
                  pl.BlockSpec((D_MODEL, tn), lambda j: (0, j)),
                  pl.BlockSpec((1, tn), lambda j: (0, j))],
        out_specs=pl.BlockSpec((rows, tn), lambda j: (0, j)),
        out_shape=jax.ShapeDtypeStruct((rows, n), F32),
        compiler_params=_cparams(("arbitrary",)),
        name="modulation",
    )(c_all, w_ada, b_ada.reshape(1, n))


def _rope(x, cos, sin_signed):
    n = x.shape[-1]
    reps = n // LANES
    if reps > 1:
        cos = jnp.concatenate([cos] * reps, axis=1)
        sin_signed = jnp.concatenate([sin_signed] * reps, axis=1)
    lane = lax.broadcasted_iota(I32, x.shape, 1)
    first_half = (lane % HEAD_DIM) < (HEAD_DIM // 2)
    partner = jnp.where(first_half, pltpu.roll(x, n - HEAD_DIM // 2, 1), pltpu.roll(x, HEAD_DIM // 2, 1))
    return x * cos + partner * sin_signed


def _inproj_kernel(prompt_state, tiles_per_seq,
                   x_ref, sh_ref, sc_ref, g_ref, w_ref, cos_ref, sin_ref,
                   qa_ref, ka_ref, va_ref, qb_ref, kb_ref, vb_ref, gt_ref,
                   ska_ref, sva_ref, skb_ref, svb_ref):
    x = x_ref[...]
    h = _group_affine(_rms(x) * g_ref[...], 1.0 + sc_ref[...], sh_ref[...]).astype(BF16)

    def proj(off, width):
        return jnp.dot(h, w_ref[:, off:off + width], preferred_element_type=F32)

    cos = cos_ref[...]
    sin = sin_ref[...]
    scale = HEAD_DIM ** -0.5 * LOG2_E
    qa_ref[...] = (proj(OFF_QA, WA) * scale).astype(BF16)
    ka = proj(OFF_KA, WA)
    va = proj(OFF_VA, WA)
    ka_ref[...] = ka.astype(BF16)
    va_ref[...] = va.astype(BF16)
    qb_ref[...] = (_rope(proj(OFF_QB, WB), cos, sin) * scale).astype(BF16)
    kb = _rope(proj(OFF_KB, WKB), cos, sin)
    vb = proj(OFF_VB, WKB)
    kb_ref[...] = kb.astype(BF16)
    vb_ref[...] = vb.astype(BF16)
    gt_ref[...] = jax.nn.sigmoid(proj(OFF_G, 2 * D_MODEL)).astype(BF16)

    if prompt_state:
        @pl.when(pl.program_id(0) % tiles_per_seq == tiles_per_seq - 1)
        def _():
            ska_ref[...] = ka
            sva_ref[...] = va
            skb_ref[...] = kb[TOK_TILE - WINDOW:, :]
            svb_ref[...] = vb[TOK_TILE - WINDOW:, :]
    else:
        ska_ref[...] = ka
        sva_ref[...] = va
        skb_ref[...] = kb
        svb_ref[...] = vb


def _inproj(x2d, shift_g, scale_g, g_pre, w_in_bf, cos_tab, sin_tab, n_seq, prompt_state):
    n = x2d.shape[0]
    nt = n // TOK_TILE
    tiles_per_seq = nt // n_seq if prompt_state else 1
    tab_tiles = cos_tab.shape[0] // TOK_TILE
    gpt = TOK_TILE // CHUNK
    row = lambda i: (i, 0)
    grp = lambda i: (i, 0, 0)
    if prompt_state:
        st_shapes = [jax.ShapeDtypeStruct((n_seq, A_ROWS, WA), F32)] * 2 + \
                    [jax.ShapeDtypeStruct((n_seq, WINDOW, WKB), F32)] * 2
        st_specs = [pl.BlockSpec((None, A_ROWS, WA), lambda i: (i // tiles_per_seq, 0, 0))] * 2 + \
                   [pl.BlockSpec((None, WINDOW, WKB), lambda i: (i // tiles_per_seq, 0, 0))] * 2
    else:
        st_shapes = [jax.ShapeDtypeStruct((n, WA), F32)] * 2 + [jax.ShapeDtypeStruct((n, WKB), F32)] * 2
        st_specs = [pl.BlockSpec((TOK_TILE, WA), row)] * 2 + [pl.BlockSpec((TOK_TILE, WKB), row)] * 2
    out_shapes = [jax.ShapeDtypeStruct((n, WA), BF16)] * 4 + [jax.ShapeDtypeStruct((n, WKB), BF16)] * 2 + \
                 [jax.ShapeDtypeStruct((n, 2 * D_MODEL), BF16)]
    out_shapes = [out_shapes[0], out_shapes[1], out_shapes[2], out_shapes[3], out_shapes[4], out_shapes[5],
                  out_shapes[6]] + st_shapes
    out_specs = [pl.BlockSpec((TOK_TILE, WA), row)] * 4 + [pl.BlockSpec((TOK_TILE, WKB), row)] * 2 + \
                [pl.BlockSpec((TOK_TILE, 2 * D_MODEL), row)] + st_specs
    return pl.pallas_call(
        functools.partial(_inproj_kernel, prompt_state, tiles_per_seq),
        grid=(nt,),
        in_specs=[pl.BlockSpec((TOK_TILE, D_MODEL), row),
                  pl.BlockSpec((gpt, 1, D_MODEL), grp),
                  pl.BlockSpec((gpt, 1, D_MODEL), grp),
                  pl.BlockSpec((1, D_MODEL), lambda i: (0, 0)),
                  pl.BlockSpec((D_MODEL, N_IN), lambda i: (0, 0)),
                  pl.BlockSpec((TOK_TILE, LANES), lambda i: (i % tab_tiles, 0)),
                  pl.BlockSpec((TOK_TILE, LANES), lambda i: (i % tab_tiles, 0))],
        out_specs=out_specs,
        out_shape=out_shapes,
        compiler_params=_cparams(("arbitrary",)),
        name="inproj_prompt" if prompt_state else "inproj_sample",
    )(x2d, shift_g, scale_g, g_pre, w_in_bf, cos_tab, sin_tab)


def _attn_kernel(n_chunks, mask_first,
                 x_ref, qa_ref, qb_ref, gt_ref,
                 kap_ref, kac_ref, vap_ref, vac_ref, kbp_ref, kbc_ref, vbp_ref, vbc_ref,
                 bias_ref, sink_ref, wba_ref, wbb_ref, wout_ref, gpost_ref, gate1_ref,
                 o_ref,
                 ka_s, va_s, kb_s, vb_s, oa_s, ob_s):
    rows = n_chunks * CHUNK
    pb = kbp_ref.shape[0]
    ka_s[0:A_ROWS, :] = kap_ref[...].astype(BF16)
    va_s[0:A_ROWS, :] = vap_ref[...].astype(BF16)
    ka_s[A_ROWS:A_ROWS + rows, :] = kac_ref[...]
    va_s[A_ROWS:A_ROWS + rows, :] = vac_ref[...]
    kb_s[0:WINDOW, :] = kbp_ref[pb - WINDOW:pb, :].astype(BF16)
    vb_s[0:WINDOW, :] = vbp_ref[pb - WINDOW:pb, :].astype(BF16)
    kb_s[WINDOW:WINDOW + rows, :] = kbc_ref[...]
    vb_s[WINDOW:WINDOW + rows, :] = vbc_ref[...]

    lane_q = lax.broadcasted_iota(I32, (CHUNK, LANES), 1)
    nt_dims = (((1,), (1,)), ((), ()))

    def chunk_body(masked, c, carry):
        c0 = pl.multiple_of(c * CHUNK, CHUNK)
        if masked:
            valid_a = c0 + lax.broadcasted_iota(I32, (1, A_BAND), 1) >= A_ROWS
            valid_b = c0 + lax.broadcasted_iota(I32, (1, B_BAND), 1) >= WINDOW

        scores = []
        for p in range(A_HEADS // 2):
            cols = slice(p * LANES, (p + 1) * LANES)
            q = qa_ref[pl.ds(c0, CHUNK), cols].astype(F32)
            qs = jnp.concatenate([jnp.where(lane_q < HEAD_DIM, q, 0.0),
                                  jnp.where(lane_q >= HEAD_DIM, q, 0.0)], axis=0).astype(BF16)
            k = ka_s[pl.ds(c0, A_BAND), cols]
            s = lax.dot_general(qs, k, nt_dims, preferred_element_type=F32) + bias_ref[p]
            if masked:
                s = jnp.where(valid_a, s, -jnp.inf)
            scores.append(s)
        for g in range(B_KV_HEADS):
            parts = []
            for r in range(B_GROUP):
                head = g * B_GROUP + r
                t, half = head // 2, head % 2
                q = qb_ref[pl.ds(c0, CHUNK), t * LANES:(t + 1) * LANES].astype(F32)
                if half != g:
                    q = pltpu.roll(q, HEAD_DIM, 1)
                in_g = (lane_q >= HEAD_DIM) if g else (lane_q < HEAD_DIM)
                parts.append(jnp.where(in_g, q, 0.0))
            qs = jnp.concatenate(parts, axis=0).astype(BF16)
            k = kb_s[pl.ds(c0, B_BAND), :]
            s = lax.dot_general(qs, k, nt_dims, preferred_element_type=F32)
            if masked:
                s = jnp.where(valid_b, s, -jnp.inf)
            scores.append(s)

        numer, denom = [], []
        for n, s in enumerate(scores):
            m = jnp.max(s, axis=1, keepdims=True)
            if n >= A_HEADS // 2:
                sk = sink_ref[n - A_HEADS // 2]
                m = jnp.maximum(m, sk)
            e = jnp.exp2(s - m)
            l = jnp.sum(e, axis=1, keepdims=True)
            if n >= A_HEADS // 2:
                l = l + jnp.exp2(sk - m)
            numer.append(e.astype(BF16))
            denom.append(l)

        outs = []
        for n, e in enumerate(numer):
            if n < A_HEADS // 2:
                v = va_s[pl.ds(c0, A_BAND), n * LANES:(n + 1) * LANES]
            else:
                v = vb_s[pl.ds(c0, B_BAND), :]
            outs.append(jnp.dot(e, v, preferred_element_type=F32) / denom[n])

        for p in range(A_HEADS // 2):
            o = outs[p]
            oa_s[pl.ds(c0, CHUNK), p * LANES:(p + 1) * LANES] = jnp.where(
                lane_q < HEAD_DIM, o[:CHUNK], o[CHUNK:]).astype(BF16)
        for g in range(B_KV_HEADS):
            o = outs[A_HEADS // 2 + g]
            for s2 in range(B_GROUP // 2):
                o_even = o[(2 * s2) * CHUNK:(2 * s2 + 1) * CHUNK]
                o_odd = o[(2 * s2 + 1) * CHUNK:(2 * s2 + 2) * CHUNK]
                if g == 0:
                    tile = jnp.where(lane_q < HEAD_DIM, o_even, pltpu.roll(o_odd, HEAD_DIM, 1))
                else:
                    tile = jnp.where(lane_q < HEAD_DIM, pltpu.roll(o_even, HEAD_DIM, 1), o_odd)
                t = g * (B_GROUP // 2) + s2
                ob_s[pl.ds(c0, CHUNK), t * LANES:(t + 1) * LANES] = tile.astype(BF16)
        return carry

    def all_chunks(masked):
        lax.fori_loop(0, n_chunks, functools.partial(chunk_body, masked), 0, unroll=4 if n_chunks > 1 else 1)

    if mask_first:
        pl.when(pl.program_id(1) == 0)(lambda: all_chunks(True))
        pl.when(pl.program_id(1) > 0)(lambda: all_chunks(False))
    else:
        all_chunks(False)

    za = jnp.dot(oa_s[...], wba_ref[...], preferred_element_type=F32)
    zb = jnp.dot(ob_s[...], wbb_ref[...], preferred_element_type=F32)
    merged = gt_ref[:, :D_MODEL].astype(F32) * za + gt_ref[:, D_MODEL:].astype(F32) * zb
    mo = jnp.dot(merged.astype(BF16), wout_ref[...], preferred_element_type=F32)
    o_ref[...] = x_ref[...] + _group_affine(_rms(mo) * gpost_ref[...], gate1_ref[...], None)


def _attn_scratch(rows):
    return [pltpu.VMEM((A_ROWS + rows, WA), BF16), pltpu.VMEM((A_ROWS + rows, WA), BF16),
            pltpu.VMEM((WINDOW + rows, WKB), BF16), pltpu.VMEM((WINDOW + rows, WKB), BF16),
            pltpu.VMEM((rows, WA), BF16), pltpu.VMEM((rows, WB), BF16)]


def _const_specs(grid_rank):
    z2 = (lambda b, j: (0, 0)) if grid_rank == 2 else (lambda b: (0, 0))
    z3 = (lambda b, j: (0, 0, 0)) if grid_rank == 2 else (lambda b: (0, 0, 0))
    return [pl.BlockSpec((A_HEADS // 2, 2 * CHUNK, A_BAND), z3),
            pl.BlockSpec((B_KV_HEADS, B_GROUP * CHUNK, 1), z3),
            pl.BlockSpec((WA, D_MODEL), z2),
            pl.BlockSpec((WB, D_MODEL), z2),
            pl.BlockSpec((D_MODEL, D_MODEL), z2),
            pl.BlockSpec((1, D_MODEL), z2)]


def _attn_prompt(x2d, proj, consts, gate1_g, n_seq):
    qa, ka, va, qb, kb, vb, gt = proj
    n = x2d.shape[0]
    tps = n // n_seq // TOK_TILE
    gpt = TOK_TILE // CHUNK
    cur = lambda b, j: (b * tps + j, 0)
    prev = lambda b, j: (b * tps + jnp.maximum(j - 1, 0), 0)
    return pl.pallas_call(
        functools.partial(_attn_kernel, TOK_TILE // CHUNK, True),
        grid=(n_seq, tps),
        in_specs=[pl.BlockSpec((TOK_TILE, D_MODEL), cur),
                  pl.BlockSpec((TOK_TILE, WA), cur),
                  pl.BlockSpec((TOK_TILE, WB), cur),
                  pl.BlockSpec((TOK_TILE, 2 * D_MODEL), cur),
                  pl.BlockSpec((TOK_TILE, WA), prev), pl.BlockSpec((TOK_TILE, WA), cur),
                  pl.BlockSpec((TOK_TILE, WA), prev), pl.BlockSpec((TOK_TILE, WA), cur),
                  pl.BlockSpec((TOK_TILE, WKB), prev), pl.BlockSpec((TOK_TILE, WKB), cur),
                  pl.BlockSpec((TOK_TILE, WKB), prev), pl.BlockSpec((TOK_TILE, WKB), cur)]
                 + _const_specs(2)
                 + [pl.BlockSpec((gpt, 1, D_MODEL), lambda b, j: (b * tps + j, 0, 0))],
        out_specs=pl.BlockSpec((TOK_TILE, D_MODEL), cur),
        out_shape=jax.ShapeDtypeStruct((n, D_MODEL), F32),
        scratch_shapes=_attn_scratch(TOK_TILE),
        compiler_params=_cparams(("arbitrary", "arbitrary")),
        name="attn_prompt",
    )(x2d, qa, qb, gt, ka, ka, va, va, kb, kb, vb, vb, *consts, gate1_g)


def _attn_sample(x2d, proj, caches, consts, gate1_g):
    qa, ka, va, qb, kb, vb, gt = proj
    cak, cav, cbk, cbv = caches
    n_seq = cak.shape[0]
    cur = lambda b: (b, 0)
    cache = lambda b: (b, 0, 0)
    return pl.pallas_call(
        functools.partial(_attn_kernel, 1, False),
        grid=(n_seq,),
        in_specs=[pl.BlockSpec((CHUNK, D_MODEL), cur),
                  pl.BlockSpec((CHUNK, WA), cur),
                  pl.BlockSpec((CHUNK, WB), cur),
                  pl.BlockSpec((CHUNK, 2 * D_MODEL), cur),
                  pl.BlockSpec((None, A_ROWS, WA), cache), pl.BlockSpec((CHUNK, WA), cur),
                  pl.BlockSpec((None, A_ROWS, WA), cache), pl.BlockSpec((CHUNK, WA), cur),
                  pl.BlockSpec((None, WINDOW, WKB), cache), pl.BlockSpec((CHUNK, WKB), cur),
                  pl.BlockSpec((None, WINDOW, WKB), cache), pl.BlockSpec((CHUNK, WKB), cur)]
                 + _const_specs(1)
                 + [pl.BlockSpec((1, 1, D_MODEL), lambda b: (b, 0, 0))],
        out_specs=pl.BlockSpec((CHUNK, D_MODEL), cur),
        out_shape=jax.ShapeDtypeStruct(x2d.shape, F32),
        scratch_shapes=_attn_scratch(CHUNK),
        compiler_params=_cparams(("arbitrary",)),
        name="attn_sample",
    )(x2d, qa, qb, gt, cak, ka, cav, va, cbk, kb, cbv, vb, *consts, gate1_g)


def _two_part_tile(n_first_tiles, first_ref, second_ref):
    return jnp.where(pl.program_id(0) < n_first_tiles, first_ref[...], second_ref[...])


def _router_kernel(n_prompt_tiles, xp_ref, xs_ref, sh_ref, sc_ref, g_ref, wrt_ref, br_ref, wsg_ref, wsu_ref, wsd_ref,
                   h_ref, idx_ref, gate_ref, rank_ref, cnt_ref, shared_ref, carry):
    i = pl.program_id(0)

    @pl.when(i == 0)
    def _():
        carry[...] = jnp.zeros_like(carry)

    tm = xp_ref.shape[0]
    x = _two_part_tile(n_prompt_tiles, xp_ref, xs_ref)
    h = _group_affine(_rms(x) * g_ref[...], 1.0 + sc_ref[...], sh_ref[...])
    _store_packed_rows(h_ref, h)

    hb = h.astype(BF16)
    logits = lax.dot_general(wrt_ref[...], hb, (((1,), (1,)), ((), ())), preferred_element_type=F32)

    sg = jnp.dot(hb, wsg_ref[...], preferred_element_type=F32)
    sa = (sg * jax.nn.sigmoid(sg) * jnp.dot(hb, wsu_ref[...], preferred_element_type=F32)).astype(BF16)
    shared_ref[...] = jnp.dot(sa, wsd_ref[...], preferred_element_type=F32)

    scores = jax.nn.sigmoid(logits)
    biased = scores + br_ref[...]
    neg = -jnp.inf

    sub = lax.broadcasted_iota(I32, (PER_GROUP, tm), 0).astype(F32)
    gs_rows = []
    for g in range(N_GROUPS):
        xg = biased[g * PER_GROUP:(g + 1) * PER_GROUP]
        m1 = jnp.max(xg, axis=0, keepdims=True)
        i1 = jnp.min(jnp.where(xg == m1, sub, float(PER_GROUP)), axis=0, keepdims=True)
        m2 = jnp.max(jnp.where(sub == i1, neg, xg), axis=0, keepdims=True)
        gs_rows.append(m1 + m2)
    gs = jnp.concatenate(gs_rows, axis=0)

    giota = lax.broadcasted_iota(I32, (N_GROUPS, tm), 0).astype(F32)
    keep = jnp.zeros((N_GROUPS, tm), F32)
    for _ in range(TOPK_GROUPS):
        m = jnp.max(gs, axis=0, keepdims=True)
        gi = jnp.min(jnp.where(gs == m, giota, float(N_GROUPS)), axis=0, keepdims=True)
        hit = giota == gi
        keep = jnp.where(hit, 1.0, keep)
        gs = jnp.where(hit, neg, gs)
    cand = jnp.concatenate(
        [jnp.where(keep[g:g + 1] > 0.0, biased[g * PER_GROUP:(g + 1) * PER_GROUP], neg)
         for g in range(N_GROUPS)], axis=0)

    eiota = lax.broadcasted_iota(I32, (N_EXPERTS, tm), 0).astype(F32)
    idx_rows, gate_rows = [], []
    chosen = jnp.zeros((N_EXPERTS, tm), F32)
    for _ in range(TOP_K):
        m = jnp.max(cand, axis=0, keepdims=True)
        ei = jnp.min(jnp.where(cand == m, eiota, float(N_EXPERTS)), axis=0, keepdims=True)
        sel = eiota == ei
        gate_rows.append(jnp.sum(jnp.where(sel, scores, 0.0), axis=0, keepdims=True))
        idx_rows.append(ei)
        chosen = jnp.where(sel, 1.0, chosen)
        cand = jnp.where(sel, neg, cand)
    gates = jnp.concatenate(gate_rows, axis=0)
    gates = gates / jnp.sum(gates, axis=0, keepdims=True) * ROUTED_SCALE
    gate_ref[...] = gates
    idx_ref[...] = jnp.concatenate(idx_rows, axis=0).astype(I32)

    r_i = lax.broadcasted_iota(I32, (tm, tm), 0)
    c_i = lax.broadcasted_iota(I32, (tm, tm), 1)
    upper = jnp.where(r_i < c_i, 1.0, 0.0).astype(BF16)
    before = jnp.dot(chosen.astype(BF16), upper, preferred_element_type=F32) + carry[...]
    rank_rows = [jnp.sum(jnp.where(eiota == idx_rows[k], before, 0.0), axis=0, keepdims=True)
                 for k in range(TOP_K)]
    rank_ref[...] = jnp.concatenate(rank_rows, axis=0).astype(I32)
    total = carry[...] + jnp.sum(chosen, axis=1, keepdims=True)
    carry[...] = total
    cnt_ref[...] = total


def _router(x1_p, x1_s, shift_g, scale_g, g_pre, w_router_t, b_router_col, ws_gate, ws_up, ws_down):
    n = x1_p.shape[0] + x1_s.shape[0]
    nt = n // MOE_TILE
    npt = x1_p.shape[0] // MOE_TILE
    gpt = MOE_TILE // CHUNK
    lane_blk = lambda i: (0, i)
    return pl.pallas_call(
        functools.partial(_router_kernel, npt),
        grid=(nt,),
        in_specs=[pl.BlockSpec((MOE_TILE, D_MODEL), lambda i: (jnp.minimum(i, npt - 1), 0)),
                  pl.BlockSpec((MOE_TILE, D_MODEL), lambda i: (jnp.maximum(i - npt, 0), 0)),
                  pl.BlockSpec((gpt, 1, D_MODEL), lambda i: (i, 0, 0)),
                  pl.BlockSpec((gpt, 1, D_MODEL), lambda i: (i, 0, 0)),
                  pl.BlockSpec((1, D_MODEL), lambda i: (0, 0)),
                  pl.BlockSpec((N_EXPERTS, D_MODEL), lambda i: (0, 0)),
                  pl.BlockSpec((N_EXPERTS, 1), lambda i: (0, 0)),
                  pl.BlockSpec((D_MODEL, EXPERT_DIM), lambda i: (0, 0)),
                  pl.BlockSpec((D_MODEL, EXPERT_DIM), lambda i: (0, 0)),
                  pl.BlockSpec((EXPERT_DIM, D_MODEL), lambda i: (0, 0))],
        out_specs=[pl.BlockSpec((MOE_TILE * ROW_SUBLANES, LANES), lambda i: (i, 0)),
                   pl.BlockSpec((TOP_K, MOE_TILE), lane_blk),
                   pl.BlockSpec((TOP_K, MOE_TILE), lane_blk),
                   pl.BlockSpec((TOP_K, MOE_TILE), lane_blk),
                   pl.BlockSpec((N_EXPERTS, 1), lambda i: (0, 0)),
                   pl.BlockSpec((MOE_TILE, D_MODEL), lambda i: (i, 0))],
        out_shape=[jax.ShapeDtypeStruct((n * ROW_SUBLANES, LANES), U32),
                   jax.ShapeDtypeStruct((TOP_K, n), I32),
                   jax.ShapeDtypeStruct((TOP_K, n), F32),
                   jax.ShapeDtypeStruct((TOP_K, n), I32),
                   jax.ShapeDtypeStruct((N_EXPERTS, 1), F32),
                   jax.ShapeDtypeStruct((n, D_MODEL), F32)],
        scratch_shapes=[pltpu.VMEM((N_EXPERTS, 1), F32)],
        compiler_params=_cparams(("arbitrary",)),
        name="router",
    )(x1_p, x1_s, shift_g, scale_g, g_pre, w_router_t, b_router_col, ws_gate, ws_up, ws_down)


def _dest_kernel(idx_ref, rank_ref, start_ref, o_ref):
    eiota = lax.broadcasted_iota(I32, (N_EXPERTS, MOE_TILE), 0)
    start = start_ref[...]
    for sub in range(o_ref.shape[0]):
        cols = slice(sub * MOE_TILE, (sub + 1) * MOE_TILE)
        rows = [jnp.sum(jnp.where(eiota == idx_ref[k:k + 1, cols], start, 0.0), axis=0, keepdims=True)
                for k in range(TOP_K)]
        o_ref[sub] = jnp.concatenate(rows, axis=0).astype(I32) + rank_ref[:, cols]


def _dest_rows(idx, rank, start_col):
    n = idx.shape[1]
    nt = n // MOE_TILE
    per_step = next(c for c in (10, 8, 5, 4, 2, 1) if nt % c == 0)
    return pl.pallas_call(
        _dest_kernel,
        grid=(nt // per_step,),
        in_specs=[pl.BlockSpec((TOP_K, per_step * MOE_TILE), lambda i: (0, i)),
                  pl.BlockSpec((TOP_K, per_step * MOE_TILE), lambda i: (0, i)),
                  pl.BlockSpec((N_EXPERTS, 1), lambda i: (0, 0))],
        out_specs=pl.BlockSpec((per_step, TOP_K, MOE_TILE), lambda i: (i, 0, 0)),
        out_shape=jax.ShapeDtypeStruct((nt, TOP_K, MOE_TILE), I32),
        compiler_params=_cparams(("arbitrary",)),
        name="dest_rows",
    )(idx, rank, start_col)


DEST_PER_TILE = TOP_K * MOE_TILE


def _row_slice(ref, row):
    return ref.at[pl.ds(pl.multiple_of(row * ROW_SUBLANES, ROW_SUBLANES), ROW_SUBLANES)]


TILE_SUBLANES = MOE_TILE * ROW_SUBLANES


def _dispatch_kernel(n_cast_steps, dest_hbm, h_hbm, wg_ref, wu_ref, wd_ref, xs_hbm, wgb_ref, wub_ref, wdb_ref,
                     dest_s, hbuf, dsem, hsem, rsem):
    i = pl.program_id(0)
    nt = pl.num_programs(0)

    @pl.when(i < n_cast_steps)
    def _():
        wgb_ref[...] = wg_ref[...].astype(BF16)
        wub_ref[...] = wu_ref[...].astype(BF16)
        wdb_ref[...] = wd_ref[...].astype(BF16)

    def load(tile):
        dst = dest_s.at[pl.ds((tile % 2) * DEST_PER_TILE, DEST_PER_TILE)]
        rows = h_hbm.at[pl.ds(pl.multiple_of(tile * TILE_SUBLANES, TILE_SUBLANES), TILE_SUBLANES)]
        return (pltpu.make_async_copy(dest_hbm.at[pl.ds(tile * DEST_PER_TILE, DEST_PER_TILE)], dst, dsem.at[tile % 2]),
                pltpu.make_async_copy(rows, hbuf.at[tile % 3], hsem.at[tile % 3]))

    def drain(tile):
        for k in range(TOP_K):
            pltpu.make_async_copy(hbuf.at[tile % 3], xs_hbm.at[pl.ds(0, TILE_SUBLANES)],
                                  rsem.at[(tile % 2) * TOP_K + k]).wait()

    @pl.when(i == 0)
    def _():
        for cp in load(0):
            cp.start()

    @pl.when(i + 1 < nt)
    def _():
        for cp in load(i + 1):
            cp.start()

    for cp in load(i):
        cp.wait()
    base = (i % 2) * DEST_PER_TILE
    src_buf = hbuf.at[i % 3]
    sem0 = (i % 2) * TOP_K

    def body(t, carry):
        src = _row_slice(src_buf, t)
        for k in range(TOP_K):
            d = dest_s[base + k * MOE_TILE + t]
            pltpu.make_async_copy(src, _row_slice(xs_hbm, d), rsem.at[sem0 + k]).start(priority=k % 2)
        return carry

    lax.fori_loop(0, MOE_TILE, body, 0, unroll=2)

    @pl.when(i >= 1)
    def _():
        drain(i - 1)

    @pl.when(i == nt - 1)
    def _():
        drain(i)


def _dispatch(dest_flat, h_packed, we_gate, we_up, we_down, n_rows):
    nt = h_packed.shape[0] // TILE_SUBLANES
    per_step = -(-N_EXPERTS // nt)
    while N_EXPERTS % per_step:
        per_step += 1
    n_cast = N_EXPERTS // per_step
    wmap = lambda i: (jnp.minimum(i, n_cast - 1), 0, 0)
    up_blk = pl.BlockSpec((per_step, D_MODEL, EXPERT_DIM), wmap)
    down_blk = pl.BlockSpec((per_step, EXPERT_DIM, D_MODEL), wmap)
    return pl.pallas_call(
        functools.partial(_dispatch_kernel, n_cast),
        grid=(nt,),
        in_specs=[pl.BlockSpec(memory_space=pl.ANY),
                  pl.BlockSpec(memory_space=pl.ANY),
                  up_blk, up_blk, down_blk],
        out_specs=[pl.BlockSpec(memory_space=pl.ANY), up_blk, up_blk, down_blk],
        out_shape=[jax.ShapeDtypeStruct((n_rows * ROW_SUBLANES, LANES), U32),
                   jax.ShapeDtypeStruct(we_gate.shape, BF16),
                   jax.ShapeDtypeStruct(we_up.shape, BF16),
                   jax.ShapeDtypeStruct(we_down.shape, BF16)],
        scratch_shapes=[pltpu.SMEM((2 * DEST_PER_TILE,), I32),
                        pltpu.VMEM((3, TILE_SUBLANES, LANES), U32),
                        pltpu.SemaphoreType.DMA((2,)),
                        pltpu.SemaphoreType.DMA((3,)),
                        pltpu.SemaphoreType.DMA((2 * TOP_K,))],
        compiler_params=_cparams(("arbitrary",)),
        name="dispatch",
    )(dest_flat, h_packed, we_gate, we_up, we_down)


BLOCK_SUBLANES = EXPERT_ROWS * ROW_SUBLANES
EXPERT_AHEAD = 4
EXPERT_GROUP = 4
EXPERT_SLOTS = EXPERT_AHEAD + EXPERT_GROUP


def _expert_kernel(first_ref, last_ref, cnt_ref, na_ref, xs_hbm, wg_ref, wu_ref, wd_ref, ys_hbm,
                   xbuf, ybuf, isem, osem):
    e = pl.program_id(0)
    n_active = na_ref[0]
    first, last = first_ref[e], last_ref[e]

    def block_rows(ref, g):
        return ref.at[pl.ds(pl.multiple_of(g * BLOCK_SUBLANES, BLOCK_SUBLANES), BLOCK_SUBLANES)]

    def fetch(g, slot):
        return pltpu.make_async_copy(block_rows(xs_hbm, g), xbuf.at[slot], isem.at[slot])

    def flush(g, slot):
        return pltpu.make_async_copy(ybuf.at[slot], block_rows(ys_hbm, g), osem.at[slot])

    @pl.when(e == 0)
    def _():
        for g0 in range(EXPERT_AHEAD):
            @pl.when(g0 < n_active)
            def _():
                fetch(g0, g0).start()

    def enter(g):
        slot = g % EXPERT_SLOTS
        fetch(g, slot).wait()

        @pl.when(g + EXPERT_AHEAD < n_active)
        def _():
            fetch(g + EXPERT_AHEAD, (g + EXPERT_AHEAD) % EXPERT_SLOTS).start()

        @pl.when(g >= EXPERT_SLOTS)
        def _():
            flush(g - EXPERT_SLOTS, slot).wait()
        return slot

    def swiglu(halves):
        xa = jnp.concatenate([h[0] for h in halves], axis=0).astype(BF16)
        xb = jnp.concatenate([h[1] for h in halves], axis=0).astype(BF16)

        def up(w_ref):
            return (jnp.dot(xa, w_ref[:HALF_D, :], preferred_element_type=F32)
                    + jnp.dot(xb, w_ref[HALF_D:, :], preferred_element_type=F32))

        gate = up(wg_ref)
        a = (gate * jax.nn.sigmoid(gate) * up(wu_ref)).astype(BF16)
        return jnp.dot(a, wd_ref[...], preferred_element_type=F32)

    def n_valid(g):
        return cnt_ref[e] - (g - first) * EXPERT_ROWS

    def run_group(g, n_blk):
        slots = [enter(g + j) for j in range(n_blk)]
        halves = [_load_packed_rows(xbuf.at[slots[j]], EXPERT_ROWS, n_valid(g + j) if j == n_blk - 1 else None)
                  for j in range(n_blk)]
        y = swiglu(halves)
        for j in range(n_blk):
            _store_packed_rows(ybuf.at[slots[j]], y[j * EXPERT_ROWS:(j + 1) * EXPERT_ROWS])
            flush(g + j, slots[j]).start()

    n_blk_e = last - first
    n_full = lax.shift_right_logical(n_blk_e, EXPERT_GROUP.bit_length() - 1)

    def full_group(q, carry):
        run_group(first + q * EXPERT_GROUP, EXPERT_GROUP)
        return carry

    lax.fori_loop(0, n_full, full_group, 0)
    done = n_full * EXPERT_GROUP
    size = EXPERT_GROUP // 2
    while size >= 1:
        take = (n_blk_e & size) != 0
        pl.when(take)(functools.partial(run_group, first + done, size))
        done = done + jnp.where(take, size, 0)
        size //= 2

    @pl.when(e == pl.num_programs(0) - 1)
    def _():
        for back in range(EXPERT_SLOTS, 0, -1):
            @pl.when(n_active >= back)
            def _():
                flush(n_active - back, (n_active - back) % EXPERT_SLOTS).wait()


def _experts(blk_first, blk_last, counts, n_active, xs, we_gate, we_up, we_down):
    wmap = lambda e, *_: (e, 0, 0)
    return pl.pallas_call(
        _expert_kernel,
        grid_spec=pltpu.PrefetchScalarGridSpec(
            num_scalar_prefetch=4,
            grid=(N_EXPERTS,),
            in_specs=[pl.BlockSpec(memory_space=pl.ANY),
                      pl.BlockSpec((None, D_MODEL, EXPERT_DIM), wmap),
                      pl.BlockSpec((None, D_MODEL, EXPERT_DIM), wmap),
                      pl.BlockSpec((None, EXPERT_DIM, D_MODEL), wmap)],
            out_specs=pl.BlockSpec(memory_space=pl.ANY),
            scratch_shapes=[pltpu.VMEM((EXPERT_SLOTS, BLOCK_SUBLANES, LANES), U32),
                            pltpu.VMEM((EXPERT_SLOTS, BLOCK_SUBLANES, LANES), U32),
                            pltpu.SemaphoreType.DMA((EXPERT_SLOTS,)),
                            pltpu.SemaphoreType.DMA((EXPERT_SLOTS,))]),
        out_shape=jax.ShapeDtypeStruct(xs.shape, U32),
        compiler_params=_cparams(("arbitrary",)),
        name="experts",
    )(blk_first, blk_last, counts, n_active, xs, we_gate, we_up, we_down)


COMBINE_ROWS = 8


def _combine_kernel(n_prompt_tiles,
                    dest_hbm, ys_hbm, xp_ref, xs_ref, shared_ref, gate_ref, gpost_ref, gate2_ref,
                    yp_ref, ysm_ref, dest_s, buf, gcol_s, y_s, dsem, rsem):
    i = pl.program_id(0)
    nt = pl.num_programs(0)

    def dest_copy(tile):
        slot = tile % 3
        return pltpu.make_async_copy(dest_hbm.at[pl.ds(tile * DEST_PER_TILE, DEST_PER_TILE)],
                                     dest_s.at[pl.ds(slot * DEST_PER_TILE, DEST_PER_TILE)], dsem.at[slot])

    def gather_token(base, bset, t):
        dst0 = bset * TILE_SUBLANES + t * ROW_SUBLANES
        for k in range(TOP_K):
            d = dest_s[base + t + k * MOE_TILE]
            dst = buf.at[pl.ds(pl.multiple_of(dst0 + k * TILE_SUBLANES, ROW_SUBLANES), ROW_SUBLANES)]
            pltpu.make_async_copy(_row_slice(ys_hbm, d), dst, rsem.at[bset + k]).start(priority=k % 2)

    def wait_rows(bset):
        for k in range(TOP_K):
            pltpu.make_async_copy(ys_hbm.at[pl.ds(0, TILE_SUBLANES)], buf.at[pl.ds(0, TILE_SUBLANES)],
                                  rsem.at[bset + k]).wait()

    @pl.when(i == 0)
    def _():
        dest_copy(0).start()

        @pl.when(nt > 1)
        def _():
            dest_copy(1).start()

        dest_copy(0).wait()

        def first_tile(t, carry):
            gather_token(0, 0, t)
            return carry

        lax.fori_loop(0, MOE_TILE, first_tile, 0, unroll=4)

    @pl.when(i + 2 < nt)
    def _():
        dest_copy(i + 2).start()

    @pl.when(i + 1 < nt)
    def _():
        dest_copy(i + 1).wait()

    nxt = jnp.minimum(i + 1, nt - 1)
    base_next = (nxt % 3) * DEST_PER_TILE
    bset = (i % 2) * TOP_K
    bset_next = ((i + 1) % 2) * TOP_K

    gpad = jnp.concatenate([gate_ref[...], jnp.zeros((LANES - TOP_K, MOE_TILE), F32)], axis=0)
    gcol_s[...] = gpad.T

    wait_rows(bset)
    in_first = i < n_prompt_tiles

    def chunk(j, carry):
        r0 = pl.multiple_of(j * COMBINE_ROWS, COMBINE_ROWS)
        rows = pl.ds(r0, COMBINE_ROWS)
        acc_hi = [jnp.zeros((COMBINE_ROWS, LANES), F32) for _ in range(ROW_SUBLANES)]
        acc_lo = [jnp.zeros((COMBINE_ROWS, LANES), F32) for _ in range(ROW_SUBLANES)]
        src0 = bset * TILE_SUBLANES + r0 * ROW_SUBLANES
        for k in range(TOP_K):
            gk = jnp.broadcast_to(gcol_s[rows, k:k + 1], (COMBINE_ROWS, LANES))
            for s in range(ROW_SUBLANES):
                hi, lo = _unpack_halves(
                    buf[pl.ds(src0 + k * TILE_SUBLANES + s, COMBINE_ROWS, stride=ROW_SUBLANES), :])
                acc_hi[s] = acc_hi[s] + hi * gk
                acc_lo[s] = acc_lo[s] + lo * gk
        f = jnp.concatenate(acc_hi + acc_lo, axis=1) + shared_ref[rows, :]
        x = jnp.where(in_first, xp_ref[rows, :], xs_ref[rows, :])
        gate2 = gate2_ref[lax.shift_right_logical(j * COMBINE_ROWS, CHUNK.bit_length() - 1)]
        y = x + _rms(f) * gpost_ref[...] * gate2
        for t in range(COMBINE_ROWS):
            gather_token(base_next, bset_next, r0 + t)
        y_s[rows, :] = y
        return carry

    lax.fori_loop(0, MOE_TILE // COMBINE_ROWS, chunk, 0)

    @pl.when(in_first)
    def _():
        yp_ref[...] = y_s[...]

    @pl.when(jnp.logical_not(in_first))
    def _():
        ysm_ref[...] = y_s[...]

    @pl.when(i == nt - 1)
    def _():
        wait_rows(bset_next)


def _combine(dest_flat, ys, x1_p, x1_s, shared, gates, g_post, gate2_g):
    n_prompt = x1_p.shape[0]
    n = n_prompt + x1_s.shape[0]
    nt = n // MOE_TILE
    npt = n_prompt // MOE_TILE
    gpt = MOE_TILE // CHUNK
    z2 = lambda i: (0, 0)
    return pl.pallas_call(
        functools.partial(_combine_kernel, npt),
        grid=(nt,),
        in_specs=[pl.BlockSpec(memory_space=pl.ANY),
                  pl.BlockSpec(memory_space=pl.ANY),
                  pl.BlockSpec((MOE_TILE, D_MODEL), lambda i: (jnp.minimum(i, npt - 1), 0)),
                  pl.BlockSpec((MOE_TILE, D_MODEL), lambda i: (jnp.maximum(i - npt, 0), 0)),
                  pl.BlockSpec((MOE_TILE, D_MODEL), lambda i: (i, 0)),
                  pl.BlockSpec((TOP_K, MOE_TILE), lambda i: (0, i)),
                  pl.BlockSpec((1, D_MODEL), z2),
                  pl.BlockSpec((gpt, 1, D_MODEL), lambda i: (i, 0, 0))],
        out_specs=[pl.BlockSpec((MOE_TILE, D_MODEL), lambda i: (jnp.minimum(i, npt - 1), 0)),
                   pl.BlockSpec((MOE_TILE, D_MODEL), lambda i: (jnp.maximum(i - npt, 0), 0))],
        out_shape=[jax.ShapeDtypeStruct((n_prompt, D_MODEL), F32),
                   jax.ShapeDtypeStruct((n - n_prompt, D_MODEL), F32)],
        scratch_shapes=[pltpu.SMEM((3 * DEST_PER_TILE,), I32),
                        pltpu.VMEM((2 * TOP_K * TILE_SUBLANES, LANES), U32),
                        pltpu.VMEM((MOE_TILE, LANES), F32),
                        pltpu.VMEM((MOE_TILE, D_MODEL), F32),
                        pltpu.SemaphoreType.DMA((3,)),
                        pltpu.SemaphoreType.DMA((2 * TOP_K,))],
        compiler_params=_cparams(("arbitrary",)),
        name="combine",
    )(dest_flat, ys, x1_p, x1_s, shared, gates, g_post, gate2_g)


def _rope_tables(pos):
    half = HEAD_DIM // 2
    inv = ROPE_THETA ** (-jnp.arange(half, dtype=F32) / half)
    ang = pos.astype(F32)[:, None] * inv[None, :]
    cos, sin = jnp.cos(ang), jnp.sin(ang)
    return jnp.concatenate([cos] * 4, axis=1), jnp.concatenate([-sin, sin, -sin, sin], axis=1)


def _groups(vec_rows, reps):
    rows, width = vec_rows.shape
    return jnp.broadcast_to(vec_rows[:, None, :], (rows, reps, width)).reshape(rows * reps, 1, width)


def kernel(x_prompt, x_sample, cache_a_k, cache_a_v, cache_b_k, cache_b_v, c_prompt, c_sample, w_ada, b_ada,
           g_pre_mix, g_post_mix, w_in, rel_bias_a, sinks_b, w_branch_a, w_branch_b, w_out, g_pre_ffn,
           g_post_ffn, w_router, b_router, we_gate, we_up, we_down, ws_gate, ws_up, ws_down):
    assert w_ada.shape[0] == 1, "single layer"
    nb, seq, d = x_prompt.shape
    ns, dec = x_sample.shape[:2]
    assert d == D_MODEL and dec == CHUNK and seq % TOK_TILE == 0 and (ns * dec) == TOK_TILE
    n_p, n_s = nb * seq, ns * dec
    n_all = n_p + n_s
    assert n_all % MOE_TILE == 0 and n_p % MOE_TILE == 0

    c_all = jnp.concatenate([c_prompt, c_sample], axis=0)
    pad = (-c_all.shape[0]) % 8
    c_all = jnp.pad(c_all, ((0, pad), (0, 0)))
    mod = _modulation(c_all, w_ada[0], b_ada[0])
    mod_p, mod_s = mod[:nb], mod[nb:nb + ns]
    cpp = seq // CHUNK

    def part(k):
        return mod_p[:, k * d:(k + 1) * d], mod_s[:, k * d:(k + 1) * d]

    (sh1p, sh1s), (sc1p, sc1s), (g1p, g1s), (sh2p, sh2s), (sc2p, sc2s), (g2p, g2s) = [part(k) for k in range(6)]
    both = lambda p, s: jnp.concatenate([_groups(p, cpp), _groups(s, 1)], axis=0)

    w_in_bf = w_in[0].astype(BF16)
    g_pre = g_pre_mix[0].reshape(1, d)
    cos_p, sin_p = _rope_tables(jnp.arange(seq))
    cos_s, sin_s = _rope_tables(PAST_LEN + jnp.arange(dec))
    cos_s, sin_s = jnp.tile(cos_s, (ns, 1)), jnp.tile(sin_s, (ns, 1))

    xp2 = x_prompt.reshape(n_p, d)
    xs2 = x_sample.reshape(n_s, d)
    outs_p = _inproj(xp2, _groups(sh1p, cpp), _groups(sc1p, cpp), g_pre, w_in_bf, cos_p, sin_p, nb, True)
    outs_s = _inproj(xs2, _groups(sh1s, 1), _groups(sc1s, 1), g_pre, w_in_bf, cos_s, sin_s, ns, False)

    table = rel_bias_a[0].astype(F32)
    n_far = A_BAND - 1 - REL_CLIP
    ext = jnp.concatenate([jnp.broadcast_to(table[:, 2 * REL_CLIP:], (A_HEADS, n_far)),
                           jnp.flip(table[:, REL_CLIP - (CHUNK - 1):], axis=1)], axis=1)
    bias = jnp.stack([ext[:, CHUNK - 1 - q:CHUNK - 1 - q + A_BAND] for q in range(CHUNK)], axis=1)
    bias_pairs = bias.reshape(A_HEADS // 2, 2 * CHUNK, A_BAND) * LOG2_E
    sink_rows = jnp.broadcast_to((sinks_b[0].astype(F32) * LOG2_E).reshape(B_KV_HEADS, B_GROUP, 1),
                                 (B_KV_HEADS, B_GROUP, CHUNK)).reshape(B_KV_HEADS, B_GROUP * CHUNK, 1)
    consts = (bias_pairs, sink_rows, w_branch_a[0].astype(BF16), w_branch_b[0].astype(BF16),
              w_out[0].astype(BF16), g_post_mix[0].reshape(1, d))

    x1_p = _attn_prompt(xp2, outs_p[:7], consts, _groups(g1p, cpp), nb)
    caches = (cache_a_k[0].reshape(ns, A_ROWS, WA), cache_a_v[0].reshape(ns, A_ROWS, WA),
              cache_b_k[0].reshape(ns, WINDOW, WKB), cache_b_v[0].reshape(ns, WINDOW, WKB))
    x1_s = _attn_sample(xs2, outs_s[:7], caches, consts, _groups(g1s, 1))

    h_packed, idx, gates, rank, counts, shared = _router(
        x1_p, x1_s, both(sh2p, sh2s), both(sc2p, sc2s), g_pre_ffn[0].reshape(1, d),
        w_router[0].T.astype(BF16), b_router[0].astype(F32).reshape(N_EXPERTS, 1),
        ws_gate[0].astype(BF16), ws_up[0].astype(BF16), ws_down[0].astype(BF16))
    n_blocks = (n_all * TOP_K) // EXPERT_ROWS + N_EXPERTS
    cnt = counts[:, 0].astype(I32)
    blocks_e = (cnt + EXPERT_ROWS - 1) // EXPERT_ROWS
    blk_end = jnp.cumsum(blocks_e)
    blk_start = blk_end - blocks_e
    n_active = blk_end[-1:]
    start_col = (blk_start * EXPERT_ROWS).astype(F32).reshape(N_EXPERTS, 1)

    dest_flat = _dest_rows(idx, rank, start_col).reshape(-1)
    xs, wg_bf, wu_bf, wd_bf = _dispatch(dest_flat, h_packed, we_gate[0], we_up[0], we_down[0],
                                        n_blocks * EXPERT_ROWS)
    ys = _experts(blk_start.astype(I32), blk_end.astype(I32), cnt, n_active.astype(I32), xs, wg_bf, wu_bf, wd_bf)
    y_p, y_s = _combine(dest_flat, ys, x1_p, x1_s, shared, gates, g_post_ffn[0].reshape(1, d), both(g2p, g2s))

    a_heads = (A_HEADS, HEAD_DIM)
    b_heads = (B_KV_HEADS, HEAD_DIM)
    return (y_p.reshape(nb, seq, d), y_s.reshape(ns, dec, d),
            outs_p[7].reshape(1, nb, A_ROWS, *a_heads), outs_p[8].reshape(1, nb, A_ROWS, *a_heads),
            outs_p[9].reshape(1, nb, WINDOW, *b_heads), outs_p[10].reshape(1, nb, WINDOW, *b_heads),
            outs_s[7].reshape(1, ns, dec, *a_heads), outs_s[8].reshape(1, ns, dec, *a_heads),
            outs_s[9].reshape(1, ns, dec, *b_heads), outs_s[10].reshape(1, ns, dec, *b_heads))
```

```python
import functools

import jax
import jax.numpy as jnp
from jax import lax
from jax.experimental import pallas as pl
from jax.experimental.pallas import tpu as pltpu

F32 = jnp.float32
BF16 = jnp.bfloat16
I32 = jnp.int32
U32 = jnp.uint32
HIGHEST = lax.Precision.HIGHEST
LOG2_E = 1.4426950408889634

D_MODEL = 1024
CHUNK = 64
HEAD_DIM = 64
A_HEADS = 8
A_PREV_CHUNKS = 8
A_ROWS = A_PREV_CHUNKS * CHUNK
A_BAND = A_ROWS + CHUNK
REL_CLIP = 128
B_HEADS = 8
B_KV_HEADS = 2
B_GROUP = B_HEADS // B_KV_HEADS
WINDOW = 128
B_BAND = WINDOW + CHUNK
ROPE_THETA = 10000.0
N_EXPERTS = 256
TOP_K = 8
N_GROUPS = 8
PER_GROUP = N_EXPERTS // N_GROUPS
TOPK_GROUPS = 4
EXPERT_DIM = 256
ROUTED_SCALE = 2.5
EPS = 1e-6
PAST_LEN = 4096

WA = A_HEADS * HEAD_DIM
WB = B_HEADS * HEAD_DIM
WKB = B_KV_HEADS * HEAD_DIM
OFF_QA, OFF_KA, OFF_VA = 0, WA, 2 * WA
OFF_QB = 3 * WA
OFF_KB = OFF_QB + WB
OFF_VB = OFF_KB + WKB
OFF_G = OFF_VB + WKB
N_IN = OFF_G + 2 * D_MODEL

LANES = 128
TOK_TILE = 512
MOE_TILE = 256
EXPERT_ROWS = 256
HALF_D = D_MODEL // 2
VMEM_LIMIT = 56 * 1024 * 1024


def _cparams(sem, vmem=VMEM_LIMIT):
    return pltpu.CompilerParams(dimension_semantics=sem, vmem_limit_bytes=vmem)


def _rms(x):
    return x * lax.rsqrt(jnp.mean(x * x, axis=-1, keepdims=True) + EPS)


def _group_affine(y, mul, add):
    g = mul.shape[0]
    y3 = y.reshape(g, CHUNK, y.shape[-1]) * mul
    if add is not None:
        y3 = y3 + add
    return y3.reshape(g * CHUNK, y.shape[-1])


def _pack_halves(a, b):
    ua = lax.bitcast_convert_type(a.astype(BF16).astype(F32), U32)
    ub = lax.bitcast_convert_type(b.astype(BF16).astype(F32), U32)
    return (ua & jnp.uint32(0xFFFF0000)) | (ub >> 16)


def _unpack_halves(u):
    a = lax.bitcast_convert_type(u & jnp.uint32(0xFFFF0000), F32)
    b = lax.bitcast_convert_type(u << 16, F32)
    return a, b


ROW_SUBLANES = HALF_D // LANES


def _store_packed_rows(ref, x):
    rows = x.shape[0]
    p = _pack_halves(x[:, :HALF_D], x[:, HALF_D:])
    for s in range(ROW_SUBLANES):
        ref[pl.ds(s, rows, stride=ROW_SUBLANES), :] = p[:, s * LANES:(s + 1) * LANES]


def _load_packed_rows(ref, rows, n_valid=None):
    his, los = [], []
    for s in range(ROW_SUBLANES):
        u = ref[pl.ds(s, rows, stride=ROW_SUBLANES), :]
        if n_valid is not None:
            u = jnp.where(lax.broadcasted_iota(I32, u.shape, 0) < n_valid, u, jnp.uint32(0))
        a, b = _unpack_halves(u)
        his.append(a)
        los.append(b)
    return jnp.concatenate(his, axis=1), jnp.concatenate(los, axis=1)


def _mod_kernel(c_ref, w_ref, b_ref, o_ref):
    c = c_ref[...]
    s = c * jax.nn.sigmoid(c)
    o_ref[...] = jnp.dot(s, w_ref[...], precision=HIGHEST, preferred_element_type=F32) + b_ref[...]


def _modulation(c_all, w_ada, b_ada):
    rows = c_all.shape[0]
    n = w_ada.shape[1]
    tn = 512
    return pl.pallas_call(
        _mod_kernel,
        grid=(n // tn,),
        in_specs=[pl.BlockSpec((rows, D_MODEL), lambda j: (0, 0)),
                  pl.BlockSpec((D_MODEL, tn), lambda j: (0, j)),
                  pl.BlockSpec((1, tn), lambda j: (0, j))],
        out_specs=pl.BlockSpec((rows, tn), lambda j: (0, j)),
        out_shape=jax.ShapeDtypeStruct((rows, n), F32),
        compiler_params=_cparams(("arbitrary",)),
        name="modulation",
    )(c_all, w_ada, b_ada.reshape(1, n))


def _rope(x, cos, sin_signed):
    n = x.shape[-1]
    reps = n // LANES
    if reps > 1:
        cos = jnp.concatenate([cos] * reps, axis=1)
        sin_signed = jnp.concatenate([sin_signed] * reps, axis=1)
    lane = lax.broadcasted_iota(I32, x.shape, 1)
    first_half = (lane % HEAD_DIM) < (HEAD_DIM // 2)
    partner = jnp.where(first_half, pltpu.roll(x, n - HEAD_DIM // 2, 1), pltpu.roll(x, HEAD_DIM // 2, 1))
    return x * cos + partner * sin_signed


def _inproj_kernel(prompt_state, tiles_per_seq,
                   x_ref, sh_ref, sc_ref, g_ref, w_ref, cos_ref, sin_ref,
                   qa_ref, ka_ref, va_ref, qb_ref, kb_ref, vb_ref, gt_ref,
                   ska_ref, sva_ref, skb_ref, svb_ref):
    x = x_ref[...]
    h = _group_affine(_rms(x) * g_ref[...], 1.0 + sc_ref[...], sh_ref[...]).astype(BF16)

    def proj(off, width):
        return jnp.dot(h, w_ref[:, off:off + width], preferred_element_type=F32)

    cos = cos_ref[...]
    sin = sin_ref[...]
    scale = HEAD_DIM ** -0.5 * LOG2_E
    qa_ref[...] = (proj(OFF_QA, WA) * scale).astype(BF16)
    ka = proj(OFF_KA, WA)
    va = proj(OFF_VA, WA)
    ka_ref[...] = ka.astype(BF16)
    va_ref[...] = va.astype(BF16)
    qb_ref[...] = (_rope(proj(OFF_QB, WB), cos, sin) * scale).astype(BF16)
    kb = _rope(proj(OFF_KB, WKB), cos, sin)
    vb = proj(OFF_VB, WKB)
    kb_ref[...] = kb.astype(BF16)
    vb_ref[...] = vb.astype(BF16)
    gt_ref[...] = jax.nn.sigmoid(proj(OFF_G, 2 * D_MODEL)).astype(BF16)

    if prompt_state:
        @pl.when(pl.program_id(0) % tiles_per_seq == tiles_per_seq - 1)
        def _():
            ska_ref[...] = ka
            sva_ref[...] = va
            skb_ref[...] = kb[TOK_TILE - WINDOW:, :]
            svb_ref[...] = vb[TOK_TILE - WINDOW:, :]
    else:
        ska_ref[...] = ka
        sva_ref[...] = va
        skb_ref[...] = kb
        svb_ref[...] = vb


def _inproj(x2d, shift_g, scale_g, g_pre, w_in_bf, cos_tab, sin_tab, n_seq, prompt_state):
    n = x2d.shape[0]
    nt = n // TOK_TILE
    tiles_per_seq = nt // n_seq if prompt_state else 1
    tab_tiles = cos_tab.shape[0] // TOK_TILE
    gpt = TOK_TILE // CHUNK
    row = lambda i: (i, 0)
    grp = lambda i: (i, 0, 0)
    if prompt_state:
        st_shapes = [jax.ShapeDtypeStruct((n_seq, A_ROWS, WA), F32)] * 2 + \
                    [jax.ShapeDtypeStruct((n_seq, WINDOW, WKB), F32)] * 2
        st_specs = [pl.BlockSpec((None, A_ROWS, WA), lambda i: (i // tiles_per_seq, 0, 0))] * 2 + \
                   [pl.BlockSpec((None, WINDOW, WKB), lambda i: (i // tiles_per_seq, 0, 0))] * 2
    else:
        st_shapes = [jax.ShapeDtypeStruct((n, WA), F32)] * 2 + [jax.ShapeDtypeStruct((n, WKB), F32)] * 2
        st_specs = [pl.BlockSpec((TOK_TILE, WA), row)] * 2 + [pl.BlockSpec((TOK_TILE, WKB), row)] * 2
    out_shapes = [jax.ShapeDtypeStruct((n, WA), BF16)] * 4 + [jax.ShapeDtypeStruct((n, WKB), BF16)] * 2 + \
                 [jax.ShapeDtypeStruct((n, 2 * D_MODEL), BF16)] + st_shapes
    out_specs = [pl.BlockSpec((TOK_TILE, WA), row)] * 4 + [pl.BlockSpec((TOK_TILE, WKB), row)] * 2 + \
                [pl.BlockSpec((TOK_TILE, 2 * D_MODEL), row)] + st_specs
    return pl.pallas_call(
        functools.partial(_inproj_kernel, prompt_state, tiles_per_seq),
        grid=(nt,),
        in_specs=[pl.BlockSpec((TOK_TILE, D_MODEL), row),
                  pl.BlockSpec((gpt, 1, D_MODEL), grp),
                  pl.BlockSpec((gpt, 1, D_MODEL), grp),
                  pl.BlockSpec((1, D_MODEL), lambda i: (0, 0)),
                  pl.BlockSpec((D_MODEL, N_IN), lambda i: (0, 0)),
                  pl.BlockSpec((TOK_TILE, LANES), lambda i: (i % tab_tiles, 0)),
                  pl.BlockSpec((TOK_TILE, LANES), lambda i: (i % tab_tiles, 0))],
        out_specs=out_specs,
        out_shape=out_shapes,
        compiler_params=_cparams(("arbitrary",)),
        name="inproj_prompt" if prompt_state else "inproj_sample",
    )(x2d, shift_g, scale_g, g_pre, w_in_bf, cos_tab, sin_tab)


def _attn_kernel(n_chunks, mask_first,
                 x_ref, qa_ref, qb_ref, gt_ref,
                 kap_ref, kac_ref, vap_ref, vac_ref, kbp_ref, kbc_ref, vbp_ref, vbc_ref,
                 bias_ref, sink_ref, wba_ref, wbb_ref, wout_ref, gpost_ref, gate1_ref,
                 o_ref,
                 ka_s, va_s, kb_s, vb_s, oa_s, ob_s):
    rows = n_chunks * CHUNK
    pb = kbp_ref.shape[0]
    ka_s[0:A_ROWS, :] = kap_ref[...].astype(BF16)
    va_s[0:A_ROWS, :] = vap_ref[...].astype(BF16)
    ka_s[A_ROWS:A_ROWS + rows, :] = kac_ref[...]
    va_s[A_ROWS:A_ROWS + rows, :] = vac_ref[...]
    kb_s[0:WINDOW, :] = kbp_ref[pb - WINDOW:pb, :].astype(BF16)
    vb_s[0:WINDOW, :] = vbp_ref[pb - WINDOW:pb, :].astype(BF16)
    kb_s[WINDOW:WINDOW + rows, :] = kbc_ref[...]
    vb_s[WINDOW:WINDOW + rows, :] = vbc_ref[...]

    lane_q = lax.broadcasted_iota(I32, (CHUNK, LANES), 1)
    nt_dims = (((1,), (1,)), ((), ()))

    def chunk_body(masked, c, carry):
        c0 = pl.multiple_of(c * CHUNK, CHUNK)
        if masked:
            valid_a = c0 + lax.broadcasted_iota(I32, (1, A_BAND), 1) >= A_ROWS
            valid_b = c0 + lax.broadcasted_iota(I32, (1, B_BAND), 1) >= WINDOW

        scores = []
        for p in range(A_HEADS // 2):
            cols = slice(p * LANES, (p + 1) * LANES)
            q = qa_ref[pl.ds(c0, CHUNK), cols].astype(F32)
            qs = jnp.concatenate([jnp.where(lane_q < HEAD_DIM, q, 0.0),
                                  jnp.where(lane_q >= HEAD_DIM, q, 0.0)], axis=0).astype(BF16)
            k = ka_s[pl.ds(c0, A_BAND), cols]
            s = lax.dot_general(qs, k, nt_dims, preferred_element_type=F32) + bias_ref[p]
            if masked:
                s = jnp.where(valid_a, s, -jnp.inf)
            scores.append(s)
        for g in range(B_KV_HEADS):
            parts = []
            for r in range(B_GROUP):
                head = g * B_GROUP + r
                t, half = head // 2, head % 2
                q = qb_ref[pl.ds(c0, CHUNK), t * LANES:(t + 1) * LANES].astype(F32)
                if half != g:
                    q = pltpu.roll(q, HEAD_DIM, 1)
                in_g = (lane_q >= HEAD_DIM) if g else (lane_q < HEAD_DIM)
                parts.append(jnp.where(in_g, q, 0.0))
            qs = jnp.concatenate(parts, axis=0).astype(BF16)
            k = kb_s[pl.ds(c0, B_BAND), :]
            s = lax.dot_general(qs, k, nt_dims, preferred_element_type=F32)
            if masked:
                s = jnp.where(valid_b, s, -jnp.inf)
            scores.append(s)

        numer, denom = [], []
        for n, s in enumerate(scores):
            m = jnp.max(s, axis=1, keepdims=True)
            if n >= A_HEADS // 2:
                sk = sink_ref[n - A_HEADS // 2]
                m = jnp.maximum(m, sk)
            e = jnp.exp2(s - m)
            l = jnp.sum(e, axis=1, keepdims=True)
            if n >= A_HEADS // 2:
                l = l + jnp.exp2(sk - m)
            numer.append(e.astype(BF16))
            denom.append(l)

        outs = []
        for n, e in enumerate(numer):
            if n < A_HEADS // 2:
                v = va_s[pl.ds(c0, A_BAND), n * LANES:(n + 1) * LANES]
            else:
                v = vb_s[pl.ds(c0, B_BAND), :]
            outs.append(jnp.dot(e, v, preferred_element_type=F32) / denom[n])

        for p in range(A_HEADS // 2):
            o = outs[p]
            oa_s[pl.ds(c0, CHUNK), p * LANES:(p + 1) * LANES] = jnp.where(
                lane_q < HEAD_DIM, o[:CHUNK], o[CHUNK:]).astype(BF16)
        for g in range(B_KV_HEADS):
            o = outs[A_HEADS // 2 + g]
            for s2 in range(B_GROUP // 2):
                o_even = o[(2 * s2) * CHUNK:(2 * s2 + 1) * CHUNK]
                o_odd = o[(2 * s2 + 1) * CHUNK:(2 * s2 + 2) * CHUNK]
                if g == 0:
                    tile = jnp.where(lane_q < HEAD_DIM, o_even, pltpu.roll(o_odd, HEAD_DIM, 1))
                else:
                    tile = jnp.where(lane_q < HEAD_DIM, pltpu.roll(o_even, HEAD_DIM, 1), o_odd)
                t = g * (B_GROUP // 2) + s2
                ob_s[pl.ds(c0, CHUNK), t * LANES:(t + 1) * LANES] = tile.astype(BF16)
        return carry

    def all_chunks(masked):
        lax.fori_loop(0, n_chunks, functools.partial(chunk_body, masked), 0, unroll=4 if n_chunks > 1 else 1)

    if mask_first:
        pl.when(pl.program_id(1) == 0)(lambda: all_chunks(True))
        pl.when(pl.program_id(1) > 0)(lambda: all_chunks(False))
    else:
        all_chunks(False)

    za = jnp.dot(oa_s[...], wba_ref[...], preferred_element_type=F32)
    zb = jnp.dot(ob_s[...], wbb_ref[...], preferred_element_type=F32)
    merged = gt_ref[:, :D_MODEL].astype(F32) * za + gt_ref[:, D_MODEL:].astype(F32) * zb
    mo = jnp.dot(merged.astype(BF16), wout_ref[...], preferred_element_type=F32)
    o_ref[...] = x_ref[...] + _group_affine(_rms(mo) * gpost_ref[...], gate1_ref[...], None)


def _attn_scratch(rows):
    return [pltpu.VMEM((A_ROWS + rows, WA), BF16), pltpu.VMEM((A_ROWS + rows, WA), BF16),
            pltpu.VMEM((WINDOW + rows, WKB), BF16), pltpu.VMEM((WINDOW + rows, WKB), BF16),
            pltpu.VMEM((rows, WA), BF16), pltpu.VMEM((rows, WB), BF16)]


def _const_specs(grid_rank):
    z2 = (lambda b, j: (0, 0)) if grid_rank == 2 else (lambda b: (0, 0))
    z3 = (lambda b, j: (0, 0, 0)) if grid_rank == 2 else (lambda b: (0, 0, 0))
    return [pl.BlockSpec((A_HEADS // 2, 2 * CHUNK, A_BAND), z3),
            pl.BlockSpec((B_KV_HEADS, B_GROUP * CHUNK, 1), z3),
            pl.BlockSpec((WA, D_MODEL), z2),
            pl.BlockSpec((WB, D_MODEL), z2),
            pl.BlockSpec((D_MODEL, D_MODEL), z2),
            pl.BlockSpec((1, D_MODEL), z2)]


def _attn_prompt(x2d, proj, consts, gate1_g, n_seq):
    qa, ka, va, qb, kb, vb, gt = proj
    n = x2d.shape[0]
    tps = n // n_seq // TOK_TILE
    gpt = TOK_TILE // CHUNK
    cur = lambda b, j: (b * tps + j, 0)
    prev = lambda b, j: (b * tps + jnp.maximum(j - 1, 0), 0)
    return pl.pallas_call(
        functools.partial(_attn_kernel, TOK_TILE // CHUNK, True),
        grid=(n_seq, tps),
        in_specs=[pl.BlockSpec((TOK_TILE, D_MODEL), cur),
                  pl.BlockSpec((TOK_TILE, WA), cur),
                  pl.BlockSpec((TOK_TILE, WB), cur),
                  pl.BlockSpec((TOK_TILE, 2 * D_MODEL), cur),
                  pl.BlockSpec((TOK_TILE, WA), prev), pl.BlockSpec((TOK_TILE, WA), cur),
                  pl.BlockSpec((TOK_TILE, WA), prev), pl.BlockSpec((TOK_TILE, WA), cur),
                  pl.BlockSpec((TOK_TILE, WKB), prev), pl.BlockSpec((TOK_TILE, WKB), cur),
                  pl.BlockSpec((TOK_TILE, WKB), prev), pl.BlockSpec((TOK_TILE, WKB), cur)]
                 + _const_specs(2)
                 + [pl.BlockSpec((gpt, 1, D_MODEL), lambda b, j: (b * tps + j, 0, 0))],
        out_specs=pl.BlockSpec((TOK_TILE, D_MODEL), cur),
        out_shape=jax.ShapeDtypeStruct((n, D_MODEL), F32),
        scratch_shapes=_attn_scratch(TOK_TILE),
        compiler_params=_cparams(("arbitrary", "arbitrary")),
        name="attn_prompt",
    )(x2d, qa, qb, gt, ka, ka, va, va, kb, kb, vb, vb, *consts, gate1_g)


def _attn_sample(x2d, proj, caches, consts, gate1_g):
    qa, ka, va, qb, kb, vb, gt = proj
    cak, cav, cbk, cbv = caches
    n_seq = cak.shape[0]
    cur = lambda b: (b, 0)
    cache = lambda b: (b, 0, 0)
    return pl.pallas_call(
        functools.partial(_attn_kernel, 1, False),
        grid=(n_seq,),
        in_specs=[pl.BlockSpec((CHUNK, D_MODEL), cur),
                  pl.BlockSpec((CHUNK, WA), cur),
                  pl.BlockSpec((CHUNK, WB), cur),
                  pl.BlockSpec((CHUNK, 2 * D_MODEL), cur),
                  pl.BlockSpec((None, A_ROWS, WA), cache), pl.BlockSpec((CHUNK, WA), cur),
                  pl.BlockSpec((None, A_ROWS, WA), cache), pl.BlockSpec((CHUNK, WA), cur),
                  pl.BlockSpec((None, WINDOW, WKB), cache), pl.BlockSpec((CHUNK, WKB), cur),
                  pl.BlockSpec((None, WINDOW, WKB), cache), pl.BlockSpec((CHUNK, WKB), cur)]
                 + _const_specs(1)
                 + [pl.BlockSpec((1, 1, D_MODEL), lambda b: (b, 0, 0))],
        out_specs=pl.BlockSpec((CHUNK, D_MODEL), cur),
        out_shape=jax.ShapeDtypeStruct(x2d.shape, F32),
        scratch_shapes=_attn_scratch(CHUNK),
        compiler_params=_cparams(("arbitrary",)),
        name="attn_sample",
    )(x2d, qa, qb, gt, cak, ka, cav, va, cbk, kb, cbv, vb, *consts, gate1_g)


def _two_part_tile(n_first_tiles, first_ref, second_ref):
    return jnp.where(pl.program_id(0) < n_first_tiles, first_ref[...], second_ref[...])


def _router_kernel(n_prompt_tiles, xp_ref, xs_ref, shp_ref, shs_ref, scp_ref, scs_ref, g_ref, wrt_ref, br_ref,
                   h_ref, idx_ref, gate_ref, rank_ref, cnt_ref, carry):
    i = pl.program_id(0)

    @pl.when(i == 0)
    def _():
        carry[...] = jnp.zeros_like(carry)

    tm = xp_ref.shape[0]
    x = _two_part_tile(n_prompt_tiles, xp_ref, xs_ref)
    h = _group_affine(_rms(x) * g_ref[...], 1.0 + _two_part_tile(n_prompt_tiles, scp_ref, scs_ref),
                      _two_part_tile(n_prompt_tiles, shp_ref, shs_ref))
    _store_packed_rows(h_ref, h)

    logits = lax.dot_general(wrt_ref[...], h.astype(BF16), (((1,), (1,)), ((), ())),
                             preferred_element_type=F32)
    scores = jax.nn.sigmoid(logits)
    biased = scores + br_ref[...]
    neg = -jnp.inf

    sub = lax.broadcasted_iota(I32, (PER_GROUP, tm), 0).astype(F32)
    gs_rows = []
    for g in range(N_GROUPS):
        xg = biased[g * PER_GROUP:(g + 1) * PER_GROUP]
        m1 = jnp.max(xg, axis=0, keepdims=True)
        i1 = jnp.min(jnp.where(xg == m1, sub, float(PER_GROUP)), axis=0, keepdims=True)
        m2 = jnp.max(jnp.where(sub == i1, neg, xg), axis=0, keepdims=True)
        gs_rows.append(m1 + m2)
    gs = jnp.concatenate(gs_rows, axis=0)

    giota = lax.broadcasted_iota(I32, (N_GROUPS, tm), 0).astype(F32)
    keep = jnp.zeros((N_GROUPS, tm), F32)
    for _ in range(TOPK_GROUPS):
        m = jnp.max(gs, axis=0, keepdims=True)
        gi = jnp.min(jnp.where(gs == m, giota, float(N_GROUPS)), axis=0, keepdims=True)
        hit = giota == gi
        keep = jnp.where(hit, 1.0, keep)
        gs = jnp.where(hit, neg, gs)
    cand = jnp.concatenate(
        [jnp.where(keep[g:g + 1] > 0.0, biased[g * PER_GROUP:(g + 1) * PER_GROUP], neg)
         for g in range(N_GROUPS)], axis=0)

    eiota = lax.broadcasted_iota(I32, (N_EXPERTS, tm), 0).astype(F32)
    idx_rows, gate_rows = [], []
    chosen = jnp.zeros((N_EXPERTS, tm), F32)
    for _ in range(TOP_K):
        m = jnp.max(cand, axis=0, keepdims=True)
        ei = jnp.min(jnp.where(cand == m, eiota, float(N_EXPERTS)), axis=0, keepdims=True)
        sel = eiota == ei
        gate_rows.append(jnp.sum(jnp.where(sel, scores, 0.0), axis=0, keepdims=True))
        idx_rows.append(ei)
        chosen = jnp.where(sel, 1.0, chosen)
        cand = jnp.where(sel, neg, cand)
    gates = jnp.concatenate(gate_rows, axis=0)
    gates = gates / jnp.sum(gates, axis=0, keepdims=True) * ROUTED_SCALE
    gate_ref[...] = gates
    idx_ref[...] = jnp.concatenate(idx_rows, axis=0).astype(I32)

    r_i = lax.broadcasted_iota(I32, (tm, tm), 0)
    c_i = lax.broadcasted_iota(I32, (tm, tm), 1)
    upper = jnp.where(r_i < c_i, 1.0, 0.0).astype(BF16)
    before = jnp.dot(chosen.astype(BF16), upper, preferred_element_type=F32) + carry[...]
    rank_rows = [jnp.sum(jnp.where(eiota == idx_rows[k], before, 0.0), axis=0, keepdims=True)
                 for k in range(TOP_K)]
    rank_ref[...] = jnp.concatenate(rank_rows, axis=0).astype(I32)
    total = carry[...] + jnp.sum(chosen, axis=1, keepdims=True)
    carry[...] = total
    cnt_ref[...] = total


def _router(x1_p, x1_s, shift_gp, shift_gs, scale_gp, scale_gs, g_pre, w_router_t, b_router_col):
    n = x1_p.shape[0] + x1_s.shape[0]
    nt = n // MOE_TILE
    npt = x1_p.shape[0] // MOE_TILE
    gpt = MOE_TILE // CHUNK
    lane_blk = lambda i: (0, i)
    return pl.pallas_call(
        functools.partial(_router_kernel, npt),
        grid=(nt,),
        in_specs=[pl.BlockSpec((MOE_TILE, D_MODEL), lambda i: (jnp.minimum(i, npt - 1), 0)),
                  pl.BlockSpec((MOE_TILE, D_MODEL), lambda i: (jnp.maximum(i - npt, 0), 0)),
                  pl.BlockSpec((gpt, 1, D_MODEL), lambda i: (jnp.minimum(i, npt - 1), 0, 0)),
                  pl.BlockSpec((gpt, 1, D_MODEL), lambda i: (jnp.maximum(i - npt, 0), 0, 0)),
                  pl.BlockSpec((gpt, 1, D_MODEL), lambda i: (jnp.minimum(i, npt - 1), 0, 0)),
                  pl.BlockSpec((gpt, 1, D_MODEL), lambda i: (jnp.maximum(i - npt, 0), 0, 0)),
                  pl.BlockSpec((1, D_MODEL), lambda i: (0, 0)),
                  pl.BlockSpec((N_EXPERTS, D_MODEL), lambda i: (0, 0)),
                  pl.BlockSpec((N_EXPERTS, 1), lambda i: (0, 0))],
        out_specs=[pl.BlockSpec((MOE_TILE * ROW_SUBLANES, LANES), lambda i: (i, 0)),
                   pl.BlockSpec((TOP_K, MOE_TILE), lane_blk),
                   pl.BlockSpec((TOP_K, MOE_TILE), lane_blk),
                   pl.BlockSpec((TOP_K, MOE_TILE), lane_blk),
                   pl.BlockSpec((N_EXPERTS, 1), lambda i: (0, 0))],
        out_shape=[jax.ShapeDtypeStruct((n * ROW_SUBLANES, LANES), U32),
                   jax.ShapeDtypeStruct((TOP_K, n), I32),
                   jax.ShapeDtypeStruct((TOP_K, n), F32),
                   jax.ShapeDtypeStruct((TOP_K, n), I32),
                   jax.ShapeDtypeStruct((N_EXPERTS, 1), F32)],
        scratch_shapes=[pltpu.VMEM((N_EXPERTS, 1), F32)],
        compiler_params=_cparams(("arbitrary",)),
        name="router",
    )(x1_p, x1_s, shift_gp, shift_gs, scale_gp, scale_gs, g_pre, w_router_t, b_router_col)


def _dest_kernel(idx_ref, rank_ref, start_ref, o_ref):
    eiota = lax.broadcasted_iota(I32, (N_EXPERTS, MOE_TILE), 0)
    start = start_ref[...]
    for sub in range(o_ref.shape[0]):
        cols = slice(sub * MOE_TILE, (sub + 1) * MOE_TILE)
        rows = [jnp.sum(jnp.where(eiota == idx_ref[k:k + 1, cols], start, 0.0), axis=0, keepdims=True)
                for k in range(TOP_K)]
        o_ref[sub] = jnp.concatenate(rows, axis=0).astype(I32) + rank_ref[:, cols]


def _dest_rows(idx, rank, start_col):
    n = idx.shape[1]
    nt = n // MOE_TILE
    per_step = next(c for c in (10, 8, 5, 4, 2, 1) if nt % c == 0)
    return pl.pallas_call(
        _dest_kernel,
        grid=(nt // per_step,),
        in_specs=[pl.BlockSpec((TOP_K, per_step * MOE_TILE), lambda i: (0, i)),
                  pl.BlockSpec((TOP_K, per_step * MOE_TILE), lambda i: (0, i)),
                  pl.BlockSpec((N_EXPERTS, 1), lambda i: (0, 0))],
        out_specs=pl.BlockSpec((per_step, TOP_K, MOE_TILE), lambda i: (i, 0, 0)),
        out_shape=jax.ShapeDtypeStruct((nt, TOP_K, MOE_TILE), I32),
        compiler_params=_cparams(("arbitrary",)),
        name="dest_rows",
    )(idx, rank, start_col)


DEST_PER_TILE = TOP_K * MOE_TILE


def _row_slice(ref, row):
    return ref.at[pl.ds(pl.multiple_of(row * ROW_SUBLANES, ROW_SUBLANES), ROW_SUBLANES)]


TILE_SUBLANES = MOE_TILE * ROW_SUBLANES


def _dispatch_kernel(dest_hbm, h_hbm, xs_hbm, dest_s, hbuf, dsem, hsem, rsem):
    i = pl.program_id(0)
    nt = pl.num_programs(0)

    def load(tile):
        dst = dest_s.at[pl.ds((tile % 2) * DEST_PER_TILE, DEST_PER_TILE)]
        rows = h_hbm.at[pl.ds(pl.multiple_of(tile * TILE_SUBLANES, TILE_SUBLANES), TILE_SUBLANES)]
        return (pltpu.make_async_copy(dest_hbm.at[pl.ds(tile * DEST_PER_TILE, DEST_PER_TILE)], dst, dsem.at[tile % 2]),
                pltpu.make_async_copy(rows, hbuf.at[tile % 3], hsem.at[tile % 3]))

    def drain(tile):
        for k in range(TOP_K):
            pltpu.make_async_copy(hbuf.at[tile % 3], xs_hbm.at[pl.ds(0, TILE_SUBLANES)],
                                  rsem.at[(tile % 2) * TOP_K + k]).wait()

    @pl.when(i == 0)
    def _():
        for cp in load(0):
            cp.start()

    @pl.when(i + 1 < nt)
    def _():
        for cp in load(i + 1):
            cp.start()

    for cp in load(i):
        cp.wait()
    base = (i % 2) * DEST_PER_TILE
    src_buf = hbuf.at[i % 3]
    sem0 = (i % 2) * TOP_K

    def body(t, carry):
        src = _row_slice(src_buf, t)
        for k in range(TOP_K):
            d = dest_s[base + k * MOE_TILE + t]
            pltpu.make_async_copy(src, _row_slice(xs_hbm, d), rsem.at[sem0 + k]).start(priority=k % 2)
        return carry

    lax.fori_loop(0, MOE_TILE, body, 0, unroll=4)

    @pl.when(i >= 1)
    def _():
        drain(i - 1)

    @pl.when(i == nt - 1)
    def _():
        drain(i)


def _dispatch(dest_flat, h_packed, n_rows):
    nt = h_packed.shape[0] // TILE_SUBLANES
    return pl.pallas_call(
        _dispatch_kernel,
        grid=(nt,),
        in_specs=[pl.BlockSpec(memory_space=pl.ANY),
                  pl.BlockSpec(memory_space=pl.ANY)],
        out_specs=pl.BlockSpec(memory_space=pl.ANY),
        out_shape=jax.ShapeDtypeStruct((n_rows * ROW_SUBLANES, LANES), U32),
        scratch_shapes=[pltpu.SMEM((2 * DEST_PER_TILE,), I32),
                        pltpu.VMEM((3, TILE_SUBLANES, LANES), U32),
                        pltpu.SemaphoreType.DMA((2,)),
                        pltpu.SemaphoreType.DMA((3,)),
                        pltpu.SemaphoreType.DMA((2 * TOP_K,))],
        compiler_params=_cparams(("arbitrary",)),
        name="dispatch",
    )(dest_flat, h_packed)


BLOCK_SUBLANES = EXPERT_ROWS * ROW_SUBLANES
EXPERT_AHEAD = 4
EXPERT_SLOTS = EXPERT_AHEAD + 2


def _expert_kernel(first_ref, last_ref, cnt_ref, na_ref, xs_hbm, wg_ref, wu_ref, wd_ref, ys_hbm,
                   wg_s, wu_s, wd_s, xbuf, ybuf, isem, osem):
    e = pl.program_id(0)
    n_active = na_ref[0]
    first, last = first_ref[e], last_ref[e]

    def block_rows(ref, g):
        return ref.at[pl.ds(pl.multiple_of(g * BLOCK_SUBLANES, BLOCK_SUBLANES), BLOCK_SUBLANES)]

    def fetch(g, slot):
        return pltpu.make_async_copy(block_rows(xs_hbm, g), xbuf.at[slot], isem.at[slot])

    def flush(g, slot):
        return pltpu.make_async_copy(ybuf.at[slot], block_rows(ys_hbm, g), osem.at[slot])

    @pl.when(e == 0)
    def _():
        for g0 in range(EXPERT_AHEAD):
            @pl.when(g0 < n_active)
            def _():
                fetch(g0, g0).start()

    @pl.when(last > first)
    def _():
        wg_s[...] = wg_ref[...].astype(BF16)
        wu_s[...] = wu_ref[...].astype(BF16)
        wd_s[...] = wd_ref[...].astype(BF16)

    def enter(g):
        slot = g % EXPERT_SLOTS
        fetch(g, slot).wait()

        @pl.when(g + EXPERT_AHEAD < n_active)
        def _():
            fetch(g + EXPERT_AHEAD, (g + EXPERT_AHEAD) % EXPERT_SLOTS).start()

        @pl.when(g >= EXPERT_SLOTS)
        def _():
            flush(g - EXPERT_SLOTS, slot).wait()
        return slot

    def swiglu(halves):
        xa = jnp.concatenate([h[0] for h in halves], axis=0).astype(BF16)
        xb = jnp.concatenate([h[1] for h in halves], axis=0).astype(BF16)

        def up(w_s):
            return (jnp.dot(xa, w_s[:HALF_D, :], preferred_element_type=F32)
                    + jnp.dot(xb, w_s[HALF_D:, :], preferred_element_type=F32))

        gate = up(wg_s)
        a = (gate * jax.nn.sigmoid(gate) * up(wu_s)).astype(BF16)
        return jnp.dot(a, wd_s[...], preferred_element_type=F32)

    def n_valid(g):
        return cnt_ref[e] - (g - first) * EXPERT_ROWS

    def pair_body(p, carry):
        g = first + 2 * p
        s0 = enter(g)
        s1 = enter(g + 1)
        y = swiglu([_load_packed_rows(xbuf.at[s0], EXPERT_ROWS),
                    _load_packed_rows(xbuf.at[s1], EXPERT_ROWS, n_valid(g + 1))])
        _store_packed_rows(ybuf.at[s0], y[:EXPERT_ROWS])
        _store_packed_rows(ybuf.at[s1], y[EXPERT_ROWS:])
        flush(g, s0).start()
        flush(g + 1, s1).start()
        return carry

    n_pairs = lax.shift_right_logical(last - first, 1)
    lax.fori_loop(0, n_pairs, pair_body, 0)

    @pl.when((last - first) % 2 == 1)
    def _():
        g = last - 1
        s0 = enter(g)
        y = swiglu([_load_packed_rows(xbuf.at[s0], EXPERT_ROWS, n_valid(g))])
        _store_packed_rows(ybuf.at[s0], y)
        flush(g, s0).start()

    @pl.when(e == pl.num_programs(0) - 1)
    def _():
        for back in range(EXPERT_SLOTS, 0, -1):
            @pl.when(n_active >= back)
            def _():
                flush(n_active - back, (n_active - back) % EXPERT_SLOTS).wait()


def _experts(blk_first, blk_last, counts, n_active, xs, we_gate, we_up, we_down):
    wmap = lambda e, *_: (e, 0, 0)
    return pl.pallas_call(
        _expert_kernel,
        grid_spec=pltpu.PrefetchScalarGridSpec(
            num_scalar_prefetch=4,
            grid=(N_EXPERTS,),
            in_specs=[pl.BlockSpec(memory_space=pl.ANY),
                      pl.BlockSpec((None, D_MODEL, EXPERT_DIM), wmap),
                      pl.BlockSpec((None, D_MODEL, EXPERT_DIM), wmap),
                      pl.BlockSpec((None, EXPERT_DIM, D_MODEL), wmap)],
            out_specs=pl.BlockSpec(memory_space=pl.ANY),
            scratch_shapes=[pltpu.VMEM((D_MODEL, EXPERT_DIM), BF16),
                            pltpu.VMEM((D_MODEL, EXPERT_DIM), BF16),
                            pltpu.VMEM((EXPERT_DIM, D_MODEL), BF16),
                            pltpu.VMEM((EXPERT_SLOTS, BLOCK_SUBLANES, LANES), U32),
                            pltpu.VMEM((EXPERT_SLOTS, BLOCK_SUBLANES, LANES), U32),
                            pltpu.SemaphoreType.DMA((EXPERT_SLOTS,)),
                            pltpu.SemaphoreType.DMA((EXPERT_SLOTS,))]),
        out_shape=jax.ShapeDtypeStruct(xs.shape, U32),
        compiler_params=_cparams(("arbitrary",)),
        name="experts",
    )(blk_first, blk_last, counts, n_active, xs, we_gate, we_up, we_down)


COMBINE_ROWS = 8


def _combine_kernel(n_prompt_tiles,
                    dest_hbm, ys_hbm, xp_ref, xs_ref, h_ref, gate_ref, wsg_ref, wsu_ref, wsd_ref, gpost_ref,
                    gate2p_ref, gate2s_ref,
                    yp_ref, ysm_ref, dest_s, buf, shared_s, gcol_s, gate2_s, y_s, dsem, rsem):
    i = pl.program_id(0)
    nt = pl.num_programs(0)

    def dest_copy(tile):
        slot = tile % 3
        return pltpu.make_async_copy(dest_hbm.at[pl.ds(tile * DEST_PER_TILE, DEST_PER_TILE)],
                                     dest_s.at[pl.ds(slot * DEST_PER_TILE, DEST_PER_TILE)], dsem.at[slot])

    def gather_token(base, bset, t):
        dst0 = bset * TILE_SUBLANES + t * ROW_SUBLANES
        for k in range(TOP_K):
            d = dest_s[base + t + k * MOE_TILE]
            dst = buf.at[pl.ds(pl.multiple_of(dst0 + k * TILE_SUBLANES, ROW_SUBLANES), ROW_SUBLANES)]
            pltpu.make_async_copy(_row_slice(ys_hbm, d), dst, rsem.at[bset + k]).start(priority=k % 2)

    def wait_rows(bset):
        for k in range(TOP_K):
            pltpu.make_async_copy(ys_hbm.at[pl.ds(0, TILE_SUBLANES)], buf.at[pl.ds(0, TILE_SUBLANES)],
                                  rsem.at[bset + k]).wait()

    @pl.when(i == 0)
    def _():
        dest_copy(0).start()

        @pl.when(nt > 1)
        def _():
            dest_copy(1).start()

        dest_copy(0).wait()

        def first_tile(t, carry):
            gather_token(0, 0, t)
            return carry

        lax.fori_loop(0, MOE_TILE, first_tile, 0, unroll=4)

    @pl.when(i + 2 < nt)
    def _():
        dest_copy(i + 2).start()

    @pl.when(i + 1 < nt)
    def _():
        dest_copy(i + 1).wait()

    nxt = jnp.minimum(i + 1, nt - 1)
    base_next = (nxt % 3) * DEST_PER_TILE
    bset = (i % 2) * TOP_K
    bset_next = ((i + 1) % 2) * TOP_K

    ha, hb = _load_packed_rows(h_ref, MOE_TILE)
    ha = ha.astype(BF16)
    hb = hb.astype(BF16)

    def up(w_ref):
        return (jnp.dot(ha, w_ref[:HALF_D, :], preferred_element_type=F32)
                + jnp.dot(hb, w_ref[HALF_D:, :], preferred_element_type=F32))

    g = up(wsg_ref)
    a = (g * jax.nn.sigmoid(g) * up(wsu_ref)).astype(BF16)
    shared_s[...] = jnp.dot(a, wsd_ref[...], preferred_element_type=F32)

    gpad = jnp.concatenate([gate_ref[...], jnp.zeros((LANES - TOP_K, MOE_TILE), F32)], axis=0)
    gcol_s[...] = gpad.T

    wait_rows(bset)
    in_first = i < n_prompt_tiles
    gate2_s[...] = _two_part_tile(n_prompt_tiles, gate2p_ref, gate2s_ref)

    def chunk(j, carry):
        r0 = pl.multiple_of(j * COMBINE_ROWS, COMBINE_ROWS)
        rows = pl.ds(r0, COMBINE_ROWS)
        acc_hi = [jnp.zeros((COMBINE_ROWS, LANES), F32) for _ in range(ROW_SUBLANES)]
        acc_lo = [jnp.zeros((COMBINE_ROWS, LANES), F32) for _ in range(ROW_SUBLANES)]
        src0 = bset * TILE_SUBLANES + r0 * ROW_SUBLANES
        for k in range(TOP_K):
            gk = jnp.broadcast_to(gcol_s[rows, k:k + 1], (COMBINE_ROWS, LANES))
            for s in range(ROW_SUBLANES):
                hi, lo = _unpack_halves(
                    buf[pl.ds(src0 + k * TILE_SUBLANES + s, COMBINE_ROWS, stride=ROW_SUBLANES), :])
                acc_hi[s] = acc_hi[s] + hi * gk
                acc_lo[s] = acc_lo[s] + lo * gk
        f = jnp.concatenate(acc_hi + acc_lo, axis=1) + shared_s[rows, :]
        x = jnp.where(in_first, xp_ref[rows, :], xs_ref[rows, :])
        gate2 = gate2_s[lax.shift_right_logical(j * COMBINE_ROWS, CHUNK.bit_length() - 1)]
        y = x + _rms(f) * gpost_ref[...] * gate2
        for t in range(COMBINE_ROWS):
            gather_token(base_next, bset_next, r0 + t)
        y_s[rows, :] = y
        return carry

    lax.fori_loop(0, MOE_TILE // COMBINE_ROWS, chunk, 0)

    @pl.when(in_first)
    def _():
        yp_ref[...] = y_s[...]

    @pl.when(jnp.logical_not(in_first))
    def _():
        ysm_ref[...] = y_s[...]

    @pl.when(i == nt - 1)
    def _():
        wait_rows(bset_next)


def _combine(dest_flat, ys, x1_p, x1_s, h_packed, gates, ws_gate, ws_up, ws_down, g_post, gate2_gp, gate2_gs):
    n_prompt = x1_p.shape[0]
    n = n_prompt + x1_s.shape[0]
    nt = n // MOE_TILE
    npt = n_prompt // MOE_TILE
    gpt = MOE_TILE // CHUNK
    z2 = lambda i: (0, 0)
    return pl.pallas_call(
        functools.partial(_combine_kernel, npt),
        grid=(nt,),
        in_specs=[pl.BlockSpec(memory_space=pl.ANY),
                  pl.BlockSpec(memory_space=pl.ANY),
                  pl.BlockSpec((MOE_TILE, D_MODEL), lambda i: (jnp.minimum(i, npt - 1), 0)),
                  pl.BlockSpec((MOE_TILE, D_MODEL), lambda i: (jnp.maximum(i - npt, 0), 0)),
                  pl.BlockSpec((MOE_TILE * ROW_SUBLANES, LANES), lambda i: (i, 0)),
                  pl.BlockSpec((TOP_K, MOE_TILE), lambda i: (0, i)),
                  pl.BlockSpec((D_MODEL, EXPERT_DIM), z2),
                  pl.BlockSpec((D_MODEL, EXPERT_DIM), z2),
                  pl.BlockSpec((EXPERT_DIM, D_MODEL), z2),
                  pl.BlockSpec((1, D_MODEL), z2),
                  pl.BlockSpec((gpt, 1, D_MODEL), lambda i: (jnp.minimum(i, npt - 1), 0, 0)),
                  pl.BlockSpec((gpt, 1, D_MODEL), lambda i: (jnp.maximum(i - npt, 0), 0, 0))],
        out_specs=[pl.BlockSpec((MOE_TILE, D_MODEL), lambda i: (jnp.minimum(i, npt - 1), 0)),
                   pl.BlockSpec((MOE_TILE, D_MODEL), lambda i: (jnp.maximum(i - npt, 0), 0))],
        out_shape=[jax.ShapeDtypeStruct((n_prompt, D_MODEL), F32),
                   jax.ShapeDtypeStruct((n - n_prompt, D_MODEL), F32)],
        scratch_shapes=[pltpu.SMEM((3 * DEST_PER_TILE,), I32),
                        pltpu.VMEM((2 * TOP_K * TILE_SUBLANES, LANES), U32),
                        pltpu.VMEM((MOE_TILE, D_MODEL), F32),
                        pltpu.VMEM((MOE_TILE, LANES), F32),
                        pltpu.VMEM((MOE_TILE // CHUNK, 1, D_MODEL), F32),
                        pltpu.VMEM((MOE_TILE, D_MODEL), F32),
                        pltpu.SemaphoreType.DMA((3,)),
                        pltpu.SemaphoreType.DMA((2 * TOP_K,))],
        compiler_params=_cparams(("arbitrary",)),
        name="combine",
    )(dest_flat, ys, x1_p, x1_s, h_packed, gates, ws_gate, ws_up, ws_down, g_post, gate2_gp, gate2_gs)


def _rope_tables(pos):
    half = HEAD_DIM // 2
    inv = ROPE_THETA ** (-jnp.arange(half, dtype=F32) / half)
    ang = pos.astype(F32)[:, None] * inv[None, :]
    cos, sin = jnp.cos(ang), jnp.sin(ang)
    return jnp.concatenate([cos] * 4, axis=1), jnp.concatenate([-sin, sin, -sin, sin], axis=1)


def _groups(vec_rows, reps):
    rows, width = vec_rows.shape
    return jnp.broadcast_to(vec_rows[:, None, :], (rows, reps, width)).reshape(rows * reps, 1, width)


def kernel(x_prompt, x_sample, cache_a_k, cache_a_v, cache_b_k, cache_b_v, c_prompt, c_sample, w_ada, b_ada,
           g_pre_mix, g_post_mix, w_in, rel_bias_a, sinks_b, w_branch_a, w_branch_b, w_out, g_pre_ffn,
           g_post_ffn, w_router, b_router, we_gate, we_up, we_down, ws_gate, ws_up, ws_down):
    assert w_ada.shape[0] == 1, "single layer"
    nb, seq, d = x_prompt.shape
    ns, dec = x_sample.shape[:2]
    assert d == D_MODEL and dec == CHUNK and seq % TOK_TILE == 0 and (ns * dec) == TOK_TILE
    n_p, n_s = nb * seq, ns * dec
    n_all = n_p + n_s
    assert n_all % MOE_TILE == 0 and n_p % MOE_TILE == 0

    c_all = jnp.concatenate([c_prompt, c_sample], axis=0)
    pad = (-c_all.shape[0]) % 8
    c_all = jnp.pad(c_all, ((0, pad), (0, 0)))
    mod = _modulation(c_all, w_ada[0], b_ada[0])
    mod_p, mod_s = mod[:nb], mod[nb:nb + ns]
    cpp = seq // CHUNK

    def part(k):
        return mod_p[:, k * d:(k + 1) * d], mod_s[:, k * d:(k + 1) * d]

    (sh1p, sh1s), (sc1p, sc1s), (g1p, g1s), (sh2p, sh2s), (sc2p, sc2s), (g2p, g2s) = [part(k) for k in range(6)]

    w_in_bf = w_in[0].astype(BF16)
    g_pre = g_pre_mix[0].reshape(1, d)
    cos_p, sin_p = _rope_tables(jnp.arange(seq))
    cos_s, sin_s = _rope_tables(PAST_LEN + jnp.arange(dec))
    cos_s, sin_s = jnp.tile(cos_s, (ns, 1)), jnp.tile(sin_s, (ns, 1))

    xp2 = x_prompt.reshape(n_p, d)
    xs2 = x_sample.reshape(n_s, d)
    outs_p = _inproj(xp2, _groups(sh1p, cpp), _groups(sc1p, cpp), g_pre, w_in_bf, cos_p, sin_p, nb, True)
    outs_s = _inproj(xs2, _groups(sh1s, 1), _groups(sc1s, 1), g_pre, w_in_bf, cos_s, sin_s, ns, False)

    table = rel_bias_a[0].astype(F32)
    n_far = A_BAND - 1 - REL_CLIP
    ext = jnp.concatenate([jnp.broadcast_to(table[:, 2 * REL_CLIP:], (A_HEADS, n_far)),
                           jnp.flip(table[:, REL_CLIP - (CHUNK - 1):], axis=1)], axis=1)
    bias = jnp.stack([ext[:, CHUNK - 1 - q:CHUNK - 1 - q + A_BAND] for q in range(CHUNK)], axis=1)
    bias_pairs = bias.reshape(A_HEADS // 2, 2 * CHUNK, A_BAND) * LOG2_E
    sink_rows = jnp.broadcast_to((sinks_b[0].astype(F32) * LOG2_E).reshape(B_KV_HEADS, B_GROUP, 1),
                                 (B_KV_HEADS, B_GROUP, CHUNK)).reshape(B_KV_HEADS, B_GROUP * CHUNK, 1)
    consts = (bias_pairs, sink_rows, w_branch_a[0].astype(BF16), w_branch_b[0].astype(BF16),
              w_out[0].astype(BF16), g_post_mix[0].reshape(1, d))

    x1_p = _attn_prompt(xp2, outs_p[:7], consts, _groups(g1p, cpp), nb)
    caches = (cache_a_k[0].reshape(ns, A_ROWS, WA), cache_a_v[0].reshape(ns, A_ROWS, WA),
              cache_b_k[0].reshape(ns, WINDOW, WKB), cache_b_v[0].reshape(ns, WINDOW, WKB))
    x1_s = _attn_sample(xs2, outs_s[:7], caches, consts, _groups(g1s, 1))

    h_packed, idx, gates, rank, counts = _router(
        x1_p, x1_s, _groups(sh2p, cpp), _groups(sh2s, 1), _groups(sc2p, cpp), _groups(sc2s, 1),
        g_pre_ffn[0].reshape(1, d),
        w_router[0].T.astype(BF16), b_router[0].astype(F32).reshape(N_EXPERTS, 1))
    n_blocks = (n_all * TOP_K) // EXPERT_ROWS + N_EXPERTS
    cnt = counts[:, 0].astype(I32)
    blocks_e = (cnt + EXPERT_ROWS - 1) // EXPERT_ROWS
    blk_end = jnp.cumsum(blocks_e)
    blk_start = blk_end - blocks_e
    n_active = blk_end[-1:]
    start_col = (blk_start * EXPERT_ROWS).astype(F32).reshape(N_EXPERTS, 1)

    dest_flat = _dest_rows(idx, rank, start_col).reshape(-1)
    xs = _dispatch(dest_flat, h_packed, n_blocks * EXPERT_ROWS)
    ys = _experts(blk_start.astype(I32), blk_end.astype(I32), cnt, n_active.astype(I32), xs,
                  we_gate[0], we_up[0], we_down[0])
    y_p, y_s = _combine(dest_flat, ys, x1_p, x1_s, h_packed, gates, ws_gate[0].astype(BF16),
                        ws_up[0].astype(BF16), ws_down[0].astype(BF16), g_post_ffn[0].reshape(1, d),
                        _groups(g2p, cpp), _groups(g2s, 1))

    a_heads = (A_HEADS, HEAD_DIM)
    b_heads = (B_KV_HEADS, HEAD_DIM)
    return (y_p.reshape(nb, seq, d), y_s.reshape(ns, dec, d),
            outs_p[7].reshape(1, nb, A_ROWS, *a_heads), outs_p[8].reshape(1, nb, A_ROWS, *a_heads),
            outs_p[9].reshape(1, nb, WINDOW, *b_heads), outs_p[10].reshape(1, nb, WINDOW, *b_heads),
            outs_s[7].reshape(1, ns, dec, *a_heads), outs_s[8].reshape(1, ns, dec, *a_heads),
            outs_s[9].reshape(1, ns, dec, *b_heads), outs_s[10].reshape(1, ns, dec, *b_heads))
```

```python
import functools

import jax
import jax.numpy as jnp
from jax import lax
from jax.experimental import pallas as pl
from jax.experimental.pallas import tpu as pltpu

F32 = jnp.float32
BF16 = jnp.bfloat16
I32 = jnp.int32
U32 = jnp.uint32
HIGHEST = lax.Precision.HIGHEST
LOG2_E = 1.4426950408889634

D_MODEL = 1024
CHUNK = 64
HEAD_DIM = 64
A_HEADS = 8
A_PREV_CHUNKS = 8
A_ROWS = A_PREV_CHUNKS * CHUNK
A_BAND = A_ROWS + CHUNK
REL_CLIP = 128
B_HEADS = 8
B_KV_HEADS = 2
B_GROUP = B_HEADS // B_KV_HEADS
WINDOW = 128
B_BAND = WINDOW + CHUNK
ROPE_THETA = 10000.0
N_EXPERTS = 256
TOP_K = 8
N_GROUPS = 8
PER_GROUP = N_EXPERTS // N_GROUPS
TOPK_GROUPS = 4
EXPERT_DIM = 256
ROUTED_SCALE = 2.5
EPS = 1e-6
PAST_LEN = 4096

WA = A_HEADS * HEAD_DIM
WB = B_HEADS * HEAD_DIM
WKB = B_KV_HEADS * HEAD_DIM
OFF_QA, OFF_KA, OFF_VA = 0, WA, 2 * WA
OFF_QB = 3 * WA
OFF_KB = OFF_QB + WB
OFF_VB = OFF_KB + WKB
OFF_G = OFF_VB + WKB
N_IN = OFF_G + 2 * D_MODEL

LANES = 128
TOK_TILE = 512
MOE_TILE = 256
EXPERT_ROWS = 256
HALF_D = D_MODEL // 2
VMEM_LIMIT = 56 * 1024 * 1024


def _cparams(sem, vmem=VMEM_LIMIT):
    return pltpu.CompilerParams(dimension_semantics=sem, vmem_limit_bytes=vmem)


def _rms(x):
    return x * lax.rsqrt(jnp.mean(x * x, axis=-1, keepdims=True) + EPS)


def _group_affine(y, mul, add):
    g = mul.shape[0]
    y3 = y.reshape(g, CHUNK, y.shape[-1]) * mul
    if add is not None:
        y3 = y3 + add
    return y3.reshape(g * CHUNK, y.shape[-1])


def _pack_halves(a, b):
    ua = lax.bitcast_convert_type(a.astype(BF16).astype(F32), U32)
    ub = lax.bitcast_convert_type(b.astype(BF16).astype(F32), U32)
    return (ua & jnp.uint32(0xFFFF0000)) | (ub >> 16)


def _unpack_halves(u):
    a = lax.bitcast_convert_type(u & jnp.uint32(0xFFFF0000), F32)
    b = lax.bitcast_convert_type(u << 16, F32)
    return a, b


ROW_SUBLANES = HALF_D // LANES


def _store_packed_rows(ref, x):
    rows = x.shape[0]
    p = _pack_halves(x[:, :HALF_D], x[:, HALF_D:])
    for s in range(ROW_SUBLANES):
        ref[pl.ds(s, rows, stride=ROW_SUBLANES), :] = p[:, s * LANES:(s + 1) * LANES]


def _load_packed_rows(ref, rows, n_valid=None):
    his, los = [], []
    for s in range(ROW_SUBLANES):
        u = ref[pl.ds(s, rows, stride=ROW_SUBLANES), :]
        if n_valid is not None:
            u = jnp.where(lax.broadcasted_iota(I32, u.shape, 0) < n_valid, u, jnp.uint32(0))
        a, b = _unpack_halves(u)
        his.append(a)
        los.append(b)
    return jnp.concatenate(his, axis=1), jnp.concatenate(los, axis=1)


def _mod_kernel(c_ref, w_ref, b_ref, o_ref):
    c = c_ref[...]
    s = c * jax.nn.sigmoid(c)
    o_ref[...] = jnp.dot(s, w_ref[...], precision=HIGHEST, preferred_element_type=F32) + b_ref[...]


def _modulation(c_all, w_ada, b_ada):
    rows = c_all.shape[0]
    n = w_ada.shape[1]
    tn = 512
    return pl.pallas_call(
        _mod_kernel,
        grid=(n // tn,),
        in_specs=[pl.BlockSpec((rows, D_MODEL), lambda j: (0, 0)),
                  pl.BlockSpec((D_MODEL, tn), lambda j: (0, j)),
                  pl.BlockSpec((1, tn), lambda j: (0, j))],
        out_specs=pl.BlockSpec((rows, tn), lambda j: (0, j)),
        out_shape=jax.ShapeDtypeStruct((rows, n), F32),
        compiler_params=_cparams(("arbitrary",)),
        name="modulation",
    )(c_all, w_ada, b_ada.reshape(1, n))


def _rope(x, cos, sin_signed):
    n = x.shape[-1]
    reps = n // LANES
    if reps > 1:
        cos = jnp.concatenate([cos] * reps, axis=1)
        sin_signed = jnp.concatenate([sin_signed] * reps, axis=1)
    lane = lax.broadcasted_iota(I32, x.shape, 1)
    first_half = (lane % HEAD_DIM) < (HEAD_DIM // 2)
    partner = jnp.where(first_half, pltpu.roll(x, n - HEAD_DIM // 2, 1), pltpu.roll(x, HEAD_DIM // 2, 1))
    return x * cos + partner * sin_signed


def _inproj_kernel(prompt_state, tiles_per_seq,
                   x_ref, sh_ref, sc_ref, g_ref, w_ref, cos_ref, sin_ref,
                   qa_ref, ka_ref, va_ref, qb_ref, kb_ref, vb_ref, gt_ref,
                   ska_ref, sva_ref, skb_ref, svb_ref):
    x = x_ref[...]
    h = _group_affine(_rms(x) * g_ref[...], 1.0 + sc_ref[...], sh_ref[...]).astype(BF16)

    def proj(off, width):
        return jnp.dot(h, w_ref[:, off:off + width], preferred_element_type=F32)

    cos = cos_ref[...]
    sin = sin_ref[...]
    scale = HEAD_DIM ** -0.5 * LOG2_E
    qa_ref[...] = (proj(OFF_QA, WA) * scale).astype(BF16)
    ka = proj(OFF_KA, WA)
    va = proj(OFF_VA, WA)
    ka_ref[...] = ka.astype(BF16)
    va_ref[...] = va.astype(BF16)
    qb_ref[...] = (_rope(proj(OFF_QB, WB), cos, sin) * scale).astype(BF16)
    kb = _rope(proj(OFF_KB, WKB), cos, sin)
    vb = proj(OFF_VB, WKB)
    kb_ref[...] = kb.astype(BF16)
    vb_ref[...] = vb.astype(BF16)
    gt_ref[...] = jax.nn.sigmoid(proj(OFF_G, 2 * D_MODEL)).astype(BF16)

    if prompt_state:
        @pl.when(pl.program_id(0) % tiles_per_seq == tiles_per_seq - 1)
        def _():
            ska_ref[...] = ka
            sva_ref[...] = va
            skb_ref[...] = kb[TOK_TILE - WINDOW:, :]
            svb_ref[...] = vb[TOK_TILE - WINDOW:, :]
    else:
        ska_ref[...] = ka
        sva_ref[...] = va
        skb_ref[...] = kb
        svb_ref[...] = vb


def _inproj(x2d, shift_g, scale_g, g_pre, w_in_bf, cos_tab, sin_tab, n_seq, prompt_state):
    n = x2d.shape[0]
    nt = n // TOK_TILE
    tiles_per_seq = nt // n_seq if prompt_state else 1
    tab_tiles = cos_tab.shape[0] // TOK_TILE
    gpt = TOK_TILE // CHUNK
    row = lambda i: (i, 0)
    grp = lambda i: (i, 0, 0)
    if prompt_state:
        st_shapes = [jax.ShapeDtypeStruct((n_seq, A_ROWS, WA), F32)] * 2 + \
                    [jax.ShapeDtypeStruct((n_seq, WINDOW, WKB), F32)] * 2
        st_specs = [pl.BlockSpec((None, A_ROWS, WA), lambda i: (i // tiles_per_seq, 0, 0))] * 2 + \
                   [pl.BlockSpec((None, WINDOW, WKB), lambda i: (i // tiles_per_seq, 0, 0))] * 2
    else:
        st_shapes = [jax.ShapeDtypeStruct((n, WA), F32)] * 2 + [jax.ShapeDtypeStruct((n, WKB), F32)] * 2
        st_specs = [pl.BlockSpec((TOK_TILE, WA), row)] * 2 + [pl.BlockSpec((TOK_TILE, WKB), row)] * 2
    out_shapes = [jax.ShapeDtypeStruct((n, WA), BF16)] * 4 + [jax.ShapeDtypeStruct((n, WKB), BF16)] * 2 + \
                 [jax.ShapeDtypeStruct((n, 2 * D_MODEL), BF16)] + st_shapes
    out_specs = [pl.BlockSpec((TOK_TILE, WA), row)] * 4 + [pl.BlockSpec((TOK_TILE, WKB), row)] * 2 + \
                [pl.BlockSpec((TOK_TILE, 2 * D_MODEL), row)] + st_specs
    return pl.pallas_call(
        functools.partial(_inproj_kernel, prompt_state, tiles_per_seq),
        grid=(nt,),
        in_specs=[pl.BlockSpec((TOK_TILE, D_MODEL), row),
                  pl.BlockSpec((gpt, 1, D_MODEL), grp),
                  pl.BlockSpec((gpt, 1, D_MODEL), grp),
                  pl.BlockSpec((1, D_MODEL), lambda i: (0, 0)),
                  pl.BlockSpec((D_MODEL, N_IN), lambda i: (0, 0)),
                  pl.BlockSpec((TOK_TILE, LANES), lambda i: (i % tab_tiles, 0)),
                  pl.BlockSpec((TOK_TILE, LANES), lambda i: (i % tab_tiles, 0))],
        out_specs=out_specs,
        out_shape=out_shapes,
        compiler_params=_cparams(("arbitrary",)),
        name="inproj_prompt" if prompt_state else "inproj_sample",
    )(x2d, shift_g, scale_g, g_pre, w_in_bf, cos_tab, sin_tab)


def _attn_kernel(n_chunks, mask_first,
                 x_ref, qa_ref, qb_ref, gt_ref,
                 kap_ref, kac_ref, vap_ref, vac_ref, kbp_ref, kbc_ref, vbp_ref, vbc_ref,
                 bias_ref, sink_ref, wba_ref, wbb_ref, wout_ref, gpost_ref, gate1_ref,
                 o_ref,
                 ka_s, va_s, kb_s, vb_s, oa_s, ob_s):
    rows = n_chunks * CHUNK
    pb = kbp_ref.shape[0]
    ka_s[0:A_ROWS, :] = kap_ref[...].astype(BF16)
    va_s[0:A_ROWS, :] = vap_ref[...].astype(BF16)
    ka_s[A_ROWS:A_ROWS + rows, :] = kac_ref[...]
    va_s[A_ROWS:A_ROWS + rows, :] = vac_ref[...]
    kb_s[0:WINDOW, :] = kbp_ref[pb - WINDOW:pb, :].astype(BF16)
    vb_s[0:WINDOW, :] = vbp_ref[pb - WINDOW:pb, :].astype(BF16)
    kb_s[WINDOW:WINDOW + rows, :] = kbc_ref[...]
    vb_s[WINDOW:WINDOW + rows, :] = vbc_ref[...]

    lane_q = lax.broadcasted_iota(I32, (CHUNK, LANES), 1)
    nt_dims = (((1,), (1,)), ((), ()))

    def chunk_body(masked, c, carry):
        c0 = pl.multiple_of(c * CHUNK, CHUNK)
        if masked:
            valid_a = c0 + lax.broadcasted_iota(I32, (1, A_BAND), 1) >= A_ROWS
            valid_b = c0 + lax.broadcasted_iota(I32, (1, B_BAND), 1) >= WINDOW

        scores = []
        for p in range(A_HEADS // 2):
            cols = slice(p * LANES, (p + 1) * LANES)
            q = qa_ref[pl.ds(c0, CHUNK), cols].astype(F32)
            qs = jnp.concatenate([jnp.where(lane_q < HEAD_DIM, q, 0.0),
                                  jnp.where(lane_q >= HEAD_DIM, q, 0.0)], axis=0).astype(BF16)
            k = ka_s[pl.ds(c0, A_BAND), cols]
            s = lax.dot_general(qs, k, nt_dims, preferred_element_type=F32) + bias_ref[p]
            if masked:
                s = jnp.where(valid_a, s, -jnp.inf)
            scores.append(s)
        for g in range(B_KV_HEADS):
            parts = []
            for r in range(B_GROUP):
                head = g * B_GROUP + r
                t, half = head // 2, head % 2
                q = qb_ref[pl.ds(c0, CHUNK), t * LANES:(t + 1) * LANES].astype(F32)
                if half != g:
                    q = pltpu.roll(q, HEAD_DIM, 1)
                in_g = (lane_q >= HEAD_DIM) if g else (lane_q < HEAD_DIM)
                parts.append(jnp.where(in_g, q, 0.0))
            qs = jnp.concatenate(parts, axis=0).astype(BF16)
            k = kb_s[pl.ds(c0, B_BAND), :]
            s = lax.dot_general(qs, k, nt_dims, preferred_element_type=F32)
            if masked:
                s = jnp.where(valid_b, s, -jnp.inf)
            scores.append(s)

        numer, denom = [], []
        for n, s in enumerate(scores):
            m = jnp.max(s, axis=1, keepdims=True)
            if n >= A_HEADS // 2:
                sk = sink_ref[n - A_HEADS // 2]
                m = jnp.maximum(m, sk)
            e = jnp.exp2(s - m)
            l = jnp.sum(e, axis=1, keepdims=True)
            if n >= A_HEADS // 2:
                l = l + jnp.exp2(sk - m)
            numer.append(e.astype(BF16))
            denom.append(l)

        outs = []
        for n, e in enumerate(numer):
            if n < A_HEADS // 2:
                v = va_s[pl.ds(c0, A_BAND), n * LANES:(n + 1) * LANES]
            else:
                v = vb_s[pl.ds(c0, B_BAND), :]
            outs.append(jnp.dot(e, v, preferred_element_type=F32) / denom[n])

        for p in range(A_HEADS // 2):
            o = outs[p]
            oa_s[pl.ds(c0, CHUNK), p * LANES:(p + 1) * LANES] = jnp.where(
                lane_q < HEAD_DIM, o[:CHUNK], o[CHUNK:]).astype(BF16)
        for g in range(B_KV_HEADS):
            o = outs[A_HEADS // 2 + g]
            for s2 in range(B_GROUP // 2):
                o_even = o[(2 * s2) * CHUNK:(2 * s2 + 1) * CHUNK]
                o_odd = o[(2 * s2 + 1) * CHUNK:(2 * s2 + 2) * CHUNK]
                if g == 0:
                    tile = jnp.where(lane_q < HEAD_DIM, o_even, pltpu.roll(o_odd, HEAD_DIM, 1))
                else:
                    tile = jnp.where(lane_q < HEAD_DIM, pltpu.roll(o_even, HEAD_DIM, 1), o_odd)
                t = g * (B_GROUP // 2) + s2
                ob_s[pl.ds(c0, CHUNK), t * LANES:(t + 1) * LANES] = tile.astype(BF16)
        return carry

    def all_chunks(masked):
        lax.fori_loop(0, n_chunks, functools.partial(chunk_body, masked), 0, unroll=4 if n_chunks > 1 else 1)

    if mask_first:
        pl.when(pl.program_id(1) == 0)(lambda: all_chunks(True))
        pl.when(pl.program_id(1) > 0)(lambda: all_chunks(False))
    else:
        all_chunks(False)

    za = jnp.dot(oa_s[...], wba_ref[...], preferred_element_type=F32)
    zb = jnp.dot(ob_s[...], wbb_ref[...], preferred_element_type=F32)
    merged = gt_ref[:, :D_MODEL].astype(F32) * za + gt_ref[:, D_MODEL:].astype(F32) * zb
    mo = jnp.dot(merged.astype(BF16), wout_ref[...], preferred_element_type=F32)
    o_ref[...] = x_ref[...] + _group_affine(_rms(mo) * gpost_ref[...], gate1_ref[...], None)


def _attn_scratch(rows):
    return [pltpu.VMEM((A_ROWS + rows, WA), BF16), pltpu.VMEM((A_ROWS + rows, WA), BF16),
            pltpu.VMEM((WINDOW + rows, WKB), BF16), pltpu.VMEM((WINDOW + rows, WKB), BF16),
            pltpu.VMEM((rows, WA), BF16), pltpu.VMEM((rows, WB), BF16)]


def _const_specs(grid_rank):
    z2 = (lambda b, j: (0, 0)) if grid_rank == 2 else (lambda b: (0, 0))
    z3 = (lambda b, j: (0, 0, 0)) if grid_rank == 2 else (lambda b: (0, 0, 0))
    return [pl.BlockSpec((A_HEADS // 2, 2 * CHUNK, A_BAND), z3),
            pl.BlockSpec((B_KV_HEADS, B_GROUP * CHUNK, 1), z3),
            pl.BlockSpec((WA, D_MODEL), z2),
            pl.BlockSpec((WB, D_MODEL), z2),
            pl.BlockSpec((D_MODEL, D_MODEL), z2),
            pl.BlockSpec((1, D_MODEL), z2)]


def _attn_prompt(x2d, proj, consts, gate1_g, n_seq):
    qa, ka, va, qb, kb, vb, gt = proj
    n = x2d.shape[0]
    tps = n // n_seq // TOK_TILE
    gpt = TOK_TILE // CHUNK
    cur = lambda b, j: (b * tps + j, 0)
    prev = lambda b, j: (b * tps + jnp.maximum(j - 1, 0), 0)
    return pl.pallas_call(
        functools.partial(_attn_kernel, TOK_TILE // CHUNK, True),
        grid=(n_seq, tps),
        in_specs=[pl.BlockSpec((TOK_TILE, D_MODEL), cur),
                  pl.BlockSpec((TOK_TILE, WA), cur),
                  pl.BlockSpec((TOK_TILE, WB), cur),
                  pl.BlockSpec((TOK_TILE, 2 * D_MODEL), cur),
                  pl.BlockSpec((TOK_TILE, WA), prev), pl.BlockSpec((TOK_TILE, WA), cur),
                  pl.BlockSpec((TOK_TILE, WA), prev), pl.BlockSpec((TOK_TILE, WA), cur),
                  pl.BlockSpec((TOK_TILE, WKB), prev), pl.BlockSpec((TOK_TILE, WKB), cur),
                  pl.BlockSpec((TOK_TILE, WKB), prev), pl.BlockSpec((TOK_TILE, WKB), cur)]
                 + _const_specs(2)
                 + [pl.BlockSpec((gpt, 1, D_MODEL), lambda b, j: (b * tps + j, 0, 0))],
        out_specs=pl.BlockSpec((TOK_TILE, D_MODEL), cur),
        out_shape=jax.ShapeDtypeStruct((n, D_MODEL), F32),
        scratch_shapes=_attn_scratch(TOK_TILE),
        compiler_params=_cparams(("arbitrary", "arbitrary")),
        name="attn_prompt",
    )(x2d, qa, qb, gt, ka, ka, va, va, kb, kb, vb, vb, *consts, gate1_g)


def _attn_sample(x2d, proj, caches, consts, gate1_g):
    qa, ka, va, qb, kb, vb, gt = proj
    cak, cav, cbk, cbv = caches
    n_seq = cak.shape[0]
    cur = lambda b: (b, 0)
    cache = lambda b: (b, 0, 0)
    return pl.pallas_call(
        functools.partial(_attn_kernel, 1, False),
        grid=(n_seq,),
        in_specs=[pl.BlockSpec((CHUNK, D_MODEL), cur),
                  pl.BlockSpec((CHUNK, WA), cur),
                  pl.BlockSpec((CHUNK, WB), cur),
                  pl.BlockSpec((CHUNK, 2 * D_MODEL), cur),
                  pl.BlockSpec((None, A_ROWS, WA), cache), pl.BlockSpec((CHUNK, WA), cur),
                  pl.BlockSpec((None, A_ROWS, WA), cache), pl.BlockSpec((CHUNK, WA), cur),
                  pl.BlockSpec((None, WINDOW, WKB), cache), pl.BlockSpec((CHUNK, WKB), cur),
                  pl.BlockSpec((None, WINDOW, WKB), cache), pl.BlockSpec((CHUNK, WKB), cur)]
                 + _const_specs(1)
                 + [pl.BlockSpec((1, 1, D_MODEL), lambda b: (b, 0, 0))],
        out_specs=pl.BlockSpec((CHUNK, D_MODEL), cur),
        out_shape=jax.ShapeDtypeStruct(x2d.shape, F32),
        scratch_shapes=_attn_scratch(CHUNK),
        compiler_params=_cparams(("arbitrary",)),
        name="attn_sample",
    )(x2d, qa, qb, gt, cak, ka, cav, va, cbk, kb, cbv, vb, *consts, gate1_g)


def _two_part_tile(n_first_tiles, first_ref, second_ref):
    return jnp.where(pl.program_id(0) < n_first_tiles, first_ref[...], second_ref[...])


def _router_kernel(n_prompt_tiles, xp_ref, xs_ref, shp_ref, shs_ref, scp_ref, scs_ref, g_ref, wrt_ref, br_ref,
                   h_ref, idx_ref, gate_ref, rank_ref, cnt_ref, carry):
    i = pl.program_id(0)

    @pl.when(i == 0)
    def _():
        carry[...] = jnp.zeros_like(carry)

    tm = xp_ref.shape[0]
    x = _two_part_tile(n_prompt_tiles, xp_ref, xs_ref)
    h = _group_affine(_rms(x) * g_ref[...], 1.0 + _two_part_tile(n_prompt_tiles, scp_ref, scs_ref),
                      _two_part_tile(n_prompt_tiles, shp_ref, shs_ref))
    _store_packed_rows(h_ref, h)

    logits = lax.dot_general(wrt_ref[...], h.astype(BF16), (((1,), (1,)), ((), ())),
                             preferred_element_type=F32)
    scores = jax.nn.sigmoid(logits)
    biased = scores + br_ref[...]
    neg = -jnp.inf

    sub = lax.broadcasted_iota(I32, (PER_GROUP, tm), 0).astype(F32)
    gs_rows = []
    for g in range(N_GROUPS):
        xg = biased[g * PER_GROUP:(g + 1) * PER_GROUP]
        m1 = jnp.max(xg, axis=0, keepdims=True)
        i1 = jnp.min(jnp.where(xg == m1, sub, float(PER_GROUP)), axis=0, keepdims=True)
        m2 = jnp.max(jnp.where(sub == i1, neg, xg), axis=0, keepdims=True)
        gs_rows.append(m1 + m2)
    gs = jnp.concatenate(gs_rows, axis=0)

    giota = lax.broadcasted_iota(I32, (N_GROUPS, tm), 0).astype(F32)
    keep = jnp.zeros((N_GROUPS, tm), F32)
    for _ in range(TOPK_GROUPS):
        m = jnp.max(gs, axis=0, keepdims=True)
        gi = jnp.min(jnp.where(gs == m, giota, float(N_GROUPS)), axis=0, keepdims=True)
        hit = giota == gi
        keep = jnp.where(hit, 1.0, keep)
        gs = jnp.where(hit, neg, gs)
    cand = jnp.concatenate(
        [jnp.where(keep[g:g + 1] > 0.0, biased[g * PER_GROUP:(g + 1) * PER_GROUP], neg)
         for g in range(N_GROUPS)], axis=0)

    eiota = lax.broadcasted_iota(I32, (N_EXPERTS, tm), 0).astype(F32)
    idx_rows, gate_rows = [], []
    chosen = jnp.zeros((N_EXPERTS, tm), F32)
    for _ in range(TOP_K):
        m = jnp.max(cand, axis=0, keepdims=True)
        ei = jnp.min(jnp.where(cand == m, eiota, float(N_EXPERTS)), axis=0, keepdims=True)
        sel = eiota == ei
        gate_rows.append(jnp.sum(jnp.where(sel, scores, 0.0), axis=0, keepdims=True))
        idx_rows.append(ei)
        chosen = jnp.where(sel, 1.0, chosen)
        cand = jnp.where(sel, neg, cand)
    gates = jnp.concatenate(gate_rows, axis=0)
    gates = gates / jnp.sum(gates, axis=0, keepdims=True) * ROUTED_SCALE
    gate_ref[...] = gates
    idx_ref[...] = jnp.concatenate(idx_rows, axis=0).astype(I32)

    r_i = lax.broadcasted_iota(I32, (tm, tm), 0)
    c_i = lax.broadcasted_iota(I32, (tm, tm), 1)
    upper = jnp.where(r_i < c_i, 1.0, 0.0).astype(BF16)
    before = jnp.dot(chosen.astype(BF16), upper, preferred_element_type=F32) + carry[...]
    rank_rows = [jnp.sum(jnp.where(eiota == idx_rows[k], before, 0.0), axis=0, keepdims=True)
                 for k in range(TOP_K)]
    rank_ref[...] = jnp.concatenate(rank_rows, axis=0).astype(I32)
    total = carry[...] + jnp.sum(chosen, axis=1, keepdims=True)
    carry[...] = total
    cnt_ref[...] = total


def _router(x1_p, x1_s, shift_gp, shift_gs, scale_gp, scale_gs, g_pre, w_router_t, b_router_col):
    n = x1_p.shape[0] + x1_s.shape[0]
    nt = n // MOE_TILE
    npt = x1_p.shape[0] // MOE_TILE
    gpt = MOE_TILE // CHUNK
    lane_blk = lambda i: (0, i)
    return pl.pallas_call(
        functools.partial(_router_kernel, npt),
        grid=(nt,),
        in_specs=[pl.BlockSpec((MOE_TILE, D_MODEL), lambda i: (jnp.minimum(i, npt - 1), 0)),
                  pl.BlockSpec((MOE_TILE, D_MODEL), lambda i: (jnp.maximum(i - npt, 0), 0)),
                  pl.BlockSpec((gpt, 1, D_MODEL), lambda i: (jnp.minimum(i, npt - 1), 0, 0)),
                  pl.BlockSpec((gpt, 1, D_MODEL), lambda i: (jnp.maximum(i - npt, 0), 0, 0)),
                  pl.BlockSpec((gpt, 1, D_MODEL), lambda i: (jnp.minimum(i, npt - 1), 0, 0)),
                  pl.BlockSpec((gpt, 1, D_MODEL), lambda i: (jnp.maximum(i - npt, 0), 0, 0)),
                  pl.BlockSpec((1, D_MODEL), lambda i: (0, 0)),
                  pl.BlockSpec((N_EXPERTS, D_MODEL), lambda i: (0, 0)),
                  pl.BlockSpec((N_EXPERTS, 1), lambda i: (0, 0))],
        out_specs=[pl.BlockSpec((MOE_TILE * ROW_SUBLANES, LANES), lambda i: (i, 0)),
                   pl.BlockSpec((TOP_K, MOE_TILE), lane_blk),
                   pl.BlockSpec((TOP_K, MOE_TILE), lane_blk),
                   pl.BlockSpec((TOP_K, MOE_TILE), lane_blk),
                   pl.BlockSpec((N_EXPERTS, 1), lambda i: (0, 0))],
        out_shape=[jax.ShapeDtypeStruct((n * ROW_SUBLANES, LANES), U32),
                   jax.ShapeDtypeStruct((TOP_K, n), I32),
                   jax.ShapeDtypeStruct((TOP_K, n), F32),
                   jax.ShapeDtypeStruct((TOP_K, n), I32),
                   jax.ShapeDtypeStruct((N_EXPERTS, 1), F32)],
        scratch_shapes=[pltpu.VMEM((N_EXPERTS, 1), F32)],
        compiler_params=_cparams(("arbitrary",)),
        name="router",
    )(x1_p, x1_s, shift_gp, shift_gs, scale_gp, scale_gs, g_pre, w_router_t, b_router_col)


def _dest_kernel(idx_ref, rank_ref, start_ref, o_ref):
    eiota = lax.broadcasted_iota(I32, (N_EXPERTS, MOE_TILE), 0)
    start = start_ref[...]
    for sub in range(o_ref.shape[0]):
        cols = slice(sub * MOE_TILE, (sub + 1) * MOE_TILE)
        rows = [jnp.sum(jnp.where(eiota == idx_ref[k:k + 1, cols], start, 0.0), axis=0, keepdims=True)
                for k in range(TOP_K)]
        o_ref[sub] = jnp.concatenate(rows, axis=0).astype(I32) + rank_ref[:, cols]


def _dest_rows(idx, rank, start_col):
    n = idx.shape[1]
    nt = n // MOE_TILE
    per_step = next(c for c in (10, 8, 5, 4, 2, 1) if nt % c == 0)
    return pl.pallas_call(
        _dest_kernel,
        grid=(nt // per_step,),
        in_specs=[pl.BlockSpec((TOP_K, per_step * MOE_TILE), lambda i: (0, i)),
                  pl.BlockSpec((TOP_K, per_step * MOE_TILE), lambda i: (0, i)),
                  pl.BlockSpec((N_EXPERTS, 1), lambda i: (0, 0))],
        out_specs=pl.BlockSpec((per_step, TOP_K, MOE_TILE), lambda i: (i, 0, 0)),
        out_shape=jax.ShapeDtypeStruct((nt, TOP_K, MOE_TILE), I32),
        compiler_params=_cparams(("arbitrary",)),
        name="dest_rows",
    )(idx, rank, start_col)


DEST_PER_TILE = TOP_K * MOE_TILE


def _row_slice(ref, row):
    return ref.at[pl.ds(pl.multiple_of(row * ROW_SUBLANES, ROW_SUBLANES), ROW_SUBLANES)]


TILE_SUBLANES = MOE_TILE * ROW_SUBLANES


def _dispatch_kernel(dest_hbm, h_hbm, xs_hbm, dest_s, hbuf, dsem, hsem, rsem):
    i = pl.program_id(0)
    nt = pl.num_programs(0)

    def load(tile):
        dst = dest_s.at[pl.ds((tile % 2) * DEST_PER_TILE, DEST_PER_TILE)]
        rows = h_hbm.at[pl.ds(pl.multiple_of(tile * TILE_SUBLANES, TILE_SUBLANES), TILE_SUBLANES)]
        return (pltpu.make_async_copy(dest_hbm.at[pl.ds(tile * DEST_PER_TILE, DEST_PER_TILE)], dst, dsem.at[tile % 2]),
                pltpu.make_async_copy(rows, hbuf.at[tile % 3], hsem.at[tile % 3]))

    def drain(tile):
        for k in range(TOP_K):
            pltpu.make_async_copy(hbuf.at[tile % 3], xs_hbm.at[pl.ds(0, TILE_SUBLANES)],
                                  rsem.at[(tile % 2) * TOP_K + k]).wait()

    @pl.when(i == 0)
    def _():
        for cp in load(0):
            cp.start()

    @pl.when(i + 1 < nt)
    def _():
        for cp in load(i + 1):
            cp.start()

    for cp in load(i):
        cp.wait()
    base = (i % 2) * DEST_PER_TILE
    src_buf = hbuf.at[i % 3]
    sem0 = (i % 2) * TOP_K

    def body(t, carry):
        src = _row_slice(src_buf, t)
        for k in range(TOP_K):
            d = dest_s[base + k * MOE_TILE + t]
            pltpu.make_async_copy(src, _row_slice(xs_hbm, d), rsem.at[sem0 + k]).start(priority=k % 2)
        return carry

    lax.fori_loop(0, MOE_TILE, body, 0, unroll=4)

    @pl.when(i >= 1)
    def _():
        drain(i - 1)

    @pl.when(i == nt - 1)
    def _():
        drain(i)


def _dispatch(dest_flat, h_packed, n_rows):
    nt = h_packed.shape[0] // TILE_SUBLANES
    return pl.pallas_call(
        _dispatch_kernel,
        grid=(nt,),
        in_specs=[pl.BlockSpec(memory_space=pl.ANY),
                  pl.BlockSpec(memory_space=pl.ANY)],
        out_specs=pl.BlockSpec(memory_space=pl.ANY),
        out_shape=jax.ShapeDtypeStruct((n_rows * ROW_SUBLANES, LANES), U32),
        scratch_shapes=[pltpu.SMEM((2 * DEST_PER_TILE,), I32),
                        pltpu.VMEM((3, TILE_SUBLANES, LANES), U32),
                        pltpu.SemaphoreType.DMA((2,)),
                        pltpu.SemaphoreType.DMA((3,)),
                        pltpu.SemaphoreType.DMA((2 * TOP_K,))],
        compiler_params=_cparams(("arbitrary",)),
        name="dispatch",
    )(dest_flat, h_packed)


BLOCK_SUBLANES = EXPERT_ROWS * ROW_SUBLANES
EXPERT_AHEAD = 4
EXPERT_SLOTS = EXPERT_AHEAD + 2


def _expert_kernel(first_ref, last_ref, cnt_ref, na_ref, xs_hbm, wg_ref, wu_ref, wd_ref, ys_hbm,
                   wg_s, wu_s, wd_s, xbuf, ybuf, isem, osem):
    e = pl.program_id(0)
    n_active = na_ref[0]
    first, last = first_ref[e], last_ref[e]

    def block_rows(ref, g):
        return ref.at[pl.ds(pl.multiple_of(g * BLOCK_SUBLANES, BLOCK_SUBLANES), BLOCK_SUBLANES)]

    def fetch(g, slot):
        return pltpu.make_async_copy(block_rows(xs_hbm, g), xbuf.at[slot], isem.at[slot])

    def flush(g, slot):
        return pltpu.make_async_copy(ybuf.at[slot], block_rows(ys_hbm, g), osem.at[slot])

    @pl.when(e == 0)
    def _():
        for g0 in range(EXPERT_AHEAD):
            @pl.when(g0 < n_active)
            def _():
                fetch(g0, g0).start()

    @pl.when(last > first)
    def _():
        wg_s[...] = wg_ref[...].astype(BF16)
        wu_s[...] = wu_ref[...].astype(BF16)
        wd_s[...] = wd_ref[...].astype(BF16)

    def enter(g):
        slot = g % EXPERT_SLOTS
        fetch(g, slot).wait()

        @pl.when(g + EXPERT_AHEAD < n_active)
        def _():
            fetch(g + EXPERT_AHEAD, (g + EXPERT_AHEAD) % EXPERT_SLOTS).start()

        @pl.when(g >= EXPERT_SLOTS)
        def _():
            flush(g - EXPERT_SLOTS, slot).wait()
        return slot

    def swiglu(halves):
        xa = jnp.concatenate([h[0] for h in halves], axis=0).astype(BF16)
        xb = jnp.concatenate([h[1] for h in halves], axis=0).astype(BF16)

        def up(w_s):
            return (jnp.dot(xa, w_s[:HALF_D, :], preferred_element_type=F32)
                    + jnp.dot(xb, w_s[HALF_D:, :], preferred_element_type=F32))

        gate = up(wg_s)
        a = (gate * jax.nn.sigmoid(gate) * up(wu_s)).astype(BF16)
        return jnp.dot(a, wd_s[...], preferred_element_type=F32)

    def n_valid(g):
        return cnt_ref[e] - (g - first) * EXPERT_ROWS

    def pair_body(p, carry):
        g = first + 2 * p
        s0 = enter(g)
        s1 = enter(g + 1)
        y = swiglu([_load_packed_rows(xbuf.at[s0], EXPERT_ROWS),
                    _load_packed_rows(xbuf.at[s1], EXPERT_ROWS, n_valid(g + 1))])
        _store_packed_rows(ybuf.at[s0], y[:EXPERT_ROWS])
        _store_packed_rows(ybuf.at[s1], y[EXPERT_ROWS:])
        flush(g, s0).start()
        flush(g + 1, s1).start()
        return carry

    n_pairs = lax.shift_right_logical(last - first, 1)
    lax.fori_loop(0, n_pairs, pair_body, 0)

    @pl.when((last - first) % 2 == 1)
    def _():
        g = last - 1
        s0 = enter(g)
        y = swiglu([_load_packed_rows(xbuf.at[s0], EXPERT_ROWS, n_valid(g))])
        _store_packed_rows(ybuf.at[s0], y)
        flush(g, s0).start()

    @pl.when(e == pl.num_programs(0) - 1)
    def _():
        for back in range(EXPERT_SLOTS, 0, -1):
            @pl.when(n_active >= back)
            def _():
                flush(n_active - back, (n_active - back) % EXPERT_SLOTS).wait()


def _experts(blk_first, blk_last, counts, n_active, xs, we_gate, we_up, we_down):
    wmap = lambda e, *_: (e, 0, 0)
    return pl.pallas_call(
        _expert_kernel,
        grid_spec=pltpu.PrefetchScalarGridSpec(
            num_scalar_prefetch=4,
            grid=(N_EXPERTS,),
            in_specs=[pl.BlockSpec(memory_space=pl.ANY),
                      pl.BlockSpec((None, D_MODEL, EXPERT_DIM), wmap),
                      pl.BlockSpec((None, D_MODEL, EXPERT_DIM), wmap),
                      pl.BlockSpec((None, EXPERT_DIM, D_MODEL), wmap)],
            out_specs=pl.BlockSpec(memory_space=pl.ANY),
            scratch_shapes=[pltpu.VMEM((D_MODEL, EXPERT_DIM), BF16),
                            pltpu.VMEM((D_MODEL, EXPERT_DIM), BF16),
                            pltpu.VMEM((EXPERT_DIM, D_MODEL), BF16),
                            pltpu.VMEM((EXPERT_SLOTS, BLOCK_SUBLANES, LANES), U32),
                            pltpu.VMEM((EXPERT_SLOTS, BLOCK_SUBLANES, LANES), U32),
                            pltpu.SemaphoreType.DMA((EXPERT_SLOTS,)),
                            pltpu.SemaphoreType.DMA((EXPERT_SLOTS,))]),
        out_shape=jax.ShapeDtypeStruct(xs.shape, U32),
        compiler_params=_cparams(("arbitrary",)),
        name="experts",
    )(blk_first, blk_last, counts, n_active, xs, we_gate, we_up, we_down)


COMBINE_ROWS = 8


def _combine_kernel(n_prompt_tiles,
                    dest_hbm, ys_hbm, xp_ref, xs_ref, h_ref, gate_ref, wsg_ref, wsu_ref, wsd_ref, gpost_ref,
                    gate2p_ref, gate2s_ref,
                    yp_ref, ysm_ref, dest_s, buf, shared_s, gcol_s, gate2_s, y_s, dsem, rsem):
    i = pl.program_id(0)
    nt = pl.num_programs(0)

    def dest_copy(tile):
        slot = tile % 3
        return pltpu.make_async_copy(dest_hbm.at[pl.ds(tile * DEST_PER_TILE, DEST_PER_TILE)],
                                     dest_s.at[pl.ds(slot * DEST_PER_TILE, DEST_PER_TILE)], dsem.at[slot])

    def gather_token(base, bset, t):
        dst0 = bset * TILE_SUBLANES + t * ROW_SUBLANES
        for k in range(TOP_K):
            d = dest_s[base + t + k * MOE_TILE]
            dst = buf.at[pl.ds(pl.multiple_of(dst0 + k * TILE_SUBLANES, ROW_SUBLANES), ROW_SUBLANES)]
            pltpu.make_async_copy(_row_slice(ys_hbm, d), dst, rsem.at[bset + k]).start(priority=k % 2)

    def wait_rows(bset):
        for k in range(TOP_K):
            pltpu.make_async_copy(ys_hbm.at[pl.ds(0, TILE_SUBLANES)], buf.at[pl.ds(0, TILE_SUBLANES)],
                                  rsem.at[bset + k]).wait()

    @pl.when(i == 0)
    def _():
        dest_copy(0).start()

        @pl.when(nt > 1)
        def _():
            dest_copy(1).start()

        dest_copy(0).wait()

        def first_tile(t, carry):
            gather_token(0, 0, t)
            return carry

        lax.fori_loop(0, MOE_TILE, first_tile, 0, unroll=4)

    @pl.when(i + 2 < nt)
    def _():
        dest_copy(i + 2).start()

    @pl.when(i + 1 < nt)
    def _():
        dest_copy(i + 1).wait()

    nxt = jnp.minimum(i + 1, nt - 1)
    base_next = (nxt % 3) * DEST_PER_TILE
    bset = (i % 2) * TOP_K
    bset_next = ((i + 1) % 2) * TOP_K

    ha, hb = _load_packed_rows(h_ref, MOE_TILE)
    ha = ha.astype(BF16)
    hb = hb.astype(BF16)

    def up(w_ref):
        return (jnp.dot(ha, w_ref[:HALF_D, :], preferred_element_type=F32)
                + jnp.dot(hb, w_ref[HALF_D:, :], preferred_element_type=F32))

    g = up(wsg_ref)
    a = (g * jax.nn.sigmoid(g) * up(wsu_ref)).astype(BF16)
    shared_s[...] = jnp.dot(a, wsd_ref[...], preferred_element_type=F32)

    gpad = jnp.concatenate([gate_ref[...], jnp.zeros((LANES - TOP_K, MOE_TILE), F32)], axis=0)
    gcol_s[...] = gpad.T

    wait_rows(bset)
    in_first = i < n_prompt_tiles
    gate2_s[...] = _two_part_tile(n_prompt_tiles, gate2p_ref, gate2s_ref)

    def chunk(j, carry):
        r0 = pl.multiple_of(j * COMBINE_ROWS, COMBINE_ROWS)
        rows = pl.ds(r0, COMBINE_ROWS)
        acc_hi = [jnp.zeros((COMBINE_ROWS, LANES), F32) for _ in range(ROW_SUBLANES)]
        acc_lo = [jnp.zeros((COMBINE_ROWS, LANES), F32) for _ in range(ROW_SUBLANES)]
        src0 = bset * TILE_SUBLANES + r0 * ROW_SUBLANES
        for k in range(TOP_K):
            gk = jnp.broadcast_to(gcol_s[rows, k:k + 1], (COMBINE_ROWS, LANES))
            for s in range(ROW_SUBLANES):
                hi, lo = _unpack_halves(
                    buf[pl.ds(src0 + k * TILE_SUBLANES + s, COMBINE_ROWS, stride=ROW_SUBLANES), :])
                acc_hi[s] = acc_hi[s] + hi * gk
                acc_lo[s] = acc_lo[s] + lo * gk
        f = jnp.concatenate(acc_hi + acc_lo, axis=1) + shared_s[rows, :]
        x = jnp.where(in_first, xp_ref[rows, :], xs_ref[rows, :])
        gate2 = gate2_s[lax.shift_right_logical(j * COMBINE_ROWS, CHUNK.bit_length() - 1)]
        y = x + _rms(f) * gpost_ref[...] * gate2
        for t in range(COMBINE_ROWS):
            gather_token(base_next, bset_next, r0 + t)
        y_s[rows, :] = y
        return carry

    lax.fori_loop(0, MOE_TILE // COMBINE_ROWS, chunk, 0)

    @pl.when(in_first)
    def _():
        yp_ref[...] = y_s[...]

    @pl.when(jnp.logical_not(in_first))
    def _():
        ysm_ref[...] = y_s[...]

    @pl.when(i == nt - 1)
    def _():
        wait_rows(bset_next)


def _combine(dest_flat, ys, x1_p, x1_s, h_packed, gates, ws_gate, ws_up, ws_down, g_post, gate2_gp, gate2_gs):
    n_prompt = x1_p.shape[0]
    n = n_prompt + x1_s.shape[0]
    nt = n // MOE_TILE
    npt = n_prompt // MOE_TILE
    gpt = MOE_TILE // CHUNK
    z2 = lambda i: (0, 0)
    return pl.pallas_call(
        functools.partial(_combine_kernel, npt),
        grid=(nt,),
        in_specs=[pl.BlockSpec(memory_space=pl.ANY),
                  pl.BlockSpec(memory_space=pl.ANY),
                  pl.BlockSpec((MOE_TILE, D_MODEL), lambda i: (jnp.minimum(i, npt - 1), 0)),
                  pl.BlockSpec((MOE_TILE, D_MODEL), lambda i: (jnp.maximum(i - npt, 0), 0)),
                  pl.BlockSpec((MOE_TILE * ROW_SUBLANES, LANES), lambda i: (i, 0)),
                  pl.BlockSpec((TOP_K, MOE_TILE), lambda i: (0, i)),
                  pl.BlockSpec((D_MODEL, EXPERT_DIM), z2),
                  pl.BlockSpec((D_MODEL, EXPERT_DIM), z2),
                  pl.BlockSpec((EXPERT_DIM, D_MODEL), z2),
                  pl.BlockSpec((1, D_MODEL), z2),
                  pl.BlockSpec((gpt, 1, D_MODEL), lambda i: (jnp.minimum(i, npt - 1), 0, 0)),
                  pl.BlockSpec((gpt, 1, D_MODEL), lambda i: (jnp.maximum(i - npt, 0), 0, 0))],
        out_specs=[pl.BlockSpec((MOE_TILE, D_MODEL), lambda i: (jnp.minimum(i, npt - 1), 0)),
                   pl.BlockSpec((MOE_TILE, D_MODEL), lambda i: (jnp.maximum(i - npt, 0), 0))],
        out_shape=[jax.ShapeDtypeStruct((n_prompt, D_MODEL), F32),
                   jax.ShapeDtypeStruct((n - n_prompt, D_MODEL), F32)],
        scratch_shapes=[pltpu.SMEM((3 * DEST_PER_TILE,), I32),
                        pltpu.VMEM((2 * TOP_K * TILE_SUBLANES, LANES), U32),
                        pltpu.VMEM((MOE_TILE, D_MODEL), F32),
                        pltpu.VMEM((MOE_TILE, LANES), F32),
                        pltpu.VMEM((MOE_TILE // CHUNK, 1, D_MODEL), F32),
                        pltpu.VMEM((MOE_TILE, D_MODEL), F32),
                        pltpu.SemaphoreType.DMA((3,)),
                        pltpu.SemaphoreType.DMA((2 * TOP_K,))],
        compiler_params=_cparams(("arbitrary",)),
        name="combine",
    )(dest_flat, ys, x1_p, x1_s, h_packed, gates, ws_gate, ws_up, ws_down, g_post, gate2_gp, gate2_gs)


ROPE_SPLIT = 128


def _rope_tables(start, length):
    half = HEAD_DIM // 2
    inv = ROPE_THETA ** (-jnp.arange(half, dtype=F32) / half)

    def cos_sin(pos):
        ang = pos.astype(F32)[:, None] * inv[None, :]
        return jnp.cos(ang), jnp.sin(ang)

    if start == 0 and length % ROPE_SPLIT == 0 and length > ROPE_SPLIT:
        ch, sh = cos_sin(jnp.arange(length // ROPE_SPLIT) * ROPE_SPLIT)
        cl, sl = cos_sin(jnp.arange(ROPE_SPLIT))
        ch, sh, cl, sl = ch[:, None, :], sh[:, None, :], cl[None], sl[None]
        cos = (ch * cl - sh * sl).reshape(length, half)
        sin = (sh * cl + ch * sl).reshape(length, half)
    else:
        cos, sin = cos_sin(start + jnp.arange(length))
    return jnp.concatenate([cos] * 4, axis=1), jnp.concatenate([-sin, sin, -sin, sin], axis=1)


def _groups(vec_rows, reps):
    rows, width = vec_rows.shape
    return jnp.broadcast_to(vec_rows[:, None, :], (rows, reps, width)).reshape(rows * reps, 1, width)


def kernel(x_prompt, x_sample, cache_a_k, cache_a_v, cache_b_k, cache_b_v, c_prompt, c_sample, w_ada, b_ada,
           g_pre_mix, g_post_mix, w_in, rel_bias_a, sinks_b, w_branch_a, w_branch_b, w_out, g_pre_ffn,
           g_post_ffn, w_router, b_router, we_gate, we_up, we_down, ws_gate, ws_up, ws_down):
    assert w_ada.shape[0] == 1, "single layer"
    nb, seq, d = x_prompt.shape
    ns, dec = x_sample.shape[:2]
    assert d == D_MODEL and dec == CHUNK and seq % TOK_TILE == 0 and (ns * dec) == TOK_TILE
    n_p, n_s = nb * seq, ns * dec
    n_all = n_p + n_s
    assert n_all % MOE_TILE == 0 and n_p % MOE_TILE == 0

    c_all = jnp.concatenate([c_prompt, c_sample], axis=0)
    pad = (-c_all.shape[0]) % 8
    c_all = jnp.pad(c_all, ((0, pad), (0, 0)))
    mod = _modulation(c_all, w_ada[0], b_ada[0])
    mod_p, mod_s = mod[:nb], mod[nb:nb + ns]
    cpp = seq // CHUNK

    def part(k):
        return mod_p[:, k * d:(k + 1) * d], mod_s[:, k * d:(k + 1) * d]

    (sh1p, sh1s), (sc1p, sc1s), (g1p, g1s), (sh2p, sh2s), (sc2p, sc2s), (g2p, g2s) = [part(k) for k in range(6)]

    w_in_bf = w_in[0].astype(BF16)
    g_pre = g_pre_mix[0].reshape(1, d)
    cos_p, sin_p = _rope_tables(0, seq)
    cos_s, sin_s = _rope_tables(PAST_LEN, dec)
    cos_s, sin_s = jnp.tile(cos_s, (ns, 1)), jnp.tile(sin_s, (ns, 1))

    xp2 = x_prompt.reshape(n_p, d)
    xs2 = x_sample.reshape(n_s, d)
    outs_p = _inproj(xp2, _groups(sh1p, cpp), _groups(sc1p, cpp), g_pre, w_in_bf, cos_p, sin_p, nb, True)
    outs_s = _inproj(xs2, _groups(sh1s, 1), _groups(sc1s, 1), g_pre, w_in_bf, cos_s, sin_s, ns, False)

    table = rel_bias_a[0].astype(F32)
    n_far = A_BAND - 1 - REL_CLIP
    ext = jnp.concatenate([jnp.broadcast_to(table[:, 2 * REL_CLIP:], (A_HEADS, n_far)),
                           jnp.flip(table[:, REL_CLIP - (CHUNK - 1):], axis=1)], axis=1)
    n_ext = ext.shape[1]
    skew = jnp.tile(jnp.pad(ext, ((0, 0), (0, 1))), (1, CHUNK))[:, :CHUNK * n_ext].reshape(A_HEADS, CHUNK, n_ext)
    bias = skew[:, :, CHUNK - 1:CHUNK - 1 + A_BAND]
    bias_pairs = bias.reshape(A_HEADS // 2, 2 * CHUNK, A_BAND) * LOG2_E
    sink_rows = jnp.broadcast_to((sinks_b[0].astype(F32) * LOG2_E).reshape(B_KV_HEADS, B_GROUP, 1),
                                 (B_KV_HEADS, B_GROUP, CHUNK)).reshape(B_KV_HEADS, B_GROUP * CHUNK, 1)
    consts = (bias_pairs, sink_rows, w_branch_a[0].astype(BF16), w_branch_b[0].astype(BF16),
              w_out[0].astype(BF16), g_post_mix[0].reshape(1, d))

    x1_p = _attn_prompt(xp2, outs_p[:7], consts, _groups(g1p, cpp), nb)
    caches = (cache_a_k[0].reshape(ns, A_ROWS, WA), cache_a_v[0].reshape(ns, A_ROWS, WA),
              cache_b_k[0].reshape(ns, WINDOW, WKB), cache_b_v[0].reshape(ns, WINDOW, WKB))
    x1_s = _attn_sample(xs2, outs_s[:7], caches, consts, _groups(g1s, 1))

    h_packed, idx, gates, rank, counts = _router(
        x1_p, x1_s, _groups(sh2p, cpp), _groups(sh2s, 1), _groups(sc2p, cpp), _groups(sc2s, 1),
        g_pre_ffn[0].reshape(1, d),
        w_router[0].T.astype(BF16), b_router[0].astype(F32).reshape(N_EXPERTS, 1))
    n_blocks = (n_all * TOP_K) // EXPERT_ROWS + N_EXPERTS
    cnt = counts[:, 0].astype(I32)
    blocks_e = (cnt + EXPERT_ROWS - 1) // EXPERT_ROWS
    blk_end = jnp.cumsum(blocks_e)
    blk_start = blk_end - blocks_e
    n_active = blk_end[-1:]
    start_col = (blk_start * EXPERT_ROWS).astype(F32).reshape(N_EXPERTS, 1)

    dest_flat = _dest_rows(idx, rank, start_col).reshape(-1)
    xs = _dispatch(dest_flat, h_packed, n_blocks * EXPERT_ROWS)
    ys = _experts(blk_start.astype(I32), blk_end.astype(I32), cnt, n_active.astype(I32), xs,
                  we_gate[0], we_up[0], we_down[0])
    y_p, y_s = _combine(dest_flat, ys, x1_p, x1_s, h_packed, gates, ws_gate[0].astype(BF16),
                        ws_up[0].astype(BF16), ws_down[0].astype(BF16), g_post_ffn[0].reshape(1, d),
                        _groups(g2p, cpp), _groups(g2s, 1))

    a_heads = (A_HEADS, HEAD_DIM)
    b_heads = (B_KV_HEADS, HEAD_DIM)
    return (y_p.reshape(nb, seq, d), y_s.reshape(ns, dec, d),
            outs_p[7].reshape(1, nb, A_ROWS, *a_heads), outs_p[8].reshape(1, nb, A_ROWS, *a_heads),
            outs_p[9].reshape(1, nb, WINDOW, *b_heads), outs_p[10].reshape(1, nb, WINDOW, *b_heads),
            outs_s[7].reshape(1, ns, dec, *a_heads), outs_s[8].reshape(1, ns, dec, *a_heads),
            outs_s[9].reshape(1, ns, dec, *b_heads), outs_s[10].reshape(1, ns, dec, *b_heads))
```

```python
import functools

import jax
import jax.numpy as jnp
from jax import lax
from jax.experimental import pallas as pl
from jax.experimental.pallas import tpu as pltpu

F32 = jnp.float32
BF16 = jnp.bfloat16
I32 = jnp.int32
U32 = jnp.uint32
HIGHEST = lax.Precision.HIGHEST
LOG2_E = 1.4426950408889634

D_MODEL = 1024
CHUNK = 64
HEAD_DIM = 64
A_HEADS = 8
A_PREV_CHUNKS = 8
A_ROWS = A_PREV_CHUNKS * CHUNK
A_BAND = A_ROWS + CHUNK
REL_CLIP = 128
B_HEADS = 8
B_KV_HEADS = 2
B_GROUP = B_HEADS // B_KV_HEADS
WINDOW = 128
B_BAND = WINDOW + CHUNK
ROPE_THETA = 10000.0
N_EXPERTS = 256
TOP_K = 8
N_GROUPS = 8
PER_GROUP = N_EXPERTS // N_GROUPS
TOPK_GROUPS = 4
EXPERT_DIM = 256
ROUTED_SCALE = 2.5
EPS = 1e-6
PAST_LEN = 4096

WA = A_HEADS * HEAD_DIM
WB = B_HEADS * HEAD_DIM
WKB = B_KV_HEADS * HEAD_DIM
OFF_QA, OFF_KA, OFF_VA = 0, WA, 2 * WA
OFF_QB = 3 * WA
OFF_KB = OFF_QB + WB
OFF_VB = OFF_KB + WKB
OFF_G = OFF_VB + WKB
N_IN = OFF_G + 2 * D_MODEL

LANES = 128
TOK_TILE = 512
MOE_TILE = 256
EXPERT_ROWS = 256
HALF_D = D_MODEL // 2
VMEM_LIMIT = 56 * 1024 * 1024


def _cparams(sem, vmem=VMEM_LIMIT):
    return pltpu.CompilerParams(dimension_semantics=sem, vmem_limit_bytes=vmem)


def _rms(x):
    return x * lax.rsqrt(jnp.mean(x * x, axis=-1, keepdims=True) + EPS)


def _group_affine(y, mul, add):
    g = mul.shape[0]
    y3 = y.reshape(g, CHUNK, y.shape[-1]) * mul
    if add is not None:
        y3 = y3 + add
    return y3.reshape(g * CHUNK, y.shape[-1])


def _pack_halves(a, b):
    ua = lax.bitcast_convert_type(a.astype(BF16).astype(F32), U32)
    ub = lax.bitcast_convert_type(b.astype(BF16).astype(F32), U32)
    return (ua & jnp.uint32(0xFFFF0000)) | (ub >> 16)


def _unpack_halves(u):
    a = lax.bitcast_convert_type(u & jnp.uint32(0xFFFF0000), F32)
    b = lax.bitcast_convert_type(u << 16, F32)
    return a, b


ROW_SUBLANES = HALF_D // LANES


def _store_packed_rows(ref, x):
    rows = x.shape[0]
    p = _pack_halves(x[:, :HALF_D], x[:, HALF_D:])
    for s in range(ROW_SUBLANES):
        ref[pl.ds(s, rows, stride=ROW_SUBLANES), :] = p[:, s * LANES:(s + 1) * LANES]


def _load_packed_rows(ref, rows, n_valid=None):
    his, los = [], []
    for s in range(ROW_SUBLANES):
        u = ref[pl.ds(s, rows, stride=ROW_SUBLANES), :]
        if n_valid is not None:
            u = jnp.where(lax.broadcasted_iota(I32, u.shape, 0) < n_valid, u, jnp.uint32(0))
        a, b = _unpack_halves(u)
        his.append(a)
        los.append(b)
    return jnp.concatenate(his, axis=1), jnp.concatenate(los, axis=1)


def _mod_kernel(c_ref, w_ref, b_ref, o_ref):
    c = c_ref[...]
    s = c * jax.nn.sigmoid(c)
    o_ref[...] = jnp.dot(s, w_ref[...], precision=HIGHEST, preferred_element_type=F32) + b_ref[...]


def _modulation(c_all, w_ada, b_ada):
    rows = c_all.shape[0]
    n = w_ada.shape[1]
    tn = 512
    return pl.pallas_call(
        _mod_kernel,
        grid=(n // tn,),
        in_specs=[pl.BlockSpec((rows, D_MODEL), lambda j: (0, 0)),
                  pl.BlockSpec((D_MODEL, tn), lambda j: (0, j)),
                  pl.BlockSpec((1, tn), lambda j: (0, j))],
        out_specs=pl.BlockSpec((rows, tn), lambda j: (0, j)),
        out_shape=jax.ShapeDtypeStruct((rows, n), F32),
        compiler_params=_cparams(("arbitrary",)),
        name="modulation",
    )(c_all, w_ada, b_ada.reshape(1, n))


def _rope(x, cos, sin_signed):
    n = x.shape[-1]
    reps = n // LANES
    if reps > 1:
        cos = jnp.concatenate([cos] * reps, axis=1)
        sin_signed = jnp.concatenate([sin_signed] * reps, axis=1)
    lane = lax.broadcasted_iota(I32, x.shape, 1)
    first_half = (lane % HEAD_DIM) < (HEAD_DIM // 2)
    partner = jnp.where(first_half, pltpu.roll(x, n - HEAD_DIM // 2, 1), pltpu.roll(x, HEAD_DIM // 2, 1))
    return x * cos + partner * sin_signed


def _inproj_kernel(prompt_state, tiles_per_seq,
                   x_ref, sh_ref, sc_ref, g_ref, w_ref, cos_ref, sin_ref,
                   qa_ref, ka_ref, va_ref, qb_ref, kb_ref, vb_ref, gt_ref,
                   ska_ref, sva_ref, skb_ref, svb_ref):
    x = x_ref[...]
    h = _group_affine(_rms(x) * g_ref[...], 1.0 + sc_ref[...], sh_ref[...]).astype(BF16)

    def proj(off, width):
        return jnp.dot(h, w_ref[:, off:off + width], preferred_element_type=F32)

    cos = cos_ref[...]
    sin = sin_ref[...]
    scale = HEAD_DIM ** -0.5 * LOG2_E
    qa_ref[...] = (proj(OFF_QA, WA) * scale).astype(BF16)
    ka = proj(OFF_KA, WA)
    va = proj(OFF_VA, WA)
    ka_ref[...] = ka.astype(BF16)
    va_ref[...] = va.astype(BF16)
    qb_ref[...] = (_rope(proj(OFF_QB, WB), cos, sin) * scale).astype(BF16)
    kb = _rope(proj(OFF_KB, WKB), cos, sin)
    vb = proj(OFF_VB, WKB)
    kb_ref[...] = kb.astype(BF16)
    vb_ref[...] = vb.astype(BF16)
    gt_ref[...] = jax.nn.sigmoid(proj(OFF_G, 2 * D_MODEL)).astype(BF16)

    if prompt_state:
        @pl.when(pl.program_id(0) % tiles_per_seq == tiles_per_seq - 1)
        def _():
            ska_ref[...] = ka
            sva_ref[...] = va
            skb_ref[...] = kb[TOK_TILE - WINDOW:, :]
            svb_ref[...] = vb[TOK_TILE - WINDOW:, :]
    else:
        ska_ref[...] = ka
        sva_ref[...] = va
        skb_ref[...] = kb
        svb_ref[...] = vb


def _inproj(x2d, shift_g, scale_g, g_pre, w_in_bf, cos_tab, sin_tab, n_seq, prompt_state):
    n = x2d.shape[0]
    nt = n // TOK_TILE
    tiles_per_seq = nt // n_seq if prompt_state else 1
    tab_tiles = cos_tab.shape[0] // TOK_TILE
    gpt = TOK_TILE // CHUNK
    row = lambda i: (i, 0)
    grp = lambda i: (i, 0, 0)
    if prompt_state:
        st_shapes = [jax.ShapeDtypeStruct((n_seq, A_ROWS, WA), F32)] * 2 + \
                    [jax.ShapeDtypeStruct((n_seq, WINDOW, WKB), F32)] * 2
        st_specs = [pl.BlockSpec((None, A_ROWS, WA), lambda i: (i // tiles_per_seq, 0, 0))] * 2 + \
                   [pl.BlockSpec((None, WINDOW, WKB), lambda i: (i // tiles_per_seq, 0, 0))] * 2
    else:
        st_shapes = [jax.ShapeDtypeStruct((n, WA), F32)] * 2 + [jax.ShapeDtypeStruct((n, WKB), F32)] * 2
        st_specs = [pl.BlockSpec((TOK_TILE, WA), row)] * 2 + [pl.BlockSpec((TOK_TILE, WKB), row)] * 2
    out_shapes = [jax.ShapeDtypeStruct((n, WA), BF16)] * 4 + [jax.ShapeDtypeStruct((n, WKB), BF16)] * 2 + \
                 [jax.ShapeDtypeStruct((n, 2 * D_MODEL), BF16)] + st_shapes
    out_specs = [pl.BlockSpec((TOK_TILE, WA), row)] * 4 + [pl.BlockSpec((TOK_TILE, WKB), row)] * 2 + \
                [pl.BlockSpec((TOK_TILE, 2 * D_MODEL), row)] + st_specs
    return pl.pallas_call(
        functools.partial(_inproj_kernel, prompt_state, tiles_per_seq),
        grid=(nt,),
        in_specs=[pl.BlockSpec((TOK_TILE, D_MODEL), row),
                  pl.BlockSpec((gpt, 1, D_MODEL), grp),
                  pl.BlockSpec((gpt, 1, D_MODEL), grp),
                  pl.BlockSpec((1, D_MODEL), lambda i: (0, 0)),
                  pl.BlockSpec((D_MODEL, N_IN), lambda i: (0, 0)),
                  pl.BlockSpec((TOK_TILE, LANES), lambda i: (i % tab_tiles, 0)),
                  pl.BlockSpec((TOK_TILE, LANES), lambda i: (i % tab_tiles, 0))],
        out_specs=out_specs,
        out_shape=out_shapes,
        compiler_params=_cparams(("arbitrary",)),
        name="inproj_prompt" if prompt_state else "inproj_sample",
    )(x2d, shift_g, scale_g, g_pre, w_in_bf, cos_tab, sin_tab)


def _attn_kernel(n_chunks, mask_first,
                 x_ref, qa_ref, qb_ref, gt_ref,
                 kap_ref, kac_ref, vap_ref, vac_ref, kbp_ref, kbc_ref, vbp_ref, vbc_ref,
                 bias_ref, sink_ref, wba_ref, wbb_ref, wout_ref, gpost_ref, gate1_ref,
                 o_ref,
                 ka_s, va_s, kb_s, vb_s, oa_s, ob_s):
    rows = n_chunks * CHUNK
    pb = kbp_ref.shape[0]
    ka_s[0:A_ROWS, :] = kap_ref[...].astype(BF16)
    va_s[0:A_ROWS, :] = vap_ref[...].astype(BF16)
    ka_s[A_ROWS:A_ROWS + rows, :] = kac_ref[...]
    va_s[A_ROWS:A_ROWS + rows, :] = vac_ref[...]
    kb_s[0:WINDOW, :] = kbp_ref[pb - WINDOW:pb, :].astype(BF16)
    vb_s[0:WINDOW, :] = vbp_ref[pb - WINDOW:pb, :].astype(BF16)
    kb_s[WINDOW:WINDOW + rows, :] = kbc_ref[...]
    vb_s[WINDOW:WINDOW + rows, :] = vbc_ref[...]

    lane_q = lax.broadcasted_iota(I32, (CHUNK, LANES), 1)
    nt_dims = (((1,), (1,)), ((), ()))

    def chunk_body(masked, c, carry):
        c0 = pl.multiple_of(c * CHUNK, CHUNK)
        if masked:
            valid_a = c0 + lax.broadcasted_iota(I32, (1, A_BAND), 1) >= A_ROWS
            valid_b = c0 + lax.broadcasted_iota(I32, (1, B_BAND), 1) >= WINDOW

        scores = []
        for p in range(A_HEADS // 2):
            cols = slice(p * LANES, (p + 1) * LANES)
            q = qa_ref[pl.ds(c0, CHUNK), cols].astype(F32)
            qs = jnp.concatenate([jnp.where(lane_q < HEAD_DIM, q, 0.0),
                                  jnp.where(lane_q >= HEAD_DIM, q, 0.0)], axis=0).astype(BF16)
            k = ka_s[pl.ds(c0, A_BAND), cols]
            s = lax.dot_general(qs, k, nt_dims, preferred_element_type=F32) + bias_ref[p]
            if masked:
                s = jnp.where(valid_a, s, -jnp.inf)
            scores.append(s)
        for g in range(B_KV_HEADS):
            parts = []
            for r in range(B_GROUP):
                head = g * B_GROUP + r
                t, half = head // 2, head % 2
                q = qb_ref[pl.ds(c0, CHUNK), t * LANES:(t + 1) * LANES].astype(F32)
                if half != g:
                    q = pltpu.roll(q, HEAD_DIM, 1)
                in_g = (lane_q >= HEAD_DIM) if g else (lane_q < HEAD_DIM)
                parts.append(jnp.where(in_g, q, 0.0))
            qs = jnp.concatenate(parts, axis=0).astype(BF16)
            k = kb_s[pl.ds(c0, B_BAND), :]
            s = lax.dot_general(qs, k, nt_dims, preferred_element_type=F32)
            if masked:
                s = jnp.where(valid_b, s, -jnp.inf)
            scores.append(s)

        numer, denom = [], []
        for n, s in enumerate(scores):
            m = jnp.max(s, axis=1, keepdims=True)
            if n >= A_HEADS // 2:
                sk = sink_ref[n - A_HEADS // 2]
                m = jnp.maximum(m, sk)
            e = jnp.exp2(s - m)
            l = jnp.sum(e, axis=1, keepdims=True)
            if n >= A_HEADS // 2:
                l = l + jnp.exp2(sk - m)
            numer.append(e.astype(BF16))
            denom.append(l)

        outs = []
        for n, e in enumerate(numer):
            if n < A_HEADS // 2:
                v = va_s[pl.ds(c0, A_BAND), n * LANES:(n + 1) * LANES]
            else:
                v = vb_s[pl.ds(c0, B_BAND), :]
            outs.append(jnp.dot(e, v, preferred_element_type=F32) / denom[n])

        for p in range(A_HEADS // 2):
            o = outs[p]
            oa_s[pl.ds(c0, CHUNK), p * LANES:(p + 1) * LANES] = jnp.where(
                lane_q < HEAD_DIM, o[:CHUNK], o[CHUNK:]).astype(BF16)
        for g in range(B_KV_HEADS):
            o = outs[A_HEADS // 2 + g]
            for s2 in range(B_GROUP // 2):
                o_even = o[(2 * s2) * CHUNK:(2 * s2 + 1) * CHUNK]
                o_odd = o[(2 * s2 + 1) * CHUNK:(2 * s2 + 2) * CHUNK]
                if g == 0:
                    tile = jnp.where(lane_q < HEAD_DIM, o_even, pltpu.roll(o_odd, HEAD_DIM, 1))
                else:
                    tile = jnp.where(lane_q < HEAD_DIM, pltpu.roll(o_even, HEAD_DIM, 1), o_odd)
                t = g * (B_GROUP // 2) + s2
                ob_s[pl.ds(c0, CHUNK), t * LANES:(t + 1) * LANES] = tile.astype(BF16)
        return carry

    def all_chunks(masked):
        lax.fori_loop(0, n_chunks, functools.partial(chunk_body, masked), 0, unroll=4 if n_chunks > 1 else 1)

    if mask_first:
        pl.when(pl.program_id(1) == 0)(lambda: all_chunks(True))
        pl.when(pl.program_id(1) > 0)(lambda: all_chunks(False))
    else:
        all_chunks(False)

    za = jnp.dot(oa_s[...], wba_ref[...], preferred_element_type=F32)
    zb = jnp.dot(ob_s[...], wbb_ref[...], preferred_element_type=F32)
    merged = gt_ref[:, :D_MODEL].astype(F32) * za + gt_ref[:, D_MODEL:].astype(F32) * zb
    mo = jnp.dot(merged.astype(BF16), wout_ref[...], preferred_element_type=F32)
    o_ref[...] = x_ref[...] + _group_affine(_rms(mo) * gpost_ref[...], gate1_ref[...], None)


def _attn_scratch(rows):
    return [pltpu.VMEM((A_ROWS + rows, WA), BF16), pltpu.VMEM((A_ROWS + rows, WA), BF16),
            pltpu.VMEM((WINDOW + rows, WKB), BF16), pltpu.VMEM((WINDOW + rows, WKB), BF16),
            pltpu.VMEM((rows, WA), BF16), pltpu.VMEM((rows, WB), BF16)]


def _const_specs(grid_rank):
    z2 = (lambda b, j: (0, 0)) if grid_rank == 2 else (lambda b: (0, 0))
    z3 = (lambda b, j: (0, 0, 0)) if grid_rank == 2 else (lambda b: (0, 0, 0))
    return [pl.BlockSpec((A_HEADS // 2, 2 * CHUNK, A_BAND), z3),
            pl.BlockSpec((B_KV_HEADS, B_GROUP * CHUNK, 1), z3),
            pl.BlockSpec((WA, D_MODEL), z2),
            pl.BlockSpec((WB, D_MODEL), z2),
            pl.BlockSpec((D_MODEL, D_MODEL), z2),
            pl.BlockSpec((1, D_MODEL), z2)]


def _attn_prompt(x2d, proj, consts, gate1_g, n_seq):
    qa, ka, va, qb, kb, vb, gt = proj
    n = x2d.shape[0]
    tps = n // n_seq // TOK_TILE
    gpt = TOK_TILE // CHUNK
    cur = lambda b, j: (b * tps + j, 0)
    prev = lambda b, j: (b * tps + jnp.maximum(j - 1, 0), 0)
    return pl.pallas_call(
        functools.partial(_attn_kernel, TOK_TILE // CHUNK, True),
        grid=(n_seq, tps),
        in_specs=[pl.BlockSpec((TOK_TILE, D_MODEL), cur),
                  pl.BlockSpec((TOK_TILE, WA), cur),
                  pl.BlockSpec((TOK_TILE, WB), cur),
                  pl.BlockSpec((TOK_TILE, 2 * D_MODEL), cur),
                  pl.BlockSpec((TOK_TILE, WA), prev), pl.BlockSpec((TOK_TILE, WA), cur),
                  pl.BlockSpec((TOK_TILE, WA), prev), pl.BlockSpec((TOK_TILE, WA), cur),
                  pl.BlockSpec((TOK_TILE, WKB), prev), pl.BlockSpec((TOK_TILE, WKB), cur),
                  pl.BlockSpec((TOK_TILE, WKB), prev), pl.BlockSpec((TOK_TILE, WKB), cur)]
                 + _const_specs(2)
                 + [pl.BlockSpec((gpt, 1, D_MODEL), lambda b, j: (b * tps + j, 0, 0))],
        out_specs=pl.BlockSpec((TOK_TILE, D_MODEL), cur),
        out_shape=jax.ShapeDtypeStruct((n, D_MODEL), F32),
        scratch_shapes=_attn_scratch(TOK_TILE),
        compiler_params=_cparams(("arbitrary", "arbitrary")),
        name="attn_prompt",
    )(x2d, qa, qb, gt, ka, ka, va, va, kb, kb, vb, vb, *consts, gate1_g)


def _attn_sample(x2d, proj, caches, consts, gate1_g):
    qa, ka, va, qb, kb, vb, gt = proj
    cak, cav, cbk, cbv = caches
    n_seq = cak.shape[0]
    cur = lambda b: (b, 0)
    cache = lambda b: (b, 0, 0)
    return pl.pallas_call(
        functools.partial(_attn_kernel, 1, False),
        grid=(n_seq,),
        in_specs=[pl.BlockSpec((CHUNK, D_MODEL), cur),
                  pl.BlockSpec((CHUNK, WA), cur),
                  pl.BlockSpec((CHUNK, WB), cur),
                  pl.BlockSpec((CHUNK, 2 * D_MODEL), cur),
                  pl.BlockSpec((None, A_ROWS, WA), cache), pl.BlockSpec((CHUNK, WA), cur),
                  pl.BlockSpec((None, A_ROWS, WA), cache), pl.BlockSpec((CHUNK, WA), cur),
                  pl.BlockSpec((None, WINDOW, WKB), cache), pl.BlockSpec((CHUNK, WKB), cur),
                  pl.BlockSpec((None, WINDOW, WKB), cache), pl.BlockSpec((CHUNK, WKB), cur)]
                 + _const_specs(1)
                 + [pl.BlockSpec((1, 1, D_MODEL), lambda b: (b, 0, 0))],
        out_specs=pl.BlockSpec((CHUNK, D_MODEL), cur),
        out_shape=jax.ShapeDtypeStruct(x2d.shape, F32),
        scratch_shapes=_attn_scratch(CHUNK),
        compiler_params=_cparams(("arbitrary",)),
        name="attn_sample",
    )(x2d, qa, qb, gt, cak, ka, cav, va, cbk, kb, cbv, vb, *consts, gate1_g)


def _two_part_tile(n_first_tiles, first_ref, second_ref):
    return jnp.where(pl.program_id(0) < n_first_tiles, first_ref[...], second_ref[...])


def _router_kernel(n_prompt_tiles, xp_ref, xs_ref, shp_ref, shs_ref, scp_ref, scs_ref, g_ref, wrt_ref, br_ref,
                   h_ref, idx_ref, gate_ref, rank_ref, cnt_ref, carry):
    i = pl.program_id(0)

    @pl.when(i == 0)
    def _():
        carry[...] = jnp.zeros_like(carry)

    tm = xp_ref.shape[0]
    x = _two_part_tile(n_prompt_tiles, xp_ref, xs_ref)
    h = _group_affine(_rms(x) * g_ref[...], 1.0 + _two_part_tile(n_prompt_tiles, scp_ref, scs_ref),
                      _two_part_tile(n_prompt_tiles, shp_ref, shs_ref))
    _store_packed_rows(h_ref, h)

    logits = lax.dot_general(wrt_ref[...], h.astype(BF16), (((1,), (1,)), ((), ())),
                             preferred_element_type=F32)
    scores = jax.nn.sigmoid(logits)
    biased = scores + br_ref[...]
    neg = -jnp.inf

    sub = lax.broadcasted_iota(I32, (PER_GROUP, tm), 0).astype(F32)
    gs_rows = []
    for g in range(N_GROUPS):
        xg = biased[g * PER_GROUP:(g + 1) * PER_GROUP]
        m1 = jnp.max(xg, axis=0, keepdims=True)
        i1 = jnp.min(jnp.where(xg == m1, sub, float(PER_GROUP)), axis=0, keepdims=True)
        m2 = jnp.max(jnp.where(sub == i1, neg, xg), axis=0, keepdims=True)
        gs_rows.append(m1 + m2)
    gs = jnp.concatenate(gs_rows, axis=0)

    giota = lax.broadcasted_iota(I32, (N_GROUPS, tm), 0).astype(F32)
    keep = jnp.zeros((N_GROUPS, tm), F32)
    for _ in range(TOPK_GROUPS):
        m = jnp.max(gs, axis=0, keepdims=True)
        gi = jnp.min(jnp.where(gs == m, giota, float(N_GROUPS)), axis=0, keepdims=True)
        hit = giota == gi
        keep = jnp.where(hit, 1.0, keep)
        gs = jnp.where(hit, neg, gs)
    cand = jnp.concatenate(
        [jnp.where(keep[g:g + 1] > 0.0, biased[g * PER_GROUP:(g + 1) * PER_GROUP], neg)
         for g in range(N_GROUPS)], axis=0)

    eiota = lax.broadcasted_iota(I32, (N_EXPERTS, tm), 0).astype(F32)
    idx_rows, gate_rows = [], []
    chosen = jnp.zeros((N_EXPERTS, tm), F32)
    for _ in range(TOP_K):
        m = jnp.max(cand, axis=0, keepdims=True)
        ei = jnp.min(jnp.where(cand == m, eiota, float(N_EXPERTS)), axis=0, keepdims=True)
        sel = eiota == ei
        gate_rows.append(jnp.sum(jnp.where(sel, scores, 0.0), axis=0, keepdims=True))
        idx_rows.append(ei)
        chosen = jnp.where(sel, 1.0, chosen)
        cand = jnp.where(sel, neg, cand)
    gates = jnp.concatenate(gate_rows, axis=0)
    gates = gates / jnp.sum(gates, axis=0, keepdims=True) * ROUTED_SCALE
    gate_ref[...] = gates
    idx_ref[...] = jnp.concatenate(idx_rows, axis=0).astype(I32)

    r_i = lax.broadcasted_iota(I32, (tm, tm), 0)
    c_i = lax.broadcasted_iota(I32, (tm, tm), 1)
    upper = jnp.where(r_i < c_i, 1.0, 0.0).astype(BF16)
    before = jnp.dot(chosen.astype(BF16), upper, preferred_element_type=F32) + carry[...]
    rank_rows = [jnp.sum(jnp.where(eiota == idx_rows[k], before, 0.0), axis=0, keepdims=True)
                 for k in range(TOP_K)]
    rank_ref[...] = jnp.concatenate(rank_rows, axis=0).astype(I32)
    total = carry[...] + jnp.sum(chosen, axis=1, keepdims=True)
    carry[...] = total
    cnt_ref[...] = total


def _router(x1_p, x1_s, shift_gp, shift_gs, scale_gp, scale_gs, g_pre, w_router_t, b_router_col):
    n = x1_p.shape[0] + x1_s.shape[0]
    nt = n // MOE_TILE
    npt = x1_p.shape[0] // MOE_TILE
    gpt = MOE_TILE // CHUNK
    lane_blk = lambda i: (0, i)
    return pl.pallas_call(
        functools.partial(_router_kernel, npt),
        grid=(nt,),
        in_specs=[pl.BlockSpec((MOE_TILE, D_MODEL), lambda i: (jnp.minimum(i, npt - 1), 0)),
                  pl.BlockSpec((MOE_TILE, D_MODEL), lambda i: (jnp.maximum(i - npt, 0), 0)),
                  pl.BlockSpec((gpt, 1, D_MODEL), lambda i: (jnp.minimum(i, npt - 1), 0, 0)),
                  pl.BlockSpec((gpt, 1, D_MODEL), lambda i: (jnp.maximum(i - npt, 0), 0, 0)),
                  pl.BlockSpec((gpt, 1, D_MODEL), lambda i: (jnp.minimum(i, npt - 1), 0, 0)),
                  pl.BlockSpec((gpt, 1, D_MODEL), lambda i: (jnp.maximum(i - npt, 0), 0, 0)),
                  pl.BlockSpec((1, D_MODEL), lambda i: (0, 0)),
                  pl.BlockSpec((N_EXPERTS, D_MODEL), lambda i: (0, 0)),
                  pl.BlockSpec((N_EXPERTS, 1), lambda i: (0, 0))],
        out_specs=[pl.BlockSpec((MOE_TILE * ROW_SUBLANES, LANES), lambda i: (i, 0)),
                   pl.BlockSpec((TOP_K, MOE_TILE), lane_blk),
                   pl.BlockSpec((TOP_K, MOE_TILE), lane_blk),
                   pl.BlockSpec((TOP_K, MOE_TILE), lane_blk),
                   pl.BlockSpec((N_EXPERTS, 1), lambda i: (0, 0))],
        out_shape=[jax.ShapeDtypeStruct((n * ROW_SUBLANES, LANES), U32),
                   jax.ShapeDtypeStruct((TOP_K, n), I32),
                   jax.ShapeDtypeStruct((TOP_K, n), F32),
                   jax.ShapeDtypeStruct((TOP_K, n), I32),
                   jax.ShapeDtypeStruct((N_EXPERTS, 1), F32)],
        scratch_shapes=[pltpu.VMEM((N_EXPERTS, 1), F32)],
        compiler_params=_cparams(("arbitrary",)),
        name="router",
    )(x1_p, x1_s, shift_gp, shift_gs, scale_gp, scale_gs, g_pre, w_router_t, b_router_col)


def _dest_kernel(idx_ref, rank_ref, start_ref, o_ref):
    tabs = [jnp.broadcast_to(start_ref[:, t * LANES:(t + 1) * LANES], (TOP_K, LANES))
            for t in range(N_EXPERTS // LANES)]
    for sub in range(o_ref.shape[0]):
        parts = []
        for c in range(MOE_TILE // LANES):
            cols = slice(sub * MOE_TILE + c * LANES, sub * MOE_TILE + (c + 1) * LANES)
            idx = idx_ref[:, cols]
            lane = idx % LANES
            val = jnp.take_along_axis(tabs[0], lane, axis=1)
            for t in range(1, len(tabs)):
                val = jnp.where(idx >= t * LANES, jnp.take_along_axis(tabs[t], lane, axis=1), val)
            parts.append(val + rank_ref[:, cols])
        o_ref[sub] = jnp.concatenate(parts, axis=1)


def _dest_rows(idx, rank, start_row):
    n = idx.shape[1]
    nt = n // MOE_TILE
    per_step = next(c for c in (10, 8, 5, 4, 2, 1) if nt % c == 0)
    return pl.pallas_call(
        _dest_kernel,
        grid=(nt // per_step,),
        in_specs=[pl.BlockSpec((TOP_K, per_step * MOE_TILE), lambda i: (0, i)),
                  pl.BlockSpec((TOP_K, per_step * MOE_TILE), lambda i: (0, i)),
                  pl.BlockSpec((1, N_EXPERTS), lambda i: (0, 0))],
        out_specs=pl.BlockSpec((per_step, TOP_K, MOE_TILE), lambda i: (i, 0, 0)),
        out_shape=jax.ShapeDtypeStruct((nt, TOP_K, MOE_TILE), I32),
        compiler_params=_cparams(("arbitrary",)),
        name="dest_rows",
    )(idx, rank, start_row)


DEST_PER_TILE = TOP_K * MOE_TILE


def _row_slice(ref, row):
    return ref.at[pl.ds(pl.multiple_of(row * ROW_SUBLANES, ROW_SUBLANES), ROW_SUBLANES)]


TILE_SUBLANES = MOE_TILE * ROW_SUBLANES


def _dispatch_kernel(dest_hbm, h_hbm, xs_hbm, dest_s, hbuf, dsem, hsem, rsem):
    i = pl.program_id(0)
    nt = pl.num_programs(0)

    def load(tile):
        dst = dest_s.at[pl.ds((tile % 2) * DEST_PER_TILE, DEST_PER_TILE)]
        rows = h_hbm.at[pl.ds(pl.multiple_of(tile * TILE_SUBLANES, TILE_SUBLANES), TILE_SUBLANES)]
        return (pltpu.make_async_copy(dest_hbm.at[pl.ds(tile * DEST_PER_TILE, DEST_PER_TILE)], dst, dsem.at[tile % 2]),
                pltpu.make_async_copy(rows, hbuf.at[tile % 3], hsem.at[tile % 3]))

    def drain(tile):
        for k in range(TOP_K):
            pltpu.make_async_copy(hbuf.at[tile % 3], xs_hbm.at[pl.ds(0, TILE_SUBLANES)],
                                  rsem.at[(tile % 2) * TOP_K + k]).wait()

    @pl.when(i == 0)
    def _():
        for cp in load(0):
            cp.start()

    @pl.when(i + 1 < nt)
    def _():
        for cp in load(i + 1):
            cp.start()

    for cp in load(i):
        cp.wait()
    base = (i % 2) * DEST_PER_TILE
    src_buf = hbuf.at[i % 3]
    sem0 = (i % 2) * TOP_K

    def body(t, carry):
        src = _row_slice(src_buf, t)
        for k in range(TOP_K):
            d = dest_s[base + k * MOE_TILE + t]
            pltpu.make_async_copy(src, _row_slice(xs_hbm, d), rsem.at[sem0 + k]).start(priority=k % 2)
        return carry

    lax.fori_loop(0, MOE_TILE, body, 0, unroll=4)

    @pl.when(i >= 1)
    def _():
        drain(i - 1)

    @pl.when(i == nt - 1)
    def _():
        drain(i)


def _dispatch(dest_flat, h_packed, n_rows):
    nt = h_packed.shape[0] // TILE_SUBLANES
    return pl.pallas_call(
        _dispatch_kernel,
        grid=(nt,),
        in_specs=[pl.BlockSpec(memory_space=pl.ANY),
                  pl.BlockSpec(memory_space=pl.ANY)],
        out_specs=pl.BlockSpec(memory_space=pl.ANY),
        out_shape=jax.ShapeDtypeStruct((n_rows * ROW_SUBLANES, LANES), U32),
        scratch_shapes=[pltpu.SMEM((2 * DEST_PER_TILE,), I32),
                        pltpu.VMEM((3, TILE_SUBLANES, LANES), U32),
                        pltpu.SemaphoreType.DMA((2,)),
                        pltpu.SemaphoreType.DMA((3,)),
                        pltpu.SemaphoreType.DMA((2 * TOP_K,))],
        compiler_params=_cparams(("arbitrary",)),
        name="dispatch",
    )(dest_flat, h_packed)


BLOCK_SUBLANES = EXPERT_ROWS * ROW_SUBLANES
EXPERT_AHEAD = 4
EXPERT_SLOTS = EXPERT_AHEAD + 2


def _expert_kernel(first_ref, last_ref, cnt_ref, na_ref, xs_hbm, wg_ref, wu_ref, wd_ref, ys_hbm,
                   wg_s, wu_s, wd_s, xbuf, ybuf, isem, osem):
    e = pl.program_id(0)
    n_active = na_ref[0]
    first, last = first_ref[e], last_ref[e]

    def block_rows(ref, g):
        return ref.at[pl.ds(pl.multiple_of(g * BLOCK_SUBLANES, BLOCK_SUBLANES), BLOCK_SUBLANES)]

    def fetch(g, slot):
        return pltpu.make_async_copy(block_rows(xs_hbm, g), xbuf.at[slot], isem.at[slot])

    def flush(g, slot):
        return pltpu.make_async_copy(ybuf.at[slot], block_rows(ys_hbm, g), osem.at[slot])

    @pl.when(e == 0)
    def _():
        for g0 in range(EXPERT_AHEAD):
            @pl.when(g0 < n_active)
            def _():
                fetch(g0, g0).start()

    @pl.when(last > first)
    def _():
        wg_s[...] = wg_ref[...].astype(BF16)
        wu_s[...] = wu_ref[...].astype(BF16)
        wd_s[...] = wd_ref[...].astype(BF16)

    def enter(g):
        slot = g % EXPERT_SLOTS
        fetch(g, slot).wait()

        @pl.when(g + EXPERT_AHEAD < n_active)
        def _():
            fetch(g + EXPERT_AHEAD, (g + EXPERT_AHEAD) % EXPERT_SLOTS).start()

        @pl.when(g >= EXPERT_SLOTS)
        def _():
            flush(g - EXPERT_SLOTS, slot).wait()
        return slot

    def swiglu(halves):
        xa = jnp.concatenate([h[0] for h in halves], axis=0).astype(BF16)
        xb = jnp.concatenate([h[1] for h in halves], axis=0).astype(BF16)

        def up(w_s):
            return (jnp.dot(xa, w_s[:HALF_D, :], preferred_element_type=F32)
                    + jnp.dot(xb, w_s[HALF_D:, :], preferred_element_type=F32))

        gate = up(wg_s)
        a = (gate * jax.nn.sigmoid(gate) * up(wu_s)).astype(BF16)
        return jnp.dot(a, wd_s[...], preferred_element_type=F32)

    def n_valid(g):
        return cnt_ref[e] - (g - first) * EXPERT_ROWS

    def pair_body(p, carry):
        g = first + 2 * p
        s0 = enter(g)
        s1 = enter(g + 1)
        y = swiglu([_load_packed_rows(xbuf.at[s0], EXPERT_ROWS),
                    _load_packed_rows(xbuf.at[s1], EXPERT_ROWS, n_valid(g + 1))])
        _store_packed_rows(ybuf.at[s0], y[:EXPERT_ROWS])
        _store_packed_rows(ybuf.at[s1], y[EXPERT_ROWS:])
        flush(g, s0).start()
        flush(g + 1, s1).start()
        return carry

    n_pairs = lax.shift_right_logical(last - first, 1)
    lax.fori_loop(0, n_pairs, pair_body, 0)

    @pl.when((last - first) % 2 == 1)
    def _():
        g = last - 1
        s0 = enter(g)
        y = swiglu([_load_packed_rows(xbuf.at[s0], EXPERT_ROWS, n_valid(g))])
        _store_packed_rows(ybuf.at[s0], y)
        flush(g, s0).start()

    @pl.when(e == pl.num_programs(0) - 1)
    def _():
        for back in range(EXPERT_SLOTS, 0, -1):
            @pl.when(n_active >= back)
            def _():
                flush(n_active - back, (n_active - back) % EXPERT_SLOTS).wait()


def _experts(blk_first, blk_last, counts, n_active, xs, we_gate, we_up, we_down):
    wmap = lambda e, *_: (e, 0, 0)
    return pl.pallas_call(
        _expert_kernel,
        grid_spec=pltpu.PrefetchScalarGridSpec(
            num_scalar_prefetch=4,
            grid=(N_EXPERTS,),
            in_specs=[pl.BlockSpec(memory_space=pl.ANY),
                      pl.BlockSpec((None, D_MODEL, EXPERT_DIM), wmap),
                      pl.BlockSpec((None, D_MODEL, EXPERT_DIM), wmap),
                      pl.BlockSpec((None, EXPERT_DIM, D_MODEL), wmap)],
            out_specs=pl.BlockSpec(memory_space=pl.ANY),
            scratch_shapes=[pltpu.VMEM((D_MODEL, EXPERT_DIM), BF16),
                            pltpu.VMEM((D_MODEL, EXPERT_DIM), BF16),
                            pltpu.VMEM((EXPERT_DIM, D_MODEL), BF16),
                            pltpu.VMEM((EXPERT_SLOTS, BLOCK_SUBLANES, LANES), U32),
                            pltpu.VMEM((EXPERT_SLOTS, BLOCK_SUBLANES, LANES), U32),
                            pltpu.SemaphoreType.DMA((EXPERT_SLOTS,)),
                            pltpu.SemaphoreType.DMA((EXPERT_SLOTS,))]),
        out_shape=jax.ShapeDtypeStruct(xs.shape, U32),
        compiler_params=_cparams(("arbitrary",)),
        name="experts",
    )(blk_first, blk_last, counts, n_active, xs, we_gate, we_up, we_down)


COMBINE_ROWS = 8


def _combine_kernel(n_prompt_tiles,
                    dest_hbm, ys_hbm, xp_ref, xs_ref, h_ref, gate_ref, wsg_ref, wsu_ref, wsd_ref, gpost_ref,
                    gate2p_ref, gate2s_ref,
                    yp_ref, ysm_ref, dest_s, buf, shared_s, gcol_s, gate2_s, y_s, dsem, rsem):
    i = pl.program_id(0)
    nt = pl.num_programs(0)

    def dest_copy(tile):
        slot = tile % 3
        return pltpu.make_async_copy(dest_hbm.at[pl.ds(tile * DEST_PER_TILE, DEST_PER_TILE)],
                                     dest_s.at[pl.ds(slot * DEST_PER_TILE, DEST_PER_TILE)], dsem.at[slot])

    def gather_token(base, bset, t):
        dst0 = bset * TILE_SUBLANES + t * ROW_SUBLANES
        for k in range(TOP_K):
            d = dest_s[base + t + k * MOE_TILE]
            dst = buf.at[pl.ds(pl.multiple_of(dst0 + k * TILE_SUBLANES, ROW_SUBLANES), ROW_SUBLANES)]
            pltpu.make_async_copy(_row_slice(ys_hbm, d), dst, rsem.at[bset + k]).start(priority=k % 2)

    def wait_rows(bset):
        for k in range(TOP_K):
            pltpu.make_async_copy(ys_hbm.at[pl.ds(0, TILE_SUBLANES)], buf.at[pl.ds(0, TILE_SUBLANES)],
                                  rsem.at[bset + k]).wait()

    @pl.when(i == 0)
    def _():
        dest_copy(0).start()

        @pl.when(nt > 1)
        def _():
            dest_copy(1).start()

        dest_copy(0).wait()

        def first_tile(t, carry):
            gather_token(0, 0, t)
            return carry

        lax.fori_loop(0, MOE_TILE, first_tile, 0, unroll=4)

    @pl.when(i + 2 < nt)
    def _():
        dest_copy(i + 2).start()

    @pl.when(i + 1 < nt)
    def _():
        dest_copy(i + 1).wait()

    nxt = jnp.minimum(i + 1, nt - 1)
    base_next = (nxt % 3) * DEST_PER_TILE
    bset = (i % 2) * TOP_K
    bset_next = ((i + 1) % 2) * TOP_K

    ha, hb = _load_packed_rows(h_ref, MOE_TILE)
    ha = ha.astype(BF16)
    hb = hb.astype(BF16)

    def up(w_ref):
        return (jnp.dot(ha, w_ref[:HALF_D, :], preferred_element_type=F32)
                + jnp.dot(hb, w_ref[HALF_D:, :], preferred_element_type=F32))

    g = up(wsg_ref)
    a = (g * jax.nn.sigmoid(g) * up(wsu_ref)).astype(BF16)
    shared_s[...] = jnp.dot(a, wsd_ref[...], preferred_element_type=F32)

    gpad = jnp.concatenate([gate_ref[...], jnp.zeros((LANES - TOP_K, MOE_TILE), F32)], axis=0)
    gcol_s[...] = gpad.T

    wait_rows(bset)
    in_first = i < n_prompt_tiles
    gate2_s[...] = _two_part_tile(n_prompt_tiles, gate2p_ref, gate2s_ref)

    def chunk(j, carry):
        r0 = pl.multiple_of(j * COMBINE_ROWS, COMBINE_ROWS)
        rows = pl.ds(r0, COMBINE_ROWS)
        acc_hi = [jnp.zeros((COMBINE_ROWS, LANES), F32) for _ in range(ROW_SUBLANES)]
        acc_lo = [jnp.zeros((COMBINE_ROWS, LANES), F32) for _ in range(ROW_SUBLANES)]
        src0 = bset * TILE_SUBLANES + r0 * ROW_SUBLANES
        for k in range(TOP_K):
            gk = jnp.broadcast_to(gcol_s[rows, k:k + 1], (COMBINE_ROWS, LANES))
            for s in range(ROW_SUBLANES):
                hi, lo = _unpack_halves(
                    buf[pl.ds(src0 + k * TILE_SUBLANES + s, COMBINE_ROWS, stride=ROW_SUBLANES), :])
                acc_hi[s] = acc_hi[s] + hi * gk
                acc_lo[s] = acc_lo[s] + lo * gk
        f = jnp.concatenate(acc_hi + acc_lo, axis=1) + shared_s[rows, :]
        x = jnp.where(in_first, xp_ref[rows, :], xs_ref[rows, :])
        gate2 = gate2_s[lax.shift_right_logical(j * COMBINE_ROWS, CHUNK.bit_length() - 1)]
        y = x + _rms(f) * gpost_ref[...] * gate2
        for t in range(COMBINE_ROWS):
            gather_token(base_next, bset_next, r0 + t)
        y_s[rows, :] = y
        return carry

    lax.fori_loop(0, MOE_TILE // COMBINE_ROWS, chunk, 0)

    @pl.when(in_first)
    def _():
        yp_ref[...] = y_s[...]

    @pl.when(jnp.logical_not(in_first))
    def _():
        ysm_ref[...] = y_s[...]

    @pl.when(i == nt - 1)
    def _():
        wait_rows(bset_next)


def _combine(dest_flat, ys, x1_p, x1_s, h_packed, gates, ws_gate, ws_up, ws_down, g_post, gate2_gp, gate2_gs):
    n_prompt = x1_p.shape[0]
    n = n_prompt + x1_s.shape[0]
    nt = n // MOE_TILE
    npt = n_prompt // MOE_TILE
    gpt = MOE_TILE // CHUNK
    z2 = lambda i: (0, 0)
    return pl.pallas_call(
        functools.partial(_combine_kernel, npt),
        grid=(nt,),
        in_specs=[pl.BlockSpec(memory_space=pl.ANY),
                  pl.BlockSpec(memory_space=pl.ANY),
                  pl.BlockSpec((MOE_TILE, D_MODEL), lambda i: (jnp.minimum(i, npt - 1), 0)),
                  pl.BlockSpec((MOE_TILE, D_MODEL), lambda i: (jnp.maximum(i - npt, 0), 0)),
                  pl.BlockSpec((MOE_TILE * ROW_SUBLANES, LANES), lambda i: (i, 0)),
                  pl.BlockSpec((TOP_K, MOE_TILE), lambda i: (0, i)),
                  pl.BlockSpec((D_MODEL, EXPERT_DIM), z2),
                  pl.BlockSpec((D_MODEL, EXPERT_DIM), z2),
                  pl.BlockSpec((EXPERT_DIM, D_MODEL), z2),
                  pl.BlockSpec((1, D_MODEL), z2),
                  pl.BlockSpec((gpt, 1, D_MODEL), lambda i: (jnp.minimum(i, npt - 1), 0, 0)),
                  pl.BlockSpec((gpt, 1, D_MODEL), lambda i: (jnp.maximum(i - npt, 0), 0, 0))],
        out_specs=[pl.BlockSpec((MOE_TILE, D_MODEL), lambda i: (jnp.minimum(i, npt - 1), 0)),
                   pl.BlockSpec((MOE_TILE, D_MODEL), lambda i: (jnp.maximum(i - npt, 0), 0))],
        out_shape=[jax.ShapeDtypeStruct((n_prompt, D_MODEL), F32),
                   jax.ShapeDtypeStruct((n - n_prompt, D_MODEL), F32)],
        scratch_shapes=[pltpu.SMEM((3 * DEST_PER_TILE,), I32),
                        pltpu.VMEM((2 * TOP_K * TILE_SUBLANES, LANES), U32),
                        pltpu.VMEM((MOE_TILE, D_MODEL), F32),
                        pltpu.VMEM((MOE_TILE, LANES), F32),
                        pltpu.VMEM((MOE_TILE // CHUNK, 1, D_MODEL), F32),
                        pltpu.VMEM((MOE_TILE, D_MODEL), F32),
                        pltpu.SemaphoreType.DMA((3,)),
                        pltpu.SemaphoreType.DMA((2 * TOP_K,))],
        compiler_params=_cparams(("arbitrary",)),
        name="combine",
    )(dest_flat, ys, x1_p, x1_s, h_packed, gates, ws_gate, ws_up, ws_down, g_post, gate2_gp, gate2_gs)


ROPE_SPLIT = 128


def _rope_tables(start, length):
    half = HEAD_DIM // 2
    inv = ROPE_THETA ** (-jnp.arange(half, dtype=F32) / half)

    def cos_sin(pos):
        ang = pos.astype(F32)[:, None] * inv[None, :]
        return jnp.cos(ang), jnp.sin(ang)

    if start == 0 and length % ROPE_SPLIT == 0 and length > ROPE_SPLIT:
        ch, sh = cos_sin(jnp.arange(length // ROPE_SPLIT) * ROPE_SPLIT)
        cl, sl = cos_sin(jnp.arange(ROPE_SPLIT))
        ch, sh, cl, sl = ch[:, None, :], sh[:, None, :], cl[None], sl[None]
        cos = (ch * cl - sh * sl).reshape(length, half)
        sin = (sh * cl + ch * sl).reshape(length, half)
    else:
        cos, sin = cos_sin(start + jnp.arange(length))
    reps = LANES // half
    sign = jnp.where((jnp.arange(LANES) // half) % 2 == 0, -1.0, 1.0).astype(F32)
    return jnp.tile(cos, (1, reps)), jnp.tile(sin, (1, reps)) * sign


def _groups(vec_rows, reps):
    rows, width = vec_rows.shape
    return jnp.broadcast_to(vec_rows[:, None, :], (rows, reps, width)).reshape(rows * reps, 1, width)


def kernel(x_prompt, x_sample, cache_a_k, cache_a_v, cache_b_k, cache_b_v, c_prompt, c_sample, w_ada, b_ada,
           g_pre_mix, g_post_mix, w_in, rel_bias_a, sinks_b, w_branch_a, w_branch_b, w_out, g_pre_ffn,
           g_post_ffn, w_router, b_router, we_gate, we_up, we_down, ws_gate, ws_up, ws_down):
    assert w_ada.shape[0] == 1, "single layer"
    nb, seq, d = x_prompt.shape
    ns, dec = x_sample.shape[:2]
    assert d == D_MODEL and dec == CHUNK and seq % TOK_TILE == 0 and (ns * dec) == TOK_TILE
    n_p, n_s = nb * seq, ns * dec
    n_all = n_p + n_s
    assert n_all % MOE_TILE == 0 and n_p % MOE_TILE == 0

    c_all = jnp.concatenate([c_prompt, c_sample], axis=0)
    pad = (-c_all.shape[0]) % 8
    c_all = jnp.pad(c_all, ((0, pad), (0, 0)))
    mod = _modulation(c_all, w_ada[0], b_ada[0])
    mod_p, mod_s = mod[:nb], mod[nb:nb + ns]
    cpp = seq // CHUNK

    def part(k):
        return mod_p[:, k * d:(k + 1) * d], mod_s[:, k * d:(k + 1) * d]

    (sh1p, sh1s), (sc1p, sc1s), (g1p, g1s), (sh2p, sh2s), (sc2p, sc2s), (g2p, g2s) = [part(k) for k in range(6)]

    w_in_bf = w_in[0].astype(BF16)
    g_pre = g_pre_mix[0].reshape(1, d)
    cos_p, sin_p = _rope_tables(0, seq)
    cos_s, sin_s = _rope_tables(PAST_LEN, dec)
    cos_s, sin_s = jnp.tile(cos_s, (ns, 1)), jnp.tile(sin_s, (ns, 1))

    xp2 = x_prompt.reshape(n_p, d)
    xs2 = x_sample.reshape(n_s, d)
    outs_p = _inproj(xp2, _groups(sh1p, cpp), _groups(sc1p, cpp), g_pre, w_in_bf, cos_p, sin_p, nb, True)
    outs_s = _inproj(xs2, _groups(sh1s, 1), _groups(sc1s, 1), g_pre, w_in_bf, cos_s, sin_s, ns, False)

    table = rel_bias_a[0].astype(F32)
    n_far = A_BAND - 1 - REL_CLIP
    ext = jnp.concatenate([jnp.broadcast_to(table[:, 2 * REL_CLIP:], (A_HEADS, n_far)),
                           jnp.flip(table[:, REL_CLIP - (CHUNK - 1):], axis=1)], axis=1)
    n_ext = ext.shape[1]
    skew = jnp.tile(jnp.pad(ext, ((0, 0), (0, 1))), (1, CHUNK))[:, :CHUNK * n_ext].reshape(A_HEADS, CHUNK, n_ext)
    bias = skew[:, :, CHUNK - 1:CHUNK - 1 + A_BAND]
    bias_pairs = bias.reshape(A_HEADS // 2, 2 * CHUNK, A_BAND) * LOG2_E
    sink_rows = jnp.broadcast_to((sinks_b[0].astype(F32) * LOG2_E).reshape(B_KV_HEADS, B_GROUP, 1),
                                 (B_KV_HEADS, B_GROUP, CHUNK)).reshape(B_KV_HEADS, B_GROUP * CHUNK, 1)
    consts = (bias_pairs, sink_rows, w_branch_a[0].astype(BF16), w_branch_b[0].astype(BF16),
              w_out[0].astype(BF16), g_post_mix[0].reshape(1, d))

    x1_p = _attn_prompt(xp2, outs_p[:7], consts, _groups(g1p, cpp), nb)
    caches = (cache_a_k[0].reshape(ns, A_ROWS, WA), cache_a_v[0].reshape(ns, A_ROWS, WA),
              cache_b_k[0].reshape(ns, WINDOW, WKB), cache_b_v[0].reshape(ns, WINDOW, WKB))
    x1_s = _attn_sample(xs2, outs_s[:7], caches, consts, _groups(g1s, 1))

    h_packed, idx, gates, rank, counts = _router(
        x1_p, x1_s, _groups(sh2p, cpp), _groups(sh2s, 1), _groups(sc2p, cpp), _groups(sc2s, 1),
        g_pre_ffn[0].reshape(1, d),
        w_router[0].T.astype(BF16), b_router[0].astype(F32).reshape(N_EXPERTS, 1))
    n_blocks = (n_all * TOP_K) // EXPERT_ROWS + N_EXPERTS
    cnt = counts[:, 0].astype(I32)
    blocks_e = (cnt + EXPERT_ROWS - 1) // EXPERT_ROWS
    blk_end = jnp.cumsum(blocks_e)
    blk_start = blk_end - blocks_e
    n_active = blk_end[-1:]
    start_row = (blk_start * EXPERT_ROWS).astype(I32).reshape(1, N_EXPERTS)

    dest_flat = _dest_rows(idx, rank, start_row).reshape(-1)
    xs = _dispatch(dest_flat, h_packed, n_blocks * EXPERT_ROWS)
    ys = _experts(blk_start.astype(I32), blk_end.astype(I32), cnt, n_active.astype(I32), xs,
                  we_gate[0], we_up[0], we_down[0])
    y_p, y_s = _combine(dest_flat, ys, x1_p, x1_s, h_packed, gates, ws_gate[0].astype(BF16),
                        ws_up[0].astype(BF16), ws_down[0].astype(BF16), g_post_ffn[0].reshape(1, d),
                        _groups(g2p, cpp), _groups(g2s, 1))

    a_heads = (A_HEADS, HEAD_DIM)
    b_heads = (B_KV_HEADS, HEAD_DIM)
    return (y_p.reshape(nb, seq, d), y_s.reshape(ns, dec, d),
            outs_p[7].reshape(1, nb, A_ROWS, *a_heads), outs_p[8].reshape(1, nb, A_ROWS, *a_heads),
            outs_p[9].reshape(1, nb, WINDOW, *b_heads), outs_p[10].reshape(1, nb, WINDOW, *b_heads),
            outs_s[7].reshape(1, ns, dec, *a_heads), outs_s[8].reshape(1, ns, dec, *a_heads),
            outs_s[9].reshape(1, ns, dec, *b_heads), outs_s[10].reshape(1, ns, dec, *b_heads))
```

```python
import functools

import jax
import jax.numpy as jnp
from jax import lax
from jax.experimental import pallas as pl
from jax.experimental.pallas import tpu as pltpu

F32 = jnp.float32
BF16 = jnp.bfloat16
I32 = jnp.int32
U32 = jnp.uint32
HIGHEST = lax.Precision.HIGHEST
LOG2_E = 1.4426950408889634

D_MODEL = 1024
CHUNK = 64
HEAD_DIM = 64
A_HEADS = 8
A_PREV_CHUNKS = 8
A_ROWS = A_PREV_CHUNKS * CHUNK
A_BAND = A_ROWS + CHUNK
REL_CLIP = 128
B_HEADS = 8
B_KV_HEADS = 2
B_GROUP = B_HEADS // B_KV_HEADS
WINDOW = 128
B_BAND = WINDOW + CHUNK
ROPE_THETA = 10000.0
N_EXPERTS = 256
TOP_K = 8
N_GROUPS = 8
PER_GROUP = N_EXPERTS // N_GROUPS
TOPK_GROUPS = 4
EXPERT_DIM = 256
ROUTED_SCALE = 2.5
EPS = 1e-6
PAST_LEN = 4096

WA = A_HEADS * HEAD_DIM
WB = B_HEADS * HEAD_DIM
WKB = B_KV_HEADS * HEAD_DIM
OFF_QA, OFF_KA, OFF_VA = 0, WA, 2 * WA
OFF_QB = 3 * WA
OFF_KB = OFF_QB + WB
OFF_VB = OFF_KB + WKB
OFF_G = OFF_VB + WKB
N_IN = OFF_G + 2 * D_MODEL

LANES = 128
TOK_TILE = 512
MOE_TILE = 256
EXPERT_ROWS = 256
HALF_D = D_MODEL // 2
VMEM_LIMIT = 56 * 1024 * 1024


def _cparams(sem, vmem=VMEM_LIMIT):
    return pltpu.CompilerParams(dimension_semantics=sem, vmem_limit_bytes=vmem)


def _rms(x):
    return x * lax.rsqrt(jnp.mean(x * x, axis=-1, keepdims=True) + EPS)


def _group_affine(y, mul, add):
    g = mul.shape[0]
    y3 = y.reshape(g, CHUNK, y.shape[-1]) * mul
    if add is not None:
        y3 = y3 + add
    return y3.reshape(g * CHUNK, y.shape[-1])


def _pack_halves(a, b):
    ua = lax.bitcast_convert_type(a.astype(BF16).astype(F32), U32)
    ub = lax.bitcast_convert_type(b.astype(BF16).astype(F32), U32)
    return (ua & jnp.uint32(0xFFFF0000)) | (ub >> 16)


def _unpack_halves(u):
    a = lax.bitcast_convert_type(u & jnp.uint32(0xFFFF0000), F32)
    b = lax.bitcast_convert_type(u << 16, F32)
    return a, b


ROW_SUBLANES = HALF_D // LANES


def _store_packed_rows(ref, x):
    rows = x.shape[0]
    p = _pack_halves(x[:, :HALF_D], x[:, HALF_D:])
    for s in range(ROW_SUBLANES):
        ref[pl.ds(s, rows, stride=ROW_SUBLANES), :] = p[:, s * LANES:(s + 1) * LANES]


def _load_packed_rows(ref, rows, n_valid=None):
    his, los = [], []
    for s in range(ROW_SUBLANES):
        u = ref[pl.ds(s, rows, stride=ROW_SUBLANES), :]
        if n_valid is not None:
            u = jnp.where(lax.broadcasted_iota(I32, u.shape, 0) < n_valid, u, jnp.uint32(0))
        a, b = _unpack_halves(u)
        his.append(a)
        los.append(b)
    return jnp.concatenate(his, axis=1), jnp.concatenate(los, axis=1)


def _mod_kernel(c_ref, w_ref, b_ref, o_ref):
    c = c_ref[...]
    s = c * jax.nn.sigmoid(c)
    o_ref[...] = jnp.dot(s, w_ref[...], precision=HIGHEST, preferred_element_type=F32) + b_ref[...]


def _modulation(c_all, w_ada, b_ada):
    rows = c_all.shape[0]
    n = w_ada.shape[1]
    tn = 512
    return pl.pallas_call(
        _mod_kernel,
        grid=(n // tn,),
        in_specs=[pl.BlockSpec((rows, D_MODEL), lambda j: (0, 0)),
                  pl.BlockSpec((D_MODEL, tn), lambda j: (0, j)),
                  pl.BlockSpec((1, tn), lambda j: (0, j))],
        out_specs=pl.BlockSpec((rows, tn), lambda j: (0, j)),
        out_shape=jax.ShapeDtypeStruct((rows, n), F32),
        compiler_params=_cparams(("arbitrary",)),
        name="modulation",
    )(c_all, w_ada, b_ada.reshape(1, n))


def _rope(x, cos, sin_signed):
    n = x.shape[-1]
    reps = n // LANES
    if reps > 1:
        cos = jnp.concatenate([cos] * reps, axis=1)
        sin_signed = jnp.concatenate([sin_signed] * reps, axis=1)
    lane = lax.broadcasted_iota(I32, x.shape, 1)
    first_half = (lane % HEAD_DIM) < (HEAD_DIM // 2)
    partner = jnp.where(first_half, pltpu.roll(x, n - HEAD_DIM // 2, 1), pltpu.roll(x, HEAD_DIM // 2, 1))
    return x * cos + partner * sin_signed


def _inproj_kernel(prompt_state, tiles_per_seq,
                   x_ref, sh_ref, sc_ref, g_ref, w_ref, cos_ref, sin_ref,
                   qa_ref, ka_ref, va_ref, qb_ref, kb_ref, vb_ref, gt_ref,
                   ska_ref, sva_ref, skb_ref, svb_ref):
    x = x_ref[...]
    h = _group_affine(_rms(x) * g_ref[...], 1.0 + sc_ref[...], sh_ref[...]).astype(BF16)

    def proj(off, width):
        return jnp.dot(h, w_ref[:, off:off + width], preferred_element_type=F32)

    cos = cos_ref[...]
    sin = sin_ref[...]
    scale = HEAD_DIM ** -0.5 * LOG2_E
    qa_ref[...] = (proj(OFF_QA, WA) * scale).astype(BF16)
    ka = proj(OFF_KA, WA)
    va = proj(OFF_VA, WA)
    ka_ref[...] = ka.astype(BF16)
    va_ref[...] = va.astype(BF16)
    qb_ref[...] = (_rope(proj(OFF_QB, WB), cos, sin) * scale).astype(BF16)
    kb = _rope(proj(OFF_KB, WKB), cos, sin)
    vb = proj(OFF_VB, WKB)
    kb_ref[...] = kb.astype(BF16)
    vb_ref[...] = vb.astype(BF16)
    gt_ref[...] = jax.nn.sigmoid(proj(OFF_G, 2 * D_MODEL)).astype(BF16)

    if prompt_state:
        @pl.when(pl.program_id(0) % tiles_per_seq == tiles_per_seq - 1)
        def _():
            ska_ref[...] = ka
            sva_ref[...] = va
            skb_ref[...] = kb[TOK_TILE - WINDOW:, :]
            svb_ref[...] = vb[TOK_TILE - WINDOW:, :]
    else:
        ska_ref[...] = ka
        sva_ref[...] = va
        skb_ref[...] = kb
        svb_ref[...] = vb


def _inproj(x2d, shift_g, scale_g, g_pre, w_in_bf, cos_tab, sin_tab, n_seq, prompt_state):
    n = x2d.shape[0]
    nt = n // TOK_TILE
    tiles_per_seq = nt // n_seq if prompt_state else 1
    tab_tiles = cos_tab.shape[0] // TOK_TILE
    gpt = TOK_TILE // CHUNK
    row = lambda i: (i, 0)
    grp = lambda i: (i, 0, 0)
    if prompt_state:
        st_shapes = [jax.ShapeDtypeStruct((n_seq, A_ROWS, WA), F32)] * 2 + \
                    [jax.ShapeDtypeStruct((n_seq, WINDOW, WKB), F32)] * 2
        st_specs = [pl.BlockSpec((None, A_ROWS, WA), lambda i: (i // tiles_per_seq, 0, 0))] * 2 + \
                   [pl.BlockSpec((None, WINDOW, WKB), lambda i: (i // tiles_per_seq, 0, 0))] * 2
    else:
        st_shapes = [jax.ShapeDtypeStruct((n, WA), F32)] * 2 + [jax.ShapeDtypeStruct((n, WKB), F32)] * 2
        st_specs = [pl.BlockSpec((TOK_TILE, WA), row)] * 2 + [pl.BlockSpec((TOK_TILE, WKB), row)] * 2
    out_shapes = [jax.ShapeDtypeStruct((n, WA), BF16)] * 4 + [jax.ShapeDtypeStruct((n, WKB), BF16)] * 2 + \
                 [jax.ShapeDtypeStruct((n, 2 * D_MODEL), BF16)] + st_shapes
    out_specs = [pl.BlockSpec((TOK_TILE, WA), row)] * 4 + [pl.BlockSpec((TOK_TILE, WKB), row)] * 2 + \
                [pl.BlockSpec((TOK_TILE, 2 * D_MODEL), row)] + st_specs
    return pl.pallas_call(
        functools.partial(_inproj_kernel, prompt_state, tiles_per_seq),
        grid=(nt,),
        in_specs=[pl.BlockSpec((TOK_TILE, D_MODEL), row),
                  pl.BlockSpec((gpt, 1, D_MODEL), grp),
                  pl.BlockSpec((gpt, 1, D_MODEL), grp),
                  pl.BlockSpec((1, D_MODEL), lambda i: (0, 0)),
                  pl.BlockSpec((D_MODEL, N_IN), lambda i: (0, 0)),
                  pl.BlockSpec((TOK_TILE, LANES), lambda i: (i % tab_tiles, 0)),
                  pl.BlockSpec((TOK_TILE, LANES), lambda i: (i % tab_tiles, 0))],
        out_specs=out_specs,
        out_shape=out_shapes,
        compiler_params=_cparams(("arbitrary",)),
        name="inproj_prompt" if prompt_state else "inproj_sample",
    )(x2d, shift_g, scale_g, g_pre, w_in_bf, cos_tab, sin_tab)


def _attn_kernel(n_chunks, mask_first,
                 x_ref, qa_ref, qb_ref, gt_ref,
                 kap_ref, kac_ref, vap_ref, vac_ref, kbp_ref, kbc_ref, vbp_ref, vbc_ref,
                 bias_ref, sink_ref, wba_ref, wbb_ref, wout_ref, gpost_ref, gate1_ref,
                 o_ref,
                 ka_s, va_s, kb_s, vb_s, oa_s, ob_s):
    rows = n_chunks * CHUNK
    pb = kbp_ref.shape[0]
    ka_s[0:A_ROWS, :] = kap_ref[...].astype(BF16)
    va_s[0:A_ROWS, :] = vap_ref[...].astype(BF16)
    ka_s[A_ROWS:A_ROWS + rows, :] = kac_ref[...]
    va_s[A_ROWS:A_ROWS + rows, :] = vac_ref[...]
    kb_s[0:WINDOW, :] = kbp_ref[pb - WINDOW:pb, :].astype(BF16)
    vb_s[0:WINDOW, :] = vbp_ref[pb - WINDOW:pb, :].astype(BF16)
    kb_s[WINDOW:WINDOW + rows, :] = kbc_ref[...]
    vb_s[WINDOW:WINDOW + rows, :] = vbc_ref[...]

    lane_q = lax.broadcasted_iota(I32, (CHUNK, LANES), 1)
    nt_dims = (((1,), (1,)), ((), ()))

    def chunk_body(masked, c, carry):
        c0 = pl.multiple_of(c * CHUNK, CHUNK)
        if masked:
            valid_a = c0 + lax.broadcasted_iota(I32, (1, A_BAND), 1) >= A_ROWS
            valid_b = c0 + lax.broadcasted_iota(I32, (1, B_BAND), 1) >= WINDOW

        scores = []
        for p in range(A_HEADS // 2):
            cols = slice(p * LANES, (p + 1) * LANES)
            q = qa_ref[pl.ds(c0, CHUNK), cols].astype(F32)
            qs = jnp.concatenate([jnp.where(lane_q < HEAD_DIM, q, 0.0),
                                  jnp.where(lane_q >= HEAD_DIM, q, 0.0)], axis=0).astype(BF16)
            k = ka_s[pl.ds(c0, A_BAND), cols]
            s = lax.dot_general(qs, k, nt_dims, preferred_element_type=F32) + bias_ref[p]
            if masked:
                s = jnp.where(valid_a, s, -jnp.inf)
            scores.append(s)
        for g in range(B_KV_HEADS):
            parts = []
            for r in range(B_GROUP):
                head = g * B_GROUP + r
                t, half = head // 2, head % 2
                q = qb_ref[pl.ds(c0, CHUNK), t * LANES:(t + 1) * LANES].astype(F32)
                if half != g:
                    q = pltpu.roll(q, HEAD_DIM, 1)
                in_g = (lane_q >= HEAD_DIM) if g else (lane_q < HEAD_DIM)
                parts.append(jnp.where(in_g, q, 0.0))
            qs = jnp.concatenate(parts, axis=0).astype(BF16)
            k = kb_s[pl.ds(c0, B_BAND), :]
            s = lax.dot_general(qs, k, nt_dims, preferred_element_type=F32)
            if masked:
                s = jnp.where(valid_b, s, -jnp.inf)
            scores.append(s)

        numer, denom = [], []
        for n, s in enumerate(scores):
            m = jnp.max(s, axis=1, keepdims=True)
            if n >= A_HEADS // 2:
                sk = sink_ref[n - A_HEADS // 2]
                m = jnp.maximum(m, sk)
            e = jnp.exp2(s - m)
            l = jnp.sum(e, axis=1, keepdims=True)
            if n >= A_HEADS // 2:
                l = l + jnp.exp2(sk - m)
            numer.append(e.astype(BF16))
            denom.append(l)

        outs = []
        for n, e in enumerate(numer):
            if n < A_HEADS // 2:
                v = va_s[pl.ds(c0, A_BAND), n * LANES:(n + 1) * LANES]
            else:
                v = vb_s[pl.ds(c0, B_BAND), :]
            outs.append(jnp.dot(e, v, preferred_element_type=F32) / denom[n])

        for p in range(A_HEADS // 2):
            o = outs[p]
            oa_s[pl.ds(c0, CHUNK), p * LANES:(p + 1) * LANES] = jnp.where(
                lane_q < HEAD_DIM, o[:CHUNK], o[CHUNK:]).astype(BF16)
        for g in range(B_KV_HEADS):
            o = outs[A_HEADS // 2 + g]
            for s2 in range(B_GROUP // 2):
                o_even = o[(2 * s2) * CHUNK:(2 * s2 + 1) * CHUNK]
                o_odd = o[(2 * s2 + 1) * CHUNK:(2 * s2 + 2) * CHUNK]
                if g == 0:
                    tile = jnp.where(lane_q < HEAD_DIM, o_even, pltpu.roll(o_odd, HEAD_DIM, 1))
                else:
                    tile = jnp.where(lane_q < HEAD_DIM, pltpu.roll(o_even, HEAD_DIM, 1), o_odd)
                t = g * (B_GROUP // 2) + s2
                ob_s[pl.ds(c0, CHUNK), t * LANES:(t + 1) * LANES] = tile.astype(BF16)
        return carry

    def all_chunks(masked):
        lax.fori_loop(0, n_chunks, functools.partial(chunk_body, masked), 0, unroll=4 if n_chunks > 1 else 1)

    if mask_first:
        pl.when(pl.program_id(1) == 0)(lambda: all_chunks(True))
        pl.when(pl.program_id(1) > 0)(lambda: all_chunks(False))
    else:
        all_chunks(False)

    za = jnp.dot(oa_s[...], wba_ref[...], preferred_element_type=F32)
    zb = jnp.dot(ob_s[...], wbb_ref[...], preferred_element_type=F32)
    merged = gt_ref[:, :D_MODEL].astype(F32) * za + gt_ref[:, D_MODEL:].astype(F32) * zb
    mo = jnp.dot(merged.astype(BF16), wout_ref[...], preferred_element_type=F32)
    o_ref[...] = x_ref[...] + _group_affine(_rms(mo) * gpost_ref[...], gate1_ref[...], None)


def _attn_scratch(rows):
    return [pltpu.VMEM((A_ROWS + rows, WA), BF16), pltpu.VMEM((A_ROWS + rows, WA), BF16),
            pltpu.VMEM((WINDOW + rows, WKB), BF16), pltpu.VMEM((WINDOW + rows, WKB), BF16),
            pltpu.VMEM((rows, WA), BF16), pltpu.VMEM((rows, WB), BF16)]


def _const_specs(grid_rank):
    z2 = (lambda b, j: (0, 0)) if grid_rank == 2 else (lambda b: (0, 0))
    z3 = (lambda b, j: (0, 0, 0)) if grid_rank == 2 else (lambda b: (0, 0, 0))
    return [pl.BlockSpec((A_HEADS // 2, 2 * CHUNK, A_BAND), z3),
            pl.BlockSpec((B_KV_HEADS, B_GROUP * CHUNK, 1), z3),
            pl.BlockSpec((WA, D_MODEL), z2),
            pl.BlockSpec((WB, D_MODEL), z2),
            pl.BlockSpec((D_MODEL, D_MODEL), z2),
            pl.BlockSpec((1, D_MODEL), z2)]


def _attn_prompt(x2d, proj, consts, gate1_g, n_seq):
    qa, ka, va, qb, kb, vb, gt = proj
    n = x2d.shape[0]
    tps = n // n_seq // TOK_TILE
    gpt = TOK_TILE // CHUNK
    cur = lambda b, j: (b * tps + j, 0)
    prev = lambda b, j: (b * tps + jnp.maximum(j - 1, 0), 0)
    return pl.pallas_call(
        functools.partial(_attn_kernel, TOK_TILE // CHUNK, True),
        grid=(n_seq, tps),
        in_specs=[pl.BlockSpec((TOK_TILE, D_MODEL), cur),
                  pl.BlockSpec((TOK_TILE, WA), cur),
                  pl.BlockSpec((TOK_TILE, WB), cur),
                  pl.BlockSpec((TOK_TILE, 2 * D_MODEL), cur),
                  pl.BlockSpec((TOK_TILE, WA), prev), pl.BlockSpec((TOK_TILE, WA), cur),
                  pl.BlockSpec((TOK_TILE, WA), prev), pl.BlockSpec((TOK_TILE, WA), cur),
                  pl.BlockSpec((TOK_TILE, WKB), prev), pl.BlockSpec((TOK_TILE, WKB), cur),
                  pl.BlockSpec((TOK_TILE, WKB), prev), pl.BlockSpec((TOK_TILE, WKB), cur)]
                 + _const_specs(2)
                 + [pl.BlockSpec((gpt, 1, D_MODEL), lambda b, j: (b * tps + j, 0, 0))],
        out_specs=pl.BlockSpec((TOK_TILE, D_MODEL), cur),
        out_shape=jax.ShapeDtypeStruct((n, D_MODEL), F32),
        scratch_shapes=_attn_scratch(TOK_TILE),
        compiler_params=_cparams(("arbitrary", "arbitrary")),
        name="attn_prompt",
    )(x2d, qa, qb, gt, ka, ka, va, va, kb, kb, vb, vb, *consts, gate1_g)


def _attn_sample(x2d, proj, caches, consts, gate1_g):
    qa, ka, va, qb, kb, vb, gt = proj
    cak, cav, cbk, cbv = caches
    n_seq = cak.shape[0]
    cur = lambda b: (b, 0)
    cache = lambda b: (b, 0, 0)
    return pl.pallas_call(
        functools.partial(_attn_kernel, 1, False),
        grid=(n_seq,),
        in_specs=[pl.BlockSpec((CHUNK, D_MODEL), cur),
                  pl.BlockSpec((CHUNK, WA), cur),
                  pl.BlockSpec((CHUNK, WB), cur),
                  pl.BlockSpec((CHUNK, 2 * D_MODEL), cur),
                  pl.BlockSpec((None, A_ROWS, WA), cache), pl.BlockSpec((CHUNK, WA), cur),
                  pl.BlockSpec((None, A_ROWS, WA), cache), pl.BlockSpec((CHUNK, WA), cur),
                  pl.BlockSpec((None, WINDOW, WKB), cache), pl.BlockSpec((CHUNK, WKB), cur),
                  pl.BlockSpec((None, WINDOW, WKB), cache), pl.BlockSpec((CHUNK, WKB), cur)]
                 + _const_specs(1)
                 + [pl.BlockSpec((1, 1, D_MODEL), lambda b: (b, 0, 0))],
        out_specs=pl.BlockSpec((CHUNK, D_MODEL), cur),
        out_shape=jax.ShapeDtypeStruct(x2d.shape, F32),
        scratch_shapes=_attn_scratch(CHUNK),
        compiler_params=_cparams(("arbitrary",)),
        name="attn_sample",
    )(x2d, qa, qb, gt, cak, ka, cav, va, cbk, kb, cbv, vb, *consts, gate1_g)


def _two_part_tile(n_first_tiles, first_ref, second_ref):
    return jnp.where(pl.program_id(0) < n_first_tiles, first_ref[...], second_ref[...])


def _router_kernel(n_prompt_tiles, xp_ref, xs_ref, shp_ref, shs_ref, scp_ref, scs_ref, g_ref, wrt_ref, br_ref,
                   h_ref, idx_ref, gate_ref, rank_ref, cnt_ref, carry):
    i = pl.program_id(0)

    @pl.when(i == 0)
    def _():
        carry[...] = jnp.zeros_like(carry)

    tm = xp_ref.shape[0]
    x = _two_part_tile(n_prompt_tiles, xp_ref, xs_ref)
    h = _group_affine(_rms(x) * g_ref[...], 1.0 + _two_part_tile(n_prompt_tiles, scp_ref, scs_ref),
                      _two_part_tile(n_prompt_tiles, shp_ref, shs_ref))
    _store_packed_rows(h_ref, h)

    logits = lax.dot_general(wrt_ref[...], h.astype(BF16), (((1,), (1,)), ((), ())),
                             preferred_element_type=F32)
    scores = jax.nn.sigmoid(logits)
    biased = scores + br_ref[...]
    neg = -jnp.inf

    sub = lax.broadcasted_iota(I32, (PER_GROUP, tm), 0).astype(F32)
    gs_rows = []
    for g in range(N_GROUPS):
        xg = biased[g * PER_GROUP:(g + 1) * PER_GROUP]
        m1 = jnp.max(xg, axis=0, keepdims=True)
        i1 = jnp.min(jnp.where(xg == m1, sub, float(PER_GROUP)), axis=0, keepdims=True)
        m2 = jnp.max(jnp.where(sub == i1, neg, xg), axis=0, keepdims=True)
        gs_rows.append(m1 + m2)
    gs = jnp.concatenate(gs_rows, axis=0)

    giota = lax.broadcasted_iota(I32, (N_GROUPS, tm), 0).astype(F32)
    keep = jnp.zeros((N_GROUPS, tm), F32)
    for _ in range(TOPK_GROUPS):
        m = jnp.max(gs, axis=0, keepdims=True)
        gi = jnp.min(jnp.where(gs == m, giota, float(N_GROUPS)), axis=0, keepdims=True)
        hit = giota == gi
        keep = jnp.where(hit, 1.0, keep)
        gs = jnp.where(hit, neg, gs)
    cand = jnp.concatenate(
        [jnp.where(keep[g:g + 1] > 0.0, biased[g * PER_GROUP:(g + 1) * PER_GROUP], neg)
         for g in range(N_GROUPS)], axis=0)

    eiota = lax.broadcasted_iota(I32, (N_EXPERTS, tm), 0).astype(F32)
    idx_rows, gate_rows = [], []
    chosen = jnp.zeros((N_EXPERTS, tm), F32)
    for _ in range(TOP_K):
        m = jnp.max(cand, axis=0, keepdims=True)
        ei = jnp.min(jnp.where(cand == m, eiota, float(N_EXPERTS)), axis=0, keepdims=True)
        sel = eiota == ei
        gate_rows.append(jnp.sum(jnp.where(sel, scores, 0.0), axis=0, keepdims=True))
        idx_rows.append(ei)
        chosen = jnp.where(sel, 1.0, chosen)
        cand = jnp.where(sel, neg, cand)
    gates = jnp.concatenate(gate_rows, axis=0)
    gates = gates / jnp.sum(gates, axis=0, keepdims=True) * ROUTED_SCALE
    gate_ref[...] = gates
    idx_ref[...] = jnp.concatenate(idx_rows, axis=0).astype(I32)

    r_i = lax.broadcasted_iota(I32, (tm, tm), 0)
    c_i = lax.broadcasted_iota(I32, (tm, tm), 1)
    upper = jnp.where(r_i < c_i, 1.0, 0.0).astype(BF16)
    before = jnp.dot(chosen.astype(BF16), upper, preferred_element_type=F32) + carry[...]
    rank_rows = [jnp.sum(jnp.where(eiota == idx_rows[k], before, 0.0), axis=0, keepdims=True)
                 for k in range(TOP_K)]
    rank_ref[...] = jnp.concatenate(rank_rows, axis=0).astype(I32)
    total = carry[...] + jnp.sum(chosen, axis=1, keepdims=True)
    carry[...] = total
    cnt_ref[...] = total


def _router(x1_p, x1_s, shift_gp, shift_gs, scale_gp, scale_gs, g_pre, w_router_t, b_router_col):
    n = x1_p.shape[0] + x1_s.shape[0]
    nt = n // MOE_TILE
    npt = x1_p.shape[0] // MOE_TILE
    gpt = MOE_TILE // CHUNK
    lane_blk = lambda i: (0, i)
    return pl.pallas_call(
        functools.partial(_router_kernel, npt),
        grid=(nt,),
        in_specs=[pl.BlockSpec((MOE_TILE, D_MODEL), lambda i: (jnp.minimum(i, npt - 1), 0)),
                  pl.BlockSpec((MOE_TILE, D_MODEL), lambda i: (jnp.maximum(i - npt, 0), 0)),
                  pl.BlockSpec((gpt, 1, D_MODEL), lambda i: (jnp.minimum(i, npt - 1), 0, 0)),
                  pl.BlockSpec((gpt, 1, D_MODEL), lambda i: (jnp.maximum(i - npt, 0), 0, 0)),
                  pl.BlockSpec((gpt, 1, D_MODEL), lambda i: (jnp.minimum(i, npt - 1), 0, 0)),
                  pl.BlockSpec((gpt, 1, D_MODEL), lambda i: (jnp.maximum(i - npt, 0), 0, 0)),
                  pl.BlockSpec((1, D_MODEL), lambda i: (0, 0)),
                  pl.BlockSpec((N_EXPERTS, D_MODEL), lambda i: (0, 0)),
                  pl.BlockSpec((N_EXPERTS, 1), lambda i: (0, 0))],
        out_specs=[pl.BlockSpec((MOE_TILE * ROW_SUBLANES, LANES), lambda i: (i, 0)),
                   pl.BlockSpec((TOP_K, MOE_TILE), lane_blk),
                   pl.BlockSpec((TOP_K, MOE_TILE), lane_blk),
                   pl.BlockSpec((TOP_K, MOE_TILE), lane_blk),
                   pl.BlockSpec((N_EXPERTS, 1), lambda i: (0, 0))],
        out_shape=[jax.ShapeDtypeStruct((n * ROW_SUBLANES, LANES), U32),
                   jax.ShapeDtypeStruct((TOP_K, n), I32),
                   jax.ShapeDtypeStruct((TOP_K, n), F32),
                   jax.ShapeDtypeStruct((TOP_K, n), I32),
                   jax.ShapeDtypeStruct((N_EXPERTS, 1), F32)],
        scratch_shapes=[pltpu.VMEM((N_EXPERTS, 1), F32)],
        compiler_params=_cparams(("arbitrary",)),
        name="router",
    )(x1_p, x1_s, shift_gp, shift_gs, scale_gp, scale_gs, g_pre, w_router_t, b_router_col)


def _dest_kernel(idx_ref, rank_ref, start_ref, o_ref):
    tabs = [jnp.broadcast_to(start_ref[:, t * LANES:(t + 1) * LANES], (TOP_K, LANES))
            for t in range(N_EXPERTS // LANES)]
    for sub in range(o_ref.shape[0]):
        parts = []
        for c in range(MOE_TILE // LANES):
            cols = slice(sub * MOE_TILE + c * LANES, sub * MOE_TILE + (c + 1) * LANES)
            idx = idx_ref[:, cols]
            lane = idx % LANES
            val = jnp.take_along_axis(tabs[0], lane, axis=1)
            for t in range(1, len(tabs)):
                val = jnp.where(idx >= t * LANES, jnp.take_along_axis(tabs[t], lane, axis=1), val)
            parts.append(val + rank_ref[:, cols])
        o_ref[sub] = jnp.concatenate(parts, axis=1)


def _dest_rows(idx, rank, start_row):
    n = idx.shape[1]
    nt = n // MOE_TILE
    per_step = next(c for c in (10, 8, 5, 4, 2, 1) if nt % c == 0)
    return pl.pallas_call(
        _dest_kernel,
        grid=(nt // per_step,),
        in_specs=[pl.BlockSpec((TOP_K, per_step * MOE_TILE), lambda i: (0, i)),
                  pl.BlockSpec((TOP_K, per_step * MOE_TILE), lambda i: (0, i)),
                  pl.BlockSpec((1, N_EXPERTS), lambda i: (0, 0))],
        out_specs=pl.BlockSpec((per_step, TOP_K, MOE_TILE), lambda i: (i, 0, 0)),
        out_shape=jax.ShapeDtypeStruct((nt, TOP_K, MOE_TILE), I32),
        compiler_params=_cparams(("arbitrary",)),
        name="dest_rows",
    )(idx, rank, start_row)


DEST_PER_TILE = TOP_K * MOE_TILE


def _row_slice(ref, row):
    return ref.at[pl.ds(pl.multiple_of(row * ROW_SUBLANES, ROW_SUBLANES), ROW_SUBLANES)]


TILE_SUBLANES = MOE_TILE * ROW_SUBLANES


def _dispatch_kernel(dest_hbm, h_hbm, xs_hbm, dest_s, hbuf, dsem, hsem, rsem):
    i = pl.program_id(0)
    nt = pl.num_programs(0)

    def load(tile):
        dst = dest_s.at[pl.ds((tile % 2) * DEST_PER_TILE, DEST_PER_TILE)]
        rows = h_hbm.at[pl.ds(pl.multiple_of(tile * TILE_SUBLANES, TILE_SUBLANES), TILE_SUBLANES)]
        return (pltpu.make_async_copy(dest_hbm.at[pl.ds(tile * DEST_PER_TILE, DEST_PER_TILE)], dst, dsem.at[tile % 2]),
                pltpu.make_async_copy(rows, hbuf.at[tile % 3], hsem.at[tile % 3]))

    def drain(tile):
        for k in range(TOP_K):
            pltpu.make_async_copy(hbuf.at[tile % 3], xs_hbm.at[pl.ds(0, TILE_SUBLANES)],
                                  rsem.at[(tile % 2) * TOP_K + k]).wait()

    @pl.when(i == 0)
    def _():
        for cp in load(0):
            cp.start()

    @pl.when(i + 1 < nt)
    def _():
        for cp in load(i + 1):
            cp.start()

    for cp in load(i):
        cp.wait()
    base = (i % 2) * DEST_PER_TILE
    src_buf = hbuf.at[i % 3]
    sem0 = (i % 2) * TOP_K

    def body(t, carry):
        src = _row_slice(src_buf, t)
        for k in range(TOP_K):
            d = dest_s[base + k * MOE_TILE + t]
            pltpu.make_async_copy(src, _row_slice(xs_hbm, d), rsem.at[sem0 + k]).start(priority=k % 2)
        return carry

    lax.fori_loop(0, MOE_TILE, body, 0, unroll=4)

    @pl.when(i >= 1)
    def _():
        drain(i - 1)

    @pl.when(i == nt - 1)
    def _():
        drain(i)


def _dispatch(dest_flat, h_packed, n_rows):
    nt = h_packed.shape[0] // TILE_SUBLANES
    return pl.pallas_call(
        _dispatch_kernel,
        grid=(nt,),
        in_specs=[pl.BlockSpec(memory_space=pl.ANY),
                  pl.BlockSpec(memory_space=pl.ANY)],
        out_specs=pl.BlockSpec(memory_space=pl.ANY),
        out_shape=jax.ShapeDtypeStruct((n_rows * ROW_SUBLANES, LANES), U32),
        scratch_shapes=[pltpu.SMEM((2 * DEST_PER_TILE,), I32),
                        pltpu.VMEM((3, TILE_SUBLANES, LANES), U32),
                        pltpu.SemaphoreType.DMA((2,)),
                        pltpu.SemaphoreType.DMA((3,)),
                        pltpu.SemaphoreType.DMA((2 * TOP_K,))],
        compiler_params=_cparams(("arbitrary",)),
        name="dispatch",
    )(dest_flat, h_packed)


BLOCK_SUBLANES = EXPERT_ROWS * ROW_SUBLANES
EXPERT_AHEAD = 4
EXPERT_SLOTS = EXPERT_AHEAD + 2


def _expert_kernel(first_ref, last_ref, cnt_ref, na_ref, xs_hbm, wg_ref, wu_ref, wd_ref, ys_hbm,
                   wg_s, wu_s, wd_s, xbuf, ybuf, isem, osem):
    e = pl.program_id(0)
    n_active = na_ref[0]
    first, last = first_ref[e], last_ref[e]

    def block_rows(ref, g):
        return ref.at[pl.ds(pl.multiple_of(g * BLOCK_SUBLANES, BLOCK_SUBLANES), BLOCK_SUBLANES)]

    def fetch(g, slot):
        return pltpu.make_async_copy(block_rows(xs_hbm, g), xbuf.at[slot], isem.at[slot])

    def flush(g, slot):
        return pltpu.make_async_copy(ybuf.at[slot], block_rows(ys_hbm, g), osem.at[slot])

    @pl.when(e == 0)
    def _():
        for g0 in range(EXPERT_AHEAD):
            @pl.when(g0 < n_active)
            def _():
                fetch(g0, g0).start()

    @pl.when(last > first)
    def _():
        wg_s[...] = wg_ref[...].astype(BF16)
        wu_s[...] = wu_ref[...].astype(BF16)
        wd_s[...] = wd_ref[...].astype(BF16)

    def enter(g):
        slot = g % EXPERT_SLOTS
        fetch(g, slot).wait()

        @pl.when(g + EXPERT_AHEAD < n_active)
        def _():
            fetch(g + EXPERT_AHEAD, (g + EXPERT_AHEAD) % EXPERT_SLOTS).start()

        @pl.when(g >= EXPERT_SLOTS)
        def _():
            flush(g - EXPERT_SLOTS, slot).wait()
        return slot

    def swiglu(halves):
        xa = jnp.concatenate([h[0] for h in halves], axis=0).astype(BF16)
        xb = jnp.concatenate([h[1] for h in halves], axis=0).astype(BF16)

        def up(w_s):
            return (jnp.dot(xa, w_s[:HALF_D, :], preferred_element_type=F32)
                    + jnp.dot(xb, w_s[HALF_D:, :], preferred_element_type=F32))

        gate = up(wg_s)
        a = (gate * jax.nn.sigmoid(gate) * up(wu_s)).astype(BF16)
        return jnp.dot(a, wd_s[...], preferred_element_type=F32)

    def n_valid(g):
        return cnt_ref[e] - (g - first) * EXPERT_ROWS

    def pair_body(p, carry):
        g = first + 2 * p
        s0 = enter(g)
        s1 = enter(g + 1)
        y = swiglu([_load_packed_rows(xbuf.at[s0], EXPERT_ROWS),
                    _load_packed_rows(xbuf.at[s1], EXPERT_ROWS, n_valid(g + 1))])
        _store_packed_rows(ybuf.at[s0], y[:EXPERT_ROWS])
        _store_packed_rows(ybuf.at[s1], y[EXPERT_ROWS:])
        flush(g, s0).start()
        flush(g + 1, s1).start()
        return carry

    n_pairs = lax.shift_right_logical(last - first, 1)
    lax.fori_loop(0, n_pairs, pair_body, 0)

    @pl.when((last - first) % 2 == 1)
    def _():
        g = last - 1
        s0 = enter(g)
        y = swiglu([_load_packed_rows(xbuf.at[s0], EXPERT_ROWS, n_valid(g))])
        _store_packed_rows(ybuf.at[s0], y)
        flush(g, s0).start()

    @pl.when(e == pl.num_programs(0) - 1)
    def _():
        for back in range(EXPERT_SLOTS, 0, -1):
            @pl.when(n_active >= back)
            def _():
                flush(n_active - back, (n_active - back) % EXPERT_SLOTS).wait()


def _experts(blk_first, blk_last, counts, n_active, xs, we_gate, we_up, we_down):
    wmap = lambda e, *_: (e, 0, 0)
    return pl.pallas_call(
        _expert_kernel,
        grid_spec=pltpu.PrefetchScalarGridSpec(
            num_scalar_prefetch=4,
            grid=(N_EXPERTS,),
            in_specs=[pl.BlockSpec(memory_space=pl.ANY),
                      pl.BlockSpec((None, D_MODEL, EXPERT_DIM), wmap),
                      pl.BlockSpec((None, D_MODEL, EXPERT_DIM), wmap),
                      pl.BlockSpec((None, EXPERT_DIM, D_MODEL), wmap)],
            out_specs=pl.BlockSpec(memory_space=pl.ANY),
            scratch_shapes=[pltpu.VMEM((D_MODEL, EXPERT_DIM), BF16),
                            pltpu.VMEM((D_MODEL, EXPERT_DIM), BF16),
                            pltpu.VMEM((EXPERT_DIM, D_MODEL), BF16),
                            pltpu.VMEM((EXPERT_SLOTS, BLOCK_SUBLANES, LANES), U32),
                            pltpu.VMEM((EXPERT_SLOTS, BLOCK_SUBLANES, LANES), U32),
                            pltpu.SemaphoreType.DMA((EXPERT_SLOTS,)),
                            pltpu.SemaphoreType.DMA((EXPERT_SLOTS,))]),
        out_shape=jax.ShapeDtypeStruct(xs.shape, U32),
        compiler_params=_cparams(("arbitrary",)),
        name="experts",
    )(blk_first, blk_last, counts, n_active, xs, we_gate, we_up, we_down)


COMBINE_ROWS = 8


def _combine_kernel(n_prompt_tiles,
                    dest_hbm, ys_hbm, xp_ref, xs_ref, h_ref, gate_ref, wsg_ref, wsu_ref, wsd_ref, gpost_ref,
                    gate2p_ref, gate2s_ref,
                    yp_ref, ysm_ref, dest_s, buf, shared_s, gcol_s, gate2_s, y_s, dsem, rsem):
    i = pl.program_id(0)
    nt = pl.num_programs(0)

    def dest_copy(tile):
        slot = tile % 3
        return pltpu.make_async_copy(dest_hbm.at[pl.ds(tile * DEST_PER_TILE, DEST_PER_TILE)],
                                     dest_s.at[pl.ds(slot * DEST_PER_TILE, DEST_PER_TILE)], dsem.at[slot])

    def gather_token(base, bset, t):
        dst0 = bset * TILE_SUBLANES + t * ROW_SUBLANES
        for k in range(TOP_K):
            d = dest_s[base + t + k * MOE_TILE]
            dst = buf.at[pl.ds(pl.multiple_of(dst0 + k * TILE_SUBLANES, ROW_SUBLANES), ROW_SUBLANES)]
            pltpu.make_async_copy(_row_slice(ys_hbm, d), dst, rsem.at[bset + k]).start(priority=k % 2)

    def wait_rows(bset):
        for k in range(TOP_K):
            pltpu.make_async_copy(ys_hbm.at[pl.ds(0, TILE_SUBLANES)], buf.at[pl.ds(0, TILE_SUBLANES)],
                                  rsem.at[bset + k]).wait()

    @pl.when(i == 0)
    def _():
        dest_copy(0).start()

        @pl.when(nt > 1)
        def _():
            dest_copy(1).start()

        dest_copy(0).wait()

        def first_tile(t, carry):
            gather_token(0, 0, t)
            return carry

        lax.fori_loop(0, MOE_TILE, first_tile, 0, unroll=4)

    @pl.when(i + 2 < nt)
    def _():
        dest_copy(i + 2).start()

    @pl.when(i + 1 < nt)
    def _():
        dest_copy(i + 1).wait()

    nxt = jnp.minimum(i + 1, nt - 1)
    base_next = (nxt % 3) * DEST_PER_TILE
    bset = (i % 2) * TOP_K
    bset_next = ((i + 1) % 2) * TOP_K

    ha, hb = _load_packed_rows(h_ref, MOE_TILE)
    ha = ha.astype(BF16)
    hb = hb.astype(BF16)

    def up(w_ref):
        return (jnp.dot(ha, w_ref[:HALF_D, :], preferred_element_type=F32)
                + jnp.dot(hb, w_ref[HALF_D:, :], preferred_element_type=F32))

    g = up(wsg_ref)
    a = (g * jax.nn.sigmoid(g) * up(wsu_ref)).astype(BF16)
    shared_s[...] = jnp.dot(a, wsd_ref[...], preferred_element_type=F32)

    gpad = jnp.concatenate([gate_ref[...], jnp.zeros((LANES - TOP_K, MOE_TILE), F32)], axis=0)
    gcol_s[...] = gpad.T

    wait_rows(bset)
    in_first = i < n_prompt_tiles
    gate2_s[...] = _two_part_tile(n_prompt_tiles, gate2p_ref, gate2s_ref)

    def chunk(j, carry):
        r0 = pl.multiple_of(j * COMBINE_ROWS, COMBINE_ROWS)
        rows = pl.ds(r0, COMBINE_ROWS)
        acc_hi = [jnp.zeros((COMBINE_ROWS, LANES), F32) for _ in range(ROW_SUBLANES)]
        acc_lo = [jnp.zeros((COMBINE_ROWS, LANES), F32) for _ in range(ROW_SUBLANES)]
        src0 = bset * TILE_SUBLANES + r0 * ROW_SUBLANES
        for k in range(TOP_K):
            gk = jnp.broadcast_to(gcol_s[rows, k:k + 1], (COMBINE_ROWS, LANES))
            for s in range(ROW_SUBLANES):
                hi, lo = _unpack_halves(
                    buf[pl.ds(src0 + k * TILE_SUBLANES + s, COMBINE_ROWS, stride=ROW_SUBLANES), :])
                acc_hi[s] = acc_hi[s] + hi * gk
                acc_lo[s] = acc_lo[s] + lo * gk
        f = jnp.concatenate(acc_hi + acc_lo, axis=1) + shared_s[rows, :]
        x = jnp.where(in_first, xp_ref[rows, :], xs_ref[rows, :])
        gate2 = gate2_s[lax.shift_right_logical(j * COMBINE_ROWS, CHUNK.bit_length() - 1)]
        y = x + _rms(f) * gpost_ref[...] * gate2
        for t in range(COMBINE_ROWS):
            gather_token(base_next, bset_next, r0 + t)
        y_s[rows, :] = y
        return carry

    lax.fori_loop(0, MOE_TILE // COMBINE_ROWS, chunk, 0, unroll=2)

    @pl.when(in_first)
    def _():
        yp_ref[...] = y_s[...]

    @pl.when(jnp.logical_not(in_first))
    def _():
        ysm_ref[...] = y_s[...]

    @pl.when(i == nt - 1)
    def _():
        wait_rows(bset_next)


def _combine(dest_flat, ys, x1_p, x1_s, h_packed, gates, ws_gate, ws_up, ws_down, g_post, gate2_gp, gate2_gs):
    n_prompt = x1_p.shape[0]
    n = n_prompt + x1_s.shape[0]
    nt = n // MOE_TILE
    npt = n_prompt // MOE_TILE
    gpt = MOE_TILE // CHUNK
    z2 = lambda i: (0, 0)
    return pl.pallas_call(
        functools.partial(_combine_kernel, npt),
        grid=(nt,),
        in_specs=[pl.BlockSpec(memory_space=pl.ANY),
                  pl.BlockSpec(memory_space=pl.ANY),
                  pl.BlockSpec((MOE_TILE, D_MODEL), lambda i: (jnp.minimum(i, npt - 1), 0)),
                  pl.BlockSpec((MOE_TILE, D_MODEL), lambda i: (jnp.maximum(i - npt, 0), 0)),
                  pl.BlockSpec((MOE_TILE * ROW_SUBLANES, LANES), lambda i: (i, 0)),
                  pl.BlockSpec((TOP_K, MOE_TILE), lambda i: (0, i)),
                  pl.BlockSpec((D_MODEL, EXPERT_DIM), z2),
                  pl.BlockSpec((D_MODEL, EXPERT_DIM), z2),
                  pl.BlockSpec((EXPERT_DIM, D_MODEL), z2),
                  pl.BlockSpec((1, D_MODEL), z2),
                  pl.BlockSpec((gpt, 1, D_MODEL), lambda i: (jnp.minimum(i, npt - 1), 0, 0)),
                  pl.BlockSpec((gpt, 1, D_MODEL), lambda i: (jnp.maximum(i - npt, 0), 0, 0))],
        out_specs=[pl.BlockSpec((MOE_TILE, D_MODEL), lambda i: (jnp.minimum(i, npt - 1), 0)),
                   pl.BlockSpec((MOE_TILE, D_MODEL), lambda i: (jnp.maximum(i - npt, 0), 0))],
        out_shape=[jax.ShapeDtypeStruct((n_prompt, D_MODEL), F32),
                   jax.ShapeDtypeStruct((n - n_prompt, D_MODEL), F32)],
        scratch_shapes=[pltpu.SMEM((3 * DEST_PER_TILE,), I32),
                        pltpu.VMEM((2 * TOP_K * TILE_SUBLANES, LANES), U32),
                        pltpu.VMEM((MOE_TILE, D_MODEL), F32),
                        pltpu.VMEM((MOE_TILE, LANES), F32),
                        pltpu.VMEM((MOE_TILE // CHUNK, 1, D_MODEL), F32),
                        pltpu.VMEM((MOE_TILE, D_MODEL), F32),
                        pltpu.SemaphoreType.DMA((3,)),
                        pltpu.SemaphoreType.DMA((2 * TOP_K,))],
        compiler_params=_cparams(("arbitrary",)),
        name="combine",
    )(dest_flat, ys, x1_p, x1_s, h_packed, gates, ws_gate, ws_up, ws_down, g_post, gate2_gp, gate2_gs)


ROPE_SPLIT = 128


def _rope_tables(start, length):
    half = HEAD_DIM // 2
    inv = ROPE_THETA ** (-jnp.arange(half, dtype=F32) / half)

    def cos_sin(pos):
        ang = pos.astype(F32)[:, None] * inv[None, :]
        return jnp.cos(ang), jnp.sin(ang)

    if start == 0 and length % ROPE_SPLIT == 0 and length > ROPE_SPLIT:
        ch, sh = cos_sin(jnp.arange(length // ROPE_SPLIT) * ROPE_SPLIT)
        cl, sl = cos_sin(jnp.arange(ROPE_SPLIT))
        ch, sh, cl, sl = ch[:, None, :], sh[:, None, :], cl[None], sl[None]
        cos = (ch * cl - sh * sl).reshape(length, half)
        sin = (sh * cl + ch * sl).reshape(length, half)
    else:
        cos, sin = cos_sin(start + jnp.arange(length))
    reps = LANES // half
    sign = jnp.where((jnp.arange(LANES) // half) % 2 == 0, -1.0, 1.0).astype(F32)
    return jnp.tile(cos, (1, reps)), jnp.tile(sin, (1, reps)) * sign


def _groups(vec_rows, reps):
    rows, width = vec_rows.shape
    return jnp.broadcast_to(vec_rows[:, None, :], (rows, reps, width)).reshape(rows * reps, 1, width)


def kernel(x_prompt, x_sample, cache_a_k, cache_a_v, cache_b_k, cache_b_v, c_prompt, c_sample, w_ada, b_ada,
           g_pre_mix, g_post_mix, w_in, rel_bias_a, sinks_b, w_branch_a, w_branch_b, w_out, g_pre_ffn,
           g_post_ffn, w_router, b_router, we_gate, we_up, we_down, ws_gate, ws_up, ws_down):
    assert w_ada.shape[0] == 1, "single layer"
    nb, seq, d = x_prompt.shape
    ns, dec = x_sample.shape[:2]
    assert d == D_MODEL and dec == CHUNK and seq % TOK_TILE == 0 and (ns * dec) == TOK_TILE
    n_p, n_s = nb * seq, ns * dec
    n_all = n_p + n_s
    assert n_all % MOE_TILE == 0 and n_p % MOE_TILE == 0

    c_all = jnp.concatenate([c_prompt, c_sample], axis=0)
    pad = (-c_all.shape[0]) % 8
    c_all = jnp.pad(c_all, ((0, pad), (0, 0)))
    mod = _modulation(c_all, w_ada[0], b_ada[0])
    mod_p, mod_s = mod[:nb], mod[nb:nb + ns]
    cpp = seq // CHUNK

    def part(k):
        return mod_p[:, k * d:(k + 1) * d], mod_s[:, k * d:(k + 1) * d]

    (sh1p, sh1s), (sc1p, sc1s), (g1p, g1s), (sh2p, sh2s), (sc2p, sc2s), (g2p, g2s) = [part(k) for k in range(6)]

    w_in_bf = w_in[0].astype(BF16)
    g_pre = g_pre_mix[0].reshape(1, d)
    cos_p, sin_p = _rope_tables(0, seq)
    cos_s, sin_s = _rope_tables(PAST_LEN, dec)
    cos_s, sin_s = jnp.tile(cos_s, (ns, 1)), jnp.tile(sin_s, (ns, 1))

    xp2 = x_prompt.reshape(n_p, d)
    xs2 = x_sample.reshape(n_s, d)
    outs_p = _inproj(xp2, _groups(sh1p, cpp), _groups(sc1p, cpp), g_pre, w_in_bf, cos_p, sin_p, nb, True)
    outs_s = _inproj(xs2, _groups(sh1s, 1), _groups(sc1s, 1), g_pre, w_in_bf, cos_s, sin_s, ns, False)

    table = rel_bias_a[0].astype(F32)
    n_far = A_BAND - 1 - REL_CLIP
    ext = jnp.concatenate([jnp.broadcast_to(table[:, 2 * REL_CLIP:], (A_HEADS, n_far)),
                           jnp.flip(table[:, REL_CLIP - (CHUNK - 1):], axis=1)], axis=1)
    n_ext = ext.shape[1]
    skew = jnp.tile(jnp.pad(ext, ((0, 0), (0, 1))), (1, CHUNK))[:, :CHUNK * n_ext].reshape(A_HEADS, CHUNK, n_ext)
    bias = skew[:, :, CHUNK - 1:CHUNK - 1 + A_BAND]
    bias_pairs = bias.reshape(A_HEADS // 2, 2 * CHUNK, A_BAND) * LOG2_E
    sink_rows = jnp.broadcast_to((sinks_b[0].astype(F32) * LOG2_E).reshape(B_KV_HEADS, B_GROUP, 1),
                                 (B_KV_HEADS, B_GROUP, CHUNK)).reshape(B_KV_HEADS, B_GROUP * CHUNK, 1)
    consts = (bias_pairs, sink_rows, w_branch_a[0].astype(BF16), w_branch_b[0].astype(BF16),
              w_out[0].astype(BF16), g_post_mix[0].reshape(1, d))

    x1_p = _attn_prompt(xp2, outs_p[:7], consts, _groups(g1p, cpp), nb)
    caches = (cache_a_k[0].reshape(ns, A_ROWS, WA), cache_a_v[0].reshape(ns, A_ROWS, WA),
              cache_b_k[0].reshape(ns, WINDOW, WKB), cache_b_v[0].reshape(ns, WINDOW, WKB))
    x1_s = _attn_sample(xs2, outs_s[:7], caches, consts, _groups(g1s, 1))

    h_packed, idx, gates, rank, counts = _router(
        x1_p, x1_s, _groups(sh2p, cpp), _groups(sh2s, 1), _groups(sc2p, cpp), _groups(sc2s, 1),
        g_pre_ffn[0].reshape(1, d),
        w_router[0].T.astype(BF16), b_router[0].astype(F32).reshape(N_EXPERTS, 1))
    n_blocks = (n_all * TOP_K) // EXPERT_ROWS + N_EXPERTS
    cnt = counts[:, 0].astype(I32)
    blocks_e = (cnt + EXPERT_ROWS - 1) // EXPERT_ROWS
    blk_end = jnp.cumsum(blocks_e)
    blk_start = blk_end - blocks_e
    n_active = blk_end[-1:]
    start_row = (blk_start * EXPERT_ROWS).astype(I32).reshape(1, N_EXPERTS)

    dest_flat = _dest_rows(idx, rank, start_row).reshape(-1)
    xs = _dispatch(dest_flat, h_packed, n_blocks * EXPERT_ROWS)
    ys = _experts(blk_start.astype(I32), blk_end.astype(I32), cnt, n_active.astype(I32), xs,
                  we_gate[0], we_up[0], we_down[0])
    y_p, y_s = _combine(dest_flat, ys, x1_p, x1_s, h_packed, gates, ws_gate[0].astype(BF16),
                        ws_up[0].astype(BF16), ws_down[0].astype(BF16), g_post_ffn[0].reshape(1, d),
                        _groups(g2p, cpp), _groups(g2s, 1))

    a_heads = (A_HEADS, HEAD_DIM)
    b_heads = (B_KV_HEADS, HEAD_DIM)
    return (y_p.reshape(nb, seq, d), y_s.reshape(ns, dec, d),
            outs_p[7].reshape(1, nb, A_ROWS, *a_heads), outs_p[8].reshape(1, nb, A_ROWS, *a_heads),
            outs_p[9].reshape(1, nb, WINDOW, *b_heads), outs_p[10].reshape(1, nb, WINDOW, *b_heads),
            outs_s[7].reshape(1, ns, dec, *a_heads), outs_s[8].reshape(1, ns, dec, *a_heads),
            outs_s[9].reshape(1, ns, dec, *b_heads), outs_s[10].reshape(1, ns, dec, *b_heads))
```

```python
import functools

import jax
import jax.numpy as jnp
from jax import lax
from jax.experimental import pallas as pl
from jax.experimental.pallas import tpu as pltpu

F32 = jnp.float32
BF16 = jnp.bfloat16
I32 = jnp.int32
U32 = jnp.uint32
HIGHEST = lax.Precision.HIGHEST
LOG2_E = 1.4426950408889634

D_MODEL = 1024
CHUNK = 64
HEAD_DIM = 64
A_HEADS = 8
A_PREV_CHUNKS = 8
A_ROWS = A_PREV_CHUNKS * CHUNK
A_BAND = A_ROWS + CHUNK
REL_CLIP = 128
B_HEADS = 8
B_KV_HEADS = 2
B_GROUP = B_HEADS // B_KV_HEADS
WINDOW = 128
B_BAND = WINDOW + CHUNK
ROPE_THETA = 10000.0
N_EXPERTS = 256
TOP_K = 8
N_GROUPS = 8
PER_GROUP = N_EXPERTS // N_GROUPS
TOPK_GROUPS = 4
EXPERT_DIM = 256
ROUTED_SCALE = 2.5
EPS = 1e-6
PAST_LEN = 4096

WA = A_HEADS * HEAD_DIM
WB = B_HEADS * HEAD_DIM
WKB = B_KV_HEADS * HEAD_DIM
OFF_QA, OFF_KA, OFF_VA = 0, WA, 2 * WA
OFF_QB = 3 * WA
OFF_KB = OFF_QB + WB
OFF_VB = OFF_KB + WKB
OFF_G = OFF_VB + WKB
N_IN = OFF_G + 2 * D_MODEL

LANES = 128
TOK_TILE = 512
MOE_TILE = 256
EXPERT_ROWS = 256
HALF_D = D_MODEL // 2
VMEM_LIMIT = 56 * 1024 * 1024


def _cparams(sem, vmem=VMEM_LIMIT):
    return pltpu.CompilerParams(dimension_semantics=sem, vmem_limit_bytes=vmem)


def _rms(x):
    return x * lax.rsqrt(jnp.mean(x * x, axis=-1, keepdims=True) + EPS)


def _group_affine(y, mul, add):
    g = mul.shape[0]
    y3 = y.reshape(g, CHUNK, y.shape[-1]) * mul
    if add is not None:
        y3 = y3 + add
    return y3.reshape(g * CHUNK, y.shape[-1])


def _pack_halves(a, b):
    ua = lax.bitcast_convert_type(a.astype(BF16).astype(F32), U32)
    ub = lax.bitcast_convert_type(b.astype(BF16).astype(F32), U32)
    return (ua & jnp.uint32(0xFFFF0000)) | (ub >> 16)


def _unpack_halves(u):
    a = lax.bitcast_convert_type(u & jnp.uint32(0xFFFF0000), F32)
    b = lax.bitcast_convert_type(u << 16, F32)
    return a, b


ROW_SUBLANES = HALF_D // LANES


def _store_packed_rows(ref, x):
    rows = x.shape[0]
    p = _pack_halves(x[:, :HALF_D], x[:, HALF_D:])
    for s in range(ROW_SUBLANES):
        ref[pl.ds(s, rows, stride=ROW_SUBLANES), :] = p[:, s * LANES:(s + 1) * LANES]


def _load_packed_rows(ref, rows, n_valid=None):
    his, los = [], []
    for s in range(ROW_SUBLANES):
        u = ref[pl.ds(s, rows, stride=ROW_SUBLANES), :]
        if n_valid is not None:
            u = jnp.where(lax.broadcasted_iota(I32, u.shape, 0) < n_valid, u, jnp.uint32(0))
        a, b = _unpack_halves(u)
        his.append(a)
        los.append(b)
    return jnp.concatenate(his, axis=1), jnp.concatenate(los, axis=1)


def _mod_kernel(c_ref, w_ref, b_ref, o_ref):
    c = c_ref[...]
    s = c * jax.nn.sigmoid(c)
    o_ref[...] = jnp.dot(s, w_ref[...], precision=HIGHEST, preferred_element_type=F32) + b_ref[...]


def _modulation(c_all, w_ada, b_ada):
    rows = c_all.shape[0]
    n = w_ada.shape[1]
    tn = 512
    return pl.pallas_call(
        _mod_kernel,
        grid=(n // tn,),
        in_specs=[pl.BlockSpec((rows, D_MODEL), lambda j: (0, 0)),
                  pl.BlockSpec((D_MODEL, tn), lambda j: (0, j)),
                  pl.BlockSpec((1, tn), lambda j: (0, j))],
        out_specs=pl.BlockSpec((rows, tn), lambda j: (0, j)),
        out_shape=jax.ShapeDtypeStruct((rows, n), F32),
        compiler_params=_cparams(("arbitrary",)),
        name="modulation",
    )(c_all, w_ada, b_ada.reshape(1, n))


def _rope(x, cos, sin_signed):
    n = x.shape[-1]
    reps = n // LANES
    if reps > 1:
        cos = jnp.concatenate([cos] * reps, axis=1)
        sin_signed = jnp.concatenate([sin_signed] * reps, axis=1)
    lane = lax.broadcasted_iota(I32, x.shape, 1)
    first_half = (lane % HEAD_DIM) < (HEAD_DIM // 2)
    partner = jnp.where(first_half, pltpu.roll(x, n - HEAD_DIM // 2, 1), pltpu.roll(x, HEAD_DIM // 2, 1))
    return x * cos + partner * sin_signed


def _inproj_kernel(prompt_state, tiles_per_seq,
                   x_ref, sh_ref, sc_ref, g_ref, w_ref, cos_ref, sin_ref,
                   qa_ref, ka_ref, va_ref, qb_ref, kb_ref, vb_ref, gt_ref,
                   ska_ref, sva_ref, skb_ref, svb_ref):
    x = x_ref[...]
    h = _group_affine(_rms(x) * g_ref[...], 1.0 + sc_ref[...], sh_ref[...]).astype(BF16)

    def proj(off, width):
        return jnp.dot(h, w_ref[:, off:off + width], preferred_element_type=F32)

    cos = cos_ref[...]
    sin = sin_ref[...]
    scale = HEAD_DIM ** -0.5 * LOG2_E
    qa_ref[...] = (proj(OFF_QA, WA) * scale).astype(BF16)
    ka = proj(OFF_KA, WA)
    va = proj(OFF_VA, WA)
    ka_ref[...] = ka.astype(BF16)
    va_ref[...] = va.astype(BF16)
    qb_ref[...] = (_rope(proj(OFF_QB, WB), cos, sin) * scale).astype(BF16)
    kb = _rope(proj(OFF_KB, WKB), cos, sin)
    vb = proj(OFF_VB, WKB)
    kb_ref[...] = kb.astype(BF16)
    vb_ref[...] = vb.astype(BF16)
    gt_ref[...] = jax.nn.sigmoid(proj(OFF_G, 2 * D_MODEL)).astype(BF16)

    if prompt_state:
        @pl.when(pl.program_id(0) % tiles_per_seq == tiles_per_seq - 1)
        def _():
            ska_ref[...] = ka
            sva_ref[...] = va
            skb_ref[...] = kb[TOK_TILE - WINDOW:, :]
            svb_ref[...] = vb[TOK_TILE - WINDOW:, :]
    else:
        ska_ref[...] = ka
        sva_ref[...] = va
        skb_ref[...] = kb
        svb_ref[...] = vb


def _inproj(x2d, shift_g, scale_g, g_pre, w_in_bf, cos_tab, sin_tab, n_seq, prompt_state):
    n = x2d.shape[0]
    nt = n // TOK_TILE
    tiles_per_seq = nt // n_seq if prompt_state else 1
    tab_tiles = cos_tab.shape[0] // TOK_TILE
    gpt = TOK_TILE // CHUNK
    row = lambda i: (i, 0)
    grp = lambda i: (i, 0, 0)
    if prompt_state:
        st_shapes = [jax.ShapeDtypeStruct((n_seq, A_ROWS, WA), F32)] * 2 + \
                    [jax.ShapeDtypeStruct((n_seq, WINDOW, WKB), F32)] * 2
        st_specs = [pl.BlockSpec((None, A_ROWS, WA), lambda i: (i // tiles_per_seq, 0, 0))] * 2 + \
                   [pl.BlockSpec((None, WINDOW, WKB), lambda i: (i // tiles_per_seq, 0, 0))] * 2
    else:
        st_shapes = [jax.ShapeDtypeStruct((n, WA), F32)] * 2 + [jax.ShapeDtypeStruct((n, WKB), F32)] * 2
        st_specs = [pl.BlockSpec((TOK_TILE, WA), row)] * 2 + [pl.BlockSpec((TOK_TILE, WKB), row)] * 2
    out_shapes = [jax.ShapeDtypeStruct((n, WA), BF16)] * 4 + [jax.ShapeDtypeStruct((n, WKB), BF16)] * 2 + \
                 [jax.ShapeDtypeStruct((n, 2 * D_MODEL), BF16)] + st_shapes
    out_specs = [pl.BlockSpec((TOK_TILE, WA), row)] * 4 + [pl.BlockSpec((TOK_TILE, WKB), row)] * 2 + \
                [pl.BlockSpec((TOK_TILE, 2 * D_MODEL), row)] + st_specs
    return pl.pallas_call(
        functools.partial(_inproj_kernel, prompt_state, tiles_per_seq),
        grid=(nt,),
        in_specs=[pl.BlockSpec((TOK_TILE, D_MODEL), row),
                  pl.BlockSpec((gpt, 1, D_MODEL), grp),
                  pl.BlockSpec((gpt, 1, D_MODEL), grp),
                  pl.BlockSpec((1, D_MODEL), lambda i: (0, 0)),
                  pl.BlockSpec((D_MODEL, N_IN), lambda i: (0, 0)),
                  pl.BlockSpec((TOK_TILE, LANES), lambda i: (i % tab_tiles, 0)),
                  pl.BlockSpec((TOK_TILE, LANES), lambda i: (i % tab_tiles, 0))],
        out_specs=out_specs,
        out_shape=out_shapes,
        compiler_params=_cparams(("arbitrary",)),
        name="inproj_prompt" if prompt_state else "inproj_sample",
    )(x2d, shift_g, scale_g, g_pre, w_in_bf, cos_tab, sin_tab)


def _attn_kernel(n_chunks, mask_first,
                 x_ref, qa_ref, qb_ref, gt_ref,
                 kap_ref, kac_ref, vap_ref, vac_ref, kbp_ref, kbc_ref, vbp_ref, vbc_ref,
                 bias_ref, sink_ref, wba_ref, wbb_ref, wout_ref, gpost_ref, gate1_ref,
                 o_ref,
                 ka_s, va_s, kb_s, vb_s, oa_s, ob_s):
    rows = n_chunks * CHUNK
    pb = kbp_ref.shape[0]
    ka_s[0:A_ROWS, :] = kap_ref[...].astype(BF16)
    va_s[0:A_ROWS, :] = vap_ref[...].astype(BF16)
    ka_s[A_ROWS:A_ROWS + rows, :] = kac_ref[...]
    va_s[A_ROWS:A_ROWS + rows, :] = vac_ref[...]
    kb_s[0:WINDOW, :] = kbp_ref[pb - WINDOW:pb, :].astype(BF16)
    vb_s[0:WINDOW, :] = vbp_ref[pb - WINDOW:pb, :].astype(BF16)
    kb_s[WINDOW:WINDOW + rows, :] = kbc_ref[...]
    vb_s[WINDOW:WINDOW + rows, :] = vbc_ref[...]

    lane_q = lax.broadcasted_iota(I32, (CHUNK, LANES), 1)
    nt_dims = (((1,), (1,)), ((), ()))

    def chunk_body(masked, c, carry):
        c0 = pl.multiple_of(c * CHUNK, CHUNK)
        if masked:
            valid_a = c0 + lax.broadcasted_iota(I32, (1, A_BAND), 1) >= A_ROWS
            valid_b = c0 + lax.broadcasted_iota(I32, (1, B_BAND), 1) >= WINDOW

        scores = []
        for p in range(A_HEADS // 2):
            cols = slice(p * LANES, (p + 1) * LANES)
            q = qa_ref[pl.ds(c0, CHUNK), cols].astype(F32)
            qs = jnp.concatenate([jnp.where(lane_q < HEAD_DIM, q, 0.0),
                                  jnp.where(lane_q >= HEAD_DIM, q, 0.0)], axis=0).astype(BF16)
            k = ka_s[pl.ds(c0, A_BAND), cols]
            s = lax.dot_general(qs, k, nt_dims, preferred_element_type=F32) + bias_ref[p]
            if masked:
                s = jnp.where(valid_a, s, -jnp.inf)
            scores.append(s)
        for g in range(B_KV_HEADS):
            parts = []
            for r in range(B_GROUP):
                head = g * B_GROUP + r
                t, half = head // 2, head % 2
                q = qb_ref[pl.ds(c0, CHUNK), t * LANES:(t + 1) * LANES].astype(F32)
                if half != g:
                    q = pltpu.roll(q, HEAD_DIM, 1)
                in_g = (lane_q >= HEAD_DIM) if g else (lane_q < HEAD_DIM)
                parts.append(jnp.where(in_g, q, 0.0))
            qs = jnp.concatenate(parts, axis=0).astype(BF16)
            k = kb_s[pl.ds(c0, B_BAND), :]
            s = lax.dot_general(qs, k, nt_dims, preferred_element_type=F32)
            if masked:
                s = jnp.where(valid_b, s, -jnp.inf)
            scores.append(s)

        numer, denom = [], []
        for n, s in enumerate(scores):
            m = jnp.max(s, axis=1, keepdims=True)
            if n >= A_HEADS // 2:
                sk = sink_ref[n - A_HEADS // 2]
                m = jnp.maximum(m, sk)
            e = jnp.exp2(s - m)
            l = jnp.sum(e, axis=1, keepdims=True)
            if n >= A_HEADS // 2:
                l = l + jnp.exp2(sk - m)
            numer.append(e.astype(BF16))
            denom.append(l)

        outs = []
        for n, e in enumerate(numer):
            if n < A_HEADS // 2:
                v = va_s[pl.ds(c0, A_BAND), n * LANES:(n + 1) * LANES]
            else:
                v = vb_s[pl.ds(c0, B_BAND), :]
            outs.append(jnp.dot(e, v, preferred_element_type=F32) / denom[n])

        for p in range(A_HEADS // 2):
            o = outs[p]
            oa_s[pl.ds(c0, CHUNK), p * LANES:(p + 1) * LANES] = jnp.where(
                lane_q < HEAD_DIM, o[:CHUNK], o[CHUNK:]).astype(BF16)
        for g in range(B_KV_HEADS):
            o = outs[A_HEADS // 2 + g]
            for s2 in range(B_GROUP // 2):
                o_even = o[(2 * s2) * CHUNK:(2 * s2 + 1) * CHUNK]
                o_odd = o[(2 * s2 + 1) * CHUNK:(2 * s2 + 2) * CHUNK]
                if g == 0:
                    tile = jnp.where(lane_q < HEAD_DIM, o_even, pltpu.roll(o_odd, HEAD_DIM, 1))
                else:
                    tile = jnp.where(lane_q < HEAD_DIM, pltpu.roll(o_even, HEAD_DIM, 1), o_odd)
                t = g * (B_GROUP // 2) + s2
                ob_s[pl.ds(c0, CHUNK), t * LANES:(t + 1) * LANES] = tile.astype(BF16)
        return carry

    def all_chunks(masked):
        lax.fori_loop(0, n_chunks, functools.partial(chunk_body, masked), 0, unroll=4 if n_chunks > 1 else 1)

    if mask_first:
        pl.when(pl.program_id(1) == 0)(lambda: all_chunks(True))
        pl.when(pl.program_id(1) > 0)(lambda: all_chunks(False))
    else:
        all_chunks(False)

    za = jnp.dot(oa_s[...], wba_ref[...], preferred_element_type=F32)
    zb = jnp.dot(ob_s[...], wbb_ref[...], preferred_element_type=F32)
    merged = gt_ref[:, :D_MODEL].astype(F32) * za + gt_ref[:, D_MODEL:].astype(F32) * zb
    mo = jnp.dot(merged.astype(BF16), wout_ref[...], preferred_element_type=F32)
    o_ref[...] = x_ref[...] + _group_affine(_rms(mo) * gpost_ref[...], gate1_ref[...], None)


def _attn_scratch(rows):
    return [pltpu.VMEM((A_ROWS + rows, WA), BF16), pltpu.VMEM((A_ROWS + rows, WA), BF16),
            pltpu.VMEM((WINDOW + rows, WKB), BF16), pltpu.VMEM((WINDOW + rows, WKB), BF16),
            pltpu.VMEM((rows, WA), BF16), pltpu.VMEM((rows, WB), BF16)]


def _const_specs(grid_rank):
    z2 = (lambda b, j: (0, 0)) if grid_rank == 2 else (lambda b: (0, 0))
    z3 = (lambda b, j: (0, 0, 0)) if grid_rank == 2 else (lambda b: (0, 0, 0))
    return [pl.BlockSpec((A_HEADS // 2, 2 * CHUNK, A_BAND), z3),
            pl.BlockSpec((B_KV_HEADS, B_GROUP * CHUNK, 1), z3),
            pl.BlockSpec((WA, D_MODEL), z2),
            pl.BlockSpec((WB, D_MODEL), z2),
            pl.BlockSpec((D_MODEL, D_MODEL), z2),
            pl.BlockSpec((1, D_MODEL), z2)]


def _attn_prompt(x2d, proj, consts, gate1_g, n_seq):
    qa, ka, va, qb, kb, vb, gt = proj
    n = x2d.shape[0]
    tps = n // n_seq // TOK_TILE
    gpt = TOK_TILE // CHUNK
    cur = lambda b, j: (b * tps + j, 0)
    prev = lambda b, j: (b * tps + jnp.maximum(j - 1, 0), 0)
    return pl.pallas_call(
        functools.partial(_attn_kernel, TOK_TILE // CHUNK, True),
        grid=(n_seq, tps),
        in_specs=[pl.BlockSpec((TOK_TILE, D_MODEL), cur),
                  pl.BlockSpec((TOK_TILE, WA), cur),
                  pl.BlockSpec((TOK_TILE, WB), cur),
                  pl.BlockSpec((TOK_TILE, 2 * D_MODEL), cur),
                  pl.BlockSpec((TOK_TILE, WA), prev), pl.BlockSpec((TOK_TILE, WA), cur),
                  pl.BlockSpec((TOK_TILE, WA), prev), pl.BlockSpec((TOK_TILE, WA), cur),
                  pl.BlockSpec((TOK_TILE, WKB), prev), pl.BlockSpec((TOK_TILE, WKB), cur),
                  pl.BlockSpec((TOK_TILE, WKB), prev), pl.BlockSpec((TOK_TILE, WKB), cur)]
                 + _const_specs(2)
                 + [pl.BlockSpec((gpt, 1, D_MODEL), lambda b, j: (b * tps + j, 0, 0))],
        out_specs=pl.BlockSpec((TOK_TILE, D_MODEL), cur),
        out_shape=jax.ShapeDtypeStruct((n, D_MODEL), F32),
        scratch_shapes=_attn_scratch(TOK_TILE),
        compiler_params=_cparams(("arbitrary", "arbitrary")),
        name="attn_prompt",
    )(x2d, qa, qb, gt, ka, ka, va, va, kb, kb, vb, vb, *consts, gate1_g)


def _attn_sample(x2d, proj, caches, consts, gate1_g):
    qa, ka, va, qb, kb, vb, gt = proj
    cak, cav, cbk, cbv = caches
    n_seq = cak.shape[0]
    cur = lambda b: (b, 0)
    cache = lambda b: (b, 0, 0)
    return pl.pallas_call(
        functools.partial(_attn_kernel, 1, False),
        grid=(n_seq,),
        in_specs=[pl.BlockSpec((CHUNK, D_MODEL), cur),
                  pl.BlockSpec((CHUNK, WA), cur),
                  pl.BlockSpec((CHUNK, WB), cur),
                  pl.BlockSpec((CHUNK, 2 * D_MODEL), cur),
                  pl.BlockSpec((None, A_ROWS, WA), cache), pl.BlockSpec((CHUNK, WA), cur),
                  pl.BlockSpec((None, A_ROWS, WA), cache), pl.BlockSpec((CHUNK, WA), cur),
                  pl.BlockSpec((None, WINDOW, WKB), cache), pl.BlockSpec((CHUNK, WKB), cur),
                  pl.BlockSpec((None, WINDOW, WKB), cache), pl.BlockSpec((CHUNK, WKB), cur)]
                 + _const_specs(1)
                 + [pl.BlockSpec((1, 1, D_MODEL), lambda b: (b, 0, 0))],
        out_specs=pl.BlockSpec((CHUNK, D_MODEL), cur),
        out_shape=jax.ShapeDtypeStruct(x2d.shape, F32),
        scratch_shapes=_attn_scratch(CHUNK),
        compiler_params=_cparams(("arbitrary",)),
        name="attn_sample",
    )(x2d, qa, qb, gt, cak, ka, cav, va, cbk, kb, cbv, vb, *consts, gate1_g)


def _two_part_tile(n_first_tiles, first_ref, second_ref):
    return jnp.where(pl.program_id(0) < n_first_tiles, first_ref[...], second_ref[...])


def _router_kernel(n_prompt_tiles, xp_ref, xs_ref, shp_ref, shs_ref, scp_ref, scs_ref, g_ref, wrt_ref, br_ref,
                   wsg_ref, wsu_ref, wsd_ref,
                   h_ref, idx_ref, gate_ref, rank_ref, cnt_ref, shared_ref, carry):
    i = pl.program_id(0)

    @pl.when(i == 0)
    def _():
        carry[...] = jnp.zeros_like(carry)

    tm = xp_ref.shape[0]
    x = _two_part_tile(n_prompt_tiles, xp_ref, xs_ref)
    h = _group_affine(_rms(x) * g_ref[...], 1.0 + _two_part_tile(n_prompt_tiles, scp_ref, scs_ref),
                      _two_part_tile(n_prompt_tiles, shp_ref, shs_ref))
    _store_packed_rows(h_ref, h)

    hb = h.astype(BF16)
    logits = lax.dot_general(wrt_ref[...], hb, (((1,), (1,)), ((), ())), preferred_element_type=F32)

    sg = jnp.dot(hb, wsg_ref[...], preferred_element_type=F32)
    sa = (sg * jax.nn.sigmoid(sg) * jnp.dot(hb, wsu_ref[...], preferred_element_type=F32)).astype(BF16)
    shared_ref[...] = jnp.dot(sa, wsd_ref[...], preferred_element_type=F32)

    scores = jax.nn.sigmoid(logits)
    biased = scores + br_ref[...]
    neg = -jnp.inf

    sub = lax.broadcasted_iota(I32, (PER_GROUP, tm), 0).astype(F32)
    gs_rows = []
    for g in range(N_GROUPS):
        xg = biased[g * PER_GROUP:(g + 1) * PER_GROUP]
        m1 = jnp.max(xg, axis=0, keepdims=True)
        i1 = jnp.min(jnp.where(xg == m1, sub, float(PER_GROUP)), axis=0, keepdims=True)
        m2 = jnp.max(jnp.where(sub == i1, neg, xg), axis=0, keepdims=True)
        gs_rows.append(m1 + m2)
    gs = jnp.concatenate(gs_rows, axis=0)

    giota = lax.broadcasted_iota(I32, (N_GROUPS, tm), 0).astype(F32)
    keep = jnp.zeros((N_GROUPS, tm), F32)
    for _ in range(TOPK_GROUPS):
        m = jnp.max(gs, axis=0, keepdims=True)
        gi = jnp.min(jnp.where(gs == m, giota, float(N_GROUPS)), axis=0, keepdims=True)
        hit = giota == gi
        keep = jnp.where(hit, 1.0, keep)
        gs = jnp.where(hit, neg, gs)
    cand = jnp.concatenate(
        [jnp.where(keep[g:g + 1] > 0.0, biased[g * PER_GROUP:(g + 1) * PER_GROUP], neg)
         for g in range(N_GROUPS)], axis=0)

    eiota = lax.broadcasted_iota(I32, (N_EXPERTS, tm), 0).astype(F32)
    idx_rows, gate_rows = [], []
    chosen = jnp.zeros((N_EXPERTS, tm), F32)
    for _ in range(TOP_K):
        m = jnp.max(cand, axis=0, keepdims=True)
        ei = jnp.min(jnp.where(cand == m, eiota, float(N_EXPERTS)), axis=0, keepdims=True)
        sel = eiota == ei
        gate_rows.append(jnp.sum(jnp.where(sel, scores, 0.0), axis=0, keepdims=True))
        idx_rows.append(ei)
        chosen = jnp.where(sel, 1.0, chosen)
        cand = jnp.where(sel, neg, cand)
    gates = jnp.concatenate(gate_rows, axis=0)
    gates = gates / jnp.sum(gates, axis=0, keepdims=True) * ROUTED_SCALE
    gate_ref[...] = gates
    idx_ref[...] = jnp.concatenate(idx_rows, axis=0).astype(I32)

    r_i = lax.broadcasted_iota(I32, (tm, tm), 0)
    c_i = lax.broadcasted_iota(I32, (tm, tm), 1)
    upper = jnp.where(r_i < c_i, 1.0, 0.0).astype(BF16)
    before = jnp.dot(chosen.astype(BF16), upper, preferred_element_type=F32) + carry[...]
    rank_rows = [jnp.sum(jnp.where(eiota == idx_rows[k], before, 0.0), axis=0, keepdims=True)
                 for k in range(TOP_K)]
    rank_ref[...] = jnp.concatenate(rank_rows, axis=0).astype(I32)
    total = carry[...] + jnp.sum(chosen, axis=1, keepdims=True)
    carry[...] = total
    cnt_ref[...] = total


def _router(x1_p, x1_s, shift_gp, shift_gs, scale_gp, scale_gs, g_pre, w_router_t, b_router_col,
            ws_gate, ws_up, ws_down):
    n = x1_p.shape[0] + x1_s.shape[0]
    nt = n // MOE_TILE
    npt = x1_p.shape[0] // MOE_TILE
    gpt = MOE_TILE // CHUNK
    lane_blk = lambda i: (0, i)
    return pl.pallas_call(
        functools.partial(_router_kernel, npt),
        grid=(nt,),
        in_specs=[pl.BlockSpec((MOE_TILE, D_MODEL), lambda i: (jnp.minimum(i, npt - 1), 0)),
                  pl.BlockSpec((MOE_TILE, D_MODEL), lambda i: (jnp.maximum(i - npt, 0), 0)),
                  pl.BlockSpec((gpt, 1, D_MODEL), lambda i: (jnp.minimum(i, npt - 1), 0, 0)),
                  pl.BlockSpec((gpt, 1, D_MODEL), lambda i: (jnp.maximum(i - npt, 0), 0, 0)),
                  pl.BlockSpec((gpt, 1, D_MODEL), lambda i: (jnp.minimum(i, npt - 1), 0, 0)),
                  pl.BlockSpec((gpt, 1, D_MODEL), lambda i: (jnp.maximum(i - npt, 0), 0, 0)),
                  pl.BlockSpec((1, D_MODEL), lambda i: (0, 0)),
                  pl.BlockSpec((N_EXPERTS, D_MODEL), lambda i: (0, 0)),
                  pl.BlockSpec((N_EXPERTS, 1), lambda i: (0, 0)),
                  pl.BlockSpec((D_MODEL, EXPERT_DIM), lambda i: (0, 0)),
                  pl.BlockSpec((D_MODEL, EXPERT_DIM), lambda i: (0, 0)),
                  pl.BlockSpec((EXPERT_DIM, D_MODEL), lambda i: (0, 0))],
        out_specs=[pl.BlockSpec((MOE_TILE * ROW_SUBLANES, LANES), lambda i: (i, 0)),
                   pl.BlockSpec((TOP_K, MOE_TILE), lane_blk),
                   pl.BlockSpec((TOP_K, MOE_TILE), lane_blk),
                   pl.BlockSpec((TOP_K, MOE_TILE), lane_blk),
                   pl.BlockSpec((N_EXPERTS, 1), lambda i: (0, 0)),
                   pl.BlockSpec((MOE_TILE, D_MODEL), lambda i: (i, 0))],
        out_shape=[jax.ShapeDtypeStruct((n * ROW_SUBLANES, LANES), U32),
                   jax.ShapeDtypeStruct((TOP_K, n), I32),
                   jax.ShapeDtypeStruct((TOP_K, n), F32),
                   jax.ShapeDtypeStruct((TOP_K, n), I32),
                   jax.ShapeDtypeStruct((N_EXPERTS, 1), F32),
                   jax.ShapeDtypeStruct((n, D_MODEL), F32)],
        scratch_shapes=[pltpu.VMEM((N_EXPERTS, 1), F32)],
        compiler_params=_cparams(("arbitrary",)),
        name="router",
    )(x1_p, x1_s, shift_gp, shift_gs, scale_gp, scale_gs, g_pre, w_router_t, b_router_col, ws_gate, ws_up, ws_down)


def _dest_kernel(idx_ref, rank_ref, start_ref, o_ref):
    tabs = [jnp.broadcast_to(start_ref[:, t * LANES:(t + 1) * LANES], (TOP_K, LANES))
            for t in range(N_EXPERTS // LANES)]
    for sub in range(o_ref.shape[0]):
        parts = []
        for c in range(MOE_TILE // LANES):
            cols = slice(sub * MOE_TILE + c * LANES, sub * MOE_TILE + (c + 1) * LANES)
            idx = idx_ref[:, cols]
            lane = idx % LANES
            val = jnp.take_along_axis(tabs[0], lane, axis=1)
            for t in range(1, len(tabs)):
                val = jnp.where(idx >= t * LANES, jnp.take_along_axis(tabs[t], lane, axis=1), val)
            parts.append(val + rank_ref[:, cols])
        o_ref[sub] = jnp.concatenate(parts, axis=1)


def _dest_rows(idx, rank, start_row):
    n = idx.shape[1]
    nt = n // MOE_TILE
    per_step = next(c for c in (10, 8, 5, 4, 2, 1) if nt % c == 0)
    return pl.pallas_call(
        _dest_kernel,
        grid=(nt // per_step,),
        in_specs=[pl.BlockSpec((TOP_K, per_step * MOE_TILE), lambda i: (0, i)),
                  pl.BlockSpec((TOP_K, per_step * MOE_TILE), lambda i: (0, i)),
                  pl.BlockSpec((1, N_EXPERTS), lambda i: (0, 0))],
        out_specs=pl.BlockSpec((per_step, TOP_K, MOE_TILE), lambda i: (i, 0, 0)),
        out_shape=jax.ShapeDtypeStruct((nt, TOP_K, MOE_TILE), I32),
        compiler_params=_cparams(("arbitrary",)),
        name="dest_rows",
    )(idx, rank, start_row)


DEST_PER_TILE = TOP_K * MOE_TILE


def _row_slice(ref, row):
    return ref.at[pl.ds(pl.multiple_of(row * ROW_SUBLANES, ROW_SUBLANES), ROW_SUBLANES)]


TILE_SUBLANES = MOE_TILE * ROW_SUBLANES


def _dispatch_kernel(dest_hbm, h_hbm, xs_hbm, dest_s, hbuf, dsem, hsem, rsem):
    i = pl.program_id(0)
    nt = pl.num_programs(0)

    def load(tile):
        dst = dest_s.at[pl.ds((tile % 2) * DEST_PER_TILE, DEST_PER_TILE)]
        rows = h_hbm.at[pl.ds(pl.multiple_of(tile * TILE_SUBLANES, TILE_SUBLANES), TILE_SUBLANES)]
        return (pltpu.make_async_copy(dest_hbm.at[pl.ds(tile * DEST_PER_TILE, DEST_PER_TILE)], dst, dsem.at[tile % 2]),
                pltpu.make_async_copy(rows, hbuf.at[tile % 3], hsem.at[tile % 3]))

    def drain(tile):
        for k in range(TOP_K):
            pltpu.make_async_copy(hbuf.at[tile % 3], xs_hbm.at[pl.ds(0, TILE_SUBLANES)],
                                  rsem.at[(tile % 2) * TOP_K + k]).wait()

    @pl.when(i == 0)
    def _():
        for cp in load(0):
            cp.start()

    @pl.when(i + 1 < nt)
    def _():
        for cp in load(i + 1):
            cp.start()

    for cp in load(i):
        cp.wait()
    base = (i % 2) * DEST_PER_TILE
    src_buf = hbuf.at[i % 3]
    sem0 = (i % 2) * TOP_K

    def body(t, carry):
        src = _row_slice(src_buf, t)
        for k in range(TOP_K):
            d = dest_s[base + k * MOE_TILE + t]
            pltpu.make_async_copy(src, _row_slice(xs_hbm, d), rsem.at[sem0 + k]).start(priority=k % 2)
        return carry

    lax.fori_loop(0, MOE_TILE, body, 0, unroll=4)

    @pl.when(i >= 1)
    def _():
        drain(i - 1)

    @pl.when(i == nt - 1)
    def _():
        drain(i)


def _dispatch(dest_flat, h_packed, n_rows):
    nt = h_packed.shape[0] // TILE_SUBLANES
    return pl.pallas_call(
        _dispatch_kernel,
        grid=(nt,),
        in_specs=[pl.BlockSpec(memory_space=pl.ANY),
                  pl.BlockSpec(memory_space=pl.ANY)],
        out_specs=pl.BlockSpec(memory_space=pl.ANY),
        out_shape=jax.ShapeDtypeStruct((n_rows * ROW_SUBLANES, LANES), U32),
        scratch_shapes=[pltpu.SMEM((2 * DEST_PER_TILE,), I32),
                        pltpu.VMEM((3, TILE_SUBLANES, LANES), U32),
                        pltpu.SemaphoreType.DMA((2,)),
                        pltpu.SemaphoreType.DMA((3,)),
                        pltpu.SemaphoreType.DMA((2 * TOP_K,))],
        compiler_params=_cparams(("arbitrary",)),
        name="dispatch",
    )(dest_flat, h_packed)


BLOCK_SUBLANES = EXPERT_ROWS * ROW_SUBLANES
EXPERT_AHEAD = 4
EXPERT_SLOTS = EXPERT_AHEAD + 2


def _expert_kernel(first_ref, last_ref, cnt_ref, na_ref, xs_hbm, wg_ref, wu_ref, wd_ref, ys_hbm,
                   wg_s, wu_s, wd_s, xbuf, ybuf, isem, osem):
    e = pl.program_id(0)
    n_active = na_ref[0]
    first, last = first_ref[e], last_ref[e]

    def block_rows(ref, g):
        return ref.at[pl.ds(pl.multiple_of(g * BLOCK_SUBLANES, BLOCK_SUBLANES), BLOCK_SUBLANES)]

    def fetch(g, slot):
        return pltpu.make_async_copy(block_rows(xs_hbm, g), xbuf.at[slot], isem.at[slot])

    def flush(g, slot):
        return pltpu.make_async_copy(ybuf.at[slot], block_rows(ys_hbm, g), osem.at[slot])

    @pl.when(e == 0)
    def _():
        for g0 in range(EXPERT_AHEAD):
            @pl.when(g0 < n_active)
            def _():
                fetch(g0, g0).start()

    @pl.when(last > first)
    def _():
        wg_s[...] = wg_ref[...].astype(BF16)
        wu_s[...] = wu_ref[...].astype(BF16)
        wd_s[...] = wd_ref[...].astype(BF16)

    def enter(g):
        slot = g % EXPERT_SLOTS
        fetch(g, slot).wait()

        @pl.when(g + EXPERT_AHEAD < n_active)
        def _():
            fetch(g + EXPERT_AHEAD, (g + EXPERT_AHEAD) % EXPERT_SLOTS).start()

        @pl.when(g >= EXPERT_SLOTS)
        def _():
            flush(g - EXPERT_SLOTS, slot).wait()
        return slot

    def swiglu(halves):
        xa = jnp.concatenate([h[0] for h in halves], axis=0).astype(BF16)
        xb = jnp.concatenate([h[1] for h in halves], axis=0).astype(BF16)

        def up(w_s):
            return (jnp.dot(xa, w_s[:HALF_D, :], preferred_element_type=F32)
                    + jnp.dot(xb, w_s[HALF_D:, :], preferred_element_type=F32))

        gate = up(wg_s)
        a = (gate * jax.nn.sigmoid(gate) * up(wu_s)).astype(BF16)
        return jnp.dot(a, wd_s[...], preferred_element_type=F32)

    def n_valid(g):
        return cnt_ref[e] - (g - first) * EXPERT_ROWS

    def pair_body(p, carry):
        g = first + 2 * p
        s0 = enter(g)
        s1 = enter(g + 1)
        y = swiglu([_load_packed_rows(xbuf.at[s0], EXPERT_ROWS),
                    _load_packed_rows(xbuf.at[s1], EXPERT_ROWS, n_valid(g + 1))])
        _store_packed_rows(ybuf.at[s0], y[:EXPERT_ROWS])
        _store_packed_rows(ybuf.at[s1], y[EXPERT_ROWS:])
        flush(g, s0).start()
        flush(g + 1, s1).start()
        return carry

    n_pairs = lax.shift_right_logical(last - first, 1)
    lax.fori_loop(0, n_pairs, pair_body, 0)

    @pl.when((last - first) % 2 == 1)
    def _():
        g = last - 1
        s0 = enter(g)
        y = swiglu([_load_packed_rows(xbuf.at[s0], EXPERT_ROWS, n_valid(g))])
        _store_packed_rows(ybuf.at[s0], y)
        flush(g, s0).start()

    @pl.when(e == pl.num_programs(0) - 1)
    def _():
        for back in range(EXPERT_SLOTS, 0, -1):
            @pl.when(n_active >= back)
            def _():
                flush(n_active - back, (n_active - back) % EXPERT_SLOTS).wait()


def _experts(blk_first, blk_last, counts, n_active, xs, we_gate, we_up, we_down):
    wmap = lambda e, *_: (e, 0, 0)
    return pl.pallas_call(
        _expert_kernel,
        grid_spec=pltpu.PrefetchScalarGridSpec(
            num_scalar_prefetch=4,
            grid=(N_EXPERTS,),
            in_specs=[pl.BlockSpec(memory_space=pl.ANY),
                      pl.BlockSpec((None, D_MODEL, EXPERT_DIM), wmap),
                      pl.BlockSpec((None, D_MODEL, EXPERT_DIM), wmap),
                      pl.BlockSpec((None, EXPERT_DIM, D_MODEL), wmap)],
            out_specs=pl.BlockSpec(memory_space=pl.ANY),
            scratch_shapes=[pltpu.VMEM((D_MODEL, EXPERT_DIM), BF16),
                            pltpu.VMEM((D_MODEL, EXPERT_DIM), BF16),
                            pltpu.VMEM((EXPERT_DIM, D_MODEL), BF16),
                            pltpu.VMEM((EXPERT_SLOTS, BLOCK_SUBLANES, LANES), U32),
                            pltpu.VMEM((EXPERT_SLOTS, BLOCK_SUBLANES, LANES), U32),
                            pltpu.SemaphoreType.DMA((EXPERT_SLOTS,)),
                            pltpu.SemaphoreType.DMA((EXPERT_SLOTS,))]),
        out_shape=jax.ShapeDtypeStruct(xs.shape, U32),
        compiler_params=_cparams(("arbitrary",)),
        name="experts",
    )(blk_first, blk_last, counts, n_active, xs, we_gate, we_up, we_down)


COMBINE_ROWS = 8


def _combine_kernel(n_prompt_tiles,
                    dest_hbm, ys_hbm, xp_ref, xs_ref, shared_ref, gate_ref, gpost_ref, gate2p_ref, gate2s_ref,
                    yp_ref, ysm_ref, dest_s, buf, gcol_s, gate2_s, y_s, dsem, rsem):
    i = pl.program_id(0)
    nt = pl.num_programs(0)

    def dest_copy(tile):
        slot = tile % 3
        return pltpu.make_async_copy(dest_hbm.at[pl.ds(tile * DEST_PER_TILE, DEST_PER_TILE)],
                                     dest_s.at[pl.ds(slot * DEST_PER_TILE, DEST_PER_TILE)], dsem.at[slot])

    def gather_token(base, bset, t):
        dst0 = bset * TILE_SUBLANES + t * ROW_SUBLANES
        for k in range(TOP_K):
            d = dest_s[base + t + k * MOE_TILE]
            dst = buf.at[pl.ds(pl.multiple_of(dst0 + k * TILE_SUBLANES, ROW_SUBLANES), ROW_SUBLANES)]
            pltpu.make_async_copy(_row_slice(ys_hbm, d), dst, rsem.at[bset + k]).start(priority=k % 2)

    def wait_rows(bset):
        for k in range(TOP_K):
            pltpu.make_async_copy(ys_hbm.at[pl.ds(0, TILE_SUBLANES)], buf.at[pl.ds(0, TILE_SUBLANES)],
                                  rsem.at[bset + k]).wait()

    @pl.when(i == 0)
    def _():
        dest_copy(0).start()

        @pl.when(nt > 1)
        def _():
            dest_copy(1).start()

        dest_copy(0).wait()

        def first_tile(t, carry):
            gather_token(0, 0, t)
            return carry

        lax.fori_loop(0, MOE_TILE, first_tile, 0, unroll=4)

    @pl.when(i + 2 < nt)
    def _():
        dest_copy(i + 2).start()

    @pl.when(i + 1 < nt)
    def _():
        dest_copy(i + 1).wait()

    nxt = jnp.minimum(i + 1, nt - 1)
    base_next = (nxt % 3) * DEST_PER_TILE
    bset = (i % 2) * TOP_K
    bset_next = ((i + 1) % 2) * TOP_K

    gpad = jnp.concatenate([gate_ref[...], jnp.zeros((LANES - TOP_K, MOE_TILE), F32)], axis=0)
    gcol_s[...] = gpad.T

    wait_rows(bset)
    in_first = i < n_prompt_tiles
    gate2_s[...] = _two_part_tile(n_prompt_tiles, gate2p_ref, gate2s_ref)

    def chunk(j, carry):
        r0 = pl.multiple_of(j * COMBINE_ROWS, COMBINE_ROWS)
        rows = pl.ds(r0, COMBINE_ROWS)
        acc_hi = [jnp.zeros((COMBINE_ROWS, LANES), F32) for _ in range(ROW_SUBLANES)]
        acc_lo = [jnp.zeros((COMBINE_ROWS, LANES), F32) for _ in range(ROW_SUBLANES)]
        src0 = bset * TILE_SUBLANES + r0 * ROW_SUBLANES
        for k in range(TOP_K):
            gk = jnp.broadcast_to(gcol_s[rows, k:k + 1], (COMBINE_ROWS, LANES))
            for s in range(ROW_SUBLANES):
                hi, lo = _unpack_halves(
                    buf[pl.ds(src0 + k * TILE_SUBLANES + s, COMBINE_ROWS, stride=ROW_SUBLANES), :])
                acc_hi[s] = acc_hi[s] + hi * gk
                acc_lo[s] = acc_lo[s] + lo * gk
        f = jnp.concatenate(acc_hi + acc_lo, axis=1) + shared_ref[rows, :]
        x = jnp.where(in_first, xp_ref[rows, :], xs_ref[rows, :])
        gate2 = gate2_s[lax.shift_right_logical(j * COMBINE_ROWS, CHUNK.bit_length() - 1)]
        y = x + _rms(f) * gpost_ref[...] * gate2
        for t in range(COMBINE_ROWS):
            gather_token(base_next, bset_next, r0 + t)
        y_s[rows, :] = y
        return carry

    lax.fori_loop(0, MOE_TILE // COMBINE_ROWS, chunk, 0, unroll=2)

    @pl.when(in_first)
    def _():
        yp_ref[...] = y_s[...]

    @pl.when(jnp.logical_not(in_first))
    def _():
        ysm_ref[...] = y_s[...]

    @pl.when(i == nt - 1)
    def _():
        wait_rows(bset_next)


def _combine(dest_flat, ys, x1_p, x1_s, shared, gates, g_post, gate2_gp, gate2_gs):
    n_prompt = x1_p.shape[0]
    n = n_prompt + x1_s.shape[0]
    nt = n // MOE_TILE
    npt = n_prompt // MOE_TILE
    gpt = MOE_TILE // CHUNK
    z2 = lambda i: (0, 0)
    return pl.pallas_call(
        functools.partial(_combine_kernel, npt),
        grid=(nt,),
        in_specs=[pl.BlockSpec(memory_space=pl.ANY),
                  pl.BlockSpec(memory_space=pl.ANY),
                  pl.BlockSpec((MOE_TILE, D_MODEL), lambda i: (jnp.minimum(i, npt - 1), 0)),
                  pl.BlockSpec((MOE_TILE, D_MODEL), lambda i: (jnp.maximum(i - npt, 0), 0)),
                  pl.BlockSpec((MOE_TILE, D_MODEL), lambda i: (i, 0)),
                  pl.BlockSpec((TOP_K, MOE_TILE), lambda i: (0, i)),
                  pl.BlockSpec((1, D_MODEL), z2),
                  pl.BlockSpec((gpt, 1, D_MODEL), lambda i: (jnp.minimum(i, npt - 1), 0, 0)),
                  pl.BlockSpec((gpt, 1, D_MODEL), lambda i: (jnp.maximum(i - npt, 0), 0, 0))],
        out_specs=[pl.BlockSpec((MOE_TILE, D_MODEL), lambda i: (jnp.minimum(i, npt - 1), 0)),
                   pl.BlockSpec((MOE_TILE, D_MODEL), lambda i: (jnp.maximum(i - npt, 0), 0))],
        out_shape=[jax.ShapeDtypeStruct((n_prompt, D_MODEL), F32),
                   jax.ShapeDtypeStruct((n - n_prompt, D_MODEL), F32)],
        scratch_shapes=[pltpu.SMEM((3 * DEST_PER_TILE,), I32),
                        pltpu.VMEM((2 * TOP_K * TILE_SUBLANES, LANES), U32),
                        pltpu.VMEM((MOE_TILE, LANES), F32),
                        pltpu.VMEM((MOE_TILE // CHUNK, 1, D_MODEL), F32),
                        pltpu.VMEM((MOE_TILE, D_MODEL), F32),
                        pltpu.SemaphoreType.DMA((3,)),
                        pltpu.SemaphoreType.DMA((2 * TOP_K,))],
        compiler_params=_cparams(("arbitrary",)),
        name="combine",
    )(dest_flat, ys, x1_p, x1_s, shared, gates, g_post, gate2_gp, gate2_gs)


ROPE_SPLIT = 128


def _rope_tables(start, length):
    half = HEAD_DIM // 2
    inv = ROPE_THETA ** (-jnp.arange(half, dtype=F32) / half)

    def cos_sin(pos):
        ang = pos.astype(F32)[:, None] * inv[None, :]
        return jnp.cos(ang), jnp.sin(ang)

    if start == 0 and length % ROPE_SPLIT == 0 and length > ROPE_SPLIT:
        ch, sh = cos_sin(jnp.arange(length // ROPE_SPLIT) * ROPE_SPLIT)
        cl, sl = cos_sin(jnp.arange(ROPE_SPLIT))
        ch, sh, cl, sl = ch[:, None, :], sh[:, None, :], cl[None], sl[None]
        cos = (ch * cl - sh * sl).reshape(length, half)
        sin = (sh * cl + ch * sl).reshape(length, half)
    else:
        cos, sin = cos_sin(start + jnp.arange(length))
    reps = LANES // half
    sign = jnp.where((jnp.arange(LANES) // half) % 2 == 0, -1.0, 1.0).astype(F32)
    return jnp.tile(cos, (1, reps)), jnp.tile(sin, (1, reps)) * sign


def _groups(vec_rows, reps):
    rows, width = vec_rows.shape
    return jnp.broadcast_to(vec_rows[:, None, :], (rows, reps, width)).reshape(rows * reps, 1, width)


def kernel(x_prompt, x_sample, cache_a_k, cache_a_v, cache_b_k, cache_b_v, c_prompt, c_sample, w_ada, b_ada,
           g_pre_mix, g_post_mix, w_in, rel_bias_a, sinks_b, w_branch_a, w_branch_b, w_out, g_pre_ffn,
           g_post_ffn, w_router, b_router, we_gate, we_up, we_down, ws_gate, ws_up, ws_down):
    assert w_ada.shape[0] == 1, "single layer"
    nb, seq, d = x_prompt.shape
    ns, dec = x_sample.shape[:2]
    assert d == D_MODEL and dec == CHUNK and seq % TOK_TILE == 0 and (ns * dec) == TOK_TILE
    n_p, n_s = nb * seq, ns * dec
    n_all = n_p + n_s
    assert n_all % MOE_TILE == 0 and n_p % MOE_TILE == 0

    c_all = jnp.concatenate([c_prompt, c_sample], axis=0)
    pad = (-c_all.shape[0]) % 8
    c_all = jnp.pad(c_all, ((0, pad), (0, 0)))
    mod = _modulation(c_all, w_ada[0], b_ada[0])
    mod_p, mod_s = mod[:nb], mod[nb:nb + ns]
    cpp = seq // CHUNK

    def part(k):
        return mod_p[:, k * d:(k + 1) * d], mod_s[:, k * d:(k + 1) * d]

    (sh1p, sh1s), (sc1p, sc1s), (g1p, g1s), (sh2p, sh2s), (sc2p, sc2s), (g2p, g2s) = [part(k) for k in range(6)]

    w_in_bf = w_in[0].astype(BF16)
    g_pre = g_pre_mix[0].reshape(1, d)
    cos_p, sin_p = _rope_tables(0, seq)
    cos_s, sin_s = _rope_tables(PAST_LEN, dec)
    cos_s, sin_s = jnp.tile(cos_s, (ns, 1)), jnp.tile(sin_s, (ns, 1))

    xp2 = x_prompt.reshape(n_p, d)
    xs2 = x_sample.reshape(n_s, d)
    outs_p = _inproj(xp2, _groups(sh1p, cpp), _groups(sc1p, cpp), g_pre, w_in_bf, cos_p, sin_p, nb, True)
    outs_s = _inproj(xs2, _groups(sh1s, 1), _groups(sc1s, 1), g_pre, w_in_bf, cos_s, sin_s, ns, False)

    table = rel_bias_a[0].astype(F32)
    n_far = A_BAND - 1 - REL_CLIP
    ext = jnp.concatenate([jnp.broadcast_to(table[:, 2 * REL_CLIP:], (A_HEADS, n_far)),
                           jnp.flip(table[:, REL_CLIP - (CHUNK - 1):], axis=1)], axis=1)
    n_ext = ext.shape[1]
    skew = jnp.tile(jnp.pad(ext, ((0, 0), (0, 1))), (1, CHUNK))[:, :CHUNK * n_ext].reshape(A_HEADS, CHUNK, n_ext)
    bias = skew[:, :, CHUNK - 1:CHUNK - 1 + A_BAND]
    bias_pairs = bias.reshape(A_HEADS // 2, 2 * CHUNK, A_BAND) * LOG2_E
    sink_rows = jnp.broadcast_to((sinks_b[0].astype(F32) * LOG2_E).reshape(B_KV_HEADS, B_GROUP, 1),
                                 (B_KV_HEADS, B_GROUP, CHUNK)).reshape(B_KV_HEADS, B_GROUP * CHUNK, 1)
    consts = (bias_pairs, sink_rows, w_branch_a[0].astype(BF16), w_branch_b[0].astype(BF16),
              w_out[0].astype(BF16), g_post_mix[0].reshape(1, d))

    x1_p = _attn_prompt(xp2, outs_p[:7], consts, _groups(g1p, cpp), nb)
    caches = (cache_a_k[0].reshape(ns, A_ROWS, WA), cache_a_v[0].reshape(ns, A_ROWS, WA),
              cache_b_k[0].reshape(ns, WINDOW, WKB), cache_b_v[0].reshape(ns, WINDOW, WKB))
    x1_s = _attn_sample(xs2, outs_s[:7], caches, consts, _groups(g1s, 1))

    h_packed, idx, gates, rank, counts, shared = _router(
        x1_p, x1_s, _groups(sh2p, cpp), _groups(sh2s, 1), _groups(sc2p, cpp), _groups(sc2s, 1),
        g_pre_ffn[0].reshape(1, d),
        w_router[0].T.astype(BF16), b_router[0].astype(F32).reshape(N_EXPERTS, 1),
        ws_gate[0].astype(BF16), ws_up[0].astype(BF16), ws_down[0].astype(BF16))
    n_blocks = (n_all * TOP_K) // EXPERT_ROWS + N_EXPERTS
    cnt = counts[:, 0].astype(I32)
    blocks_e = (cnt + EXPERT_ROWS - 1) // EXPERT_ROWS
    blk_end = jnp.cumsum(blocks_e)
    blk_start = blk_end - blocks_e
    n_active = blk_end[-1:]
    start_row = (blk_start * EXPERT_ROWS).astype(I32).reshape(1, N_EXPERTS)

    dest_flat = _dest_rows(idx, rank, start_row).reshape(-1)
    xs = _dispatch(dest_flat, h_packed, n_blocks * EXPERT_ROWS)
    ys = _experts(blk_start.astype(I32), blk_end.astype(I32), cnt, n_active.astype(I32), xs,
                  we_gate[0], we_up[0], we_down[0])
    y_p, y_s = _combine(dest_flat, ys, x1_p, x1_s, shared, gates, g_post_ffn[0].reshape(1, d),
                        _groups(g2p, cpp), _groups(g2s, 1))

    a_heads = (A_HEADS, HEAD_DIM)
    b_heads = (B_KV_HEADS, HEAD_DIM)
    return (y_p.reshape(nb, seq, d), y_s.reshape(ns, dec, d),
            outs_p[7].reshape(1, nb, A_ROWS, *a_heads), outs_p[8].reshape(1, nb, A_ROWS, *a_heads),
            outs_p[9].reshape(1, nb, WINDOW, *b_heads), outs_p[10].reshape(1, nb, WINDOW, *b_heads),
            outs_s[7].reshape(1, ns, dec, *a_heads), outs_s[8].reshape(1, ns, dec, *a_heads),
            outs_s[9].reshape(1, ns, dec, *b_heads), outs_s[10].reshape(1, ns, dec, *b_heads))
```

```python
import functools

import jax
import jax.numpy as jnp
from jax import lax
from jax.experimental import pallas as pl
from jax.experimental.pallas import tpu as pltpu

F32 = jnp.float32
BF16 = jnp.bfloat16
I32 = jnp.int32
U32 = jnp.uint32
HIGHEST = lax.Precision.HIGHEST
LOG2_E = 1.4426950408889634

D_MODEL = 1024
CHUNK = 64
HEAD_DIM = 64
A_HEADS = 8
A_PREV_CHUNKS = 8
A_ROWS = A_PREV_CHUNKS * CHUNK
A_BAND = A_ROWS + CHUNK
REL_CLIP = 128
B_HEADS = 8
B_KV_HEADS = 2
B_GROUP = B_HEADS // B_KV_HEADS
WINDOW = 128
B_BAND = WINDOW + CHUNK
ROPE_THETA = 10000.0
N_EXPERTS = 256
TOP_K = 8
N_GROUPS = 8
PER_GROUP = N_EXPERTS // N_GROUPS
TOPK_GROUPS = 4
EXPERT_DIM = 256
ROUTED_SCALE = 2.5
EPS = 1e-6
PAST_LEN = 4096

WA = A_HEADS * HEAD_DIM
WB = B_HEADS * HEAD_DIM
WKB = B_KV_HEADS * HEAD_DIM
OFF_QA, OFF_KA, OFF_VA = 0, WA, 2 * WA
OFF_QB = 3 * WA
OFF_KB = OFF_QB + WB
OFF_VB = OFF_KB + WKB
OFF_G = OFF_VB + WKB
N_IN = OFF_G + 2 * D_MODEL

LANES = 128
TOK_TILE = 512
MOE_TILE = 256
EXPERT_ROWS = 256
HALF_D = D_MODEL // 2
VMEM_LIMIT = 56 * 1024 * 1024


def _cparams(sem, vmem=VMEM_LIMIT):
    return pltpu.CompilerParams(dimension_semantics=sem, vmem_limit_bytes=vmem)


def _rms(x):
    return x * lax.rsqrt(jnp.mean(x * x, axis=-1, keepdims=True) + EPS)


def _group_affine(y, mul, add):
    g = mul.shape[0]
    y3 = y.reshape(g, CHUNK, y.shape[-1]) * mul
    if add is not None:
        y3 = y3 + add
    return y3.reshape(g * CHUNK, y.shape[-1])


def _pack_halves(a, b):
    ua = lax.bitcast_convert_type(a.astype(BF16).astype(F32), U32)
    ub = lax.bitcast_convert_type(b.astype(BF16).astype(F32), U32)
    return (ua & jnp.uint32(0xFFFF0000)) | (ub >> 16)


def _unpack_halves(u):
    a = lax.bitcast_convert_type(u & jnp.uint32(0xFFFF0000), F32)
    b = lax.bitcast_convert_type(u << 16, F32)
    return a, b


ROW_SUBLANES = HALF_D // LANES


def _store_packed_rows(ref, x):
    rows = x.shape[0]
    p = _pack_halves(x[:, :HALF_D], x[:, HALF_D:])
    for s in range(ROW_SUBLANES):
        ref[pl.ds(s, rows, stride=ROW_SUBLANES), :] = p[:, s * LANES:(s + 1) * LANES]


def _load_packed_rows(ref, rows, n_valid=None):
    his, los = [], []
    for s in range(ROW_SUBLANES):
        u = ref[pl.ds(s, rows, stride=ROW_SUBLANES), :]
        if n_valid is not None:
            u = jnp.where(lax.broadcasted_iota(I32, u.shape, 0) < n_valid, u, jnp.uint32(0))
        a, b = _unpack_halves(u)
        his.append(a)
        los.append(b)
    return jnp.concatenate(his, axis=1), jnp.concatenate(los, axis=1)


def _mod_kernel(c_ref, w_ref, b_ref, o_ref):
    c = c_ref[...]
    s = c * jax.nn.sigmoid(c)
    o_ref[...] = jnp.dot(s, w_ref[...], precision=HIGHEST, preferred_element_type=F32) + b_ref[...]


def _modulation(c_all, w_ada, b_ada):
    rows = c_all.shape[0]
    n = w_ada.shape[1]
    tn = 512
    return pl.pallas_call(
        _mod_kernel,
        grid=(n // tn,),
        in_specs=[pl.BlockSpec((rows, D_MODEL), lambda j: (0, 0)),
                  pl.BlockSpec((D_MODEL, tn), lambda j: (0, j)),
                  pl.BlockSpec((1, tn), lambda j: (0, j))],
        out_specs=pl.BlockSpec((rows, tn), lambda j: (0, j)),
        out_shape=jax.ShapeDtypeStruct((rows, n), F32),
        compiler_params=_cparams(("arbitrary",)),
        name="modulation",
    )(c_all, w_ada, b_ada.reshape(1, n))


def _rope(x, cos, sin_signed):
    n = x.shape[-1]
    reps = n // LANES
    if reps > 1:
        cos = jnp.concatenate([cos] * reps, axis=1)
        sin_signed = jnp.concatenate([sin_signed] * reps, axis=1)
    lane = lax.broadcasted_iota(I32, x.shape, 1)
    first_half = (lane % HEAD_DIM) < (HEAD_DIM // 2)
    partner = jnp.where(first_half, pltpu.roll(x, n - HEAD_DIM // 2, 1), pltpu.roll(x, HEAD_DIM // 2, 1))
    return x * cos + partner * sin_signed


def _inproj_kernel(prompt_state, tiles_per_seq,
                   x_ref, sh_ref, sc_ref, g_ref, w_ref, cos_ref, sin_ref,
                   qa_ref, ka_ref, va_ref, qb_ref, kb_ref, vb_ref, gt_ref,
                   ska_ref, sva_ref, skb_ref, svb_ref):
    x = x_ref[...]
    h = _group_affine(_rms(x) * g_ref[...], 1.0 + sc_ref[...], sh_ref[...]).astype(BF16)

    def proj(off, width):
        return jnp.dot(h, w_ref[:, off:off + width], preferred_element_type=F32)

    cos = cos_ref[...]
    sin = sin_ref[...]
    scale = HEAD_DIM ** -0.5 * LOG2_E
    qa_ref[...] = (proj(OFF_QA, WA) * scale).astype(BF16)
    ka = proj(OFF_KA, WA)
    va = proj(OFF_VA, WA)
    ka_ref[...] = ka.astype(BF16)
    va_ref[...] = va.astype(BF16)
    qb_ref[...] = (_rope(proj(OFF_QB, WB), cos, sin) * scale).astype(BF16)
    kb = _rope(proj(OFF_KB, WKB), cos, sin)
    vb = proj(OFF_VB, WKB)
    kb_ref[...] = kb.astype(BF16)
    vb_ref[...] = vb.astype(BF16)
    gt_ref[...] = jax.nn.sigmoid(proj(OFF_G, 2 * D_MODEL)).astype(BF16)

    if prompt_state:
        @pl.when(pl.program_id(0) % tiles_per_seq == tiles_per_seq - 1)
        def _():
            ska_ref[...] = ka
            sva_ref[...] = va
            skb_ref[...] = kb[TOK_TILE - WINDOW:, :]
            svb_ref[...] = vb[TOK_TILE - WINDOW:, :]
    else:
        ska_ref[...] = ka
        sva_ref[...] = va
        skb_ref[...] = kb
        svb_ref[...] = vb


def _inproj(x2d, shift_g, scale_g, g_pre, w_in_bf, cos_tab, sin_tab, n_seq, prompt_state):
    n = x2d.shape[0]
    nt = n // TOK_TILE
    tiles_per_seq = nt // n_seq if prompt_state else 1
    tab_tiles = cos_tab.shape[0] // TOK_TILE
    gpt = TOK_TILE // CHUNK
    row = lambda i: (i, 0)
    grp = lambda i: (i, 0, 0)
    if prompt_state:
        st_shapes = [jax.ShapeDtypeStruct((n_seq, A_ROWS, WA), F32)] * 2 + \
                    [jax.ShapeDtypeStruct((n_seq, WINDOW, WKB), F32)] * 2
        st_specs = [pl.BlockSpec((None, A_ROWS, WA), lambda i: (i // tiles_per_seq, 0, 0))] * 2 + \
                   [pl.BlockSpec((None, WINDOW, WKB), lambda i: (i // tiles_per_seq, 0, 0))] * 2
    else:
        st_shapes = [jax.ShapeDtypeStruct((n, WA), F32)] * 2 + [jax.ShapeDtypeStruct((n, WKB), F32)] * 2
        st_specs = [pl.BlockSpec((TOK_TILE, WA), row)] * 2 + [pl.BlockSpec((TOK_TILE, WKB), row)] * 2
    out_shapes = [jax.ShapeDtypeStruct((n, WA), BF16)] * 4 + [jax.ShapeDtypeStruct((n, WKB), BF16)] * 2 + \
                 [jax.ShapeDtypeStruct((n, 2 * D_MODEL), BF16)] + st_shapes
    out_specs = [pl.BlockSpec((TOK_TILE, WA), row)] * 4 + [pl.BlockSpec((TOK_TILE, WKB), row)] * 2 + \
                [pl.BlockSpec((TOK_TILE, 2 * D_MODEL), row)] + st_specs
    return pl.pallas_call(
        functools.partial(_inproj_kernel, prompt_state, tiles_per_seq),
        grid=(nt,),
        in_specs=[pl.BlockSpec((TOK_TILE, D_MODEL), row),
                  pl.BlockSpec((gpt, 1, D_MODEL), grp),
                  pl.BlockSpec((gpt, 1, D_MODEL), grp),
                  pl.BlockSpec((1, D_MODEL), lambda i: (0, 0)),
                  pl.BlockSpec((D_MODEL, N_IN), lambda i: (0, 0)),
                  pl.BlockSpec((TOK_TILE, LANES), lambda i: (i % tab_tiles, 0)),
                  pl.BlockSpec((TOK_TILE, LANES), lambda i: (i % tab_tiles, 0))],
        out_specs=out_specs,
        out_shape=out_shapes,
        compiler_params=_cparams(("arbitrary",)),
        name="inproj_prompt" if prompt_state else "inproj_sample",
    )(x2d, shift_g, scale_g, g_pre, w_in_bf, cos_tab, sin_tab)


def _attn_kernel(n_chunks, mask_first,
                 x_ref, qa_ref, qb_ref, gt_ref,
                 kap_ref, kac_ref, vap_ref, vac_ref, kbp_ref, kbc_ref, vbp_ref, vbc_ref,
                 bias_ref, sink_ref, wba_ref, wbb_ref, wout_ref, gpost_ref, gate1_ref,
                 o_ref,
                 ka_s, va_s, kb_s, vb_s, oa_s, ob_s):
    rows = n_chunks * CHUNK
    pb = kbp_ref.shape[0]
    ka_s[0:A_ROWS, :] = kap_ref[...].astype(BF16)
    va_s[0:A_ROWS, :] = vap_ref[...].astype(BF16)
    ka_s[A_ROWS:A_ROWS + rows, :] = kac_ref[...]
    va_s[A_ROWS:A_ROWS + rows, :] = vac_ref[...]
    kb_s[0:WINDOW, :] = kbp_ref[pb - WINDOW:pb, :].astype(BF16)
    vb_s[0:WINDOW, :] = vbp_ref[pb - WINDOW:pb, :].astype(BF16)
    kb_s[WINDOW:WINDOW + rows, :] = kbc_ref[...]
    vb_s[WINDOW:WINDOW + rows, :] = vbc_ref[...]

    lane_q = lax.broadcasted_iota(I32, (CHUNK, LANES), 1)
    nt_dims = (((1,), (1,)), ((), ()))

    def chunk_body(masked, c, carry):
        c0 = pl.multiple_of(c * CHUNK, CHUNK)
        if masked:
            valid_a = c0 + lax.broadcasted_iota(I32, (1, A_BAND), 1) >= A_ROWS
            valid_b = c0 + lax.broadcasted_iota(I32, (1, B_BAND), 1) >= WINDOW

        scores = []
        for p in range(A_HEADS // 2):
            cols = slice(p * LANES, (p + 1) * LANES)
            q = qa_ref[pl.ds(c0, CHUNK), cols].astype(F32)
            qs = jnp.concatenate([jnp.where(lane_q < HEAD_DIM, q, 0.0),
                                  jnp.where(lane_q >= HEAD_DIM, q, 0.0)], axis=0).astype(BF16)
            k = ka_s[pl.ds(c0, A_BAND), cols]
            s = lax.dot_general(qs, k, nt_dims, preferred_element_type=F32) + bias_ref[p]
            if masked:
                s = jnp.where(valid_a, s, -jnp.inf)
            scores.append(s)
        for g in range(B_KV_HEADS):
            parts = []
            for r in range(B_GROUP):
                head = g * B_GROUP + r
                t, half = head // 2, head % 2
                q = qb_ref[pl.ds(c0, CHUNK), t * LANES:(t + 1) * LANES].astype(F32)
                if half != g:
                    q = pltpu.roll(q, HEAD_DIM, 1)
                in_g = (lane_q >= HEAD_DIM) if g else (lane_q < HEAD_DIM)
                parts.append(jnp.where(in_g, q, 0.0))
            qs = jnp.concatenate(parts, axis=0).astype(BF16)
            k = kb_s[pl.ds(c0, B_BAND), :]
            s = lax.dot_general(qs, k, nt_dims, preferred_element_type=F32)
            if masked:
                s = jnp.where(valid_b, s, -jnp.inf)
            scores.append(s)

        numer, denom = [], []
        for n, s in enumerate(scores):
            m = jnp.max(s, axis=1, keepdims=True)
            if n >= A_HEADS // 2:
                sk = sink_ref[n - A_HEADS // 2]
                m = jnp.maximum(m, sk)
            e = jnp.exp2(s - m)
            l = jnp.sum(e, axis=1, keepdims=True)
            if n >= A_HEADS // 2:
                l = l + jnp.exp2(sk - m)
            numer.append(e.astype(BF16))
            denom.append(l)

        outs = []
        for n, e in enumerate(numer):
            if n < A_HEADS // 2:
                v = va_s[pl.ds(c0, A_BAND), n * LANES:(n + 1) * LANES]
            else:
                v = vb_s[pl.ds(c0, B_BAND), :]
            outs.append(jnp.dot(e, v, preferred_element_type=F32) / denom[n])

        for p in range(A_HEADS // 2):
            o = outs[p]
            oa_s[pl.ds(c0, CHUNK), p * LANES:(p + 1) * LANES] = jnp.where(
                lane_q < HEAD_DIM, o[:CHUNK], o[CHUNK:]).astype(BF16)
        for g in range(B_KV_HEADS):
            o = outs[A_HEADS // 2 + g]
            for s2 in range(B_GROUP // 2):
                o_even = o[(2 * s2) * CHUNK:(2 * s2 + 1) * CHUNK]
                o_odd = o[(2 * s2 + 1) * CHUNK:(2 * s2 + 2) * CHUNK]
                if g == 0:
                    tile = jnp.where(lane_q < HEAD_DIM, o_even, pltpu.roll(o_odd, HEAD_DIM, 1))
                else:
                    tile = jnp.where(lane_q < HEAD_DIM, pltpu.roll(o_even, HEAD_DIM, 1), o_odd)
                t = g * (B_GROUP // 2) + s2
                ob_s[pl.ds(c0, CHUNK), t * LANES:(t + 1) * LANES] = tile.astype(BF16)
        return carry

    def all_chunks(masked):
        lax.fori_loop(0, n_chunks, functools.partial(chunk_body, masked), 0, unroll=4 if n_chunks > 1 else 1)

    if mask_first:
        pl.when(pl.program_id(1) == 0)(lambda: all_chunks(True))
        pl.when(pl.program_id(1) > 0)(lambda: all_chunks(False))
    else:
        all_chunks(False)

    za = jnp.dot(oa_s[...], wba_ref[...], preferred_element_type=F32)
    zb = jnp.dot(ob_s[...], wbb_ref[...], preferred_element_type=F32)
    merged = gt_ref[:, :D_MODEL].astype(F32) * za + gt_ref[:, D_MODEL:].astype(F32) * zb
    mo = jnp.dot(merged.astype(BF16), wout_ref[...], preferred_element_type=F32)
    o_ref[...] = x_ref[...] + _group_affine(_rms(mo) * gpost_ref[...], gate1_ref[...], None)


def _attn_scratch(rows):
    return [pltpu.VMEM((A_ROWS + rows, WA), BF16), pltpu.VMEM((A_ROWS + rows, WA), BF16),
            pltpu.VMEM((WINDOW + rows, WKB), BF16), pltpu.VMEM((WINDOW + rows, WKB), BF16),
            pltpu.VMEM((rows, WA), BF16), pltpu.VMEM((rows, WB), BF16)]


def _const_specs(grid_rank):
    z2 = (lambda b, j: (0, 0)) if grid_rank == 2 else (lambda b: (0, 0))
    z3 = (lambda b, j: (0, 0, 0)) if grid_rank == 2 else (lambda b: (0, 0, 0))
    return [pl.BlockSpec((A_HEADS // 2, 2 * CHUNK, A_BAND), z3),
            pl.BlockSpec((B_KV_HEADS, B_GROUP * CHUNK, 1), z3),
            pl.BlockSpec((WA, D_MODEL), z2),
            pl.BlockSpec((WB, D_MODEL), z2),
            pl.BlockSpec((D_MODEL, D_MODEL), z2),
            pl.BlockSpec((1, D_MODEL), z2)]


def _attn_prompt(x2d, proj, consts, gate1_g, n_seq):
    qa, ka, va, qb, kb, vb, gt = proj
    n = x2d.shape[0]
    tps = n // n_seq // TOK_TILE
    gpt = TOK_TILE // CHUNK
    cur = lambda b, j: (b * tps + j, 0)
    prev = lambda b, j: (b * tps + jnp.maximum(j - 1, 0), 0)
    return pl.pallas_call(
        functools.partial(_attn_kernel, TOK_TILE // CHUNK, True),
        grid=(n_seq, tps),
        in_specs=[pl.BlockSpec((TOK_TILE, D_MODEL), cur),
                  pl.BlockSpec((TOK_TILE, WA), cur),
                  pl.BlockSpec((TOK_TILE, WB), cur),
                  pl.BlockSpec((TOK_TILE, 2 * D_MODEL), cur),
                  pl.BlockSpec((TOK_TILE, WA), prev), pl.BlockSpec((TOK_TILE, WA), cur),
                  pl.BlockSpec((TOK_TILE, WA), prev), pl.BlockSpec((TOK_TILE, WA), cur),
                  pl.BlockSpec((TOK_TILE, WKB), prev), pl.BlockSpec((TOK_TILE, WKB), cur),
                  pl.BlockSpec((TOK_TILE, WKB), prev), pl.BlockSpec((TOK_TILE, WKB), cur)]
                 + _const_specs(2)
                 + [pl.BlockSpec((gpt, 1, D_MODEL), lambda b, j: (b * tps + j, 0, 0))],
        out_specs=pl.BlockSpec((TOK_TILE, D_MODEL), cur),
        out_shape=jax.ShapeDtypeStruct((n, D_MODEL), F32),
        scratch_shapes=_attn_scratch(TOK_TILE),
        compiler_params=_cparams(("arbitrary", "arbitrary")),
        name="attn_prompt",
    )(x2d, qa, qb, gt, ka, ka, va, va, kb, kb, vb, vb, *consts, gate1_g)


def _attn_sample(x2d, proj, caches, consts, gate1_g):
    qa, ka, va, qb, kb, vb, gt = proj
    cak, cav, cbk, cbv = caches
    n_seq = cak.shape[0]
    cur = lambda b: (b, 0)
    cache = lambda b: (b, 0, 0)
    return pl.pallas_call(
        functools.partial(_attn_kernel, 1, False),
        grid=(n_seq,),
        in_specs=[pl.BlockSpec((CHUNK, D_MODEL), cur),
                  pl.BlockSpec((CHUNK, WA), cur),
                  pl.BlockSpec((CHUNK, WB), cur),
                  pl.BlockSpec((CHUNK, 2 * D_MODEL), cur),
                  pl.BlockSpec((None, A_ROWS, WA), cache), pl.BlockSpec((CHUNK, WA), cur),
                  pl.BlockSpec((None, A_ROWS, WA), cache), pl.BlockSpec((CHUNK, WA), cur),
                  pl.BlockSpec((None, WINDOW, WKB), cache), pl.BlockSpec((CHUNK, WKB), cur),
                  pl.BlockSpec((None, WINDOW, WKB), cache), pl.BlockSpec((CHUNK, WKB), cur)]
                 + _const_specs(1)
                 + [pl.BlockSpec((1, 1, D_MODEL), lambda b: (b, 0, 0))],
        out_specs=pl.BlockSpec((CHUNK, D_MODEL), cur),
        out_shape=jax.ShapeDtypeStruct(x2d.shape, F32),
        scratch_shapes=_attn_scratch(CHUNK),
        compiler_params=_cparams(("arbitrary",)),
        name="attn_sample",
    )(x2d, qa, qb, gt, cak, ka, cav, va, cbk, kb, cbv, vb, *consts, gate1_g)


def _two_part_tile(n_first_tiles, first_ref, second_ref):
    return jnp.where(pl.program_id(0) < n_first_tiles, first_ref[...], second_ref[...])


def _router_kernel(n_prompt_tiles, xp_ref, xs_ref, shp_ref, shs_ref, scp_ref, scs_ref, g_ref, wrt_ref, br_ref,
                   h_ref, idx_ref, gate_ref, rank_ref, cnt_ref, carry):
    i = pl.program_id(0)

    @pl.when(i == 0)
    def _():
        carry[...] = jnp.zeros_like(carry)

    tm = xp_ref.shape[0]
    x = _two_part_tile(n_prompt_tiles, xp_ref, xs_ref)
    h = _group_affine(_rms(x) * g_ref[...], 1.0 + _two_part_tile(n_prompt_tiles, scp_ref, scs_ref),
                      _two_part_tile(n_prompt_tiles, shp_ref, shs_ref))
    _store_packed_rows(h_ref, h)

    logits = lax.dot_general(wrt_ref[...], h.astype(BF16), (((1,), (1,)), ((), ())),
                             preferred_element_type=F32)
    scores = jax.nn.sigmoid(logits)
    biased = scores + br_ref[...]
    neg = -jnp.inf

    sub = lax.broadcasted_iota(I32, (PER_GROUP, tm), 0).astype(F32)
    gs_rows = []
    for g in range(N_GROUPS):
        xg = biased[g * PER_GROUP:(g + 1) * PER_GROUP]
        m1 = jnp.max(xg, axis=0, keepdims=True)
        i1 = jnp.min(jnp.where(xg == m1, sub, float(PER_GROUP)), axis=0, keepdims=True)
        m2 = jnp.max(jnp.where(sub == i1, neg, xg), axis=0, keepdims=True)
        gs_rows.append(m1 + m2)
    gs = jnp.concatenate(gs_rows, axis=0)

    giota = lax.broadcasted_iota(I32, (N_GROUPS, tm), 0).astype(F32)
    keep = jnp.zeros((N_GROUPS, tm), F32)
    for _ in range(TOPK_GROUPS):
        m = jnp.max(gs, axis=0, keepdims=True)
        gi = jnp.min(jnp.where(gs == m, giota, float(N_GROUPS)), axis=0, keepdims=True)
        hit = giota == gi
        keep = jnp.where(hit, 1.0, keep)
        gs = jnp.where(hit, neg, gs)
    cand = jnp.concatenate(
        [jnp.where(keep[g:g + 1] > 0.0, biased[g * PER_GROUP:(g + 1) * PER_GROUP], neg)
         for g in range(N_GROUPS)], axis=0)

    eiota = lax.broadcasted_iota(I32, (N_EXPERTS, tm), 0).astype(F32)
    idx_rows, gate_rows = [], []
    chosen = jnp.zeros((N_EXPERTS, tm), F32)
    for _ in range(TOP_K):
        m = jnp.max(cand, axis=0, keepdims=True)
        ei = jnp.min(jnp.where(cand == m, eiota, float(N_EXPERTS)), axis=0, keepdims=True)
        sel = eiota == ei
        gate_rows.append(jnp.sum(jnp.where(sel, scores, 0.0), axis=0, keepdims=True))
        idx_rows.append(ei)
        chosen = jnp.where(sel, 1.0, chosen)
        cand = jnp.where(sel, neg, cand)
    gates = jnp.concatenate(gate_rows, axis=0)
    gates = gates / jnp.sum(gates, axis=0, keepdims=True) * ROUTED_SCALE
    gate_ref[...] = gates
    idx_ref[...] = jnp.concatenate(idx_rows, axis=0).astype(I32)

    r_i = lax.broadcasted_iota(I32, (tm, tm), 0)
    c_i = lax.broadcasted_iota(I32, (tm, tm), 1)
    upper = jnp.where(r_i < c_i, 1.0, 0.0).astype(BF16)
    before = jnp.dot(chosen.astype(BF16), upper, preferred_element_type=F32) + carry[...]
    rank_rows = [jnp.sum(jnp.where(eiota == idx_rows[k], before, 0.0), axis=0, keepdims=True)
                 for k in range(TOP_K)]
    rank_ref[...] = jnp.concatenate(rank_rows, axis=0).astype(I32)
    total = carry[...] + jnp.sum(chosen, axis=1, keepdims=True)
    carry[...] = total
    cnt_ref[...] = total


def _router(x1_p, x1_s, shift_gp, shift_gs, scale_gp, scale_gs, g_pre, w_router_t, b_router_col):
    n = x1_p.shape[0] + x1_s.shape[0]
    nt = n // MOE_TILE
    npt = x1_p.shape[0] // MOE_TILE
    gpt = MOE_TILE // CHUNK
    lane_blk = lambda i: (0, i)
    return pl.pallas_call(
        functools.partial(_router_kernel, npt),
        grid=(nt,),
        in_specs=[pl.BlockSpec((MOE_TILE, D_MODEL), lambda i: (jnp.minimum(i, npt - 1), 0)),
                  pl.BlockSpec((MOE_TILE, D_MODEL), lambda i: (jnp.maximum(i - npt, 0), 0)),
                  pl.BlockSpec((gpt, 1, D_MODEL), lambda i: (jnp.minimum(i, npt - 1), 0, 0)),
                  pl.BlockSpec((gpt, 1, D_MODEL), lambda i: (jnp.maximum(i - npt, 0), 0, 0)),
                  pl.BlockSpec((gpt, 1, D_MODEL), lambda i: (jnp.minimum(i, npt - 1), 0, 0)),
                  pl.BlockSpec((gpt, 1, D_MODEL), lambda i: (jnp.maximum(i - npt, 0), 0, 0)),
                  pl.BlockSpec((1, D_MODEL), lambda i: (0, 0)),
                  pl.BlockSpec((N_EXPERTS, D_MODEL), lambda i: (0, 0)),
                  pl.BlockSpec((N_EXPERTS, 1), lambda i: (0, 0))],
        out_specs=[pl.BlockSpec((MOE_TILE * ROW_SUBLANES, LANES), lambda i: (i, 0)),
                   pl.BlockSpec((TOP_K, MOE_TILE), lane_blk),
                   pl.BlockSpec((TOP_K, MOE_TILE), lane_blk),
                   pl.BlockSpec((TOP_K, MOE_TILE), lane_blk),
                   pl.BlockSpec((N_EXPERTS, 1), lambda i: (0, 0))],
        out_shape=[jax.ShapeDtypeStruct((n * ROW_SUBLANES, LANES), U32),
                   jax.ShapeDtypeStruct((TOP_K, n), I32),
                   jax.ShapeDtypeStruct((TOP_K, n), F32),
                   jax.ShapeDtypeStruct((TOP_K, n), I32),
                   jax.ShapeDtypeStruct((N_EXPERTS, 1), F32)],
        scratch_shapes=[pltpu.VMEM((N_EXPERTS, 1), F32)],
        compiler_params=_cparams(("arbitrary",)),
        name="router",
    )(x1_p, x1_s, shift_gp, shift_gs, scale_gp, scale_gs, g_pre, w_router_t, b_router_col)


def _dest_kernel(idx_ref, rank_ref, start_ref, o_ref):
    tabs = [jnp.broadcast_to(start_ref[:, t * LANES:(t + 1) * LANES], (TOP_K, LANES))
            for t in range(N_EXPERTS // LANES)]
    for sub in range(o_ref.shape[0]):
        parts = []
        for c in range(MOE_TILE // LANES):
            cols = slice(sub * MOE_TILE + c * LANES, sub * MOE_TILE + (c + 1) * LANES)
            idx = idx_ref[:, cols]
            lane = idx % LANES
            val = jnp.take_along_axis(tabs[0], lane, axis=1)
            for t in range(1, len(tabs)):
                val = jnp.where(idx >= t * LANES, jnp.take_along_axis(tabs[t], lane, axis=1), val)
            parts.append(val + rank_ref[:, cols])
        o_ref[sub] = jnp.concatenate(parts, axis=1)


def _dest_rows(idx, rank, start_row):
    n = idx.shape[1]
    nt = n // MOE_TILE
    per_step = next(c for c in (10, 8, 5, 4, 2, 1) if nt % c == 0)
    return pl.pallas_call(
        _dest_kernel,
        grid=(nt // per_step,),
        in_specs=[pl.BlockSpec((TOP_K, per_step * MOE_TILE), lambda i: (0, i)),
                  pl.BlockSpec((TOP_K, per_step * MOE_TILE), lambda i: (0, i)),
                  pl.BlockSpec((1, N_EXPERTS), lambda i: (0, 0))],
        out_specs=pl.BlockSpec((per_step, TOP_K, MOE_TILE), lambda i: (i, 0, 0)),
        out_shape=jax.ShapeDtypeStruct((nt, TOP_K, MOE_TILE), I32),
        compiler_params=_cparams(("arbitrary",)),
        name="dest_rows",
    )(idx, rank, start_row)


DEST_PER_TILE = TOP_K * MOE_TILE


def _row_slice(ref, row):
    return ref.at[pl.ds(pl.multiple_of(row * ROW_SUBLANES, ROW_SUBLANES), ROW_SUBLANES)]


TILE_SUBLANES = MOE_TILE * ROW_SUBLANES


def _dispatch_kernel(dest_hbm, h_hbm, xs_hbm, dest_s, hbuf, dsem, hsem, rsem):
    i = pl.program_id(0)
    nt = pl.num_programs(0)

    def load(tile):
        dst = dest_s.at[pl.ds((tile % 2) * DEST_PER_TILE, DEST_PER_TILE)]
        rows = h_hbm.at[pl.ds(pl.multiple_of(tile * TILE_SUBLANES, TILE_SUBLANES), TILE_SUBLANES)]
        return (pltpu.make_async_copy(dest_hbm.at[pl.ds(tile * DEST_PER_TILE, DEST_PER_TILE)], dst, dsem.at[tile % 2]),
                pltpu.make_async_copy(rows, hbuf.at[tile % 3], hsem.at[tile % 3]))

    def drain(tile):
        for k in range(TOP_K):
            pltpu.make_async_copy(hbuf.at[tile % 3], xs_hbm.at[pl.ds(0, TILE_SUBLANES)],
                                  rsem.at[(tile % 2) * TOP_K + k]).wait()

    @pl.when(i == 0)
    def _():
        for cp in load(0):
            cp.start()

    @pl.when(i + 1 < nt)
    def _():
        for cp in load(i + 1):
            cp.start()

    for cp in load(i):
        cp.wait()
    base = (i % 2) * DEST_PER_TILE
    src_buf = hbuf.at[i % 3]
    sem0 = (i % 2) * TOP_K

    def body(t, carry):
        src = _row_slice(src_buf, t)
        for k in range(TOP_K):
            d = dest_s[base + k * MOE_TILE + t]
            pltpu.make_async_copy(src, _row_slice(xs_hbm, d), rsem.at[sem0 + k]).start(priority=k % 2)
        return carry

    lax.fori_loop(0, MOE_TILE, body, 0, unroll=4)

    @pl.when(i >= 1)
    def _():
        drain(i - 1)

    @pl.when(i == nt - 1)
    def _():
        drain(i)


def _dispatch(dest_flat, h_packed, n_rows):
    nt = h_packed.shape[0] // TILE_SUBLANES
    return pl.pallas_call(
        _dispatch_kernel,
        grid=(nt,),
        in_specs=[pl.BlockSpec(memory_space=pl.ANY),
                  pl.BlockSpec(memory_space=pl.ANY)],
        out_specs=pl.BlockSpec(memory_space=pl.ANY),
        out_shape=jax.ShapeDtypeStruct((n_rows * ROW_SUBLANES, LANES), U32),
        scratch_shapes=[pltpu.SMEM((2 * DEST_PER_TILE,), I32),
                        pltpu.VMEM((3, TILE_SUBLANES, LANES), U32),
                        pltpu.SemaphoreType.DMA((2,)),
                        pltpu.SemaphoreType.DMA((3,)),
                        pltpu.SemaphoreType.DMA((2 * TOP_K,))],
        compiler_params=_cparams(("arbitrary",)),
        name="dispatch",
    )(dest_flat, h_packed)


BLOCK_SUBLANES = EXPERT_ROWS * ROW_SUBLANES
EXPERT_AHEAD = 6
EXPERT_SLOTS = EXPERT_AHEAD + 2


def _expert_kernel(first_ref, last_ref, cnt_ref, na_ref, xs_hbm, wg_ref, wu_ref, wd_ref, ys_hbm,
                   wg_s, wu_s, wd_s, xbuf, ybuf, isem, osem):
    e = pl.program_id(0)
    n_active = na_ref[0]
    first, last = first_ref[e], last_ref[e]

    def block_rows(ref, g):
        return ref.at[pl.ds(pl.multiple_of(g * BLOCK_SUBLANES, BLOCK_SUBLANES), BLOCK_SUBLANES)]

    def fetch(g, slot):
        return pltpu.make_async_copy(block_rows(xs_hbm, g), xbuf.at[slot], isem.at[slot])

    def flush(g, slot):
        return pltpu.make_async_copy(ybuf.at[slot], block_rows(ys_hbm, g), osem.at[slot])

    @pl.when(e == 0)
    def _():
        for g0 in range(EXPERT_AHEAD):
            @pl.when(g0 < n_active)
            def _():
                fetch(g0, g0).start()

    @pl.when(last > first)
    def _():
        wg_s[...] = wg_ref[...].astype(BF16)
        wu_s[...] = wu_ref[...].astype(BF16)
        wd_s[...] = wd_ref[...].astype(BF16)

    def enter(g):
        slot = g % EXPERT_SLOTS
        fetch(g, slot).wait()

        @pl.when(g + EXPERT_AHEAD < n_active)
        def _():
            fetch(g + EXPERT_AHEAD, (g + EXPERT_AHEAD) % EXPERT_SLOTS).start()

        @pl.when(g >= EXPERT_SLOTS)
        def _():
            flush(g - EXPERT_SLOTS, slot).wait()
        return slot

    def swiglu(halves):
        xa = jnp.concatenate([h[0] for h in halves], axis=0).astype(BF16)
        xb = jnp.concatenate([h[1] for h in halves], axis=0).astype(BF16)

        def up(w_s):
            return (jnp.dot(xa, w_s[:HALF_D, :], preferred_element_type=F32)
                    + jnp.dot(xb, w_s[HALF_D:, :], preferred_element_type=F32))

        gate = up(wg_s)
        a = (gate * jax.nn.sigmoid(gate) * up(wu_s)).astype(BF16)
        return jnp.dot(a, wd_s[...], preferred_element_type=F32)

    def n_valid(g):
        return cnt_ref[e] - (g - first) * EXPERT_ROWS

    def pair_body(p, carry):
        g = first + 2 * p
        s0 = enter(g)
        s1 = enter(g + 1)
        y = swiglu([_load_packed_rows(xbuf.at[s0], EXPERT_ROWS),
                    _load_packed_rows(xbuf.at[s1], EXPERT_ROWS, n_valid(g + 1))])
        _store_packed_rows(ybuf.at[s0], y[:EXPERT_ROWS])
        _store_packed_rows(ybuf.at[s1], y[EXPERT_ROWS:])
        flush(g, s0).start()
        flush(g + 1, s1).start()
        return carry

    n_pairs = lax.shift_right_logical(last - first, 1)
    lax.fori_loop(0, n_pairs, pair_body, 0)

    @pl.when((last - first) % 2 == 1)
    def _():
        g = last - 1
        s0 = enter(g)
        y = swiglu([_load_packed_rows(xbuf.at[s0], EXPERT_ROWS, n_valid(g))])
        _store_packed_rows(ybuf.at[s0], y)
        flush(g, s0).start()

    @pl.when(e == pl.num_programs(0) - 1)
    def _():
        for back in range(EXPERT_SLOTS, 0, -1):
            @pl.when(n_active >= back)
            def _():
                flush(n_active - back, (n_active - back) % EXPERT_SLOTS).wait()


def _experts(blk_first, blk_last, counts, n_active, xs, we_gate, we_up, we_down):
    wmap = lambda e, *_: (e, 0, 0)
    return pl.pallas_call(
        _expert_kernel,
        grid_spec=pltpu.PrefetchScalarGridSpec(
            num_scalar_prefetch=4,
            grid=(N_EXPERTS,),
            in_specs=[pl.BlockSpec(memory_space=pl.ANY),
                      pl.BlockSpec((None, D_MODEL, EXPERT_DIM), wmap),
                      pl.BlockSpec((None, D_MODEL, EXPERT_DIM), wmap),
                      pl.BlockSpec((None, EXPERT_DIM, D_MODEL), wmap)],
            out_specs=pl.BlockSpec(memory_space=pl.ANY),
            scratch_shapes=[pltpu.VMEM((D_MODEL, EXPERT_DIM), BF16),
                            pltpu.VMEM((D_MODEL, EXPERT_DIM), BF16),
                            pltpu.VMEM((EXPERT_DIM, D_MODEL), BF16),
                            pltpu.VMEM((EXPERT_SLOTS, BLOCK_SUBLANES, LANES), U32),
                            pltpu.VMEM((EXPERT_SLOTS, BLOCK_SUBLANES, LANES), U32),
                            pltpu.SemaphoreType.DMA((EXPERT_SLOTS,)),
                            pltpu.SemaphoreType.DMA((EXPERT_SLOTS,))]),
        out_shape=jax.ShapeDtypeStruct(xs.shape, U32),
        compiler_params=_cparams(("arbitrary",)),
        name="experts",
    )(blk_first, blk_last, counts, n_active, xs, we_gate, we_up, we_down)


COMBINE_ROWS = 8


def _combine_kernel(n_prompt_tiles,
                    dest_hbm, ys_hbm, xp_ref, xs_ref, h_ref, gate_ref, wsg_ref, wsu_ref, wsd_ref, gpost_ref,
                    gate2p_ref, gate2s_ref,
                    yp_ref, ysm_ref, dest_s, buf, shared_s, gcol_s, gate2_s, y_s, dsem, rsem):
    i = pl.program_id(0)
    nt = pl.num_programs(0)

    def dest_copy(tile):
        slot = tile % 3
        return pltpu.make_async_copy(dest_hbm.at[pl.ds(tile * DEST_PER_TILE, DEST_PER_TILE)],
                                     dest_s.at[pl.ds(slot * DEST_PER_TILE, DEST_PER_TILE)], dsem.at[slot])

    def gather_token(base, bset, t):
        dst0 = bset * TILE_SUBLANES + t * ROW_SUBLANES
        for k in range(TOP_K):
            d = dest_s[base + t + k * MOE_TILE]
            dst = buf.at[pl.ds(pl.multiple_of(dst0 + k * TILE_SUBLANES, ROW_SUBLANES), ROW_SUBLANES)]
            pltpu.make_async_copy(_row_slice(ys_hbm, d), dst, rsem.at[bset + k]).start(priority=k % 2)

    def wait_rows(bset):
        for k in range(TOP_K):
            pltpu.make_async_copy(ys_hbm.at[pl.ds(0, TILE_SUBLANES)], buf.at[pl.ds(0, TILE_SUBLANES)],
                                  rsem.at[bset + k]).wait()

    @pl.when(i == 0)
    def _():
        dest_copy(0).start()

        @pl.when(nt > 1)
        def _():
            dest_copy(1).start()

        dest_copy(0).wait()

        def first_tile(t, carry):
            gather_token(0, 0, t)
            return carry

        lax.fori_loop(0, MOE_TILE, first_tile, 0, unroll=4)

    @pl.when(i + 2 < nt)
    def _():
        dest_copy(i + 2).start()

    @pl.when(i + 1 < nt)
    def _():
        dest_copy(i + 1).wait()

    nxt = jnp.minimum(i + 1, nt - 1)
    base_next = (nxt % 3) * DEST_PER_TILE
    bset = (i % 2) * TOP_K
    bset_next = ((i + 1) % 2) * TOP_K

    ha, hb = _load_packed_rows(h_ref, MOE_TILE)
    ha = ha.astype(BF16)
    hb = hb.astype(BF16)

    def up(w_ref):
        return (jnp.dot(ha, w_ref[:HALF_D, :], preferred_element_type=F32)
                + jnp.dot(hb, w_ref[HALF_D:, :], preferred_element_type=F32))

    g = up(wsg_ref)
    a = (g * jax.nn.sigmoid(g) * up(wsu_ref)).astype(BF16)
    shared_s[...] = jnp.dot(a, wsd_ref[...], preferred_element_type=F32)

    gpad = jnp.concatenate([gate_ref[...], jnp.zeros((LANES - TOP_K, MOE_TILE), F32)], axis=0)
    gcol_s[...] = gpad.T

    wait_rows(bset)
    in_first = i < n_prompt_tiles
    gate2_s[...] = _two_part_tile(n_prompt_tiles, gate2p_ref, gate2s_ref)

    def chunk(j, carry):
        r0 = pl.multiple_of(j * COMBINE_ROWS, COMBINE_ROWS)
        rows = pl.ds(r0, COMBINE_ROWS)
        acc_hi = [jnp.zeros((COMBINE_ROWS, LANES), F32) for _ in range(ROW_SUBLANES)]
        acc_lo = [jnp.zeros((COMBINE_ROWS, LANES), F32) for _ in range(ROW_SUBLANES)]
        src0 = bset * TILE_SUBLANES + r0 * ROW_SUBLANES
        for k in range(TOP_K):
            gk = jnp.broadcast_to(gcol_s[rows, k:k + 1], (COMBINE_ROWS, LANES))
            for s in range(ROW_SUBLANES):
                hi, lo = _unpack_halves(
                    buf[pl.ds(src0 + k * TILE_SUBLANES + s, COMBINE_ROWS, stride=ROW_SUBLANES), :])
                acc_hi[s] = acc_hi[s] + hi * gk
                acc_lo[s] = acc_lo[s] + lo * gk
        f = jnp.concatenate(acc_hi + acc_lo, axis=1) + shared_s[rows, :]
        x = jnp.where(in_first, xp_ref[rows, :], xs_ref[rows, :])
        gate2 = gate2_s[lax.shift_right_logical(j * COMBINE_ROWS, CHUNK.bit_length() - 1)]
        y = x + _rms(f) * gpost_ref[...] * gate2
        for t in range(COMBINE_ROWS):
            gather_token(base_next, bset_next, r0 + t)
        y_s[rows, :] = y
        return carry

    lax.fori_loop(0, MOE_TILE // COMBINE_ROWS, chunk, 0, unroll=2)

    @pl.when(in_first)
    def _():
        yp_ref[...] = y_s[...]

    @pl.when(jnp.logical_not(in_first))
    def _():
        ysm_ref[...] = y_s[...]

    @pl.when(i == nt - 1)
    def _():
        wait_rows(bset_next)


def _combine(dest_flat, ys, x1_p, x1_s, h_packed, gates, ws_gate, ws_up, ws_down, g_post, gate2_gp, gate2_gs):
    n_prompt = x1_p.shape[0]
    n = n_prompt + x1_s.shape[0]
    nt = n // MOE_TILE
    npt = n_prompt // MOE_TILE
    gpt = MOE_TILE // CHUNK
    z2 = lambda i: (0, 0)
    return pl.pallas_call(
        functools.partial(_combine_kernel, npt),
        grid=(nt,),
        in_specs=[pl.BlockSpec(memory_space=pl.ANY),
                  pl.BlockSpec(memory_space=pl.ANY),
                  pl.BlockSpec((MOE_TILE, D_MODEL), lambda i: (jnp.minimum(i, npt - 1), 0)),
                  pl.BlockSpec((MOE_TILE, D_MODEL), lambda i: (jnp.maximum(i - npt, 0), 0)),
                  pl.BlockSpec((MOE_TILE * ROW_SUBLANES, LANES), lambda i: (i, 0)),
                  pl.BlockSpec((TOP_K, MOE_TILE), lambda i: (0, i)),
                  pl.BlockSpec((D_MODEL, EXPERT_DIM), z2),
                  pl.BlockSpec((D_MODEL, EXPERT_DIM), z2),
                  pl.BlockSpec((EXPERT_DIM, D_MODEL), z2),
                  pl.BlockSpec((1, D_MODEL), z2),
                  pl.BlockSpec((gpt, 1, D_MODEL), lambda i: (jnp.minimum(i, npt - 1), 0, 0)),
                  pl.BlockSpec((gpt, 1, D_MODEL), lambda i: (jnp.maximum(i - npt, 0), 0, 0))],
        out_specs=[pl.BlockSpec((MOE_TILE, D_MODEL), lambda i: (jnp.minimum(i, npt - 1), 0)),
                   pl.BlockSpec((MOE_TILE, D_MODEL), lambda i: (jnp.maximum(i - npt, 0), 0))],
        out_shape=[jax.ShapeDtypeStruct((n_prompt, D_MODEL), F32),
                   jax.ShapeDtypeStruct((n - n_prompt, D_MODEL), F32)],
        scratch_shapes=[pltpu.SMEM((3 * DEST_PER_TILE,), I32),
                        pltpu.VMEM((2 * TOP_K * TILE_SUBLANES, LANES), U32),
                        pltpu.VMEM((MOE_TILE, D_MODEL), F32),
                        pltpu.VMEM((MOE_TILE, LANES), F32),
                        pltpu.VMEM((MOE_TILE // CHUNK, 1, D_MODEL), F32),
                        pltpu.VMEM((MOE_TILE, D_MODEL), F32),
                        pltpu.SemaphoreType.DMA((3,)),
                        pltpu.SemaphoreType.DMA((2 * TOP_K,))],
        compiler_params=_cparams(("arbitrary",)),
        name="combine",
    )(dest_flat, ys, x1_p, x1_s, h_packed, gates, ws_gate, ws_up, ws_down, g_post, gate2_gp, gate2_gs)


ROPE_SPLIT = 128


def _rope_tables(start, length):
    half = HEAD_DIM // 2
    inv = ROPE_THETA ** (-jnp.arange(half, dtype=F32) / half)

    def cos_sin(pos):
        ang = pos.astype(F32)[:, None] * inv[None, :]
        return jnp.cos(ang), jnp.sin(ang)

    if start == 0 and length % ROPE_SPLIT == 0 and length > ROPE_SPLIT:
        ch, sh = cos_sin(jnp.arange(length // ROPE_SPLIT) * ROPE_SPLIT)
        cl, sl = cos_sin(jnp.arange(ROPE_SPLIT))
        ch, sh, cl, sl = ch[:, None, :], sh[:, None, :], cl[None], sl[None]
        cos = (ch * cl - sh * sl).reshape(length, half)
        sin = (sh * cl + ch * sl).reshape(length, half)
    else:
        cos, sin = cos_sin(start + jnp.arange(length))
    reps = LANES // half
    sign = jnp.where((jnp.arange(LANES) // half) % 2 == 0, -1.0, 1.0).astype(F32)
    return jnp.tile(cos, (1, reps)), jnp.tile(sin, (1, reps)) * sign


def _groups(vec_rows, reps):
    rows, width = vec_rows.shape
    return jnp.broadcast_to(vec_rows[:, None, :], (rows, reps, width)).reshape(rows * reps, 1, width)


def kernel(x_prompt, x_sample, cache_a_k, cache_a_v, cache_b_k, cache_b_v, c_prompt, c_sample, w_ada, b_ada,
           g_pre_mix, g_post_mix, w_in, rel_bias_a, sinks_b, w_branch_a, w_branch_b, w_out, g_pre_ffn,
           g_post_ffn, w_router, b_router, we_gate, we_up, we_down, ws_gate, ws_up, ws_down):
    assert w_ada.shape[0] == 1, "single layer"
    nb, seq, d = x_prompt.shape
    ns, dec = x_sample.shape[:2]
    assert d == D_MODEL and dec == CHUNK and seq % TOK_TILE == 0 and (ns * dec) == TOK_TILE
    n_p, n_s = nb * seq, ns * dec
    n_all = n_p + n_s
    assert n_all % MOE_TILE == 0 and n_p % MOE_TILE == 0

    c_all = jnp.concatenate([c_prompt, c_sample], axis=0)
    pad = (-c_all.shape[0]) % 8
    c_all = jnp.pad(c_all, ((0, pad), (0, 0)))
    mod = _modulation(c_all, w_ada[0], b_ada[0])
    mod_p, mod_s = mod[:nb], mod[nb:nb + ns]
    cpp = seq // CHUNK

    def part(k):
        return mod_p[:, k * d:(k + 1) * d], mod_s[:, k * d:(k + 1) * d]

    (sh1p, sh1s), (sc1p, sc1s), (g1p, g1s), (sh2p, sh2s), (sc2p, sc2s), (g2p, g2s) = [part(k) for k in range(6)]

    w_in_bf = w_in[0].astype(BF16)
    g_pre = g_pre_mix[0].reshape(1, d)
    cos_p, sin_p = _rope_tables(0, seq)
    cos_s, sin_s = _rope_tables(PAST_LEN, dec)
    cos_s, sin_s = jnp.tile(cos_s, (ns, 1)), jnp.tile(sin_s, (ns, 1))

    xp2 = x_prompt.reshape(n_p, d)
    xs2 = x_sample.reshape(n_s, d)
    outs_p = _inproj(xp2, _groups(sh1p, cpp), _groups(sc1p, cpp), g_pre, w_in_bf, cos_p, sin_p, nb, True)
    outs_s = _inproj(xs2, _groups(sh1s, 1), _groups(sc1s, 1), g_pre, w_in_bf, cos_s, sin_s, ns, False)

    table = rel_bias_a[0].astype(F32)
    n_far = A_BAND - 1 - REL_CLIP
    ext = jnp.concatenate([jnp.broadcast_to(table[:, 2 * REL_CLIP:], (A_HEADS, n_far)),
                           jnp.flip(table[:, REL_CLIP - (CHUNK - 1):], axis=1)], axis=1)
    n_ext = ext.shape[1]
    skew = jnp.tile(jnp.pad(ext, ((0, 0), (0, 1))), (1, CHUNK))[:, :CHUNK * n_ext].reshape(A_HEADS, CHUNK, n_ext)
    bias = skew[:, :, CHUNK - 1:CHUNK - 1 + A_BAND]
    bias_pairs = bias.reshape(A_HEADS // 2, 2 * CHUNK, A_BAND) * LOG2_E
    sink_rows = jnp.broadcast_to((sinks_b[0].astype(F32) * LOG2_E).reshape(B_KV_HEADS, B_GROUP, 1),
                                 (B_KV_HEADS, B_GROUP, CHUNK)).reshape(B_KV_HEADS, B_GROUP * CHUNK, 1)
    consts = (bias_pairs, sink_rows, w_branch_a[0].astype(BF16), w_branch_b[0].astype(BF16),
              w_out[0].astype(BF16), g_post_mix[0].reshape(1, d))

    x1_p = _attn_prompt(xp2, outs_p[:7], consts, _groups(g1p, cpp), nb)
    caches = (cache_a_k[0].reshape(ns, A_ROWS, WA), cache_a_v[0].reshape(ns, A_ROWS, WA),
              cache_b_k[0].reshape(ns, WINDOW, WKB), cache_b_v[0].reshape(ns, WINDOW, WKB))
    x1_s = _attn_sample(xs2, outs_s[:7], caches, consts, _groups(g1s, 1))

    h_packed, idx, gates, rank, counts = _router(
        x1_p, x1_s, _groups(sh2p, cpp), _groups(sh2s, 1), _groups(sc2p, cpp), _groups(sc2s, 1),
        g_pre_ffn[0].reshape(1, d),
        w_router[0].T.astype(BF16), b_router[0].astype(F32).reshape(N_EXPERTS, 1))
    n_blocks = (n_all * TOP_K) // EXPERT_ROWS + N_EXPERTS
    cnt = counts[:, 0].astype(I32)
    blocks_e = (cnt + EXPERT_ROWS - 1) // EXPERT_ROWS
    blk_end = jnp.cumsum(blocks_e)
    blk_start = blk_end - blocks_e
    n_active = blk_end[-1:]
    start_row = (blk_start * EXPERT_ROWS).astype(I32).reshape(1, N_EXPERTS)

    dest_flat = _dest_rows(idx, rank, start_row).reshape(-1)
    xs = _dispatch(dest_flat, h_packed, n_blocks * EXPERT_ROWS)
    ys = _experts(blk_start.astype(I32), blk_end.astype(I32), cnt, n_active.astype(I32), xs,
                  we_gate[0], we_up[0], we_down[0])
    y_p, y_s = _combine(dest_flat, ys, x1_p, x1_s, h_packed, gates, ws_gate[0].astype(BF16),
                        ws_up[0].astype(BF16), ws_down[0].astype(BF16), g_post_ffn[0].reshape(1, d),
                        _groups(g2p, cpp), _groups(g2s, 1))

    a_heads = (A_HEADS, HEAD_DIM)
    b_heads = (B_KV_HEADS, HEAD_DIM)
    return (y_p.reshape(nb, seq, d), y_s.reshape(ns, dec, d),
            outs_p[7].reshape(1, nb, A_ROWS, *a_heads), outs_p[8].reshape(1, nb, A_ROWS, *a_heads),
            outs_p[9].reshape(1, nb, WINDOW, *b_heads), outs_p[10].reshape(1, nb, WINDOW, *b_heads),
            outs_s[7].reshape(1, ns, dec, *a_heads), outs_s[8].reshape(1, ns, dec, *a_heads),
            outs_s[9].reshape(1, ns, dec, *b_heads), outs_s[10].reshape(1, ns, dec, *b_heads))
```
